```python
import math
import jax, jax.numpy as jnp
from jax import lax
import numpy as np

D_MODEL = 2048
BATCH = 4
SEQ = 8192
DEPTH = 1

GDN_HEADS = 8
GDN_HEAD_DIM = 128
GDN_WIDTH = GDN_HEADS * GDN_HEAD_DIM
GDN_CONV = 4
GDN_CHUNK = 64
MOBA_HEADS = 8
MOBA_HEAD_DIM = 128
MOBA_WIDTH = MOBA_HEADS * MOBA_HEAD_DIM
MOBA_BLOCK = 256
MOBA_TOPK = 3
MOBA_Q_CHUNK = 16
REL_BUCKETS = 32
REL_MAX_DIST = 2048
D_FF = ((8 * D_MODEL // 3 + 255) // 256) * 256
PROJ_SPLITS = (GDN_WIDTH,) * 4 + (GDN_HEADS,) * 2 + (MOBA_WIDTH,) * 3 + (D_MODEL,) * 2
D_PROJ = sum(PROJ_SPLITS)
RMS_EPS = 1e-6
NEG_INF = -1e30

kernel_name = 'hybrid_gdn_moba_gated_block'


def _rmsnorm(x, w):
    x32 = x.astype(jnp.float32)
    y = x32 * lax.rsqrt(jnp.mean(x32 * x32, axis=-1, keepdims=True) + RMS_EPS)
    return (y * w.astype(jnp.float32)).astype(x.dtype)


def _l2norm(x):
    return x * lax.rsqrt(jnp.sum(x * x, axis=-1, keepdims=True) + RMS_EPS)


def _causal_short_conv(u, w):
    K = w.shape[0]
    T = u.shape[1]
    up = jnp.pad(u, ((0, 0), (K - 1, 0), (0, 0)))
    y = up[:, 0:T] * w[0]
    for j in range(1, K):
        y = y + up[:, j:j + T] * w[j]
    return jax.nn.silu(y)


def _chunk_gated_delta_rule(q, k, v, g, beta):
    B, H, T, Dk = q.shape
    Dv = v.shape[-1]
    C = GDN_CHUNK
    N = T // C
    q = q * (Dk ** -0.5)
    q, k, v = (t.reshape(B, H, N, C, t.shape[-1]) for t in (q, k, v))
    g = jnp.cumsum(g.reshape(B, H, N, C), axis=-1)
    beta = beta.reshape(B, H, N, C)
    incl = jnp.tril(jnp.ones((C, C), dtype=bool))
    strict = jnp.tril(jnp.ones((C, C), dtype=bool), k=-1)
    diff = g[..., :, None] - g[..., None, :]
    decay = jnp.where(incl, jnp.exp(jnp.where(incl, diff, 0.0)), 0.0)
    k_beta = k * beta[..., None]
    L = jnp.where(strict, jnp.einsum('bhnik,bhnjk->bhnij', k_beta, k) * decay, 0.0)
    eye = jnp.eye(C, dtype=q.dtype)
    rhs = jnp.concatenate([k_beta * jnp.exp(g)[..., None], v * beta[..., None]], axis=-1)
    wu = lax.linalg.triangular_solve(eye + L, rhs, left_side=True, lower=True)
    w, u = wu[..., :Dk], wu[..., Dk:]
    A = jnp.where(incl, jnp.einsum('bhnik,bhnjk->bhnij', q, k) * decay, 0.0)
    g_last = g[..., -1]
    q_dec = q * jnp.exp(g)[..., None]
    k_dec = k * jnp.exp(g_last[..., None] - g)[..., None]
    xs = tuple(jnp.moveaxis(t, 2, 0) for t in (q_dec, k_dec, w, u, A, g_last))

    def step(S, inp):
        q_n, k_n, w_n, u_n, A_n, gl_n = inp
        v_new = u_n - jnp.einsum('bhck,bhkv->bhcv', w_n, S)
        o_n = jnp.einsum('bhck,bhkv->bhcv', q_n, S) + jnp.einsum('bhcs,bhsv->bhcv', A_n, v_new)
        S = S * jnp.exp(gl_n)[..., None, None] + jnp.einsum('bhck,bhcv->bhkv', k_n, v_new)
        return S, o_n

    S0 = jnp.zeros((B, H, Dk, Dv), q.dtype)
    _, o = lax.scan(step, S0, xs)
    return jnp.moveaxis(o, 0, 2).reshape(B, H, T, Dv)


def _gated_deltanet(q, k, v, z, beta_logit, a_logit, conv_w, a_log, dt_bias, o_norm_w):
    B, T, _ = q.shape
    H, Dh = GDN_HEADS, GDN_HEAD_DIM
    f32 = jnp.float32
    qkv = _causal_short_conv(jnp.concatenate([q, k, v], axis=-1), conv_w).astype(f32)
    q, k, v = (t.reshape(B, T, H, Dh).transpose(0, 2, 1, 3) for t in jnp.split(qkv, 3, axis=-1))
    q = _l2norm(q)
    k = _l2norm(k)
    beta = jax.nn.sigmoid(beta_logit.astype(f32)).transpose(0, 2, 1)
    g = -jnp.exp(a_log.astype(f32)) * jax.nn.softplus(a_logit.astype(f32) + dt_bias.astype(f32))
    g = g.transpose(0, 2, 1)
    o = _chunk_gated_delta_rule(q, k, v, g, beta).transpose(0, 2, 1, 3)
    o = _rmsnorm(o, o_norm_w) * jax.nn.silu(z.astype(f32).reshape(B, T, H, Dh))
    return o.reshape(B, T, H * Dh).astype(z.dtype)


def _t5_bucket(dist):
    max_exact = REL_BUCKETS // 2
    d = dist.astype(jnp.float32)
    log_ratio = jnp.log(jnp.maximum(d, float(max_exact)) / max_exact) / math.log(REL_MAX_DIST / max_exact)
    large = max_exact + (log_ratio * (REL_BUCKETS - max_exact)).astype(jnp.int32)
    large = jnp.minimum(large, REL_BUCKETS - 1)
    return jnp.where(dist < max_exact, dist, large)


def _moba_attention(q, k, v, q_norm_w, k_norm_w, rel_bias):
    B, T, _ = q.shape
    H, Dh, BS, QC = MOBA_HEADS, MOBA_HEAD_DIM, MOBA_BLOCK, MOBA_Q_CHUNK
    out_dtype = q.dtype

    def heads(t):
        return t.astype(jnp.float32).reshape(B, T, H, Dh).transpose(0, 2, 1, 3)

    q = _rmsnorm(heads(q), q_norm_w)
    k = _rmsnorm(heads(k), k_norm_w)
    v = heads(v)
    Tp = -(-T // BS) * BS
    pad = ((0, 0), (0, 0), (0, Tp - T), (0, 0))
    q, k, v = (jnp.pad(t, pad) for t in (q, k, v))
    NB = Tp // BS
    topk = min(MOBA_TOPK, NB)
    k_blocks = k.reshape(B, H, NB, BS, Dh)
    v_blocks = v.reshape(B, H, NB, BS, Dh)
    k_mean = jnp.mean(k_blocks, axis=3)
    route = jnp.einsum('bhtd,bhnd->bhtn', q, k_mean)
    q_block = jnp.arange(Tp) // BS
    fully_past = jnp.arange(NB)[None, :] < q_block[:, None]
    route = jnp.where(fully_past, route, NEG_INF)
    _, sel = lax.top_k(route, topk)
    n_qc = Tp // QC
    q_c = jnp.moveaxis(q.reshape(B, H, n_qc, QC, Dh), 2, 0)
    sel_c = jnp.moveaxis(sel.reshape(B, H, n_qc, QC, topk), 2, 0)
    rel_t = rel_bias.astype(jnp.float32).T
    b_idx = jnp.arange(B)[:, None, None, None]
    h_idx = jnp.arange(H)[None, :, None, None]
    scale = Dh ** -0.5
    offs = jnp.arange(BS)

    def attend(args):
        c, q_n, sel_n = args
        q_pos = c * QC + jnp.arange(QC)
        blk = (c * QC) // BS
        k_own = lax.dynamic_index_in_dim(k_blocks, blk, axis=2, keepdims=False)
        v_own = lax.dynamic_index_in_dim(v_blocks, blk, axis=2, keepdims=False)
        dist_own = q_pos[:, None] - (blk * BS + offs)[None, :]
        s_own = jnp.einsum('bhqd,bhkd->bhqk', q_n, k_own) * scale + rel_t[:, _t5_bucket(jnp.maximum(dist_own, 0))][None]
        s_own = jnp.where(dist_own >= 0, s_own, NEG_INF)
        k_sel = k_blocks[b_idx, h_idx, sel_n]
        v_sel = v_blocks[b_idx, h_idx, sel_n]
        dist_sel = q_pos[None, None, :, None, None] - (sel_n[..., None] * BS + offs)
        s_sel = jnp.einsum('bhqd,bhqnkd->bhqnk', q_n, k_sel) * scale + rel_t[h_idx[..., None], _t5_bucket(jnp.maximum(dist_sel, 0))]
        s_sel = jnp.where((sel_n < blk)[..., None], s_sel, NEG_INF)
        logits = jnp.concatenate([s_own, s_sel.reshape(B, H, QC, topk * BS)], axis=-1)
        p = jax.nn.softmax(logits, axis=-1)
        p_own = p[..., :BS]
        p_sel = p[..., BS:].reshape(B, H, QC, topk, BS)
        return jnp.einsum('bhqk,bhkd->bhqd', p_own, v_own) + jnp.einsum('bhqnk,bhqnkd->bhqd', p_sel, v_sel)

    o = lax.map(attend, (jnp.arange(n_qc), q_c, sel_c))
    o = jnp.moveaxis(o, 0, 2).reshape(B, H, Tp, Dh)[:, :, :T]
    return o.transpose(0, 2, 1, 3).reshape(B, T, H * Dh).astype(out_dtype)


def setup_inputs(seed: int = 0) -> dict:
    key = jax.random.key(seed)
    ks = jax.random.split(key, 17)
    f32 = jnp.float32

    def normal(k, shape, fan_in):
        return jax.random.normal(k, shape, f32) * fan_in ** -0.5

    def gain(k, shape):
        return 1.0 + 0.1 * jax.random.normal(k, shape, f32)

    x = jax.random.normal(ks[0], (BATCH, SEQ, D_MODEL), f32)
    norm_mix_w = gain(ks[1], (DEPTH, D_MODEL))
    w_in = normal(ks[2], (DEPTH, D_MODEL, D_PROJ), D_MODEL)
    conv_w = normal(ks[3], (DEPTH, GDN_CONV, 3 * GDN_WIDTH), GDN_CONV)
    a_log = jnp.log(jax.random.uniform(ks[4], (DEPTH, GDN_HEADS), f32, 1.0, 16.0))
    dt = jnp.exp(jax.random.uniform(ks[5], (DEPTH, GDN_HEADS), f32, math.log(1e-3), math.log(1e-1)))
    dt_bias = dt + jnp.log(-jnp.expm1(-dt))
    gdn_o_norm_w = gain(ks[6], (DEPTH, GDN_HEAD_DIM))
    q_norm_w = gain(ks[7], (DEPTH, MOBA_HEAD_DIM))
    k_norm_w = gain(ks[8], (DEPTH, MOBA_HEAD_DIM))
    rel_bias = 0.5 * jax.random.normal(ks[9], (REL_BUCKETS, MOBA_HEADS), f32)
    w_branch_gdn = normal(ks[10], (DEPTH, GDN_WIDTH, D_MODEL), GDN_WIDTH)
    w_branch_moba = normal(ks[11], (DEPTH, MOBA_WIDTH, D_MODEL), MOBA_WIDTH)
    w_out = normal(ks[12], (DEPTH, D_MODEL, D_MODEL), D_MODEL)
    norm_ffn_w = gain(ks[13], (DEPTH, D_MODEL))
    w_ffn_gate = normal(ks[14], (DEPTH, D_MODEL, D_FF), D_MODEL)
    w_ffn_up = normal(ks[15], (DEPTH, D_MODEL, D_FF), D_MODEL)
    w_ffn_down = normal(ks[16], (DEPTH, D_FF, D_MODEL), D_FF)
    return {'x': x, 'norm_mix_w': norm_mix_w, 'w_in': w_in, 'conv_w': conv_w, 'a_log': a_log,
            'dt_bias': dt_bias, 'gdn_o_norm_w': gdn_o_norm_w, 'q_norm_w': q_norm_w, 'k_norm_w': k_norm_w,
            'rel_bias': rel_bias, 'w_branch_gdn': w_branch_gdn, 'w_branch_moba': w_branch_moba,
            'w_out': w_out, 'norm_ffn_w': norm_ffn_w, 'w_ffn_gate': w_ffn_gate, 'w_ffn_up': w_ffn_up,
            'w_ffn_down': w_ffn_down}


def reference(x, norm_mix_w, w_in, conv_w, a_log, dt_bias, gdn_o_norm_w, q_norm_w, k_norm_w,
              rel_bias, w_branch_gdn, w_branch_moba, w_out, norm_ffn_w, w_ffn_gate, w_ffn_up, w_ffn_down):
    split_at = np.cumsum(PROJ_SPLITS)[:-1].tolist()
    h = x
    for l in range(DEPTH):
        u = _rmsnorm(h, norm_mix_w[l])
        proj = jnp.einsum('btd,de->bte', u, w_in[l])
        (q_a, k_a, v_a, z_a, beta_a, dec_a, q_b, k_b, v_b, gate_a, gate_b) = jnp.split(proj, split_at, axis=-1)
        y_a = _gated_deltanet(q_a, k_a, v_a, z_a, beta_a, dec_a, conv_w[l], a_log[l], dt_bias[l], gdn_o_norm_w[l])
        y_b = _moba_attention(q_b, k_b, v_b, q_norm_w[l], k_norm_w[l], rel_bias)
        mix = (jax.nn.sigmoid(gate_a) * jnp.einsum('btc,cd->btd', y_a, w_branch_gdn[l])
               + jax.nn.sigmoid(gate_b) * jnp.einsum('btc,cd->btd', y_b, w_branch_moba[l]))
        h = h + jnp.einsum('btd,de->bte', mix, w_out[l])
        u = _rmsnorm(h, norm_ffn_w[l])
        hid = jax.nn.silu(jnp.einsum('btd,df->btf', u, w_ffn_gate[l])) * jnp.einsum('btd,df->btf', u, w_ffn_up[l])
        h = h + jnp.einsum('btf,fd->btd', hid, w_ffn_down[l])
    return h
```

```python
import functools
import math

import jax
import jax.numpy as jnp
from jax import lax
from jax.experimental import pallas as pl
from jax.experimental.pallas import tpu as pltpu

F32 = jnp.float32
BF16 = jnp.bfloat16
HIGHEST = lax.Precision.HIGHEST

LANES = 128
HEAD_DIM = 128
GDN_HEADS = 8
GDN_CONV = 4
GDN_CHUNK = 64
MOBA_HEADS = 8
MOBA_BLOCK = 256
MOBA_TOPK = 3
REL_BUCKETS = 32
REL_MAX_DIST = 2048
BIAS_TILES = 8
RMS_EPS = 1e-6
NEG_INF = -1e30
VMEM_LIMIT = 56 * 1024 * 1024


def _cparams(*sem):
    return pltpu.CompilerParams(dimension_semantics=sem, vmem_limit_bytes=VMEM_LIMIT)


def _sigmoid(x):
    return 1.0 / (1.0 + jnp.exp(-x))


def _dot(a, b, precision=None):
    return jnp.dot(a, b, preferred_element_type=F32, precision=precision)


def _dot_nt(a, b, precision=None):
    return lax.dot_general(a, b, (((1,), (1,)), ((), ())),
                           preferred_element_type=F32, precision=precision)


def _rms_rows(x, w):
    return x * lax.rsqrt(jnp.mean(x * x, axis=-1, keepdims=True) + RMS_EPS) * w


def _norm_matmul_kernel(x_ref, nw_ref, w_ref, o_ref, u_scr):
    @pl.when(pl.program_id(1) == 0)
    def _():
        u_scr[...] = _rms_rows(x_ref[...], nw_ref[...]).astype(BF16)

    o_ref[...] = _dot(u_scr[...], w_ref[...]).astype(o_ref.dtype)


def _norm_matmul(x, nw, w, tm, tn, out_dtype, name):
    m, k = x.shape
    n = w.shape[1]
    return pl.pallas_call(
        _norm_matmul_kernel,
        grid=(m // tm, n // tn),
        in_specs=[pl.BlockSpec((tm, k), lambda i, j: (i, 0)),
                  pl.BlockSpec((1, k), lambda i, j: (0, 0)),
                  pl.BlockSpec((k, tn), lambda i, j: (0, j))],
        out_specs=pl.BlockSpec((tm, tn), lambda i, j: (i, j)),
        out_shape=jax.ShapeDtypeStruct((m, n), out_dtype),
        scratch_shapes=[pltpu.VMEM((tm, k), BF16)],
        compiler_params=_cparams("parallel", "arbitrary"),
        name=name,
    )(x, nw, w)


def _matmul_res_kernel(a_ref, w_ref, r_ref, o_ref):
    o_ref[...] = r_ref[...] + _dot(a_ref[...], w_ref[...])


def _matmul_res(a, w, res, tm, tn, name):
    m, k = a.shape
    n = w.shape[1]
    return pl.pallas_call(
        _matmul_res_kernel,
        grid=(m // tm, n // tn),
        in_specs=[pl.BlockSpec((tm, k), lambda i, j: (i, 0)),
                  pl.BlockSpec((k, tn), lambda i, j: (0, j)),
                  pl.BlockSpec((tm, tn), lambda i, j: (i, j))],
        out_specs=pl.BlockSpec((tm, tn), lambda i, j: (i, j)),
        out_shape=jax.ShapeDtypeStruct((m, n), F32),
        compiler_params=_cparams("parallel", "parallel"),
        name=name,
    )(a, w, res)


def _ffn_up_kernel(x_ref, nw_ref, wg_ref, wu_ref, o_ref, u_scr):
    @pl.when(pl.program_id(1) == 0)
    def _():
        u_scr[...] = _rms_rows(x_ref[...], nw_ref[...]).astype(BF16)

    u = u_scr[...]
    g = _dot(u, wg_ref[...])
    o_ref[...] = (g * _sigmoid(g) * _dot(u, wu_ref[...])).astype(o_ref.dtype)


def _ffn_up(x, nw, wg, wu, tm, tn):
    m, k = x.shape
    n = wg.shape[1]
    return pl.pallas_call(
        _ffn_up_kernel,
        grid=(m // tm, n // tn),
        in_specs=[pl.BlockSpec((tm, k), lambda i, j: (i, 0)),
                  pl.BlockSpec((1, k), lambda i, j: (0, 0)),
                  pl.BlockSpec((k, tn), lambda i, j: (0, j)),
                  pl.BlockSpec((k, tn), lambda i, j: (0, j))],
        out_specs=pl.BlockSpec((tm, tn), lambda i, j: (i, j)),
        out_shape=jax.ShapeDtypeStruct((m, n), BF16),
        scratch_shapes=[pltpu.VMEM((tm, k), BF16)],
        compiler_params=_cparams("parallel", "arbitrary"),
        name="ffn_up",
    )(x, nw, wg, wu)


def _mix_kernel(ya_ref, yb_ref, ga_ref, gb_ref, wa_ref, wb_ref, o_ref):
    a = _dot(ya_ref[...], wa_ref[...])
    b = _dot(yb_ref[...], wb_ref[...])
    o_ref[...] = (_sigmoid(ga_ref[...]) * a + _sigmoid(gb_ref[...]) * b).astype(o_ref.dtype)


def _mix(ya, yb, proj, wa, wb, tm, tn):
    m, k = ya.shape
    n = wa.shape[1]
    nb = n // tn
    return pl.pallas_call(
        _mix_kernel,
        grid=(m // tm, nb),
        in_specs=[pl.BlockSpec((tm, k), lambda i, j: (i, 0)),
                  pl.BlockSpec((tm, k), lambda i, j: (i, 0)),
                  pl.BlockSpec((tm, tn), lambda i, j: (i, j)),
                  pl.BlockSpec((tm, tn), lambda i, j: (i, nb + j)),
                  pl.BlockSpec((k, tn), lambda i, j: (0, j)),
                  pl.BlockSpec((k, tn), lambda i, j: (0, j))],
        out_specs=pl.BlockSpec((tm, tn), lambda i, j: (i, j)),
        out_shape=jax.ShapeDtypeStruct((m, n), BF16),
        compiler_params=_cparams("parallel", "parallel"),
        name="mix",
    )(ya, yb, proj, proj, wa, wb)


def _gdn_kernel(q_ref, k_ref, v_ref, z_ref, bd_ref, cwq_ref, cwk_ref, cwv_ref,
                alog_ref, dtb_ref, onw_ref, o_ref, s_scr, tail_scr, *, tb):
    h = pl.program_id(1)
    C = GDN_CHUNK

    @pl.when(pl.program_id(2) == 0)
    def _():
        s_scr[...] = jnp.zeros_like(s_scr)
        tail_scr[...] = jnp.zeros_like(tail_scr)

    row8 = lax.broadcasted_iota(jnp.int32, (8, LANES), 0)

    def conv_silu(u_ref, cw_ref, idx):
        u = u_ref[0]
        w = cw_ref[...]
        tail = tail_scr[idx]
        y = None
        ytop = None
        for s in (3, 2, 1):
            sh = pltpu.roll(u, s, axis=0)
            top = jnp.where(row8 < s, pltpu.roll(tail, s, axis=0), sh[0:8])
            wj = w[3 - s:4 - s]
            y = sh * wj if y is None else y + sh * wj
            ytop = top * wj if ytop is None else ytop + top * wj
        y = y + u * w[3:4]
        ytop = ytop + u[0:8] * w[3:4]
        tail_scr[idx] = u[tb - 8:tb]
        y = jnp.concatenate([ytop, y[8:]], axis=0)
        return y * _sigmoid(y)

    q = conv_silu(q_ref, cwq_ref, 0)
    k = conv_silu(k_ref, cwk_ref, 1)
    v = conv_silu(v_ref, cwv_ref, 2)
    q = q * lax.rsqrt(jnp.sum(q * q, axis=-1, keepdims=True) + RMS_EPS) * (HEAD_DIM ** -0.5)
    k = k * lax.rsqrt(jnp.sum(k * k, axis=-1, keepdims=True) + RMS_EPS)

    bd = bd_ref[0]
    lane = lax.broadcasted_iota(jnp.int32, (tb, LANES), 1)
    beta_all = _sigmoid(bd)
    xg = bd + dtb_ref[...]
    softplus = jnp.maximum(xg, 0.0) + jnp.log1p(jnp.exp(-jnp.abs(xg)))
    g_all = -jnp.exp(alog_ref[...]) * softplus
    beta = jnp.sum(jnp.where(lane == h, beta_all, 0.0), axis=-1, keepdims=True)
    g = jnp.sum(jnp.where(lane == h + GDN_HEADS, g_all, 0.0), axis=-1, keepdims=True)

    ri = lax.broadcasted_iota(jnp.int32, (tb, tb), 0)
    ci = lax.broadcasted_iota(jnp.int32, (tb, tb), 1)
    blocktril = jnp.where((ri >= ci) & (ri // C == ci // C), 1.0, 0.0).astype(F32)
    gc = _dot(blocktril, jnp.broadcast_to(g, (tb, LANES)), HIGHEST)

    ii = lax.broadcasted_iota(jnp.int32, (C, C), 0)
    jj = lax.broadcasted_iota(jnp.int32, (C, C), 1)
    incl = ii >= jj
    strict = ii > jj
    eye = jnp.where(ii == jj, 1.0, 0.0).astype(F32)

    s_state = s_scr[...]
    for c in range(tb // C):
        pair = c // 2
        gct = gc[pair * 2 * C:(pair + 1) * 2 * C, :].T
        r0 = c * C
        off = (c % 2) * C
        gcc = gc[r0:r0 + C, :]
        diff = gcc[:, 0:C] - gct[0:C, off:off + C]
        decay = jnp.where(incl, jnp.exp(jnp.where(incl, diff, 0.0)), 0.0)
        qc, kc, vc = q[r0:r0 + C], k[r0:r0 + C], v[r0:r0 + C]
        bc = beta[r0:r0 + C]
        kb = kc * bc
        kc16 = kc.astype(BF16)
        lmat = jnp.where(strict, _dot_nt(kb.astype(BF16), kc16) * decay, 0.0)
        pw = -lmat
        tinv = eye + pw
        for _ in range(5):
            pw = _dot(pw, pw, HIGHEST)
            tinv = tinv + _dot(tinv, pw, HIGHEST)
        eg = jnp.exp(gcc)
        w = _dot(tinv, kb * eg, HIGHEST)
        u = _dot(tinv, vc * bc, HIGHEST)
        amat = jnp.where(incl, _dot_nt(qc.astype(BF16), kc16) * decay, 0.0)
        gl = gcc[C - 1:C, :]
        q_dec = qc * eg
        k_dec = kc * jnp.exp(gl - gcc)
        s16 = s_state.astype(BF16)
        v_new = u - _dot(w.astype(BF16), s16)
        o = _dot(q_dec.astype(BF16), s16) + _dot(amat.astype(BF16), v_new.astype(BF16))
        s_state = s_state * jnp.exp(gl) + _dot(k_dec.T.astype(BF16), v_new.astype(BF16))
        zc = z_ref[0, r0:r0 + C, :]
        o = _rms_rows(o, onw_ref[...]) * (zc * _sigmoid(zc))
        o_ref[0, r0:r0 + C, :] = o.astype(o_ref.dtype)
    s_scr[...] = s_state


def _gdn(proj3, conv_w, alog_row, dtb_row, onw, col0, bd_col, tb):
    b, t, _ = proj3.shape
    hh = GDN_HEADS

    def col(base):
        return pl.BlockSpec((1, tb, LANES), lambda bi, hi, ti: (bi, ti, base + hi))

    def cw(base):
        return pl.BlockSpec((GDN_CONV, LANES), lambda bi, hi, ti: (0, base + hi))

    row = pl.BlockSpec((1, LANES), lambda bi, hi, ti: (0, 0))
    return pl.pallas_call(
        functools.partial(_gdn_kernel, tb=tb),
        grid=(b, hh, t // tb),
        in_specs=[col(col0), col(col0 + hh), col(col0 + 2 * hh), col(col0 + 3 * hh),
                  pl.BlockSpec((1, tb, LANES), lambda bi, hi, ti: (bi, ti, bd_col)),
                  cw(0), cw(hh), cw(2 * hh), row, row, row],
        out_specs=pl.BlockSpec((1, tb, LANES), lambda bi, hi, ti: (bi, ti, hi)),
        out_shape=jax.ShapeDtypeStruct((b, t, hh * HEAD_DIM), BF16),
        scratch_shapes=[pltpu.VMEM((HEAD_DIM, HEAD_DIM), F32),
                        pltpu.VMEM((3, 8, LANES), F32)],
        compiler_params=_cparams("parallel", "parallel", "arbitrary"),
        name="gdn",
    )(proj3, proj3, proj3, proj3, proj3, conv_w, conv_w, conv_w, alog_row, dtb_row, onw)


def _bias_kernel(rel_ref, o_ref):
    h = pl.program_id(0)
    d = pl.program_id(1)
    bs = MOBA_BLOCK
    max_exact = REL_BUCKETS // 2
    kk = lax.broadcasted_iota(jnp.int32, (bs, bs), 0)
    qq = lax.broadcasted_iota(jnp.int32, (bs, bs), 1)
    dist = jnp.maximum(d * bs + qq - kk, 0)
    df = dist.astype(F32)
    log_ratio = jnp.log(jnp.maximum(df, float(max_exact)) / max_exact) / math.log(REL_MAX_DIST / max_exact)
    large = max_exact + (log_ratio * (REL_BUCKETS - max_exact)).astype(jnp.int32)
    large = jnp.minimum(large, REL_BUCKETS - 1)
    bucket = jnp.where(dist < max_exact, dist, large)
    out = jnp.zeros((bs, bs), F32)
    for b in range(REL_BUCKETS):
        out = jnp.where(bucket == b, rel_ref[b, h], out)
    o_ref[0, 0] = out


def _bias_tiles(rel_bias):
    bs = MOBA_BLOCK
    return pl.pallas_call(
        _bias_kernel,
        grid=(MOBA_HEADS, BIAS_TILES),
        in_specs=[pl.BlockSpec(memory_space=pltpu.SMEM)],
        out_specs=pl.BlockSpec((1, 1, bs, bs), lambda h, d: (h, d, 0, 0)),
        out_shape=jax.ShapeDtypeStruct((MOBA_HEADS, BIAS_TILES, bs, bs), F32),
        compiler_params=_cparams("parallel", "parallel"),
        name="bias",
    )(rel_bias)


def _moba_prep_kernel(q_ref, k_ref, v_ref, qw_ref, kw_ref, qn_ref, kn_ref, km_ref, vt_ref):
    qn_ref[0, 0] = _rms_rows(q_ref[0], qw_ref[...])
    kn = _rms_rows(k_ref[0], kw_ref[...])
    kn_ref[0, 0] = kn.astype(BF16)
    km_ref[0, 0, 0] = jnp.mean(kn, axis=0, keepdims=True)
    vt_ref[0, 0] = v_ref[0].T.astype(BF16)


def _moba_prep(proj3, qw, kw, col0):
    b, t, _ = proj3.shape
    hh, bs, dh = MOBA_HEADS, MOBA_BLOCK, HEAD_DIM
    nb = t // bs

    def col(base):
        return pl.BlockSpec((1, bs, LANES), lambda bi, hi, ti: (bi, ti, base + hi))

    row = pl.BlockSpec((1, LANES), lambda bi, hi, ti: (0, 0))
    return pl.pallas_call(
        _moba_prep_kernel,
        grid=(b, hh, nb),
        in_specs=[col(col0), col(col0 + hh), col(col0 + 2 * hh), row, row],
        out_specs=[pl.BlockSpec((1, 1, bs, dh), lambda bi, hi, ti: (bi, hi, ti, 0)),
                   pl.BlockSpec((1, 1, bs, dh), lambda bi, hi, ti: (bi, hi, ti, 0)),
                   pl.BlockSpec((1, 1, 1, 1, dh), lambda bi, hi, ti: (bi, hi, ti, 0, 0)),
                   pl.BlockSpec((1, 1, dh, bs), lambda bi, hi, ti: (bi, hi, 0, ti))],
        out_shape=[jax.ShapeDtypeStruct((b, hh, t, dh), F32),
                   jax.ShapeDtypeStruct((b, hh, t, dh), BF16),
                   jax.ShapeDtypeStruct((b, hh, nb, 1, dh), F32),
                   jax.ShapeDtypeStruct((b, hh, dh, t), BF16)],
        compiler_params=_cparams("parallel", "parallel", "parallel"),
        name="moba_prep",
    )(proj3, proj3, proj3, qw, kw)


def _moba_kernel(qn_ref, kn_ref, vt_ref, km_ref, bias_ref, o_ref, sel_scr, *, nb):
    i = pl.program_id(2)
    bs = MOBA_BLOCK
    scale = HEAD_DIM ** -0.5
    q = qn_ref[0, 0]
    q16 = q.astype(BF16)

    route = _dot_nt(km_ref[0, 0], q, HIGHEST)
    blk = lax.broadcasted_iota(jnp.int32, (nb, bs), 0)
    valid = blk < i
    route = jnp.where(valid, route, NEG_INF)
    rank = jnp.zeros((nb, bs), jnp.int32)
    for j in range(nb):
        rj = route[j:j + 1, :]
        ahead = jnp.where(rj > route, 1, jnp.where((rj == route) & (blk > j), 1, 0))
        rank = rank + ahead
    sel_scr[...] = jnp.where(valid & (rank < MOBA_TOPK), 1.0, 0.0).astype(F32)

    kk = lax.broadcasted_iota(jnp.int32, (bs, bs), 0)
    qq = lax.broadcasted_iota(jnp.int32, (bs, bs), 1)

    row0 = pl.multiple_of(i * bs, bs)
    s = _dot_nt(kn_ref[0, 0, pl.ds(row0, bs), :], q16) * scale + bias_ref[0, 0]
    s = jnp.where(kk <= qq, s, NEG_INF)
    m = jnp.max(s, axis=0, keepdims=True)
    p = jnp.exp(s - m)
    l = jnp.sum(p, axis=0, keepdims=True)
    acc = _dot(vt_ref[0, 0, :, pl.ds(row0, bs)], p.astype(BF16))

    def body(j, carry):
        m, l, acc = carry
        r = pl.multiple_of(j * bs, bs)
        d = jnp.minimum(i - j, BIAS_TILES - 1)
        s = _dot_nt(kn_ref[0, 0, pl.ds(r, bs), :], q16) * scale + bias_ref[0, d]
        s = jnp.where(sel_scr[pl.ds(j, 1), :] > 0.0, s, NEG_INF)
        m_new = jnp.maximum(m, jnp.max(s, axis=0, keepdims=True))
        alpha = jnp.exp(m - m_new)
        p = jnp.exp(s - m_new)
        l = alpha * l + jnp.sum(p, axis=0, keepdims=True)
        acc = alpha * acc + _dot(vt_ref[0, 0, :, pl.ds(r, bs)], p.astype(BF16))
        return m_new, l, acc

    m, l, acc = lax.fori_loop(0, i, body, (m, l, acc))
    o_ref[0] = (acc / l).T.astype(o_ref.dtype)


def _moba(qn, kn, vt, km, bias):
    b, hh, t, dh = qn.shape
    bs = MOBA_BLOCK
    nb = t // bs
    return pl.pallas_call(
        functools.partial(_moba_kernel, nb=nb),
        grid=(b, hh, nb),
        in_specs=[pl.BlockSpec((1, 1, bs, dh), lambda bi, hi, ti: (bi, hi, ti, 0)),
                  pl.BlockSpec((1, 1, t, dh), lambda bi, hi, ti: (bi, hi, 0, 0)),
                  pl.BlockSpec((1, 1, dh, t), lambda bi, hi, ti: (bi, hi, 0, 0)),
                  pl.BlockSpec((1, 1, nb, dh), lambda bi, hi, ti: (bi, hi, 0, 0)),
                  pl.BlockSpec((1, BIAS_TILES, bs, bs), lambda bi, hi, ti: (hi, 0, 0, 0))],
        out_specs=pl.BlockSpec((1, bs, dh), lambda bi, hi, ti: (bi, ti, hi)),
        out_shape=jax.ShapeDtypeStruct((b, t, hh * dh), BF16),
        scratch_shapes=[pltpu.VMEM((nb, bs), F32)],
        compiler_params=_cparams("parallel", "parallel", "parallel"),
        name="moba",
    )(qn, kn, vt, km, bias)


def _tile(n, want):
    t = min(n, want)
    assert n % t == 0, (n, want)
    return t


def kernel(x, norm_mix_w, w_in, conv_w, a_log, dt_bias, gdn_o_norm_w, q_norm_w, k_norm_w, rel_bias,
           w_branch_gdn, w_branch_moba, w_out, norm_ffn_w, w_ffn_gate, w_ffn_up, w_ffn_down):
    b, t, d = x.shape
    m = b * t
    gw = GDN_HEADS * HEAD_DIM
    mw = MOBA_HEADS * HEAD_DIM
    assert t % MOBA_BLOCK == 0 and w_in.shape[0] == 1
    assert w_in.shape[2] == 4 * gw + 2 * GDN_HEADS + 3 * mw + 2 * d

    wi = w_in[0]
    o_bd = 4 * gw
    o_moba = o_bd + 2 * GDN_HEADS
    o_gate = o_moba + 3 * mw
    n_main = 2 * d + 4 * gw + 3 * mw
    tn_proj = 1280
    n_proj = -(-(n_main + LANES) // tn_proj) * tn_proj
    w_proj = jnp.concatenate(
        [wi[:, o_gate:], wi[:, :o_bd], wi[:, o_moba:o_gate], wi[:, o_bd:o_moba],
         jnp.zeros((d, n_proj - n_main - 2 * GDN_HEADS), wi.dtype)], axis=1).astype(BF16)
    col_gdn = (2 * d) // LANES
    col_moba = col_gdn + 4 * GDN_HEADS
    col_bd = col_moba + 3 * MOBA_HEADS

    x2 = x.reshape(m, d)
    proj = _norm_matmul(x2, norm_mix_w, w_proj, _tile(m, 512), tn_proj, F32, "proj")
    proj3 = proj.reshape(b, t, n_proj)

    lane_pad = jnp.zeros((1, LANES - 2 * GDN_HEADS), F32)
    head_pad = jnp.zeros((1, GDN_HEADS), F32)
    alog_row = jnp.concatenate([head_pad, a_log, lane_pad], axis=1)
    dtb_row = jnp.concatenate([head_pad, dt_bias, lane_pad], axis=1)
    y_a = _gdn(proj3, conv_w[0], alog_row, dtb_row, gdn_o_norm_w, col_gdn, col_bd, _tile(t, 256))

    bias = _bias_tiles(rel_bias)
    qn, kn, km, vt = _moba_prep(proj3, q_norm_w, k_norm_w, col_moba)
    y_b = _moba(qn, kn, vt, km.reshape(b, MOBA_HEADS, t // MOBA_BLOCK, HEAD_DIM), bias)

    mix = _mix(y_a.reshape(m, gw), y_b.reshape(m, mw), proj,
               w_branch_gdn[0].astype(BF16), w_branch_moba[0].astype(BF16), _tile(m, 512), 1024)
    h1 = _matmul_res(mix, w_out[0].astype(BF16), x2, _tile(m, 512), 1024, "out")

    hid = _ffn_up(h1, norm_ffn_w, w_ffn_gate[0].astype(BF16), w_ffn_up[0].astype(BF16),
                  _tile(m, 1024), 512)
    h2 = _matmul_res(hid, w_ffn_down[0].astype(BF16), h1, _tile(m, 512), 512, "ffn_down")
    return h2.reshape(b, t, d)
```

```python
import functools
import math

import jax
import jax.numpy as jnp
from jax import lax
from jax.experimental import pallas as pl
from jax.experimental.pallas import tpu as pltpu

F32 = jnp.float32
BF16 = jnp.bfloat16
HIGHEST = lax.Precision.HIGHEST

LANES = 128
HEAD_DIM = 128
GDN_HEADS = 8
GDN_CONV = 4
GDN_CHUNK = 64
MOBA_HEADS = 8
MOBA_BLOCK = 256
MOBA_TOPK = 3
REL_BUCKETS = 32
REL_MAX_DIST = 2048
BIAS_TILES = 8
MOBA_TILES_PER_STEP = 4
LOG2E = math.log2(math.e)
RMS_EPS = 1e-6
NEG_INF = -1e30
VMEM_LIMIT = 56 * 1024 * 1024


def _cparams(*sem):
    return pltpu.CompilerParams(dimension_semantics=sem, vmem_limit_bytes=VMEM_LIMIT)


def _sigmoid(x):
    return 1.0 / (1.0 + jnp.exp(-x))


def _dot(a, b, precision=None):
    return jnp.dot(a, b, preferred_element_type=F32, precision=precision)


def _dot_nt(a, b, precision=None):
    return lax.dot_general(a, b, (((1,), (1,)), ((), ())),
                           preferred_element_type=F32, precision=precision)


def _rms_rows(x, w):
    return x * lax.rsqrt(jnp.mean(x * x, axis=-1, keepdims=True) + RMS_EPS) * w


def _norm_matmul_kernel(x_ref, nw_ref, w_ref, o_ref, u_scr):
    @pl.when(pl.program_id(1) == 0)
    def _():
        u_scr[...] = _rms_rows(x_ref[...], nw_ref[...]).astype(BF16)

    o_ref[...] = _dot(u_scr[...], w_ref[...]).astype(o_ref.dtype)


def _norm_matmul(x, nw, w, tm, tn, out_dtype, name):
    m, k = x.shape
    n = w.shape[1]
    return pl.pallas_call(
        _norm_matmul_kernel,
        grid=(m // tm, n // tn),
        in_specs=[pl.BlockSpec((tm, k), lambda i, j: (i, 0)),
                  pl.BlockSpec((1, k), lambda i, j: (0, 0)),
                  pl.BlockSpec((k, tn), lambda i, j: (0, j))],
        out_specs=pl.BlockSpec((tm, tn), lambda i, j: (i, j)),
        out_shape=jax.ShapeDtypeStruct((m, n), out_dtype),
        scratch_shapes=[pltpu.VMEM((tm, k), BF16)],
        compiler_params=_cparams("parallel", "arbitrary"),
        name=name,
    )(x, nw, w)


def _matmul_res_kernel(a_ref, w_ref, r_ref, o_ref):
    o_ref[...] = r_ref[...] + _dot(a_ref[...], w_ref[...])


def _matmul_res(a, w, res, tm, tn, name):
    m, k = a.shape
    n = w.shape[1]
    return pl.pallas_call(
        _matmul_res_kernel,
        grid=(m // tm, n // tn),
        in_specs=[pl.BlockSpec((tm, k), lambda i, j: (i, 0)),
                  pl.BlockSpec((k, tn), lambda i, j: (0, j)),
                  pl.BlockSpec((tm, tn), lambda i, j: (i, j))],
        out_specs=pl.BlockSpec((tm, tn), lambda i, j: (i, j)),
        out_shape=jax.ShapeDtypeStruct((m, n), F32),
        compiler_params=_cparams("parallel", "parallel"),
        name=name,
    )(a, w, res)


def _ffn_up_kernel(x_ref, nw_ref, wg_ref, wu_ref, o_ref, u_scr):
    @pl.when(pl.program_id(1) == 0)
    def _():
        u_scr[...] = _rms_rows(x_ref[...], nw_ref[...]).astype(BF16)

    u = u_scr[...]
    g = _dot(u, wg_ref[...])
    o_ref[...] = (g * _sigmoid(g) * _dot(u, wu_ref[...])).astype(o_ref.dtype)


def _ffn_up(x, nw, wg, wu, tm, tn):
    m, k = x.shape
    n = wg.shape[1]
    return pl.pallas_call(
        _ffn_up_kernel,
        grid=(m // tm, n // tn),
        in_specs=[pl.BlockSpec((tm, k), lambda i, j: (i, 0)),
                  pl.BlockSpec((1, k), lambda i, j: (0, 0)),
                  pl.BlockSpec((k, tn), lambda i, j: (0, j)),
                  pl.BlockSpec((k, tn), lambda i, j: (0, j))],
        out_specs=pl.BlockSpec((tm, tn), lambda i, j: (i, j)),
        out_shape=jax.ShapeDtypeStruct((m, n), BF16),
        scratch_shapes=[pltpu.VMEM((tm, k), BF16)],
        compiler_params=_cparams("parallel", "arbitrary"),
        name="ffn_up",
    )(x, nw, wg, wu)


def _mix_kernel(ya_ref, yb_ref, ga_ref, gb_ref, wa_ref, wb_ref, o_ref):
    a = _dot(ya_ref[...], wa_ref[...])
    b = _dot(yb_ref[...], wb_ref[...])
    o_ref[...] = (_sigmoid(ga_ref[...]) * a + _sigmoid(gb_ref[...]) * b).astype(o_ref.dtype)


def _mix(ya, yb, proj, wa, wb, tm, tn):
    m, k = ya.shape
    n = wa.shape[1]
    nb = n // tn
    return pl.pallas_call(
        _mix_kernel,
        grid=(m // tm, nb),
        in_specs=[pl.BlockSpec((tm, k), lambda i, j: (i, 0)),
                  pl.BlockSpec((tm, k), lambda i, j: (i, 0)),
                  pl.BlockSpec((tm, tn), lambda i, j: (i, j)),
                  pl.BlockSpec((tm, tn), lambda i, j: (i, nb + j)),
                  pl.BlockSpec((k, tn), lambda i, j: (0, j)),
                  pl.BlockSpec((k, tn), lambda i, j: (0, j))],
        out_specs=pl.BlockSpec((tm, tn), lambda i, j: (i, j)),
        out_shape=jax.ShapeDtypeStruct((m, n), BF16),
        compiler_params=_cparams("parallel", "parallel"),
        name="mix",
    )(ya, yb, proj, proj, wa, wb)


def _gdn_kernel(q_ref, k_ref, v_ref, z_ref, bd_ref, cwq_ref, cwk_ref, cwv_ref,
                alog_ref, dtb_ref, onw_ref, o_ref, s_scr, tail_scr, *, tb):
    h = pl.program_id(1)
    C = GDN_CHUNK

    @pl.when(pl.program_id(2) == 0)
    def _():
        s_scr[...] = jnp.zeros_like(s_scr)
        tail_scr[...] = jnp.zeros_like(tail_scr)

    row8 = lax.broadcasted_iota(jnp.int32, (8, LANES), 0)

    def conv_silu(u_ref, cw_ref, idx):
        u = u_ref[0]
        w = cw_ref[...]
        tail = tail_scr[idx]
        y = None
        ytop = None
        for s in (3, 2, 1):
            sh = pltpu.roll(u, s, axis=0)
            top = jnp.where(row8 < s, pltpu.roll(tail, s, axis=0), sh[0:8])
            wj = w[3 - s:4 - s]
            y = sh * wj if y is None else y + sh * wj
            ytop = top * wj if ytop is None else ytop + top * wj
        y = y + u * w[3:4]
        ytop = ytop + u[0:8] * w[3:4]
        tail_scr[idx] = u[tb - 8:tb]
        y = jnp.concatenate([ytop, y[8:]], axis=0)
        return y * _sigmoid(y)

    q = conv_silu(q_ref, cwq_ref, 0)
    k = conv_silu(k_ref, cwk_ref, 1)
    v = conv_silu(v_ref, cwv_ref, 2)
    q = q * lax.rsqrt(jnp.sum(q * q, axis=-1, keepdims=True) + RMS_EPS) * (HEAD_DIM ** -0.5)
    k = k * lax.rsqrt(jnp.sum(k * k, axis=-1, keepdims=True) + RMS_EPS)

    bd = bd_ref[0]
    lane = lax.broadcasted_iota(jnp.int32, (tb, LANES), 1)
    beta_all = _sigmoid(bd)
    xg = bd + dtb_ref[...]
    softplus = jnp.maximum(xg, 0.0) + jnp.log1p(jnp.exp(-jnp.abs(xg)))
    g_all = -jnp.exp(alog_ref[...]) * softplus
    beta = jnp.sum(jnp.where(lane == h, beta_all, 0.0), axis=-1, keepdims=True)
    g = jnp.sum(jnp.where(lane == h + GDN_HEADS, g_all, 0.0), axis=-1, keepdims=True)

    ri = lax.broadcasted_iota(jnp.int32, (tb, tb), 0)
    ci = lax.broadcasted_iota(jnp.int32, (tb, tb), 1)
    blocktril = jnp.where((ri >= ci) & (ri // C == ci // C), 1.0, 0.0).astype(F32)
    gc = _dot(blocktril, jnp.broadcast_to(g, (tb, LANES)), HIGHEST)

    ii = lax.broadcasted_iota(jnp.int32, (C, C), 0)
    jj = lax.broadcasted_iota(jnp.int32, (C, C), 1)
    incl = ii >= jj
    strict = ii > jj
    eye = jnp.where(ii == jj, 1.0, 0.0).astype(F32)

    s_state = s_scr[...]
    for c in range(tb // C):
        pair = c // 2
        gct = gc[pair * 2 * C:(pair + 1) * 2 * C, :].T
        r0 = c * C
        off = (c % 2) * C
        gcc = gc[r0:r0 + C, :]
        diff = gcc[:, 0:C] - gct[0:C, off:off + C]
        decay = jnp.where(incl, jnp.exp(jnp.where(incl, diff, 0.0)), 0.0)
        qc, kc, vc = q[r0:r0 + C], k[r0:r0 + C], v[r0:r0 + C]
        bc = beta[r0:r0 + C]
        kb = kc * bc
        kc16 = kc.astype(BF16)
        lmat = jnp.where(strict, _dot_nt(kb.astype(BF16), kc16) * decay, 0.0)
        pw = -lmat
        tinv = eye + pw
        for _ in range(5):
            pw = _dot(pw, pw, HIGHEST)
            tinv = tinv + _dot(tinv, pw, HIGHEST)
        eg = jnp.exp(gcc)
        w = _dot(tinv, kb * eg, HIGHEST)
        u = _dot(tinv, vc * bc, HIGHEST)
        amat = jnp.where(incl, _dot_nt(qc.astype(BF16), kc16) * decay, 0.0)
        gl = gcc[C - 1:C, :]
        q_dec = qc * eg
        k_dec = kc * jnp.exp(gl - gcc)
        s16 = s_state.astype(BF16)
        v_new = u - _dot(w.astype(BF16), s16)
        o = _dot(q_dec.astype(BF16), s16) + _dot(amat.astype(BF16), v_new.astype(BF16))
        s_state = s_state * jnp.exp(gl) + _dot(k_dec.T.astype(BF16), v_new.astype(BF16))
        zc = z_ref[0, r0:r0 + C, :]
        o = _rms_rows(o, onw_ref[...]) * (zc * _sigmoid(zc))
        o_ref[0, r0:r0 + C, :] = o.astype(o_ref.dtype)
    s_scr[...] = s_state


def _gdn(proj3, conv_w, alog_row, dtb_row, onw, col0, bd_col, tb):
    b, t, _ = proj3.shape
    hh = GDN_HEADS

    def col(base):
        return pl.BlockSpec((1, tb, LANES), lambda bi, hi, ti: (bi, ti, base + hi))

    def cw(base):
        return pl.BlockSpec((GDN_CONV, LANES), lambda bi, hi, ti: (0, base + hi))

    row = pl.BlockSpec((1, LANES), lambda bi, hi, ti: (0, 0))
    return pl.pallas_call(
        functools.partial(_gdn_kernel, tb=tb),
        grid=(b, hh, t // tb),
        in_specs=[col(col0), col(col0 + hh), col(col0 + 2 * hh), col(col0 + 3 * hh),
                  pl.BlockSpec((1, tb, LANES), lambda bi, hi, ti: (bi, ti, bd_col)),
                  cw(0), cw(hh), cw(2 * hh), row, row, row],
        out_specs=pl.BlockSpec((1, tb, LANES), lambda bi, hi, ti: (bi, ti, hi)),
        out_shape=jax.ShapeDtypeStruct((b, t, hh * HEAD_DIM), BF16),
        scratch_shapes=[pltpu.VMEM((HEAD_DIM, HEAD_DIM), F32),
                        pltpu.VMEM((3, 8, LANES), F32)],
        compiler_params=_cparams("parallel", "parallel", "arbitrary"),
        name="gdn",
    )(proj3, proj3, proj3, proj3, proj3, conv_w, conv_w, conv_w, alog_row, dtb_row, onw)


def _bias_kernel(rel_ref, o_ref):
    h = pl.program_id(0)
    d = pl.program_id(1)
    bs = MOBA_BLOCK
    max_exact = REL_BUCKETS // 2
    kk = lax.broadcasted_iota(jnp.int32, (bs, bs), 0)
    qq = lax.broadcasted_iota(jnp.int32, (bs, bs), 1)
    dist = jnp.maximum(d * bs + qq - kk, 0)
    df = dist.astype(F32)
    log_ratio = jnp.log(jnp.maximum(df, float(max_exact)) / max_exact) / math.log(REL_MAX_DIST / max_exact)
    large = max_exact + (log_ratio * (REL_BUCKETS - max_exact)).astype(jnp.int32)
    large = jnp.minimum(large, REL_BUCKETS - 1)
    bucket = jnp.where(dist < max_exact, dist, large)
    out = jnp.zeros((bs, bs), F32)
    for b in range(REL_BUCKETS):
        out = jnp.where(bucket == b, rel_ref[b, h], out)
    o_ref[0, 0] = out * LOG2E


def _bias_tiles(rel_bias):
    bs = MOBA_BLOCK
    return pl.pallas_call(
        _bias_kernel,
        grid=(MOBA_HEADS, BIAS_TILES),
        in_specs=[pl.BlockSpec(memory_space=pltpu.SMEM)],
        out_specs=pl.BlockSpec((1, 1, bs, bs), lambda h, d: (h, d, 0, 0)),
        out_shape=jax.ShapeDtypeStruct((MOBA_HEADS, BIAS_TILES, bs, bs), F32),
        compiler_params=_cparams("parallel", "parallel"),
        name="bias",
    )(rel_bias)


def _moba_prep_kernel(q_ref, k_ref, v_ref, qw_ref, kw_ref, qn_ref, kn_ref, km_ref, vt_ref):
    qn_ref[0, 0] = _rms_rows(q_ref[0], qw_ref[...])
    kn = _rms_rows(k_ref[0], kw_ref[...])
    kn_ref[0, 0] = kn.astype(BF16)
    km_ref[0, 0, 0] = jnp.mean(kn, axis=0, keepdims=True)
    vt_ref[0, 0] = v_ref[0].T.astype(BF16)


def _moba_prep(proj3, qw, kw, col0):
    b, t, _ = proj3.shape
    hh, bs, dh = MOBA_HEADS, MOBA_BLOCK, HEAD_DIM
    nb = t // bs

    def col(base):
        return pl.BlockSpec((1, bs, LANES), lambda bi, hi, ti: (bi, ti, base + hi))

    row = pl.BlockSpec((1, LANES), lambda bi, hi, ti: (0, 0))
    return pl.pallas_call(
        _moba_prep_kernel,
        grid=(b, hh, nb),
        in_specs=[col(col0), col(col0 + hh), col(col0 + 2 * hh), row, row],
        out_specs=[pl.BlockSpec((1, 1, bs, dh), lambda bi, hi, ti: (bi, hi, ti, 0)),
                   pl.BlockSpec((1, 1, bs, dh), lambda bi, hi, ti: (bi, hi, ti, 0)),
                   pl.BlockSpec((1, 1, 1, 1, dh), lambda bi, hi, ti: (bi, hi, ti, 0, 0)),
                   pl.BlockSpec((1, 1, dh, bs), lambda bi, hi, ti: (bi, hi, 0, ti))],
        out_shape=[jax.ShapeDtypeStruct((b, hh, t, dh), F32),
                   jax.ShapeDtypeStruct((b, hh, t, dh), BF16),
                   jax.ShapeDtypeStruct((b, hh, nb, 1, dh), F32),
                   jax.ShapeDtypeStruct((b, hh, dh, t), BF16)],
        compiler_params=_cparams("parallel", "parallel", "parallel"),
        name="moba_prep",
    )(proj3, proj3, proj3, qw, kw)


def _moba_kernel(qn_ref, kn_ref, vt_ref, km_ref, bias_ref, o_ref, sel_scr, own_scr, s_scr, *, nb):
    i = pl.program_id(2)
    bs = MOBA_BLOCK
    ch = MOBA_TILES_PER_STEP
    c2 = (HEAD_DIM ** -0.5) * LOG2E
    q = qn_ref[0, 0]
    q16 = q.astype(BF16)

    route = _dot_nt(km_ref[0, 0], q, HIGHEST)
    blk = lax.broadcasted_iota(jnp.int32, (nb, bs), 0)
    valid = blk < i
    route = jnp.where(valid, route, NEG_INF)
    rank = jnp.zeros((nb, bs), jnp.int32)
    for j in range(nb):
        rj = route[j:j + 1, :]
        ahead = jnp.where(rj > route, 1, jnp.where((rj == route) & (blk > j), 1, 0))
        rank = rank + ahead
    sel_scr[...] = jnp.where(valid & (rank < MOBA_TOPK), 1.0, 0.0).astype(F32)

    kk = lax.broadcasted_iota(jnp.int32, (bs, bs), 0)
    qq = lax.broadcasted_iota(jnp.int32, (bs, bs), 1)

    row0 = pl.multiple_of(i * bs, bs)
    s = _dot_nt(kn_ref[0, 0, pl.ds(row0, bs), :], q16) * c2 + bias_ref[0, 0]
    s = jnp.where(kk <= qq, s, NEG_INF)
    own_scr[...] = s
    m = jnp.max(s, axis=0, keepdims=True)

    n_steps = (i + ch - 1) // ch

    def scores(c, m):
        r = pl.multiple_of(c * (ch * bs), ch * bs)
        s_all = _dot_nt(kn_ref[0, 0, pl.ds(r, ch * bs), :], q16)
        for t in range(ch):
            j = c * ch + t
            d = jnp.clip(i - j, 0, BIAS_TILES - 1)
            st = s_all[t * bs:(t + 1) * bs] * c2 + bias_ref[0, d]
            st = jnp.where(sel_scr[pl.ds(j, 1), :] > 0.0, st, NEG_INF)
            s_scr[pl.ds(r + t * bs, bs), :] = st
            m = jnp.maximum(m, jnp.max(st, axis=0, keepdims=True))
        return m

    m = lax.fori_loop(0, n_steps, scores, m)

    p = jnp.exp2(own_scr[...] - m)
    l = jnp.sum(p, axis=0, keepdims=True)
    acc = _dot(vt_ref[0, 0, :, pl.ds(row0, bs)], p.astype(BF16))

    def accumulate(c, carry):
        l, acc = carry
        r = pl.multiple_of(c * (ch * bs), ch * bs)
        p = jnp.exp2(s_scr[pl.ds(r, ch * bs), :] - m)
        l = l + jnp.sum(p, axis=0, keepdims=True)
        acc = acc + _dot(vt_ref[0, 0, :, pl.ds(r, ch * bs)], p.astype(BF16))
        return l, acc

    l, acc = lax.fori_loop(0, n_steps, accumulate, (l, acc))
    o_ref[0] = (acc / l).T.astype(o_ref.dtype)


def _moba(qn, kn, vt, km, bias):
    b, hh, t, dh = qn.shape
    bs = MOBA_BLOCK
    nb = t // bs
    return pl.pallas_call(
        functools.partial(_moba_kernel, nb=nb),
        grid=(b, hh, nb),
        in_specs=[pl.BlockSpec((1, 1, bs, dh), lambda bi, hi, ti: (bi, hi, ti, 0)),
                  pl.BlockSpec((1, 1, t, dh), lambda bi, hi, ti: (bi, hi, 0, 0)),
                  pl.BlockSpec((1, 1, dh, t), lambda bi, hi, ti: (bi, hi, 0, 0)),
                  pl.BlockSpec((1, 1, nb, dh), lambda bi, hi, ti: (bi, hi, 0, 0)),
                  pl.BlockSpec((1, BIAS_TILES, bs, bs), lambda bi, hi, ti: (hi, 0, 0, 0))],
        out_specs=pl.BlockSpec((1, bs, dh), lambda bi, hi, ti: (bi, ti, hi)),
        out_shape=jax.ShapeDtypeStruct((b, t, hh * dh), BF16),
        scratch_shapes=[pltpu.VMEM((nb, bs), F32),
                        pltpu.VMEM((bs, bs), F32),
                        pltpu.VMEM((t, bs), F32)],
        compiler_params=_cparams("parallel", "parallel", "parallel"),
        name="moba",
    )(qn, kn, vt, km, bias)


def _tile(n, want):
    t = min(n, want)
    assert n % t == 0, (n, want)
    return t


def kernel(x, norm_mix_w, w_in, conv_w, a_log, dt_bias, gdn_o_norm_w, q_norm_w, k_norm_w, rel_bias,
           w_branch_gdn, w_branch_moba, w_out, norm_ffn_w, w_ffn_gate, w_ffn_up, w_ffn_down):
    b, t, d = x.shape
    m = b * t
    gw = GDN_HEADS * HEAD_DIM
    mw = MOBA_HEADS * HEAD_DIM
    assert t % (MOBA_BLOCK * MOBA_TILES_PER_STEP) == 0 and w_in.shape[0] == 1
    assert w_in.shape[2] == 4 * gw + 2 * GDN_HEADS + 3 * mw + 2 * d

    wi = w_in[0]
    o_bd = 4 * gw
    o_moba = o_bd + 2 * GDN_HEADS
    o_gate = o_moba + 3 * mw
    n_main = 2 * d + 4 * gw + 3 * mw
    tn_proj = 1280
    n_proj = -(-(n_main + LANES) // tn_proj) * tn_proj
    w_proj = jnp.concatenate(
        [wi[:, o_gate:], wi[:, :o_bd], wi[:, o_moba:o_gate], wi[:, o_bd:o_moba],
         jnp.zeros((d, n_proj - n_main - 2 * GDN_HEADS), wi.dtype)], axis=1).astype(BF16)
    col_gdn = (2 * d) // LANES
    col_moba = col_gdn + 4 * GDN_HEADS
    col_bd = col_moba + 3 * MOBA_HEADS

    x2 = x.reshape(m, d)
    proj = _norm_matmul(x2, norm_mix_w, w_proj, _tile(m, 512), tn_proj, F32, "proj")
    proj3 = proj.reshape(b, t, n_proj)

    lane_pad = jnp.zeros((1, LANES - 2 * GDN_HEADS), F32)
    head_pad = jnp.zeros((1, GDN_HEADS), F32)
    alog_row = jnp.concatenate([head_pad, a_log, lane_pad], axis=1)
    dtb_row = jnp.concatenate([head_pad, dt_bias, lane_pad], axis=1)
    y_a = _gdn(proj3, conv_w[0], alog_row, dtb_row, gdn_o_norm_w, col_gdn, col_bd, _tile(t, 256))

    bias = _bias_tiles(rel_bias)
    qn, kn, km, vt = _moba_prep(proj3, q_norm_w, k_norm_w, col_moba)
    y_b = _moba(qn, kn, vt, km.reshape(b, MOBA_HEADS, t // MOBA_BLOCK, HEAD_DIM), bias)

    mix = _mix(y_a.reshape(m, gw), y_b.reshape(m, mw), proj,
               w_branch_gdn[0].astype(BF16), w_branch_moba[0].astype(BF16), _tile(m, 512), 1024)
    h1 = _matmul_res(mix, w_out[0].astype(BF16), x2, _tile(m, 512), 1024, "out")

    hid = _ffn_up(h1, norm_ffn_w, w_ffn_gate[0].astype(BF16), w_ffn_up[0].astype(BF16),
                  _tile(m, 1024), 512)
    h2 = _matmul_res(hid, w_ffn_down[0].astype(BF16), h1, _tile(m, 512), 512, "ffn_down")
    return h2.reshape(b, t, d)
```

```python
import functools
import math

import jax
import jax.numpy as jnp
from jax import lax
from jax.experimental import pallas as pl
from jax.experimental.pallas import tpu as pltpu

F32 = jnp.float32
BF16 = jnp.bfloat16
HIGHEST = lax.Precision.HIGHEST

LANES = 128
HEAD_DIM = 128
GDN_HEADS = 8
GDN_CONV = 4
GDN_CHUNK = 64
GDN_HEADS_PER_STEP = 4
MOBA_HEADS = 8
MOBA_BLOCK = 256
MOBA_TOPK = 3
REL_BUCKETS = 32
REL_MAX_DIST = 2048
BIAS_TILES = 8
MOBA_TILES_PER_STEP = 4
LOG2E = math.log2(math.e)
RMS_EPS = 1e-6
NEG_INF = -1e30
VMEM_LIMIT = 56 * 1024 * 1024


def _cparams(*sem):
    return pltpu.CompilerParams(dimension_semantics=sem, vmem_limit_bytes=VMEM_LIMIT)


def _sigmoid(x):
    return 1.0 / (1.0 + jnp.exp(-x))


def _dot(a, b, precision=None):
    return jnp.dot(a, b, preferred_element_type=F32, precision=precision)


def _dot_nt(a, b, precision=None):
    return lax.dot_general(a, b, (((1,), (1,)), ((), ())),
                           preferred_element_type=F32, precision=precision)


def _dot_tn(a, b):
    return lax.dot_general(a, b, (((0,), (0,)), ((), ())), preferred_element_type=F32)


def _rms_rows(x, w):
    return x * lax.rsqrt(jnp.mean(x * x, axis=-1, keepdims=True) + RMS_EPS) * w


def _norm_matmul_kernel(x_ref, nw_ref, w_ref, o_ref, u_scr):
    @pl.when(pl.program_id(1) == 0)
    def _():
        u_scr[...] = _rms_rows(x_ref[...], nw_ref[...]).astype(BF16)

    o_ref[...] = _dot(u_scr[...], w_ref[...]).astype(o_ref.dtype)


def _norm_matmul(x, nw, w, tm, tn, out_dtype, name):
    m, k = x.shape
    n = w.shape[1]
    return pl.pallas_call(
        _norm_matmul_kernel,
        grid=(m // tm, n // tn),
        in_specs=[pl.BlockSpec((tm, k), lambda i, j: (i, 0)),
                  pl.BlockSpec((1, k), lambda i, j: (0, 0)),
                  pl.BlockSpec((k, tn), lambda i, j: (0, j))],
        out_specs=pl.BlockSpec((tm, tn), lambda i, j: (i, j)),
        out_shape=jax.ShapeDtypeStruct((m, n), out_dtype),
        scratch_shapes=[pltpu.VMEM((tm, k), BF16)],
        compiler_params=_cparams("parallel", "arbitrary"),
        name=name,
    )(x, nw, w)


def _matmul_res_kernel(a_ref, w_ref, r_ref, o_ref):
    o_ref[...] = r_ref[...] + _dot(a_ref[...], w_ref[...])


def _matmul_res(a, w, res, tm, tn, name):
    m, k = a.shape
    n = w.shape[1]
    return pl.pallas_call(
        _matmul_res_kernel,
        grid=(m // tm, n // tn),
        in_specs=[pl.BlockSpec((tm, k), lambda i, j: (i, 0)),
                  pl.BlockSpec((k, tn), lambda i, j: (0, j)),
                  pl.BlockSpec((tm, tn), lambda i, j: (i, j))],
        out_specs=pl.BlockSpec((tm, tn), lambda i, j: (i, j)),
        out_shape=jax.ShapeDtypeStruct((m, n), F32),
        compiler_params=_cparams("parallel", "parallel"),
        name=name,
    )(a, w, res)


def _ffn_up_kernel(x_ref, nw_ref, wg_ref, wu_ref, o_ref, u_scr):
    @pl.when(pl.program_id(1) == 0)
    def _():
        u_scr[...] = _rms_rows(x_ref[...], nw_ref[...]).astype(BF16)

    u = u_scr[...]
    g = _dot(u, wg_ref[...])
    o_ref[...] = (g * _sigmoid(g) * _dot(u, wu_ref[...])).astype(o_ref.dtype)


def _ffn_up(x, nw, wg, wu, tm, tn):
    m, k = x.shape
    n = wg.shape[1]
    return pl.pallas_call(
        _ffn_up_kernel,
        grid=(m // tm, n // tn),
        in_specs=[pl.BlockSpec((tm, k), lambda i, j: (i, 0)),
                  pl.BlockSpec((1, k), lambda i, j: (0, 0)),
                  pl.BlockSpec((k, tn), lambda i, j: (0, j)),
                  pl.BlockSpec((k, tn), lambda i, j: (0, j))],
        out_specs=pl.BlockSpec((tm, tn), lambda i, j: (i, j)),
        out_shape=jax.ShapeDtypeStruct((m, n), BF16),
        scratch_shapes=[pltpu.VMEM((tm, k), BF16)],
        compiler_params=_cparams("parallel", "arbitrary"),
        name="ffn_up",
    )(x, nw, wg, wu)


def _mix_kernel(ya_ref, yb_ref, ga_ref, gb_ref, wa_ref, wb_ref, o_ref):
    a = _dot(ya_ref[...], wa_ref[...])
    b = _dot(yb_ref[...], wb_ref[...])
    o_ref[...] = (_sigmoid(ga_ref[...]) * a + _sigmoid(gb_ref[...]) * b).astype(o_ref.dtype)


def _mix(ya, yb, proj, wa, wb, tm, tn):
    m, k = ya.shape
    n = wa.shape[1]
    nb = n // tn
    return pl.pallas_call(
        _mix_kernel,
        grid=(m // tm, nb),
        in_specs=[pl.BlockSpec((tm, k), lambda i, j: (i, 0)),
                  pl.BlockSpec((tm, k), lambda i, j: (i, 0)),
                  pl.BlockSpec((tm, tn), lambda i, j: (i, j)),
                  pl.BlockSpec((tm, tn), lambda i, j: (i, nb + j)),
                  pl.BlockSpec((k, tn), lambda i, j: (0, j)),
                  pl.BlockSpec((k, tn), lambda i, j: (0, j))],
        out_specs=pl.BlockSpec((tm, tn), lambda i, j: (i, j)),
        out_shape=jax.ShapeDtypeStruct((m, n), BF16),
        compiler_params=_cparams("parallel", "parallel"),
        name="mix",
    )(ya, yb, proj, proj, wa, wb)


def _split3(x):
    a = x.astype(BF16)
    r = x - a.astype(F32)
    b = r.astype(BF16)
    c = (r - b.astype(F32)).astype(BF16)
    return a, b, c


def _gdn_kernel(q_ref, k_ref, v_ref, z_ref, bd_ref, cwq_ref, cwk_ref, cwv_ref,
                alog_ref, dtb_ref, onw_ref, o_ref, s_scr, tail_scr, *, tb):
    hp = GDN_HEADS_PER_STEP
    C = GDN_CHUNK
    D = HEAD_DIM
    head0 = pl.program_id(1) * hp

    @pl.when(pl.program_id(2) == 0)
    def _():
        s_scr[...] = jnp.zeros_like(s_scr)
        tail_scr[...] = jnp.zeros_like(tail_scr)

    row8 = lax.broadcasted_iota(jnp.int32, (8, hp * D), 0)
    tails = []

    def conv_silu(u_ref, cw_ref, idx):
        u = u_ref[0]
        w = cw_ref[...]
        tail = tail_scr[idx]
        y = None
        ytop = None
        for s in (3, 2, 1):
            sh = pltpu.roll(u, s, axis=0)
            top = jnp.where(row8 < s, pltpu.roll(tail, s, axis=0), sh[0:8])
            wj = w[3 - s:4 - s]
            y = sh * wj if y is None else y + sh * wj
            ytop = top * wj if ytop is None else ytop + top * wj
        y = y + u * w[3:4]
        ytop = ytop + u[0:8] * w[3:4]
        tails.append(u[tb - 8:tb])
        y = jnp.concatenate([ytop, y[8:]], axis=0)
        return y * _sigmoid(y)

    q_all = conv_silu(q_ref, cwq_ref, 0)
    k_all = conv_silu(k_ref, cwk_ref, 1)
    v_all = conv_silu(v_ref, cwv_ref, 2)

    bd = bd_ref[0]
    lane = lax.broadcasted_iota(jnp.int32, (tb, LANES), 1)
    beta_all = _sigmoid(bd)
    xg = bd + dtb_ref[...]
    softplus = jnp.maximum(xg, 0.0) + jnp.log1p(jnp.exp(-jnp.abs(xg)))
    g_all = -jnp.exp(alog_ref[...]) * softplus
    betas = [jnp.sum(jnp.where(lane == head0 + e, beta_all, 0.0), axis=-1, keepdims=True)
             for e in range(hp)]
    gs = [jnp.sum(jnp.where(lane == head0 + e + GDN_HEADS, g_all, 0.0), axis=-1, keepdims=True)
          for e in range(hp)]

    nc = tb // C
    ri = lax.broadcasted_iota(jnp.int32, (tb, tb), 0)
    ci = lax.broadcasted_iota(jnp.int32, (tb, tb), 1)
    same = (ri // C) == (ci // C)
    tril16 = jnp.where(same & (ri >= ci), 1.0, 0.0).astype(BF16)
    g_b = jnp.concatenate([jnp.broadcast_to(g, (tb, LANES)) for g in gs], axis=1)
    gc_all = sum(_dot(tril16, piece) for piece in _split3(g_b))

    iw = lax.broadcasted_iota(jnp.int32, (C, tb), 0)
    jw = lax.broadcasted_iota(jnp.int32, (C, tb), 1) % C
    incl_w = iw >= jw
    strict_w = iw > jw
    eye_w = jnp.where(iw == jw, 1.0, 0.0).astype(F32)
    low_half = (lax.broadcasted_iota(jnp.int32, (C, LANES), 1) < C)

    def block_diag(wide16):
        return jnp.where(same, jnp.concatenate([wide16] * nc, axis=0), jnp.zeros((), BF16))

    def pair_blocks(full):
        return jnp.where(low_half, full[:C], full[C:])

    E = range(hp)
    pairs = range(tb // LANES)

    def l2n(x):
        return x * lax.rsqrt(jnp.sum(x * x, axis=-1, keepdims=True) + RMS_EPS)

    qs = [l2n(q_all[:, e * D:(e + 1) * D]) * (D ** -0.5) for e in E]
    ks = [l2n(k_all[:, e * D:(e + 1) * D]) for e in E]
    vs = [v_all[:, e * D:(e + 1) * D] for e in E]
    gcs = [gc_all[:, e * D:(e + 1) * D] for e in E]
    g_col = [jnp.concatenate([pair_blocks(gc[p * LANES:(p + 1) * LANES]) for p in pairs], axis=1)
             for gc in gcs]
    g_row = [jnp.concatenate([gc[p * LANES:(p + 1) * LANES, :].T[0:1, :] for p in pairs], axis=1)
             for gc in gcs]
    decay = [jnp.where(incl_w, jnp.exp(jnp.where(incl_w, g_col[e] - g_row[e], 0.0)), 0.0) for e in E]
    k16 = [k.astype(BF16) for k in ks]
    q16 = [q.astype(BF16) for q in qs]
    kb = [ks[e] * betas[e] for e in E]
    kb16 = [x.astype(BF16) for x in kb]

    def pair_products(a16, b16):
        return jnp.concatenate([pair_blocks(_dot_nt(a16[p * LANES:(p + 1) * LANES],
                                                    b16[p * LANES:(p + 1) * LANES])) for p in pairs], axis=1)

    lmat = [jnp.where(strict_w, pair_products(kb16[e], k16[e]) * decay[e], 0.0) for e in E]
    amat = [jnp.where(incl_w, pair_products(q16[e], k16[e]) * decay[e], 0.0) for e in E]
    p16 = [(-lmat[e]).astype(BF16) for e in E]
    tinv = [eye_w - lmat[e] for e in E]
    pw = [_dot(p16[e], block_diag(p16[e])) for e in E]
    for _ in range(4):
        p16 = [pw[e].astype(BF16) for e in E]
        prod = [_dot(jnp.concatenate([p16[e], tinv[e].astype(BF16)], axis=0), block_diag(p16[e])) for e in E]
        pw = [prod[e][:C] for e in E]
        tinv = [tinv[e] + prod[e][C:] for e in E]
    tinv = [tinv[e] + _dot(tinv[e].astype(BF16), block_diag(pw[e].astype(BF16))) for e in E]
    eg = [jnp.exp(gc) for gc in gcs]
    rhs = [jnp.concatenate([kb[e] * eg[e], vs[e] * betas[e]], axis=1).astype(BF16) for e in E]
    wu = [_dot(block_diag(tinv[e].astype(BF16)), rhs[e]).astype(BF16) for e in E]
    au = [_dot(block_diag(amat[e].astype(BF16)), wu[e]) for e in E]
    q_eff = [(qs[e] * eg[e] - au[e][:, :D]).astype(BF16) for e in E]
    gl = [jnp.concatenate([jnp.broadcast_to(gc[c * C + C - 1:c * C + C, :], (C, LANES))
                           for c in range(nc)], axis=0) for gc in gcs]
    k_dec = [(ks[e] * jnp.exp(gl[e] - gcs[e])).astype(BF16) for e in E]
    pn = [[_dot_tn(k_dec[e][c * C:(c + 1) * C], wu[e][c * C:(c + 1) * C]) for c in range(nc)]
          for e in E]

    states = [s_scr[e] for e in E]
    onw = onw_ref[...]
    for c in range(nc):
        r0 = c * C
        lhs = [jnp.concatenate([pn[e][c][:, :D].astype(BF16), q_eff[e][r0:r0 + C]], axis=0) for e in E]
        res = [_dot(lhs[e], states[e].astype(BF16)) for e in E]
        outs = [_rms_rows(res[e][D:] + au[e][r0:r0 + C, D:], onw) for e in E]
        states = [states[e] * jnp.exp(gl[e][r0:r0 + 1, :]) - res[e][:D] + pn[e][c][:, D:] for e in E]
        zc = z_ref[0, r0:r0 + C, :]
        o_ref[0, r0:r0 + C, :] = (jnp.concatenate(outs, axis=1) * (zc * _sigmoid(zc))).astype(o_ref.dtype)
    for e in range(hp):
        s_scr[e] = states[e]
    for idx in range(3):
        tail_scr[idx] = tails[idx]


def _gdn(proj3, conv_w, alog_row, dtb_row, onw, col0, bd_col, tb):
    b, t, _ = proj3.shape
    hp = GDN_HEADS_PER_STEP
    ng = GDN_HEADS // hp
    w = hp * HEAD_DIM

    def col(base):
        return pl.BlockSpec((1, tb, w), lambda bi, hi, ti: (bi, ti, base // hp + hi))

    def cw(base):
        return pl.BlockSpec((GDN_CONV, w), lambda bi, hi, ti: (0, base // hp + hi))

    assert col0 % hp == 0
    row = pl.BlockSpec((1, LANES), lambda bi, hi, ti: (0, 0))
    return pl.pallas_call(
        functools.partial(_gdn_kernel, tb=tb),
        grid=(b, ng, t // tb),
        in_specs=[col(col0), col(col0 + GDN_HEADS), col(col0 + 2 * GDN_HEADS), col(col0 + 3 * GDN_HEADS),
                  pl.BlockSpec((1, tb, LANES), lambda bi, hi, ti: (bi, ti, bd_col)),
                  cw(0), cw(GDN_HEADS), cw(2 * GDN_HEADS), row, row, row],
        out_specs=pl.BlockSpec((1, tb, w), lambda bi, hi, ti: (bi, ti, hi)),
        out_shape=jax.ShapeDtypeStruct((b, t, GDN_HEADS * HEAD_DIM), BF16),
        scratch_shapes=[pltpu.VMEM((hp, HEAD_DIM, HEAD_DIM), F32),
                        pltpu.VMEM((3, 8, w), F32)],
        compiler_params=_cparams("parallel", "parallel", "arbitrary"),
        name="gdn",
    )(proj3, proj3, proj3, proj3, proj3, conv_w, conv_w, conv_w, alog_row, dtb_row, onw)


def _bias_kernel(rel_ref, o_ref):
    h = pl.program_id(0)
    d = pl.program_id(1)
    bs = MOBA_BLOCK
    max_exact = REL_BUCKETS // 2
    kk = lax.broadcasted_iota(jnp.int32, (bs, bs), 0)
    qq = lax.broadcasted_iota(jnp.int32, (bs, bs), 1)
    dist = jnp.maximum(d * bs + qq - kk, 0)
    df = dist.astype(F32)
    log_ratio = jnp.log(jnp.maximum(df, float(max_exact)) / max_exact) / math.log(REL_MAX_DIST / max_exact)
    large = max_exact + (log_ratio * (REL_BUCKETS - max_exact)).astype(jnp.int32)
    large = jnp.minimum(large, REL_BUCKETS - 1)
    bucket = jnp.where(dist < max_exact, dist, large)
    out = jnp.zeros((bs, bs), F32)
    for b in range(REL_BUCKETS):
        out = jnp.where(bucket == b, rel_ref[b, h], out)
    o_ref[0, 0] = out * LOG2E


def _bias_tiles(rel_bias):
    bs = MOBA_BLOCK
    return pl.pallas_call(
        _bias_kernel,
        grid=(MOBA_HEADS, BIAS_TILES),
        in_specs=[pl.BlockSpec(memory_space=pltpu.SMEM)],
        out_specs=pl.BlockSpec((1, 1, bs, bs), lambda h, d: (h, d, 0, 0)),
        out_shape=jax.ShapeDtypeStruct((MOBA_HEADS, BIAS_TILES, bs, bs), F32),
        compiler_params=_cparams("parallel", "parallel"),
        name="bias",
    )(rel_bias)


def _moba_prep_kernel(q_ref, k_ref, v_ref, qw_ref, kw_ref, qn_ref, kn_ref, km_ref, vt_ref):
    qn_ref[0, 0] = _rms_rows(q_ref[0], qw_ref[...])
    kn = _rms_rows(k_ref[0], kw_ref[...])
    kn_ref[0, 0] = kn.astype(BF16)
    km_ref[0, 0, 0] = jnp.mean(kn, axis=0, keepdims=True)
    vt_ref[0, 0] = v_ref[0].T.astype(BF16)


def _moba_prep(proj3, qw, kw, col0):
    b, t, _ = proj3.shape
    hh, bs, dh = MOBA_HEADS, MOBA_BLOCK, HEAD_DIM
    nb = t // bs

    def col(base):
        return pl.BlockSpec((1, bs, LANES), lambda bi, hi, ti: (bi, ti, base + hi))

    row = pl.BlockSpec((1, LANES), lambda bi, hi, ti: (0, 0))
    return pl.pallas_call(
        _moba_prep_kernel,
        grid=(b, hh, nb),
        in_specs=[col(col0), col(col0 + hh), col(col0 + 2 * hh), row, row],
        out_specs=[pl.BlockSpec((1, 1, bs, dh), lambda bi, hi, ti: (bi, hi, ti, 0)),
                   pl.BlockSpec((1, 1, bs, dh), lambda bi, hi, ti: (bi, hi, ti, 0)),
                   pl.BlockSpec((1, 1, 1, 1, dh), lambda bi, hi, ti: (bi, hi, ti, 0, 0)),
                   pl.BlockSpec((1, 1, dh, bs), lambda bi, hi, ti: (bi, hi, 0, ti))],
        out_shape=[jax.ShapeDtypeStruct((b, hh, t, dh), F32),
                   jax.ShapeDtypeStruct((b, hh, t, dh), BF16),
                   jax.ShapeDtypeStruct((b, hh, nb, 1, dh), F32),
                   jax.ShapeDtypeStruct((b, hh, dh, t), BF16)],
        compiler_params=_cparams("parallel", "parallel", "parallel"),
        name="moba_prep",
    )(proj3, proj3, proj3, qw, kw)


def _moba_kernel(qn_ref, kn_ref, vt_ref, km_ref, bias_ref, o_ref, sel_scr, own_scr, s_scr, *, nb):
    i = pl.program_id(2)
    bs = MOBA_BLOCK
    ch = MOBA_TILES_PER_STEP
    c2 = (HEAD_DIM ** -0.5) * LOG2E
    q = qn_ref[0, 0]
    q16 = q.astype(BF16)

    route = _dot_nt(km_ref[0, 0], q, HIGHEST)
    blk = lax.broadcasted_iota(jnp.int32, (nb, bs), 0)
    blkf = blk.astype(F32)
    valid = blk < i
    work = jnp.where(valid, route, NEG_INF)
    picked = jnp.zeros((nb, bs), F32)
    for _ in range(MOBA_TOPK):
        best = jnp.max(work, axis=0, keepdims=True)
        first = jnp.min(jnp.where(work == best, blkf, float(nb)), axis=0, keepdims=True)
        hit = blkf == first
        picked = jnp.where(hit, 1.0, picked)
        work = jnp.where(hit, -jnp.inf, work)
    sel_scr[...] = jnp.where(valid, picked, 0.0)

    kk = lax.broadcasted_iota(jnp.int32, (bs, bs), 0)
    qq = lax.broadcasted_iota(jnp.int32, (bs, bs), 1)

    row0 = pl.multiple_of(i * bs, bs)
    s = _dot_nt(kn_ref[0, 0, pl.ds(row0, bs), :], q16) * c2 + bias_ref[0, 0]
    s = jnp.where(kk <= qq, s, NEG_INF)
    own_scr[...] = s
    m = jnp.max(s, axis=0, keepdims=True)

    n_steps = (i + ch - 1) // ch

    def scores(c, m):
        r = pl.multiple_of(c * (ch * bs), ch * bs)
        s_all = _dot_nt(kn_ref[0, 0, pl.ds(r, ch * bs), :], q16)
        for t in range(ch):
            j = c * ch + t
            d = jnp.clip(i - j, 0, BIAS_TILES - 1)
            st = s_all[t * bs:(t + 1) * bs] * c2 + bias_ref[0, d]
            st = jnp.where(sel_scr[pl.ds(j, 1), :] > 0.0, st, NEG_INF)
            s_scr[pl.ds(r + t * bs, bs), :] = st
            m = jnp.maximum(m, jnp.max(st, axis=0, keepdims=True))
        return m

    m = lax.fori_loop(0, n_steps, scores, m)

    p = jnp.exp2(own_scr[...] - m)
    l = jnp.sum(p, axis=0, keepdims=True)
    acc = _dot(vt_ref[0, 0, :, pl.ds(row0, bs)], p.astype(BF16))

    def accumulate(c, carry):
        l, acc = carry
        r = pl.multiple_of(c * (ch * bs), ch * bs)
        p = jnp.exp2(s_scr[pl.ds(r, ch * bs), :] - m)
        l = l + jnp.sum(p, axis=0, keepdims=True)
        acc = acc + _dot(vt_ref[0, 0, :, pl.ds(r, ch * bs)], p.astype(BF16))
        return l, acc

    l, acc = lax.fori_loop(0, n_steps, accumulate, (l, acc))
    o_ref[0] = (acc / l).T.astype(o_ref.dtype)


def _moba(qn, kn, vt, km, bias):
    b, hh, t, dh = qn.shape
    bs = MOBA_BLOCK
    nb = t // bs
    return pl.pallas_call(
        functools.partial(_moba_kernel, nb=nb),
        grid=(b, hh, nb),
        in_specs=[pl.BlockSpec((1, 1, bs, dh), lambda bi, hi, ti: (bi, hi, ti, 0)),
                  pl.BlockSpec((1, 1, t, dh), lambda bi, hi, ti: (bi, hi, 0, 0)),
                  pl.BlockSpec((1, 1, dh, t), lambda bi, hi, ti: (bi, hi, 0, 0)),
                  pl.BlockSpec((1, 1, nb, dh), lambda bi, hi, ti: (bi, hi, 0, 0)),
                  pl.BlockSpec((1, BIAS_TILES, bs, bs), lambda bi, hi, ti: (hi, 0, 0, 0))],
        out_specs=pl.BlockSpec((1, bs, dh), lambda bi, hi, ti: (bi, ti, hi)),
        out_shape=jax.ShapeDtypeStruct((b, t, hh * dh), BF16),
        scratch_shapes=[pltpu.VMEM((nb, bs), F32),
                        pltpu.VMEM((bs, bs), F32),
                        pltpu.VMEM((t, bs), F32)],
        compiler_params=_cparams("parallel", "parallel", "parallel"),
        name="moba",
    )(qn, kn, vt, km, bias)


def _tile(n, want):
    t = min(n, want)
    assert n % t == 0, (n, want)
    return t


def kernel(x, norm_mix_w, w_in, conv_w, a_log, dt_bias, gdn_o_norm_w, q_norm_w, k_norm_w, rel_bias,
           w_branch_gdn, w_branch_moba, w_out, norm_ffn_w, w_ffn_gate, w_ffn_up, w_ffn_down):
    b, t, d = x.shape
    m = b * t
    gw = GDN_HEADS * HEAD_DIM
    mw = MOBA_HEADS * HEAD_DIM
    assert t % (MOBA_BLOCK * MOBA_TILES_PER_STEP) == 0 and w_in.shape[0] == 1
    assert w_in.shape[2] == 4 * gw + 2 * GDN_HEADS + 3 * mw + 2 * d

    wi = w_in[0]
    o_bd = 4 * gw
    o_moba = o_bd + 2 * GDN_HEADS
    o_gate = o_moba + 3 * mw
    n_main = 2 * d + 4 * gw + 3 * mw
    tn_proj = 1280
    n_proj = -(-(n_main + LANES) // tn_proj) * tn_proj
    w_proj = jnp.concatenate(
        [wi[:, o_gate:], wi[:, :o_bd], wi[:, o_moba:o_gate], wi[:, o_bd:o_moba],
         jnp.zeros((d, n_proj - n_main - 2 * GDN_HEADS), wi.dtype)], axis=1).astype(BF16)
    col_gdn = (2 * d) // LANES
    col_moba = col_gdn + 4 * GDN_HEADS
    col_bd = col_moba + 3 * MOBA_HEADS

    x2 = x.reshape(m, d)
    proj = _norm_matmul(x2, norm_mix_w, w_proj, _tile(m, 512), tn_proj, F32, "proj")
    proj3 = proj.reshape(b, t, n_proj)

    lane_pad = jnp.zeros((1, LANES - 2 * GDN_HEADS), F32)
    head_pad = jnp.zeros((1, GDN_HEADS), F32)
    alog_row = jnp.concatenate([head_pad, a_log, lane_pad], axis=1)
    dtb_row = jnp.concatenate([head_pad, dt_bias, lane_pad], axis=1)
    y_a = _gdn(proj3, conv_w[0], alog_row, dtb_row, gdn_o_norm_w, col_gdn, col_bd, _tile(t, 256))

    bias = _bias_tiles(rel_bias)
    qn, kn, km, vt = _moba_prep(proj3, q_norm_w, k_norm_w, col_moba)
    y_b = _moba(qn, kn, vt, km.reshape(b, MOBA_HEADS, t // MOBA_BLOCK, HEAD_DIM), bias)

    mix = _mix(y_a.reshape(m, gw), y_b.reshape(m, mw), proj,
               w_branch_gdn[0].astype(BF16), w_branch_moba[0].astype(BF16), _tile(m, 512), 1024)
    h1 = _matmul_res(mix, w_out[0].astype(BF16), x2, _tile(m, 512), 1024, "out")

    hid = _ffn_up(h1, norm_ffn_w, w_ffn_gate[0].astype(BF16), w_ffn_up[0].astype(BF16),
                  _tile(m, 1024), 512)
    h2 = _matmul_res(hid, w_ffn_down[0].astype(BF16), h1, _tile(m, 512), 512, "ffn_down")
    return h2.reshape(b, t, d)
```

```python
import functools
import math

import jax
import jax.numpy as jnp
from jax import lax
from jax.experimental import pallas as pl
from jax.experimental.pallas import tpu as pltpu

F32 = jnp.float32
BF16 = jnp.bfloat16
HIGHEST = lax.Precision.HIGHEST

LANES = 128
HEAD_DIM = 128
GDN_HEADS = 8
GDN_CONV = 4
GDN_CHUNK = 64
GDN_HEADS_PER_STEP = 4
MOBA_HEADS = 8
MOBA_BLOCK = 256
MOBA_TOPK = 3
REL_BUCKETS = 32
REL_MAX_DIST = 2048
BIAS_TILES = 8
MOBA_TILES_PER_STEP = 4
MOBA_STREAMS = 2
LOG2E = math.log2(math.e)
RMS_EPS = 1e-6
NEG_INF = -1e30
VMEM_LIMIT = 56 * 1024 * 1024


def _cparams(*sem):
    return pltpu.CompilerParams(dimension_semantics=sem, vmem_limit_bytes=VMEM_LIMIT)


def _sigmoid(x):
    return 1.0 / (1.0 + jnp.exp(-x))


def _silu(x):
    h = 0.5 * x
    return h + h * jnp.tanh(h)


def _dot(a, b, precision=None):
    return jnp.dot(a, b, preferred_element_type=F32, precision=precision)


def _dot_nt(a, b, precision=None):
    return lax.dot_general(a, b, (((1,), (1,)), ((), ())),
                           preferred_element_type=F32, precision=precision)


def _dot_tn(a, b):
    return lax.dot_general(a, b, (((0,), (0,)), ((), ())), preferred_element_type=F32)


def _rms_rows(x, w):
    return x * lax.rsqrt(jnp.mean(x * x, axis=-1, keepdims=True) + RMS_EPS) * w


def _norm_matmul_kernel(x_ref, nw_ref, w_ref, o_ref, u_scr):
    @pl.when(pl.program_id(1) == 0)
    def _():
        u_scr[...] = _rms_rows(x_ref[...], nw_ref[...]).astype(BF16)

    o_ref[...] = _dot(u_scr[...], w_ref[...]).astype(o_ref.dtype)


def _norm_matmul(x, nw, w, tm, tn, out_dtype, name):
    m, k = x.shape
    n = w.shape[1]
    return pl.pallas_call(
        _norm_matmul_kernel,
        grid=(m // tm, n // tn),
        in_specs=[pl.BlockSpec((tm, k), lambda i, j: (i, 0)),
                  pl.BlockSpec((1, k), lambda i, j: (0, 0)),
                  pl.BlockSpec((k, tn), lambda i, j: (0, j))],
        out_specs=pl.BlockSpec((tm, tn), lambda i, j: (i, j)),
        out_shape=jax.ShapeDtypeStruct((m, n), out_dtype),
        scratch_shapes=[pltpu.VMEM((tm, k), BF16)],
        compiler_params=_cparams("parallel", "arbitrary"),
        name=name,
    )(x, nw, w)


def _matmul_res_kernel(a_ref, w_ref, r_ref, o_ref):
    o_ref[...] = r_ref[...] + _dot(a_ref[...], w_ref[...])


def _matmul_res(a, w, res, tm, tn, name):
    m, k = a.shape
    n = w.shape[1]
    return pl.pallas_call(
        _matmul_res_kernel,
        grid=(m // tm, n // tn),
        in_specs=[pl.BlockSpec((tm, k), lambda i, j: (i, 0)),
                  pl.BlockSpec((k, tn), lambda i, j: (0, j)),
                  pl.BlockSpec((tm, tn), lambda i, j: (i, j))],
        out_specs=pl.BlockSpec((tm, tn), lambda i, j: (i, j)),
        out_shape=jax.ShapeDtypeStruct((m, n), F32),
        compiler_params=_cparams("parallel", "parallel"),
        name=name,
    )(a, w, res)


def _ffn_up_kernel(x_ref, nw_ref, wg_ref, wu_ref, o_ref, u_scr):
    @pl.when(pl.program_id(1) == 0)
    def _():
        u_scr[...] = _rms_rows(x_ref[...], nw_ref[...]).astype(BF16)

    u = u_scr[...]
    g = _dot(u, wg_ref[...])
    o_ref[...] = (g * _sigmoid(g) * _dot(u, wu_ref[...])).astype(o_ref.dtype)


def _ffn_up(x, nw, wg, wu, tm, tn):
    m, k = x.shape
    n = wg.shape[1]
    return pl.pallas_call(
        _ffn_up_kernel,
        grid=(m // tm, n // tn),
        in_specs=[pl.BlockSpec((tm, k), lambda i, j: (i, 0)),
                  pl.BlockSpec((1, k), lambda i, j: (0, 0)),
                  pl.BlockSpec((k, tn), lambda i, j: (0, j)),
                  pl.BlockSpec((k, tn), lambda i, j: (0, j))],
        out_specs=pl.BlockSpec((tm, tn), lambda i, j: (i, j)),
        out_shape=jax.ShapeDtypeStruct((m, n), BF16),
        scratch_shapes=[pltpu.VMEM((tm, k), BF16)],
        compiler_params=_cparams("parallel", "arbitrary"),
        name="ffn_up",
    )(x, nw, wg, wu)


def _mix_kernel(ya_ref, yb_ref, ga_ref, gb_ref, wa_ref, wb_ref, o_ref):
    a = _dot(ya_ref[...], wa_ref[...])
    b = _dot(yb_ref[...], wb_ref[...])
    o_ref[...] = (_sigmoid(ga_ref[...]) * a + _sigmoid(gb_ref[...]) * b).astype(o_ref.dtype)


def _mix(ya, yb, proj, wa, wb, tm, tn):
    m, k = ya.shape
    n = wa.shape[1]
    nb = n // tn
    return pl.pallas_call(
        _mix_kernel,
        grid=(m // tm, nb),
        in_specs=[pl.BlockSpec((tm, k), lambda i, j: (i, 0)),
                  pl.BlockSpec((tm, k), lambda i, j: (i, 0)),
                  pl.BlockSpec((tm, tn), lambda i, j: (i, j)),
                  pl.BlockSpec((tm, tn), lambda i, j: (i, nb + j)),
                  pl.BlockSpec((k, tn), lambda i, j: (0, j)),
                  pl.BlockSpec((k, tn), lambda i, j: (0, j))],
        out_specs=pl.BlockSpec((tm, tn), lambda i, j: (i, j)),
        out_shape=jax.ShapeDtypeStruct((m, n), BF16),
        compiler_params=_cparams("parallel", "parallel"),
        name="mix",
    )(ya, yb, proj, proj, wa, wb)


def _split3(x):
    a = x.astype(BF16)
    r = x - a.astype(F32)
    b = r.astype(BF16)
    c = (r - b.astype(F32)).astype(BF16)
    return a, b, c


def _gdn_kernel(q_ref, k_ref, v_ref, z_ref, bd_ref, cwq_ref, cwk_ref, cwv_ref,
                alog_ref, dtb_ref, onw_ref, o_ref, s_scr, tail_scr, *, tb):
    hp = GDN_HEADS_PER_STEP
    C = GDN_CHUNK
    D = HEAD_DIM
    head0 = pl.program_id(1) * hp

    @pl.when(pl.program_id(2) == 0)
    def _():
        s_scr[...] = jnp.zeros_like(s_scr)
        tail_scr[...] = jnp.zeros_like(tail_scr)

    row8 = lax.broadcasted_iota(jnp.int32, (8, hp * D), 0)
    tails = []

    def conv_silu(u_ref, cw_ref, idx):
        u = u_ref[0]
        w = cw_ref[...]
        tail = tail_scr[idx]
        y = None
        ytop = None
        for s in (3, 2, 1):
            sh = pltpu.roll(u, s, axis=0)
            top = jnp.where(row8 < s, pltpu.roll(tail, s, axis=0), sh[0:8])
            wj = w[3 - s:4 - s]
            y = sh * wj if y is None else y + sh * wj
            ytop = top * wj if ytop is None else ytop + top * wj
        y = y + u * w[3:4]
        ytop = ytop + u[0:8] * w[3:4]
        tails.append(u[tb - 8:tb])
        y = jnp.concatenate([ytop, y[8:]], axis=0)
        return _silu(y)

    q_all = conv_silu(q_ref, cwq_ref, 0)
    k_all = conv_silu(k_ref, cwk_ref, 1)
    v_all = conv_silu(v_ref, cwv_ref, 2)

    bd = bd_ref[0]
    lane = lax.broadcasted_iota(jnp.int32, (tb, LANES), 1)
    beta_all = _sigmoid(bd)
    xg = bd + dtb_ref[...]
    softplus = jnp.maximum(xg, 0.0) + jnp.log1p(jnp.exp(-jnp.abs(xg)))
    g_all = -jnp.exp(alog_ref[...]) * softplus
    betas = [jnp.sum(jnp.where(lane == head0 + e, beta_all, 0.0), axis=-1, keepdims=True)
             for e in range(hp)]
    gs = [jnp.sum(jnp.where(lane == head0 + e + GDN_HEADS, g_all, 0.0), axis=-1, keepdims=True)
          for e in range(hp)]

    nc = tb // C
    ri = lax.broadcasted_iota(jnp.int32, (tb, tb), 0)
    ci = lax.broadcasted_iota(jnp.int32, (tb, tb), 1)
    same = (ri // C) == (ci // C)
    tril16 = jnp.where(same & (ri >= ci), 1.0, 0.0).astype(BF16)
    g_b = jnp.concatenate([jnp.broadcast_to(g, (tb, LANES)) for g in gs], axis=1)
    gc_all = sum(_dot(tril16, piece) for piece in _split3(g_b))

    iw = lax.broadcasted_iota(jnp.int32, (C, tb), 0)
    jw = lax.broadcasted_iota(jnp.int32, (C, tb), 1) % C
    incl_w = iw >= jw
    strict_w = iw > jw
    eye_w = jnp.where(iw == jw, 1.0, 0.0).astype(F32)
    low_half = (lax.broadcasted_iota(jnp.int32, (C, LANES), 1) < C)

    def block_diag(wide16):
        return jnp.where(same, jnp.concatenate([wide16] * nc, axis=0), jnp.zeros((), BF16))

    def pair_blocks(full):
        return jnp.where(low_half, full[:C], full[C:])

    E = range(hp)
    pairs = range(tb // LANES)

    def l2n(x):
        return x * lax.rsqrt(jnp.sum(x * x, axis=-1, keepdims=True) + RMS_EPS)

    qs = [l2n(q_all[:, e * D:(e + 1) * D]) * (D ** -0.5) for e in E]
    ks = [l2n(k_all[:, e * D:(e + 1) * D]) for e in E]
    vs = [v_all[:, e * D:(e + 1) * D] for e in E]
    gcs = [gc_all[:, e * D:(e + 1) * D] for e in E]
    g_col = [jnp.concatenate([pair_blocks(gc[p * LANES:(p + 1) * LANES]) for p in pairs], axis=1)
             for gc in gcs]
    g_row = [jnp.concatenate([gc[p * LANES:(p + 1) * LANES, :].T[0:1, :] for p in pairs], axis=1)
             for gc in gcs]
    decay = [jnp.where(incl_w, jnp.exp(jnp.where(incl_w, g_col[e] - g_row[e], 0.0)), 0.0) for e in E]
    k16 = [k.astype(BF16) for k in ks]
    q16 = [q.astype(BF16) for q in qs]
    kb = [ks[e] * betas[e] for e in E]
    kb16 = [x.astype(BF16) for x in kb]

    def pair_products(a16, b16):
        return jnp.concatenate([pair_blocks(_dot_nt(a16[p * LANES:(p + 1) * LANES],
                                                    b16[p * LANES:(p + 1) * LANES])) for p in pairs], axis=1)

    lmat = [jnp.where(strict_w, pair_products(kb16[e], k16[e]) * decay[e], 0.0) for e in E]
    amat = [jnp.where(incl_w, pair_products(q16[e], k16[e]) * decay[e], 0.0) for e in E]
    p16 = [(-lmat[e]).astype(BF16) for e in E]
    tinv = [eye_w - lmat[e] for e in E]
    pw = [_dot(p16[e], block_diag(p16[e])) for e in E]
    for _ in range(4):
        p16 = [pw[e].astype(BF16) for e in E]
        prod = [_dot(jnp.concatenate([p16[e], tinv[e].astype(BF16)], axis=0), block_diag(p16[e])) for e in E]
        pw = [prod[e][:C] for e in E]
        tinv = [tinv[e] + prod[e][C:] for e in E]
    tinv = [tinv[e] + _dot(tinv[e].astype(BF16), block_diag(pw[e].astype(BF16))) for e in E]
    eg = [jnp.exp(gc) for gc in gcs]
    rhs = [jnp.concatenate([kb[e] * eg[e], vs[e] * betas[e]], axis=1).astype(BF16) for e in E]
    wu = [_dot(block_diag(tinv[e].astype(BF16)), rhs[e]).astype(BF16) for e in E]
    au = [_dot(block_diag(amat[e].astype(BF16)), wu[e]) for e in E]
    q_eff = [(qs[e] * eg[e] - au[e][:, :D]).astype(BF16) for e in E]
    gl = [jnp.concatenate([jnp.broadcast_to(gc[c * C + C - 1:c * C + C, :], (C, LANES))
                           for c in range(nc)], axis=0) for gc in gcs]
    k_dec = [(ks[e] * jnp.exp(gl[e] - gcs[e])).astype(BF16) for e in E]
    pn = [[_dot_tn(k_dec[e][c * C:(c + 1) * C], wu[e][c * C:(c + 1) * C]) for c in range(nc)]
          for e in E]

    states = [s_scr[e] for e in E]
    onw = onw_ref[...]
    for c in range(nc):
        r0 = c * C
        lhs = [jnp.concatenate([pn[e][c][:, :D].astype(BF16), q_eff[e][r0:r0 + C]], axis=0) for e in E]
        res = [_dot(lhs[e], states[e].astype(BF16)) for e in E]
        outs = [_rms_rows(res[e][D:] + au[e][r0:r0 + C, D:], onw) for e in E]
        states = [states[e] * jnp.exp(gl[e][r0:r0 + 1, :]) - res[e][:D] + pn[e][c][:, D:] for e in E]
        zc = z_ref[0, r0:r0 + C, :]
        o_ref[0, r0:r0 + C, :] = (jnp.concatenate(outs, axis=1) * _silu(zc)).astype(o_ref.dtype)
    for e in range(hp):
        s_scr[e] = states[e]
    for idx in range(3):
        tail_scr[idx] = tails[idx]


def _gdn(proj3, conv_w, alog_row, dtb_row, onw, col0, bd_col, tb):
    b, t, _ = proj3.shape
    hp = GDN_HEADS_PER_STEP
    ng = GDN_HEADS // hp
    w = hp * HEAD_DIM

    def col(base):
        return pl.BlockSpec((1, tb, w), lambda bi, hi, ti: (bi, ti, base // hp + hi))

    def cw(base):
        return pl.BlockSpec((GDN_CONV, w), lambda bi, hi, ti: (0, base // hp + hi))

    assert col0 % hp == 0
    row = pl.BlockSpec((1, LANES), lambda bi, hi, ti: (0, 0))
    return pl.pallas_call(
        functools.partial(_gdn_kernel, tb=tb),
        grid=(b, ng, t // tb),
        in_specs=[col(col0), col(col0 + GDN_HEADS), col(col0 + 2 * GDN_HEADS), col(col0 + 3 * GDN_HEADS),
                  pl.BlockSpec((1, tb, LANES), lambda bi, hi, ti: (bi, ti, bd_col)),
                  cw(0), cw(GDN_HEADS), cw(2 * GDN_HEADS), row, row, row],
        out_specs=pl.BlockSpec((1, tb, w), lambda bi, hi, ti: (bi, ti, hi)),
        out_shape=jax.ShapeDtypeStruct((b, t, GDN_HEADS * HEAD_DIM), BF16),
        scratch_shapes=[pltpu.VMEM((hp, HEAD_DIM, HEAD_DIM), F32),
                        pltpu.VMEM((3, 8, w), F32)],
        compiler_params=_cparams("parallel", "parallel", "arbitrary"),
        name="gdn",
    )(proj3, proj3, proj3, proj3, proj3, conv_w, conv_w, conv_w, alog_row, dtb_row, onw)


def _bias_kernel(rel_ref, o_ref):
    h = pl.program_id(0)
    d = pl.program_id(1)
    bs = MOBA_BLOCK
    max_exact = REL_BUCKETS // 2
    kk = lax.broadcasted_iota(jnp.int32, (bs, bs), 0)
    qq = lax.broadcasted_iota(jnp.int32, (bs, bs), 1)
    dist = jnp.maximum(d * bs + qq - kk, 0)
    df = dist.astype(F32)
    log_ratio = jnp.log(jnp.maximum(df, float(max_exact)) / max_exact) / math.log(REL_MAX_DIST / max_exact)
    large = max_exact + (log_ratio * (REL_BUCKETS - max_exact)).astype(jnp.int32)
    large = jnp.minimum(large, REL_BUCKETS - 1)
    bucket = jnp.where(dist < max_exact, dist, large)
    out = jnp.zeros((bs, bs), F32)
    for b in range(REL_BUCKETS):
        out = jnp.where(bucket == b, rel_ref[b, h], out)
    o_ref[0, 0] = out * LOG2E


def _bias_tiles(rel_bias):
    bs = MOBA_BLOCK
    return pl.pallas_call(
        _bias_kernel,
        grid=(MOBA_HEADS, BIAS_TILES),
        in_specs=[pl.BlockSpec(memory_space=pltpu.SMEM)],
        out_specs=pl.BlockSpec((1, 1, bs, bs), lambda h, d: (h, d, 0, 0)),
        out_shape=jax.ShapeDtypeStruct((MOBA_HEADS, BIAS_TILES, bs, bs), F32),
        compiler_params=_cparams("parallel", "parallel"),
        name="bias",
    )(rel_bias)


def _moba_prep_kernel(k_ref, v_ref, kw_ref, kn_ref, km_ref, vt_ref, *, rows):
    bs = MOBA_BLOCK
    kn = _rms_rows(k_ref[0], kw_ref[...])
    kn_ref[0, 0] = kn.astype(BF16)
    for s in range(rows // bs):
        km_ref[0, 0, s] = jnp.mean(kn[s * bs:(s + 1) * bs], axis=0, keepdims=True)
        vt_ref[0, 0, :, s * bs:(s + 1) * bs] = v_ref[0, s * bs:(s + 1) * bs, :].T.astype(BF16)


def _moba_prep(proj3, kw, col0, rows):
    b, t, _ = proj3.shape
    hh, bs, dh = MOBA_HEADS, MOBA_BLOCK, HEAD_DIM
    nb = t // bs

    def col(base):
        return pl.BlockSpec((1, rows, LANES), lambda bi, hi, ti: (bi, ti, base + hi))

    row = pl.BlockSpec((1, LANES), lambda bi, hi, ti: (0, 0))
    return pl.pallas_call(
        functools.partial(_moba_prep_kernel, rows=rows),
        grid=(b, hh, t // rows),
        in_specs=[col(col0 + hh), col(col0 + 2 * hh), row],
        out_specs=[pl.BlockSpec((1, 1, rows, dh), lambda bi, hi, ti: (bi, hi, ti, 0)),
                   pl.BlockSpec((1, 1, rows // bs, 1, dh), lambda bi, hi, ti: (bi, hi, ti, 0, 0)),
                   pl.BlockSpec((1, 1, dh, rows), lambda bi, hi, ti: (bi, hi, 0, ti))],
        out_shape=[jax.ShapeDtypeStruct((b, hh, t, dh), BF16),
                   jax.ShapeDtypeStruct((b, hh, nb, 1, dh), F32),
                   jax.ShapeDtypeStruct((b, hh, dh, t), BF16)],
        compiler_params=_cparams("parallel", "parallel", "parallel"),
        name="moba_prep",
    )(proj3, proj3, kw)


def _moba_kernel(q_ref, qw_ref, kn_ref, vt_ref, km_ref, bias_ref, o_ref,
                 sel_scr, own_scr, s_scr, p_a, p_b, *, nb):
    i = pl.program_id(2)
    bs = MOBA_BLOCK
    ch = MOBA_TILES_PER_STEP
    E = range(MOBA_STREAMS)
    qs = [_rms_rows(q_ref[e], qw_ref[...]) for e in E]
    q16 = [(q * ((HEAD_DIM ** -0.5) * LOG2E)).astype(BF16) for q in qs]

    blk = lax.broadcasted_iota(jnp.int32, (nb, bs), 0)
    blkf = blk.astype(F32)
    valid = blk < i
    work = [jnp.where(valid, _dot_nt(km_ref[e, 0], qs[e], HIGHEST), NEG_INF) for e in E]
    picked = [jnp.zeros((nb, bs), F32) for e in E]
    for _ in range(MOBA_TOPK):
        best = [jnp.max(work[e], axis=0, keepdims=True) for e in E]
        first = [jnp.min(jnp.where(work[e] == best[e], blkf, float(nb)), axis=0, keepdims=True) for e in E]
        hit = [blkf == first[e] for e in E]
        picked = [jnp.where(hit[e], 1.0, picked[e]) for e in E]
        work = [jnp.where(hit[e], -jnp.inf, work[e]) for e in E]
    for e in E:
        sel_scr[e] = jnp.where(valid, picked[e], 0.0)

    kk = lax.broadcasted_iota(jnp.int32, (bs, bs), 0)
    qq = lax.broadcasted_iota(jnp.int32, (bs, bs), 1)

    row0 = pl.multiple_of(i * bs, bs)
    own = [jnp.where(kk <= qq, _dot_nt(kn_ref[e, 0, pl.ds(row0, bs), :], q16[e]) + bias_ref[0, 0], NEG_INF)
           for e in E]
    for e in E:
        own_scr[e] = own[e]
    m = tuple(jnp.max(own[e], axis=0, keepdims=True) for e in E)

    half = ch // 2
    hrows = half * bs
    n_iter = (i + ch - 1) // ch
    last_half = nb // half - 1

    def raw_scores(e, hs):
        r = pl.multiple_of(hs * hrows, hrows)
        return _dot_nt(kn_ref[e, 0, pl.ds(r, hrows), :], q16[e])

    def finish_scores(raw, e, hs, m):
        for t in range(half):
            j = hs * half + t
            d = jnp.clip(i - j, 0, BIAS_TILES - 1)
            st = raw[t * bs:(t + 1) * bs, :] + bias_ref[0, d]
            st = jnp.where(sel_scr[e, pl.ds(j, 1), :] > 0.0, st, NEG_INF)
            s_scr[e, pl.ds(pl.multiple_of(j * bs, bs), bs), :] = st
            m = jnp.maximum(m, jnp.max(st, axis=0, keepdims=True))
        return m

    def scores(c, m):
        for hs in (2 * c, 2 * c + 1):
            raw = [raw_scores(e, hs) for e in E]
            m = tuple(finish_scores(raw[e], e, hs, m[e]) for e in E)
        return m

    m = lax.fori_loop(0, n_iter, scores, m)

    def probs(src, m_e):
        p = jnp.exp2(src - m_e)
        return p.astype(BF16), jnp.sum(p, axis=0, keepdims=True)

    def weighted_values(p_ref, e, hs):
        r = pl.multiple_of(hs * hrows, hrows)
        return _dot(vt_ref[e, 0, :, pl.ds(r, hrows)], p_ref[e])

    own_p = [probs(own_scr[e], m[e]) for e in E]
    l = tuple(own_p[e][1] for e in E)
    acc = tuple(_dot(vt_ref[e, 0, :, pl.ds(row0, bs)], own_p[e][0]) for e in E)
    p_b[...] = jnp.zeros_like(p_b)

    def accumulate(c, carry):
        l, acc = list(carry[0]), list(carry[1])
        for e in E:
            acc[e] = acc[e] + weighted_values(p_b, e, jnp.maximum(2 * c - 1, 0))
        for e in E:
            r = pl.multiple_of(2 * c * hrows, hrows)
            p_a[e], psum = probs(s_scr[e, pl.ds(r, hrows), :], m[e])
            l[e] = l[e] + psum
        for e in E:
            acc[e] = acc[e] + weighted_values(p_a, e, 2 * c)
        for e in E:
            r = pl.multiple_of((2 * c + 1) * hrows, hrows)
            p_b[e], psum = probs(s_scr[e, pl.ds(r, hrows), :], m[e])
            l[e] = l[e] + psum
        return tuple(l), tuple(acc)

    l, acc = lax.fori_loop(0, n_iter, accumulate, (l, acc))
    for e in E:
        out = acc[e] + weighted_values(p_b, e, jnp.maximum(2 * n_iter - 1, 0))
        o_ref[e] = (out / l[e]).T.astype(o_ref.dtype)


def _moba(proj3, qw, kn, vt, km, bias, col0):
    b, hh, t, dh = kn.shape
    bs = MOBA_BLOCK
    nb = t // bs
    ns = MOBA_STREAMS
    hrows = (MOBA_TILES_PER_STEP // 2) * bs
    assert b % ns == 0
    return pl.pallas_call(
        functools.partial(_moba_kernel, nb=nb),
        grid=(b // ns, hh, nb),
        in_specs=[pl.BlockSpec((ns, bs, dh), lambda bi, hi, ti: (bi, ti, col0 + hi)),
                  pl.BlockSpec((1, dh), lambda bi, hi, ti: (0, 0)),
                  pl.BlockSpec((ns, 1, t, dh), lambda bi, hi, ti: (bi, hi, 0, 0)),
                  pl.BlockSpec((ns, 1, dh, t), lambda bi, hi, ti: (bi, hi, 0, 0)),
                  pl.BlockSpec((ns, 1, nb, dh), lambda bi, hi, ti: (bi, hi, 0, 0)),
                  pl.BlockSpec((1, BIAS_TILES, bs, bs), lambda bi, hi, ti: (hi, 0, 0, 0))],
        out_specs=pl.BlockSpec((ns, bs, dh), lambda bi, hi, ti: (bi, ti, hi)),
        out_shape=jax.ShapeDtypeStruct((b, t, hh * dh), BF16),
        scratch_shapes=[pltpu.VMEM((ns, nb, bs), F32),
                        pltpu.VMEM((ns, bs, bs), F32),
                        pltpu.VMEM((ns, t, bs), F32),
                        pltpu.VMEM((ns, hrows, bs), BF16),
                        pltpu.VMEM((ns, hrows, bs), BF16)],
        compiler_params=_cparams("parallel", "parallel", "parallel"),
        name="moba",
    )(proj3, qw, kn, vt, km, bias)


def _tile(n, want):
    t = min(n, want)
    assert n % t == 0, (n, want)
    return t


def kernel(x, norm_mix_w, w_in, conv_w, a_log, dt_bias, gdn_o_norm_w, q_norm_w, k_norm_w, rel_bias,
           w_branch_gdn, w_branch_moba, w_out, norm_ffn_w, w_ffn_gate, w_ffn_up, w_ffn_down):
    b, t, d = x.shape
    m = b * t
    gw = GDN_HEADS * HEAD_DIM
    mw = MOBA_HEADS * HEAD_DIM
    assert t % (MOBA_BLOCK * MOBA_TILES_PER_STEP) == 0 and w_in.shape[0] == 1
    assert w_in.shape[2] == 4 * gw + 2 * GDN_HEADS + 3 * mw + 2 * d

    wi = w_in[0]
    o_bd = 4 * gw
    o_moba = o_bd + 2 * GDN_HEADS
    o_gate = o_moba + 3 * mw
    n_main = 2 * d + 4 * gw + 3 * mw
    tn_proj = 1280
    n_proj = -(-(n_main + LANES) // tn_proj) * tn_proj
    w_proj = jnp.concatenate(
        [wi[:, o_gate:], wi[:, :o_bd], wi[:, o_moba:o_gate], wi[:, o_bd:o_moba],
         jnp.zeros((d, n_proj - n_main - 2 * GDN_HEADS), wi.dtype)], axis=1).astype(BF16)
    col_gdn = (2 * d) // LANES
    col_moba = col_gdn + 4 * GDN_HEADS
    col_bd = col_moba + 3 * MOBA_HEADS

    x2 = x.reshape(m, d)
    proj = _norm_matmul(x2, norm_mix_w, w_proj, _tile(m, 1024), tn_proj, F32, "proj")
    proj3 = proj.reshape(b, t, n_proj)

    lane_pad = jnp.zeros((1, LANES - 2 * GDN_HEADS), F32)
    head_pad = jnp.zeros((1, GDN_HEADS), F32)
    alog_row = jnp.concatenate([head_pad, a_log, lane_pad], axis=1)
    dtb_row = jnp.concatenate([head_pad, dt_bias, lane_pad], axis=1)
    y_a = _gdn(proj3, conv_w[0], alog_row, dtb_row, gdn_o_norm_w, col_gdn, col_bd, _tile(t, 256))

    bias = _bias_tiles(rel_bias)
    kn, km, vt = _moba_prep(proj3, k_norm_w, col_moba, MOBA_BLOCK * MOBA_TILES_PER_STEP)
    y_b = _moba(proj3, q_norm_w, kn, vt, km.reshape(b, MOBA_HEADS, t // MOBA_BLOCK, HEAD_DIM), bias,
                col_moba)

    mix = _mix(y_a.reshape(m, gw), y_b.reshape(m, mw), proj,
               w_branch_gdn[0].astype(BF16), w_branch_moba[0].astype(BF16), _tile(m, 512), 1024)
    h1 = _matmul_res(mix, w_out[0].astype(BF16), x2, _tile(m, 512), 1024, "out")

    hid = _ffn_up(h1, norm_ffn_w, w_ffn_gate[0].astype(BF16), w_ffn_up[0].astype(BF16),
                  _tile(m, 1024), 512)
    h2 = _matmul_res(hid, w_ffn_down[0].astype(BF16), h1, _tile(m, 512), 512, "ffn_down")
    return h2.reshape(b, t, d)
```

```python
import functools
import math

import jax
import jax.numpy as jnp
from jax import lax
from jax.experimental import pallas as pl
from jax.experimental.pallas import tpu as pltpu

F32 = jnp.float32
BF16 = jnp.bfloat16
HIGHEST = lax.Precision.HIGHEST

LANES = 128
HEAD_DIM = 128
GDN_HEADS = 8
GDN_CONV = 4
GDN_CHUNK = 64
GDN_HEADS_PER_STEP = 4
MOBA_HEADS = 8
MOBA_BLOCK = 256
MOBA_TOPK = 3
REL_BUCKETS = 32
REL_MAX_DIST = 2048
BIAS_TILES = 8
MOBA_TILES_PER_STEP = 2
MOBA_PREP_ROWS = 1024
MOBA_STREAMS = 2
LOG2E = math.log2(math.e)
RMS_EPS = 1e-6
NEG_INF = -1e30
VMEM_LIMIT = 56 * 1024 * 1024


def _cparams(*sem):
    return pltpu.CompilerParams(dimension_semantics=sem, vmem_limit_bytes=VMEM_LIMIT)


def _sigmoid(x):
    return 1.0 / (1.0 + jnp.exp(-x))


def _silu(x):
    h = 0.5 * x
    return h + h * jnp.tanh(h)


def _dot(a, b, precision=None):
    return jnp.dot(a, b, preferred_element_type=F32, precision=precision)


def _dot_nt(a, b, precision=None):
    return lax.dot_general(a, b, (((1,), (1,)), ((), ())),
                           preferred_element_type=F32, precision=precision)


def _dot_tn(a, b):
    return lax.dot_general(a, b, (((0,), (0,)), ((), ())), preferred_element_type=F32)


def _rms_rows(x, w):
    return x * lax.rsqrt(jnp.mean(x * x, axis=-1, keepdims=True) + RMS_EPS) * w


def _norm_matmul_kernel(x_ref, nw_ref, w_ref, o_ref, u_scr):
    @pl.when(pl.program_id(1) == 0)
    def _():
        u_scr[...] = _rms_rows(x_ref[...], nw_ref[...]).astype(BF16)

    o_ref[...] = _dot(u_scr[...], w_ref[...]).astype(o_ref.dtype)


def _norm_matmul(x, nw, w, tm, tn, out_dtype, name):
    m, k = x.shape
    n = w.shape[1]
    return pl.pallas_call(
        _norm_matmul_kernel,
        grid=(m // tm, n // tn),
        in_specs=[pl.BlockSpec((tm, k), lambda i, j: (i, 0)),
                  pl.BlockSpec((1, k), lambda i, j: (0, 0)),
                  pl.BlockSpec((k, tn), lambda i, j: (0, j))],
        out_specs=pl.BlockSpec((tm, tn), lambda i, j: (i, j)),
        out_shape=jax.ShapeDtypeStruct((m, n), out_dtype),
        scratch_shapes=[pltpu.VMEM((tm, k), BF16)],
        compiler_params=_cparams("parallel", "arbitrary"),
        name=name,
    )(x, nw, w)


def _matmul_res_kernel(a_ref, w_ref, r_ref, o_ref):
    o_ref[...] = r_ref[...] + _dot(a_ref[...], w_ref[...])


def _matmul_res(a, w, res, tm, tn, name):
    m, k = a.shape
    n = w.shape[1]
    return pl.pallas_call(
        _matmul_res_kernel,
        grid=(m // tm, n // tn),
        in_specs=[pl.BlockSpec((tm, k), lambda i, j: (i, 0)),
                  pl.BlockSpec((k, tn), lambda i, j: (0, j)),
                  pl.BlockSpec((tm, tn), lambda i, j: (i, j))],
        out_specs=pl.BlockSpec((tm, tn), lambda i, j: (i, j)),
        out_shape=jax.ShapeDtypeStruct((m, n), F32),
        compiler_params=_cparams("parallel", "parallel"),
        name=name,
    )(a, w, res)


def _ffn_up_kernel(x_ref, nw_ref, wg_ref, wu_ref, o_ref, u_scr):
    @pl.when(pl.program_id(1) == 0)
    def _():
        u_scr[...] = _rms_rows(x_ref[...], nw_ref[...]).astype(BF16)

    u = u_scr[...]
    g = _dot(u, wg_ref[...])
    o_ref[...] = (g * _sigmoid(g) * _dot(u, wu_ref[...])).astype(o_ref.dtype)


def _ffn_up(x, nw, wg, wu, tm, tn):
    m, k = x.shape
    n = wg.shape[1]
    return pl.pallas_call(
        _ffn_up_kernel,
        grid=(m // tm, n // tn),
        in_specs=[pl.BlockSpec((tm, k), lambda i, j: (i, 0)),
                  pl.BlockSpec((1, k), lambda i, j: (0, 0)),
                  pl.BlockSpec((k, tn), lambda i, j: (0, j)),
                  pl.BlockSpec((k, tn), lambda i, j: (0, j))],
        out_specs=pl.BlockSpec((tm, tn), lambda i, j: (i, j)),
        out_shape=jax.ShapeDtypeStruct((m, n), BF16),
        scratch_shapes=[pltpu.VMEM((tm, k), BF16)],
        compiler_params=_cparams("parallel", "arbitrary"),
        name="ffn_up",
    )(x, nw, wg, wu)


def _mix_kernel(ya_ref, yb_ref, ga_ref, gb_ref, wa_ref, wb_ref, o_ref):
    a = _dot(ya_ref[...], wa_ref[...])
    b = _dot(yb_ref[...], wb_ref[...])
    o_ref[...] = (_sigmoid(ga_ref[...]) * a + _sigmoid(gb_ref[...]) * b).astype(o_ref.dtype)


def _mix(ya, yb, proj, wa, wb, tm, tn):
    m, k = ya.shape
    n = wa.shape[1]
    nb = n // tn
    return pl.pallas_call(
        _mix_kernel,
        grid=(m // tm, nb),
        in_specs=[pl.BlockSpec((tm, k), lambda i, j: (i, 0)),
                  pl.BlockSpec((tm, k), lambda i, j: (i, 0)),
                  pl.BlockSpec((tm, tn), lambda i, j: (i, j)),
                  pl.BlockSpec((tm, tn), lambda i, j: (i, nb + j)),
                  pl.BlockSpec((k, tn), lambda i, j: (0, j)),
                  pl.BlockSpec((k, tn), lambda i, j: (0, j))],
        out_specs=pl.BlockSpec((tm, tn), lambda i, j: (i, j)),
        out_shape=jax.ShapeDtypeStruct((m, n), BF16),
        compiler_params=_cparams("parallel", "parallel"),
        name="mix",
    )(ya, yb, proj, proj, wa, wb)


def _split3(x):
    a = x.astype(BF16)
    r = x - a.astype(F32)
    b = r.astype(BF16)
    c = (r - b.astype(F32)).astype(BF16)
    return a, b, c


def _gdn_kernel(q_ref, k_ref, v_ref, z_ref, bd_ref, cwq_ref, cwk_ref, cwv_ref,
                alog_ref, dtb_ref, onw_ref, o_ref, s_scr, tail_scr, *, tb):
    hp = GDN_HEADS_PER_STEP
    C = GDN_CHUNK
    D = HEAD_DIM
    head0 = pl.program_id(1) * hp

    @pl.when(pl.program_id(2) == 0)
    def _():
        s_scr[...] = jnp.zeros_like(s_scr)
        tail_scr[...] = jnp.zeros_like(tail_scr)

    row8 = lax.broadcasted_iota(jnp.int32, (8, hp * D), 0)
    tails = []

    def conv_silu(u_ref, cw_ref, idx):
        u = u_ref[0]
        w = cw_ref[...]
        tail = tail_scr[idx]
        y = None
        ytop = None
        for s in (3, 2, 1):
            sh = pltpu.roll(u, s, axis=0)
            top = jnp.where(row8 < s, pltpu.roll(tail, s, axis=0), sh[0:8])
            wj = w[3 - s:4 - s]
            y = sh * wj if y is None else y + sh * wj
            ytop = top * wj if ytop is None else ytop + top * wj
        y = y + u * w[3:4]
        ytop = ytop + u[0:8] * w[3:4]
        tails.append(u[tb - 8:tb])
        y = jnp.concatenate([ytop, y[8:]], axis=0)
        return _silu(y)

    q_all = conv_silu(q_ref, cwq_ref, 0)
    k_all = conv_silu(k_ref, cwk_ref, 1)
    v_all = conv_silu(v_ref, cwv_ref, 2)

    bd = bd_ref[0]
    lane = lax.broadcasted_iota(jnp.int32, (tb, LANES), 1)
    beta_all = _sigmoid(bd)
    xg = bd + dtb_ref[...]
    softplus = jnp.maximum(xg, 0.0) + jnp.log1p(jnp.exp(-jnp.abs(xg)))
    g_all = -jnp.exp(alog_ref[...]) * softplus
    betas = [jnp.sum(jnp.where(lane == head0 + e, beta_all, 0.0), axis=-1, keepdims=True)
             for e in range(hp)]
    gs = [jnp.sum(jnp.where(lane == head0 + e + GDN_HEADS, g_all, 0.0), axis=-1, keepdims=True)
          for e in range(hp)]

    nc = tb // C
    ri = lax.broadcasted_iota(jnp.int32, (tb, tb), 0)
    ci = lax.broadcasted_iota(jnp.int32, (tb, tb), 1)
    same = (ri // C) == (ci // C)
    tril16 = jnp.where(same & (ri >= ci), 1.0, 0.0).astype(BF16)
    g_b = jnp.concatenate([jnp.broadcast_to(g, (tb, LANES)) for g in gs], axis=1)
    gc_all = sum(_dot(tril16, piece) for piece in _split3(g_b))

    iw = lax.broadcasted_iota(jnp.int32, (C, tb), 0)
    jw = lax.broadcasted_iota(jnp.int32, (C, tb), 1) % C
    incl_w = iw >= jw
    strict_w = iw > jw
    eye_w = jnp.where(iw == jw, 1.0, 0.0).astype(F32)
    low_half = (lax.broadcasted_iota(jnp.int32, (C, LANES), 1) < C)

    def block_diag(wide16):
        return jnp.where(same, jnp.concatenate([wide16] * nc, axis=0), jnp.zeros((), BF16))

    def pair_blocks(full):
        return jnp.where(low_half, full[:C], full[C:])

    E = range(hp)
    pairs = range(tb // LANES)

    def l2n(x):
        return x * lax.rsqrt(jnp.sum(x * x, axis=-1, keepdims=True) + RMS_EPS)

    qs = [l2n(q_all[:, e * D:(e + 1) * D]) * (D ** -0.5) for e in E]
    ks = [l2n(k_all[:, e * D:(e + 1) * D]) for e in E]
    vs = [v_all[:, e * D:(e + 1) * D] for e in E]
    gcs = [gc_all[:, e * D:(e + 1) * D] for e in E]
    g_col = [jnp.concatenate([pair_blocks(gc[p * LANES:(p + 1) * LANES]) for p in pairs], axis=1)
             for gc in gcs]
    g_row = [jnp.concatenate([gc[p * LANES:(p + 1) * LANES, :].T[0:1, :] for p in pairs], axis=1)
             for gc in gcs]
    decay = [jnp.where(incl_w, jnp.exp(jnp.where(incl_w, g_col[e] - g_row[e], 0.0)), 0.0) for e in E]
    k16 = [k.astype(BF16) for k in ks]
    q16 = [q.astype(BF16) for q in qs]
    kb = [ks[e] * betas[e] for e in E]
    kb16 = [x.astype(BF16) for x in kb]

    def pair_products(a16, b16):
        return jnp.concatenate([pair_blocks(_dot_nt(a16[p * LANES:(p + 1) * LANES],
                                                    b16[p * LANES:(p + 1) * LANES])) for p in pairs], axis=1)

    lmat = [jnp.where(strict_w, pair_products(kb16[e], k16[e]) * decay[e], 0.0) for e in E]
    amat = [jnp.where(incl_w, pair_products(q16[e], k16[e]) * decay[e], 0.0) for e in E]
    p16 = [(-lmat[e]).astype(BF16) for e in E]
    tinv = [eye_w - lmat[e] for e in E]
    pw = [_dot(p16[e], block_diag(p16[e])) for e in E]
    for _ in range(4):
        p16 = [pw[e].astype(BF16) for e in E]
        prod = [_dot(jnp.concatenate([p16[e], tinv[e].astype(BF16)], axis=0), block_diag(p16[e])) for e in E]
        pw = [prod[e][:C] for e in E]
        tinv = [tinv[e] + prod[e][C:] for e in E]
    tinv = [tinv[e] + _dot(tinv[e].astype(BF16), block_diag(pw[e].astype(BF16))) for e in E]
    eg = [jnp.exp(gc) for gc in gcs]
    rhs = [jnp.concatenate([kb[e] * eg[e], vs[e] * betas[e]], axis=1).astype(BF16) for e in E]
    wu = [_dot(block_diag(tinv[e].astype(BF16)), rhs[e]).astype(BF16) for e in E]
    au = [_dot(block_diag(amat[e].astype(BF16)), wu[e]) for e in E]
    q_eff = [(qs[e] * eg[e] - au[e][:, :D]).astype(BF16) for e in E]
    gl = [jnp.concatenate([jnp.broadcast_to(gc[c * C + C - 1:c * C + C, :], (C, LANES))
                           for c in range(nc)], axis=0) for gc in gcs]
    k_dec = [(ks[e] * jnp.exp(gl[e] - gcs[e])).astype(BF16) for e in E]
    pn = [[_dot_tn(k_dec[e][c * C:(c + 1) * C], wu[e][c * C:(c + 1) * C]) for c in range(nc)]
          for e in E]

    states = [s_scr[e] for e in E]
    onw = onw_ref[...]
    for c in range(nc):
        r0 = c * C
        lhs = [jnp.concatenate([pn[e][c][:, :D].astype(BF16), q_eff[e][r0:r0 + C]], axis=0) for e in E]
        res = [_dot(lhs[e], states[e].astype(BF16)) for e in E]
        outs = [_rms_rows(res[e][D:] + au[e][r0:r0 + C, D:], onw) for e in E]
        states = [states[e] * jnp.exp(gl[e][r0:r0 + 1, :]) - res[e][:D] + pn[e][c][:, D:] for e in E]
        zc = z_ref[0, r0:r0 + C, :]
        o_ref[0, r0:r0 + C, :] = (jnp.concatenate(outs, axis=1) * _silu(zc)).astype(o_ref.dtype)
    for e in range(hp):
        s_scr[e] = states[e]
    for idx in range(3):
        tail_scr[idx] = tails[idx]


def _gdn(proj3, conv_w, alog_row, dtb_row, onw, col0, bd_col, tb):
    b, t, _ = proj3.shape
    hp = GDN_HEADS_PER_STEP
    ng = GDN_HEADS // hp
    w = hp * HEAD_DIM

    def col(base):
        return pl.BlockSpec((1, tb, w), lambda bi, hi, ti: (bi, ti, base // hp + hi))

    def cw(base):
        return pl.BlockSpec((GDN_CONV, w), lambda bi, hi, ti: (0, base // hp + hi))

    assert col0 % hp == 0
    row = pl.BlockSpec((1, LANES), lambda bi, hi, ti: (0, 0))
    return pl.pallas_call(
        functools.partial(_gdn_kernel, tb=tb),
        grid=(b, ng, t // tb),
        in_specs=[col(col0), col(col0 + GDN_HEADS), col(col0 + 2 * GDN_HEADS), col(col0 + 3 * GDN_HEADS),
                  pl.BlockSpec((1, tb, LANES), lambda bi, hi, ti: (bi, ti, bd_col)),
                  cw(0), cw(GDN_HEADS), cw(2 * GDN_HEADS), row, row, row],
        out_specs=pl.BlockSpec((1, tb, w), lambda bi, hi, ti: (bi, ti, hi)),
        out_shape=jax.ShapeDtypeStruct((b, t, GDN_HEADS * HEAD_DIM), BF16),
        scratch_shapes=[pltpu.VMEM((hp, HEAD_DIM, HEAD_DIM), F32),
                        pltpu.VMEM((3, 8, w), F32)],
        compiler_params=_cparams("parallel", "parallel", "arbitrary"),
        name="gdn",
    )(proj3, proj3, proj3, proj3, proj3, conv_w, conv_w, conv_w, alog_row, dtb_row, onw)


def _bias_kernel(rel_ref, o_ref):
    h = pl.program_id(0)
    d = pl.program_id(1)
    bs = MOBA_BLOCK
    max_exact = REL_BUCKETS // 2
    kk = lax.broadcasted_iota(jnp.int32, (bs, bs), 0)
    qq = lax.broadcasted_iota(jnp.int32, (bs, bs), 1)
    dist = jnp.maximum(d * bs + qq - kk, 0)
    df = dist.astype(F32)
    log_ratio = jnp.log(jnp.maximum(df, float(max_exact)) / max_exact) / math.log(REL_MAX_DIST / max_exact)
    large = max_exact + (log_ratio * (REL_BUCKETS - max_exact)).astype(jnp.int32)
    large = jnp.minimum(large, REL_BUCKETS - 1)
    bucket = jnp.where(dist < max_exact, dist, large)
    out = jnp.zeros((bs, bs), F32)
    for b in range(REL_BUCKETS):
        out = jnp.where(bucket == b, rel_ref[b, h], out)
    o_ref[0, 0] = out * LOG2E


def _bias_tiles(rel_bias):
    bs = MOBA_BLOCK
    return pl.pallas_call(
        _bias_kernel,
        grid=(MOBA_HEADS, BIAS_TILES),
        in_specs=[pl.BlockSpec(memory_space=pltpu.SMEM)],
        out_specs=pl.BlockSpec((1, 1, bs, bs), lambda h, d: (h, d, 0, 0)),
        out_shape=jax.ShapeDtypeStruct((MOBA_HEADS, BIAS_TILES, bs, bs), F32),
        compiler_params=_cparams("parallel", "parallel"),
        name="bias",
    )(rel_bias)


def _moba_prep_kernel(k_ref, v_ref, kw_ref, kn_ref, km_ref, vt_ref, *, rows):
    bs = MOBA_BLOCK
    kn = _rms_rows(k_ref[0], kw_ref[...])
    kn_ref[0, 0] = kn.astype(BF16)
    for s in range(rows // bs):
        km_ref[0, 0, s] = jnp.mean(kn[s * bs:(s + 1) * bs], axis=0, keepdims=True)
        vt_ref[0, 0, :, s * bs:(s + 1) * bs] = v_ref[0, s * bs:(s + 1) * bs, :].T.astype(BF16)


def _moba_prep(proj3, kw, col0, rows):
    b, t, _ = proj3.shape
    hh, bs, dh = MOBA_HEADS, MOBA_BLOCK, HEAD_DIM
    nb = t // bs

    def col(base):
        return pl.BlockSpec((1, rows, LANES), lambda bi, hi, ti: (bi, ti, base + hi))

    row = pl.BlockSpec((1, LANES), lambda bi, hi, ti: (0, 0))
    return pl.pallas_call(
        functools.partial(_moba_prep_kernel, rows=rows),
        grid=(b, hh, t // rows),
        in_specs=[col(col0 + hh), col(col0 + 2 * hh), row],
        out_specs=[pl.BlockSpec((1, 1, rows, dh), lambda bi, hi, ti: (bi, hi, ti, 0)),
                   pl.BlockSpec((1, 1, rows // bs, 1, dh), lambda bi, hi, ti: (bi, hi, ti, 0, 0)),
                   pl.BlockSpec((1, 1, dh, rows), lambda bi, hi, ti: (bi, hi, 0, ti))],
        out_shape=[jax.ShapeDtypeStruct((b, hh, t, dh), BF16),
                   jax.ShapeDtypeStruct((b, hh, nb, 1, dh), F32),
                   jax.ShapeDtypeStruct((b, hh, dh, t), BF16)],
        compiler_params=_cparams("parallel", "parallel", "parallel"),
        name="moba_prep",
    )(proj3, proj3, kw)


def _moba_kernel(q_ref, qw_ref, kn_ref, vt_ref, km_ref, bias_ref, o_ref,
                 sel_scr, buf_a, buf_b, *, nb):
    i = pl.program_id(2)
    bs = MOBA_BLOCK
    ch = MOBA_TILES_PER_STEP
    E = range(MOBA_STREAMS)
    qs = [_rms_rows(q_ref[e], qw_ref[...]) for e in E]
    q16 = [(q * ((HEAD_DIM ** -0.5) * LOG2E)).astype(BF16) for q in qs]

    blk = lax.broadcasted_iota(jnp.int32, (nb, bs), 0)
    blkf = blk.astype(F32)
    valid = blk < i
    work = [jnp.where(valid, _dot_nt(km_ref[e, 0], qs[e], HIGHEST), NEG_INF) for e in E]
    picked = [jnp.zeros((nb, bs), F32) for e in E]
    for _ in range(MOBA_TOPK):
        best = [jnp.max(work[e], axis=0, keepdims=True) for e in E]
        first = [jnp.min(jnp.where(work[e] == best[e], blkf, float(nb)), axis=0, keepdims=True) for e in E]
        hit = [blkf == first[e] for e in E]
        picked = [jnp.where(hit[e], 1.0, picked[e]) for e in E]
        work = [jnp.where(hit[e], -jnp.inf, work[e]) for e in E]
    for e in E:
        sel_scr[e] = jnp.where(valid, picked[e], 0.0)

    kk = lax.broadcasted_iota(jnp.int32, (bs, bs), 0)
    qq = lax.broadcasted_iota(jnp.int32, (bs, bs), 1)

    row0 = pl.multiple_of(i * bs, bs)
    own = [jnp.where(kk <= qq, _dot_nt(kn_ref[e, 0, pl.ds(row0, bs), :], q16[e]) + bias_ref[0, 0], NEG_INF)
           for e in E]
    m = [jnp.max(own[e], axis=0, keepdims=True) for e in E]
    own_p = [jnp.exp2(own[e] - m[e]) for e in E]
    l = [jnp.sum(own_p[e], axis=0, keepdims=True) for e in E]
    acc = [_dot(vt_ref[e, 0, :, pl.ds(row0, bs)], own_p[e].astype(BF16)) for e in E]

    crows = ch * bs
    n_pairs = (i + 2 * ch - 1) // (2 * ch)
    last_chunk = nb // ch - 1

    def chunk_scores(buf, e, c):
        r = pl.multiple_of(c * crows, crows)
        raw = _dot_nt(kn_ref[e, 0, pl.ds(r, crows), :], q16[e])
        mx = None
        for t in range(ch):
            j = c * ch + t
            d = jnp.clip(i - j, 0, BIAS_TILES - 1)
            st = raw[t * bs:(t + 1) * bs, :] + bias_ref[0, d]
            st = jnp.where(sel_scr[e, pl.ds(j, 1), :] > 0.0, st, NEG_INF)
            buf[e, t * bs:(t + 1) * bs, :] = st
            tmx = jnp.max(st, axis=0, keepdims=True)
            mx = tmx if mx is None else jnp.maximum(mx, tmx)
        return mx

    def absorb(buf, e, c, mx, m, l, acc):
        r = pl.multiple_of(c * crows, crows)
        m_new = jnp.maximum(m, mx)
        alpha = jnp.exp2(m - m_new)
        p = jnp.exp2(buf[e] - m_new)
        l = alpha * l + jnp.sum(p, axis=0, keepdims=True)
        acc = alpha * acc + _dot(vt_ref[e, 0, :, pl.ds(r, crows)], p.astype(BF16))
        return m_new, l, acc

    mx_a = tuple(chunk_scores(buf_a, e, 0) for e in E)

    def pair(c, carry):
        m, l, acc, mx_a = (list(x) for x in carry)
        mx_b = [chunk_scores(buf_b, e, 2 * c + 1) for e in E]
        for e in E:
            m[e], l[e], acc[e] = absorb(buf_a, e, 2 * c, mx_a[e], m[e], l[e], acc[e])
        mx_a = [chunk_scores(buf_a, e, jnp.minimum(2 * c + 2, last_chunk)) for e in E]
        for e in E:
            m[e], l[e], acc[e] = absorb(buf_b, e, 2 * c + 1, mx_b[e], m[e], l[e], acc[e])
        return tuple(m), tuple(l), tuple(acc), tuple(mx_a)

    m, l, acc, _ = lax.fori_loop(0, n_pairs, pair, (tuple(m), tuple(l), tuple(acc), mx_a))
    for e in E:
        o_ref[e] = (acc[e] / l[e]).T.astype(o_ref.dtype)


def _moba(proj3, qw, kn, vt, km, bias, col0):
    b, hh, t, dh = kn.shape
    bs = MOBA_BLOCK
    nb = t // bs
    ns = MOBA_STREAMS
    crows = MOBA_TILES_PER_STEP * bs
    assert b % ns == 0 and nb % (2 * MOBA_TILES_PER_STEP) == 0
    return pl.pallas_call(
        functools.partial(_moba_kernel, nb=nb),
        grid=(b // ns, hh, nb),
        in_specs=[pl.BlockSpec((ns, bs, dh), lambda bi, hi, ti: (bi, ti, col0 + hi)),
                  pl.BlockSpec((1, dh), lambda bi, hi, ti: (0, 0)),
                  pl.BlockSpec((ns, 1, t, dh), lambda bi, hi, ti: (bi, hi, 0, 0)),
                  pl.BlockSpec((ns, 1, dh, t), lambda bi, hi, ti: (bi, hi, 0, 0)),
                  pl.BlockSpec((ns, 1, nb, dh), lambda bi, hi, ti: (bi, hi, 0, 0)),
                  pl.BlockSpec((1, BIAS_TILES, bs, bs), lambda bi, hi, ti: (hi, 0, 0, 0))],
        out_specs=pl.BlockSpec((ns, bs, dh), lambda bi, hi, ti: (bi, ti, hi)),
        out_shape=jax.ShapeDtypeStruct((b, t, hh * dh), BF16),
        scratch_shapes=[pltpu.VMEM((ns, nb, bs), F32),
                        pltpu.VMEM((ns, crows, bs), F32),
                        pltpu.VMEM((ns, crows, bs), F32)],
        compiler_params=_cparams("parallel", "parallel", "parallel"),
        name="moba",
    )(proj3, qw, kn, vt, km, bias)


def _tile(n, want):
    t = min(n, want)
    assert n % t == 0, (n, want)
    return t


def kernel(x, norm_mix_w, w_in, conv_w, a_log, dt_bias, gdn_o_norm_w, q_norm_w, k_norm_w, rel_bias,
           w_branch_gdn, w_branch_moba, w_out, norm_ffn_w, w_ffn_gate, w_ffn_up, w_ffn_down):
    b, t, d = x.shape
    m = b * t
    gw = GDN_HEADS * HEAD_DIM
    mw = MOBA_HEADS * HEAD_DIM
    assert t % MOBA_PREP_ROWS == 0 and w_in.shape[0] == 1
    assert w_in.shape[2] == 4 * gw + 2 * GDN_HEADS + 3 * mw + 2 * d

    wi = w_in[0]
    o_bd = 4 * gw
    o_moba = o_bd + 2 * GDN_HEADS
    o_gate = o_moba + 3 * mw
    n_main = 2 * d + 4 * gw + 3 * mw
    tn_proj = 1280
    n_proj = -(-(n_main + LANES) // tn_proj) * tn_proj
    w_proj = jnp.concatenate(
        [wi[:, o_gate:], wi[:, :o_bd], wi[:, o_moba:o_gate], wi[:, o_bd:o_moba],
         jnp.zeros((d, n_proj - n_main - 2 * GDN_HEADS), wi.dtype)], axis=1).astype(BF16)
    col_gdn = (2 * d) // LANES
    col_moba = col_gdn + 4 * GDN_HEADS
    col_bd = col_moba + 3 * MOBA_HEADS

    x2 = x.reshape(m, d)
    proj = _norm_matmul(x2, norm_mix_w, w_proj, _tile(m, 1024), tn_proj, F32, "proj")
    proj3 = proj.reshape(b, t, n_proj)

    lane_pad = jnp.zeros((1, LANES - 2 * GDN_HEADS), F32)
    head_pad = jnp.zeros((1, GDN_HEADS), F32)
    alog_row = jnp.concatenate([head_pad, a_log, lane_pad], axis=1)
    dtb_row = jnp.concatenate([head_pad, dt_bias, lane_pad], axis=1)
    y_a = _gdn(proj3, conv_w[0], alog_row, dtb_row, gdn_o_norm_w, col_gdn, col_bd, _tile(t, 256))

    bias = _bias_tiles(rel_bias)
    kn, km, vt = _moba_prep(proj3, k_norm_w, col_moba, MOBA_PREP_ROWS)
    y_b = _moba(proj3, q_norm_w, kn, vt, km.reshape(b, MOBA_HEADS, t // MOBA_BLOCK, HEAD_DIM), bias,
                col_moba)

    mix = _mix(y_a.reshape(m, gw), y_b.reshape(m, mw), proj,
               w_branch_gdn[0].astype(BF16), w_branch_moba[0].astype(BF16), _tile(m, 512), d)
    h1 = _matmul_res(mix, w_out[0].astype(BF16), x2, _tile(m, 512), d, "out")

    hid = _ffn_up(h1, norm_ffn_w, w_ffn_gate[0].astype(BF16), w_ffn_up[0].astype(BF16),
                  _tile(m, 1024), 512)
    h2 = _matmul_res(hid, w_ffn_down[0].astype(BF16), h1, _tile(m, 1024), 512, "ffn_down")
    return h2.reshape(b, t, d)
```

```python
import functools
import math

import jax
import jax.numpy as jnp
from jax import lax
from jax.experimental import pallas as pl
from jax.experimental.pallas import tpu as pltpu

F32 = jnp.float32
BF16 = jnp.bfloat16
HIGHEST = lax.Precision.HIGHEST

LANES = 128
HEAD_DIM = 128
GDN_HEADS = 8
GDN_CONV = 4
GDN_CHUNK = 64
GDN_HEADS_PER_STEP = 4
MOBA_HEADS = 8
MOBA_BLOCK = 256
MOBA_TOPK = 3
REL_BUCKETS = 32
REL_MAX_DIST = 2048
BIAS_TILES = 8
MOBA_TILES_PER_STEP = 2
MOBA_PREP_ROWS = 1024
MOBA_STREAMS = 4
MOBA_V_ROWS = HEAD_DIM + 16
LOG2E = math.log2(math.e)
RMS_EPS = 1e-6
NEG_INF = -1e30
VMEM_LIMIT = 56 * 1024 * 1024


def _cparams(*sem):
    return pltpu.CompilerParams(dimension_semantics=sem, vmem_limit_bytes=VMEM_LIMIT)


def _sigmoid(x):
    return 1.0 / (1.0 + jnp.exp(-x))


def _silu(x):
    h = 0.5 * x
    return h + h * jnp.tanh(h)


def _dot(a, b, precision=None):
    return jnp.dot(a, b, preferred_element_type=F32, precision=precision)


def _dot_nt(a, b, precision=None):
    return lax.dot_general(a, b, (((1,), (1,)), ((), ())),
                           preferred_element_type=F32, precision=precision)


def _dot_tn(a, b):
    return lax.dot_general(a, b, (((0,), (0,)), ((), ())), preferred_element_type=F32)


def _rms_rows(x, w):
    return x * lax.rsqrt(jnp.mean(x * x, axis=-1, keepdims=True) + RMS_EPS) * w


def _norm_matmul_kernel(x_ref, nw_ref, w_ref, o_ref, u_scr):
    @pl.when(pl.program_id(1) == 0)
    def _():
        u_scr[...] = _rms_rows(x_ref[...], nw_ref[...]).astype(BF16)

    o_ref[...] = _dot(u_scr[...], w_ref[...]).astype(o_ref.dtype)


def _norm_matmul(x, nw, w, tm, tn, out_dtype, name):
    m, k = x.shape
    n = w.shape[1]
    return pl.pallas_call(
        _norm_matmul_kernel,
        grid=(m // tm, n // tn),
        in_specs=[pl.BlockSpec((tm, k), lambda i, j: (i, 0)),
                  pl.BlockSpec((1, k), lambda i, j: (0, 0)),
                  pl.BlockSpec((k, tn), lambda i, j: (0, j))],
        out_specs=pl.BlockSpec((tm, tn), lambda i, j: (i, j)),
        out_shape=jax.ShapeDtypeStruct((m, n), out_dtype),
        scratch_shapes=[pltpu.VMEM((tm, k), BF16)],
        compiler_params=_cparams("parallel", "arbitrary"),
        name=name,
    )(x, nw, w)


def _matmul_res_kernel(a_ref, w_ref, r_ref, o_ref):
    o_ref[...] = r_ref[...] + _dot(a_ref[...], w_ref[...])


def _matmul_res(a, w, res, tm, tn, name):
    m, k = a.shape
    n = w.shape[1]
    return pl.pallas_call(
        _matmul_res_kernel,
        grid=(m // tm, n // tn),
        in_specs=[pl.BlockSpec((tm, k), lambda i, j: (i, 0)),
                  pl.BlockSpec((k, tn), lambda i, j: (0, j)),
                  pl.BlockSpec((tm, tn), lambda i, j: (i, j))],
        out_specs=pl.BlockSpec((tm, tn), lambda i, j: (i, j)),
        out_shape=jax.ShapeDtypeStruct((m, n), F32),
        compiler_params=_cparams("parallel", "parallel"),
        name=name,
    )(a, w, res)


def _ffn_up_kernel(x_ref, nw_ref, wg_ref, wu_ref, o_ref, u_scr):
    @pl.when(pl.program_id(1) == 0)
    def _():
        u_scr[...] = _rms_rows(x_ref[...], nw_ref[...]).astype(BF16)

    u = u_scr[...]
    g = _dot(u, wg_ref[...])
    o_ref[...] = (g * _sigmoid(g) * _dot(u, wu_ref[...])).astype(o_ref.dtype)


def _ffn_up(x, nw, wg, wu, tm, tn):
    m, k = x.shape
    n = wg.shape[1]
    return pl.pallas_call(
        _ffn_up_kernel,
        grid=(m // tm, n // tn),
        in_specs=[pl.BlockSpec((tm, k), lambda i, j: (i, 0)),
                  pl.BlockSpec((1, k), lambda i, j: (0, 0)),
                  pl.BlockSpec((k, tn), lambda i, j: (0, j)),
                  pl.BlockSpec((k, tn), lambda i, j: (0, j))],
        out_specs=pl.BlockSpec((tm, tn), lambda i, j: (i, j)),
        out_shape=jax.ShapeDtypeStruct((m, n), BF16),
        scratch_shapes=[pltpu.VMEM((tm, k), BF16)],
        compiler_params=_cparams("parallel", "arbitrary"),
        name="ffn_up",
    )(x, nw, wg, wu)


def _mix_kernel(ya_ref, yb_ref, ga_ref, gb_ref, wa_ref, wb_ref, o_ref):
    a = _dot(ya_ref[...], wa_ref[...])
    b = _dot(yb_ref[...], wb_ref[...])
    o_ref[...] = (_sigmoid(ga_ref[...]) * a + _sigmoid(gb_ref[...]) * b).astype(o_ref.dtype)


def _mix(ya, yb, proj, wa, wb, tm, tn):
    m, k = ya.shape
    n = wa.shape[1]
    nb = n // tn
    return pl.pallas_call(
        _mix_kernel,
        grid=(m // tm, nb),
        in_specs=[pl.BlockSpec((tm, k), lambda i, j: (i, 0)),
                  pl.BlockSpec((tm, k), lambda i, j: (i, 0)),
                  pl.BlockSpec((tm, tn), lambda i, j: (i, j)),
                  pl.BlockSpec((tm, tn), lambda i, j: (i, nb + j)),
                  pl.BlockSpec((k, tn), lambda i, j: (0, j)),
                  pl.BlockSpec((k, tn), lambda i, j: (0, j))],
        out_specs=pl.BlockSpec((tm, tn), lambda i, j: (i, j)),
        out_shape=jax.ShapeDtypeStruct((m, n), BF16),
        compiler_params=_cparams("parallel", "parallel"),
        name="mix",
    )(ya, yb, proj, proj, wa, wb)


def _split3(x):
    a = x.astype(BF16)
    r = x - a.astype(F32)
    b = r.astype(BF16)
    c = (r - b.astype(F32)).astype(BF16)
    return a, b, c


def _gdn_kernel(q_ref, k_ref, v_ref, z_ref, bd_ref, cwq_ref, cwk_ref, cwv_ref,
                alog_ref, dtb_ref, onw_ref, o_ref, s_scr, tail_scr, *, tb):
    hp = GDN_HEADS_PER_STEP
    C = GDN_CHUNK
    D = HEAD_DIM
    head0 = pl.program_id(1) * hp

    @pl.when(pl.program_id(2) == 0)
    def _():
        s_scr[...] = jnp.zeros_like(s_scr)
        tail_scr[...] = jnp.zeros_like(tail_scr)

    row8 = lax.broadcasted_iota(jnp.int32, (8, hp * D), 0)
    tails = []

    def conv_silu(u_ref, cw_ref, idx):
        u = u_ref[0]
        w = cw_ref[...]
        tail = tail_scr[idx]
        y = None
        ytop = None
        for s in (3, 2, 1):
            sh = pltpu.roll(u, s, axis=0)
            top = jnp.where(row8 < s, pltpu.roll(tail, s, axis=0), sh[0:8])
            wj = w[3 - s:4 - s]
            y = sh * wj if y is None else y + sh * wj
            ytop = top * wj if ytop is None else ytop + top * wj
        y = y + u * w[3:4]
        ytop = ytop + u[0:8] * w[3:4]
        tails.append(u[tb - 8:tb])
        y = jnp.concatenate([ytop, y[8:]], axis=0)
        return _silu(y)

    q_all = conv_silu(q_ref, cwq_ref, 0)
    k_all = conv_silu(k_ref, cwk_ref, 1)
    v_all = conv_silu(v_ref, cwv_ref, 2)

    bd = bd_ref[0]
    lane = lax.broadcasted_iota(jnp.int32, (tb, LANES), 1)
    beta_all = _sigmoid(bd)
    xg = bd + dtb_ref[...]
    softplus = jnp.maximum(xg, 0.0) + jnp.log1p(jnp.exp(-jnp.abs(xg)))
    g_all = -jnp.exp(alog_ref[...]) * softplus
    betas = [jnp.sum(jnp.where(lane == head0 + e, beta_all, 0.0), axis=-1, keepdims=True)
             for e in range(hp)]
    gs = [jnp.sum(jnp.where(lane == head0 + e + GDN_HEADS, g_all, 0.0), axis=-1, keepdims=True)
          for e in range(hp)]

    nc = tb // C
    ri = lax.broadcasted_iota(jnp.int32, (tb, tb), 0)
    ci = lax.broadcasted_iota(jnp.int32, (tb, tb), 1)
    same = (ri // C) == (ci // C)
    tril16 = jnp.where(same & (ri >= ci), 1.0, 0.0).astype(BF16)
    g_b = jnp.concatenate([jnp.broadcast_to(g, (tb, LANES)) for g in gs], axis=1)
    gc_all = sum(_dot(tril16, piece) for piece in _split3(g_b))

    iw = lax.broadcasted_iota(jnp.int32, (C, tb), 0)
    jw = lax.broadcasted_iota(jnp.int32, (C, tb), 1) % C
    incl_w = iw >= jw
    strict_w = iw > jw
    eye_w = jnp.where(iw == jw, 1.0, 0.0).astype(F32)
    low_half = (lax.broadcasted_iota(jnp.int32, (C, LANES), 1) < C)

    def block_diag(wide16):
        return jnp.where(same, jnp.concatenate([wide16] * nc, axis=0), jnp.zeros((), BF16))

    def pair_blocks(full):
        return jnp.where(low_half, full[:C], full[C:])

    E = range(hp)
    pairs = range(tb // LANES)

    def l2n(x):
        return x * lax.rsqrt(jnp.sum(x * x, axis=-1, keepdims=True) + RMS_EPS)

    qs = [l2n(q_all[:, e * D:(e + 1) * D]) * (D ** -0.5) for e in E]
    ks = [l2n(k_all[:, e * D:(e + 1) * D]) for e in E]
    vs = [v_all[:, e * D:(e + 1) * D] for e in E]
    gcs = [gc_all[:, e * D:(e + 1) * D] for e in E]
    g_col = [jnp.concatenate([pair_blocks(gc[p * LANES:(p + 1) * LANES]) for p in pairs], axis=1)
             for gc in gcs]
    g_row = [jnp.concatenate([gc[p * LANES:(p + 1) * LANES, :].T[0:1, :] for p in pairs], axis=1)
             for gc in gcs]
    decay = [jnp.where(incl_w, jnp.exp(jnp.where(incl_w, g_col[e] - g_row[e], 0.0)), 0.0) for e in E]
    k16 = [k.astype(BF16) for k in ks]
    q16 = [q.astype(BF16) for q in qs]
    kb = [ks[e] * betas[e] for e in E]
    kb16 = [x.astype(BF16) for x in kb]

    def pair_products(a16, b16):
        return jnp.concatenate([pair_blocks(_dot_nt(a16[p * LANES:(p + 1) * LANES],
                                                    b16[p * LANES:(p + 1) * LANES])) for p in pairs], axis=1)

    lmat = [jnp.where(strict_w, pair_products(kb16[e], k16[e]) * decay[e], 0.0) for e in E]
    amat = [jnp.where(incl_w, pair_products(q16[e], k16[e]) * decay[e], 0.0) for e in E]
    p16 = [(-lmat[e]).astype(BF16) for e in E]
    tinv = [eye_w - lmat[e] for e in E]
    pw = [_dot(p16[e], block_diag(p16[e])) for e in E]
    for _ in range(4):
        p16 = [pw[e].astype(BF16) for e in E]
        prod = [_dot(jnp.concatenate([p16[e], tinv[e].astype(BF16)], axis=0), block_diag(p16[e])) for e in E]
        pw = [prod[e][:C] for e in E]
        tinv = [tinv[e] + prod[e][C:] for e in E]
    tinv = [tinv[e] + _dot(tinv[e].astype(BF16), block_diag(pw[e].astype(BF16))) for e in E]
    eg = [jnp.exp(gc) for gc in gcs]
    rhs = [jnp.concatenate([kb[e] * eg[e], vs[e] * betas[e]], axis=1).astype(BF16) for e in E]
    wu = [_dot(block_diag(tinv[e].astype(BF16)), rhs[e]).astype(BF16) for e in E]
    au = [_dot(block_diag(amat[e].astype(BF16)), wu[e]) for e in E]
    q_eff = [(qs[e] * eg[e] - au[e][:, :D]).astype(BF16) for e in E]
    gl = [jnp.concatenate([jnp.broadcast_to(gc[c * C + C - 1:c * C + C, :], (C, LANES))
                           for c in range(nc)], axis=0) for gc in gcs]
    k_dec = [(ks[e] * jnp.exp(gl[e] - gcs[e])).astype(BF16) for e in E]
    pn = [[_dot_tn(k_dec[e][c * C:(c + 1) * C], wu[e][c * C:(c + 1) * C]) for c in range(nc)]
          for e in E]

    states = [s_scr[e] for e in E]
    onw = onw_ref[...]
    for c in range(nc):
        r0 = c * C
        lhs = [jnp.concatenate([pn[e][c][:, :D].astype(BF16), q_eff[e][r0:r0 + C]], axis=0) for e in E]
        res = [_dot(lhs[e], states[e].astype(BF16)) for e in E]
        outs = [_rms_rows(res[e][D:] + au[e][r0:r0 + C, D:], onw) for e in E]
        states = [states[e] * jnp.exp(gl[e][r0:r0 + 1, :]) - res[e][:D] + pn[e][c][:, D:] for e in E]
        zc = z_ref[0, r0:r0 + C, :]
        o_ref[0, r0:r0 + C, :] = (jnp.concatenate(outs, axis=1) * _silu(zc)).astype(o_ref.dtype)
    for e in range(hp):
        s_scr[e] = states[e]
    for idx in range(3):
        tail_scr[idx] = tails[idx]


def _gdn(proj3, conv_w, alog_row, dtb_row, onw, col0, bd_col, tb):
    b, t, _ = proj3.shape
    hp = GDN_HEADS_PER_STEP
    ng = GDN_HEADS // hp
    w = hp * HEAD_DIM

    def col(base):
        return pl.BlockSpec((1, tb, w), lambda bi, hi, ti: (bi, ti, base // hp + hi))

    def cw(base):
        return pl.BlockSpec((GDN_CONV, w), lambda bi, hi, ti: (0, base // hp + hi))

    assert col0 % hp == 0
    row = pl.BlockSpec((1, LANES), lambda bi, hi, ti: (0, 0))
    return pl.pallas_call(
        functools.partial(_gdn_kernel, tb=tb),
        grid=(b, ng, t // tb),
        in_specs=[col(col0), col(col0 + GDN_HEADS), col(col0 + 2 * GDN_HEADS), col(col0 + 3 * GDN_HEADS),
                  pl.BlockSpec((1, tb, LANES), lambda bi, hi, ti: (bi, ti, bd_col)),
                  cw(0), cw(GDN_HEADS), cw(2 * GDN_HEADS), row, row, row],
        out_specs=pl.BlockSpec((1, tb, w), lambda bi, hi, ti: (bi, ti, hi)),
        out_shape=jax.ShapeDtypeStruct((b, t, GDN_HEADS * HEAD_DIM), BF16),
        scratch_shapes=[pltpu.VMEM((hp, HEAD_DIM, HEAD_DIM), F32),
                        pltpu.VMEM((3, 8, w), F32)],
        compiler_params=_cparams("parallel", "parallel", "arbitrary"),
        name="gdn",
    )(proj3, proj3, proj3, proj3, proj3, conv_w, conv_w, conv_w, alog_row, dtb_row, onw)


def _bias_kernel(rel_ref, o_ref):
    h = pl.program_id(0)
    d = pl.program_id(1)
    bs = MOBA_BLOCK
    max_exact = REL_BUCKETS // 2
    kk = lax.broadcasted_iota(jnp.int32, (bs, bs), 0)
    qq = lax.broadcasted_iota(jnp.int32, (bs, bs), 1)
    dist = jnp.maximum(d * bs + qq - kk, 0)
    df = dist.astype(F32)
    log_ratio = jnp.log(jnp.maximum(df, float(max_exact)) / max_exact) / math.log(REL_MAX_DIST / max_exact)
    large = max_exact + (log_ratio * (REL_BUCKETS - max_exact)).astype(jnp.int32)
    large = jnp.minimum(large, REL_BUCKETS - 1)
    bucket = jnp.where(dist < max_exact, dist, large)
    out = jnp.zeros((bs, bs), F32)
    for b in range(REL_BUCKETS):
        out = jnp.where(bucket == b, rel_ref[b, h], out)
    o_ref[0, 0] = out * LOG2E


def _bias_tiles(rel_bias):
    bs = MOBA_BLOCK
    return pl.pallas_call(
        _bias_kernel,
        grid=(MOBA_HEADS, BIAS_TILES),
        in_specs=[pl.BlockSpec(memory_space=pltpu.SMEM)],
        out_specs=pl.BlockSpec((1, 1, bs, bs), lambda h, d: (h, d, 0, 0)),
        out_shape=jax.ShapeDtypeStruct((MOBA_HEADS, BIAS_TILES, bs, bs), F32),
        compiler_params=_cparams("parallel", "parallel"),
        name="bias",
    )(rel_bias)


def _moba_prep_kernel(q_ref, k_ref, v_ref, qw_ref, kw_ref, q16_ref, kn_ref, vt_ref, sel_ref, km_scr,
                      *, rows, nb):
    bs = MOBA_BLOCK
    nsub = rows // bs
    ti = pl.program_id(2)

    @pl.when(ti == 0)
    def _():
        km_scr[...] = jnp.zeros_like(km_scr)

    kn = _rms_rows(k_ref[0], kw_ref[...])
    kn_ref[0, 0] = kn.astype(BF16)
    for s in range(nsub):
        km_scr[pl.ds(ti * nsub + s, 1), :] = jnp.mean(kn[s * bs:(s + 1) * bs], axis=0, keepdims=True)
        vt_ref[0, 0, 0:HEAD_DIM, s * bs:(s + 1) * bs] = v_ref[0, s * bs:(s + 1) * bs, :].T.astype(BF16)
    pad_row = lax.broadcasted_iota(jnp.int32, (MOBA_V_ROWS - HEAD_DIM, rows), 0)
    vt_ref[0, 0, HEAD_DIM:, :] = jnp.where(pad_row == 0, 1.0, 0.0).astype(BF16)

    q = _rms_rows(q_ref[0], qw_ref[...])
    q16_ref[0, 0] = (q * ((HEAD_DIM ** -0.5) * LOG2E)).astype(BF16)

    blk = lax.broadcasted_iota(jnp.int32, (nb, rows), 0)
    own_blk = ti * nsub + lax.broadcasted_iota(jnp.int32, (nb, rows), 1) // bs
    blkf = blk.astype(F32)
    valid = blk < own_blk
    work = jnp.where(valid, _dot_nt(km_scr[...], q, HIGHEST), NEG_INF)
    picked = jnp.zeros((nb, rows), F32)
    for _ in range(MOBA_TOPK):
        best = jnp.max(work, axis=0, keepdims=True)
        first = jnp.min(jnp.where(work == best, blkf, float(nb)), axis=0, keepdims=True)
        hit = blkf == first
        picked = jnp.where(hit, 1.0, picked)
        work = jnp.where(hit, -jnp.inf, work)
    sel_ref[0, 0] = jnp.where(valid, picked, 0.0)


def _moba_prep(proj3, qw, kw, col0, rows):
    b, t, _ = proj3.shape
    hh, bs, dh = MOBA_HEADS, MOBA_BLOCK, HEAD_DIM
    nb = t // bs

    def col(base):
        return pl.BlockSpec((1, rows, LANES), lambda bi, hi, ti: (bi, ti, base + hi))

    row = pl.BlockSpec((1, LANES), lambda bi, hi, ti: (0, 0))
    return pl.pallas_call(
        functools.partial(_moba_prep_kernel, rows=rows, nb=nb),
        grid=(b, hh, t // rows),
        in_specs=[col(col0), col(col0 + hh), col(col0 + 2 * hh), row, row],
        out_specs=[pl.BlockSpec((1, 1, rows, dh), lambda bi, hi, ti: (bi, hi, ti, 0)),
                   pl.BlockSpec((1, 1, rows, dh), lambda bi, hi, ti: (bi, hi, ti, 0)),
                   pl.BlockSpec((1, 1, MOBA_V_ROWS, rows), lambda bi, hi, ti: (bi, hi, 0, ti)),
                   pl.BlockSpec((1, 1, nb, rows), lambda bi, hi, ti: (bi, hi, 0, ti))],
        out_shape=[jax.ShapeDtypeStruct((b, hh, t, dh), BF16),
                   jax.ShapeDtypeStruct((b, hh, t, dh), BF16),
                   jax.ShapeDtypeStruct((b, hh, MOBA_V_ROWS, t), BF16),
                   jax.ShapeDtypeStruct((b, hh, nb, t), F32)],
        scratch_shapes=[pltpu.VMEM((nb, dh), F32)],
        compiler_params=_cparams("parallel", "parallel", "arbitrary"),
        name="moba_prep",
    )(proj3, proj3, proj3, qw, kw)


def _moba_kernel(q_ref, kn_ref, vt_ref, sel_ref, bias_ref, o_ref, buf_a, buf_b, *, nb):
    i = pl.program_id(2)
    bs = MOBA_BLOCK
    ch = MOBA_TILES_PER_STEP
    E = range(MOBA_STREAMS)
    q16 = [q_ref[e, 0] for e in E]

    kk = lax.broadcasted_iota(jnp.int32, (bs, bs), 0)
    qq = lax.broadcasted_iota(jnp.int32, (bs, bs), 1)

    row0 = pl.multiple_of(i * bs, bs)
    own = [jnp.where(kk <= qq, _dot_nt(kn_ref[e, 0, pl.ds(row0, bs), :], q16[e]) + bias_ref[0, 0], NEG_INF)
           for e in E]
    m = [jnp.max(own[e], axis=0, keepdims=True) for e in E]
    acc = [_dot(vt_ref[e, 0, :, pl.ds(row0, bs)], jnp.exp2(own[e] - m[e]).astype(BF16)) for e in E]

    crows = ch * bs
    n_pairs = (i + 2 * ch - 1) // (2 * ch)
    last_chunk = nb // ch - 1

    def chunk_scores(buf, e, c):
        r = pl.multiple_of(c * crows, crows)
        raw = _dot_nt(kn_ref[e, 0, pl.ds(r, crows), :], q16[e])
        mx = None
        for t in range(ch):
            j = c * ch + t
            d = jnp.clip(i - j, 0, BIAS_TILES - 1)
            st = raw[t * bs:(t + 1) * bs, :] + bias_ref[0, d]
            st = jnp.where(sel_ref[e, 0, pl.ds(j, 1), :] > 0.0, st, NEG_INF)
            buf[e, t * bs:(t + 1) * bs, :] = st
            tmx = jnp.max(st, axis=0, keepdims=True)
            mx = tmx if mx is None else jnp.maximum(mx, tmx)
        return mx

    def absorb(buf, e, c, mx, m, acc):
        r = pl.multiple_of(c * crows, crows)
        m_new = jnp.maximum(m, mx)
        p = jnp.exp2(buf[e] - m_new)
        acc = jnp.exp2(m - m_new) * acc + _dot(vt_ref[e, 0, :, pl.ds(r, crows)], p.astype(BF16))
        return m_new, acc

    mx_a = tuple(chunk_scores(buf_a, e, 0) for e in E)

    def pair(c, carry):
        m, acc, mx_a = (list(x) for x in carry)
        mx_b = [chunk_scores(buf_b, e, 2 * c + 1) for e in E]
        for e in E:
            m[e], acc[e] = absorb(buf_a, e, 2 * c, mx_a[e], m[e], acc[e])
        mx_a = [chunk_scores(buf_a, e, jnp.minimum(2 * c + 2, last_chunk)) for e in E]
        for e in E:
            m[e], acc[e] = absorb(buf_b, e, 2 * c + 1, mx_b[e], m[e], acc[e])
        return tuple(m), tuple(acc), tuple(mx_a)

    m, acc, _ = lax.fori_loop(0, n_pairs, pair, (tuple(m), tuple(acc), mx_a))
    for e in E:
        o_ref[e] = (acc[e][:HEAD_DIM] / acc[e][HEAD_DIM:HEAD_DIM + 1]).T.astype(o_ref.dtype)


def _moba(q16, kn, vt, sel, bias):
    b, hh, t, dh = kn.shape
    bs = MOBA_BLOCK
    nb = t // bs
    ns = MOBA_STREAMS
    crows = MOBA_TILES_PER_STEP * bs
    assert b % ns == 0 and nb % (2 * MOBA_TILES_PER_STEP) == 0
    return pl.pallas_call(
        functools.partial(_moba_kernel, nb=nb),
        grid=(b // ns, hh, nb),
        in_specs=[pl.BlockSpec((ns, 1, bs, dh), lambda bi, hi, ti: (bi, hi, ti, 0)),
                  pl.BlockSpec((ns, 1, t, dh), lambda bi, hi, ti: (bi, hi, 0, 0)),
                  pl.BlockSpec((ns, 1, MOBA_V_ROWS, t), lambda bi, hi, ti: (bi, hi, 0, 0)),
                  pl.BlockSpec((ns, 1, nb, bs), lambda bi, hi, ti: (bi, hi, 0, ti)),
                  pl.BlockSpec((1, BIAS_TILES, bs, bs), lambda bi, hi, ti: (hi, 0, 0, 0))],
        out_specs=pl.BlockSpec((ns, bs, dh), lambda bi, hi, ti: (bi, ti, hi)),
        out_shape=jax.ShapeDtypeStruct((b, t, hh * dh), BF16),
        scratch_shapes=[pltpu.VMEM((ns, crows, bs), F32),
                        pltpu.VMEM((ns, crows, bs), F32)],
        compiler_params=_cparams("parallel", "parallel", "parallel"),
        name="moba",
    )(q16, kn, vt, sel, bias)


def _tile(n, want):
    t = min(n, want)
    assert n % t == 0, (n, want)
    return t


def kernel(x, norm_mix_w, w_in, conv_w, a_log, dt_bias, gdn_o_norm_w, q_norm_w, k_norm_w, rel_bias,
           w_branch_gdn, w_branch_moba, w_out, norm_ffn_w, w_ffn_gate, w_ffn_up, w_ffn_down):
    b, t, d = x.shape
    m = b * t
    gw = GDN_HEADS * HEAD_DIM
    mw = MOBA_HEADS * HEAD_DIM
    assert t % MOBA_PREP_ROWS == 0 and w_in.shape[0] == 1
    assert w_in.shape[2] == 4 * gw + 2 * GDN_HEADS + 3 * mw + 2 * d

    wi = w_in[0]
    o_bd = 4 * gw
    o_moba = o_bd + 2 * GDN_HEADS
    o_gate = o_moba + 3 * mw
    n_main = 2 * d + 4 * gw + 3 * mw
    tn_proj = 1280
    n_proj = -(-(n_main + LANES) // tn_proj) * tn_proj
    w_proj = jnp.concatenate(
        [wi[:, o_gate:], wi[:, :o_bd], wi[:, o_moba:o_gate], wi[:, o_bd:o_moba],
         jnp.zeros((d, n_proj - n_main - 2 * GDN_HEADS), wi.dtype)], axis=1).astype(BF16)
    col_gdn = (2 * d) // LANES
    col_moba = col_gdn + 4 * GDN_HEADS
    col_bd = col_moba + 3 * MOBA_HEADS

    x2 = x.reshape(m, d)
    proj = _norm_matmul(x2, norm_mix_w, w_proj, _tile(m, 1024), tn_proj, F32, "proj")
    proj3 = proj.reshape(b, t, n_proj)

    lane_pad = jnp.zeros((1, LANES - 2 * GDN_HEADS), F32)
    head_pad = jnp.zeros((1, GDN_HEADS), F32)
    alog_row = jnp.concatenate([head_pad, a_log, lane_pad], axis=1)
    dtb_row = jnp.concatenate([head_pad, dt_bias, lane_pad], axis=1)
    y_a = _gdn(proj3, conv_w[0], alog_row, dtb_row, gdn_o_norm_w, col_gdn, col_bd, _tile(t, 256))

    bias = _bias_tiles(rel_bias)
    q16, kn, vt, sel = _moba_prep(proj3, q_norm_w, k_norm_w, col_moba, MOBA_PREP_ROWS)
    y_b = _moba(q16, kn, vt, sel, bias)

    mix = _mix(y_a.reshape(m, gw), y_b.reshape(m, mw), proj,
               w_branch_gdn[0].astype(BF16), w_branch_moba[0].astype(BF16), _tile(m, 512), d)
    h1 = _matmul_res(mix, w_out[0].astype(BF16), x2, _tile(m, 512), d, "out")

    hid = _ffn_up(h1, norm_ffn_w, w_ffn_gate[0].astype(BF16), w_ffn_up[0].astype(BF16),
                  _tile(m, 1024), 512)
    h2 = _matmul_res(hid, w_ffn_down[0].astype(BF16), h1, _tile(m, 1024), 512, "ffn_down")
    return h2.reshape(b, t, d)
```

```python
import functools
import math

import jax
import jax.numpy as jnp
from jax import lax
from jax.experimental import pallas as pl
from jax.experimental.pallas import tpu as pltpu

F32 = jnp.float32
BF16 = jnp.bfloat16
HIGHEST = lax.Precision.HIGHEST

LANES = 128
HEAD_DIM = 128
GDN_HEADS = 8
GDN_CONV = 4
GDN_CHUNK = 64
GDN_HEADS_PER_STEP = 8
MOBA_HEADS = 8
MOBA_BLOCK = 256
MOBA_TOPK = 3
REL_BUCKETS = 32
REL_MAX_DIST = 2048
BIAS_TILES = 8
MOBA_TILES_PER_STEP = 2
MOBA_PREP_ROWS = 2048
MOBA_STREAMS = 4
MOBA_V_ROWS = HEAD_DIM + 16
LOG2E = math.log2(math.e)
RMS_EPS = 1e-6
NEG_INF = -1e30
VMEM_LIMIT = 56 * 1024 * 1024


def _cparams(*sem):
    return pltpu.CompilerParams(dimension_semantics=sem, vmem_limit_bytes=VMEM_LIMIT)


def _sigmoid(x):
    return 1.0 / (1.0 + jnp.exp(-x))


def _silu(x):
    h = 0.5 * x
    return h + h * jnp.tanh(h)


def _dot(a, b, precision=None):
    return jnp.dot(a, b, preferred_element_type=F32, precision=precision)


def _dot_nt(a, b, precision=None):
    return lax.dot_general(a, b, (((1,), (1,)), ((), ())),
                           preferred_element_type=F32, precision=precision)


def _dot_tn(a, b):
    return lax.dot_general(a, b, (((0,), (0,)), ((), ())), preferred_element_type=F32)


def _rms_rows(x, w):
    return x * lax.rsqrt(jnp.mean(x * x, axis=-1, keepdims=True) + RMS_EPS) * w


def _norm_matmul_kernel(x_ref, nw_ref, w_ref, o_ref, u_scr):
    @pl.when(pl.program_id(1) == 0)
    def _():
        u_scr[...] = _rms_rows(x_ref[...], nw_ref[...]).astype(BF16)

    o_ref[...] = _dot(u_scr[...], w_ref[...]).astype(o_ref.dtype)


def _norm_matmul(x, nw, w, tm, tn, out_dtype, name):
    m, k = x.shape
    n = w.shape[1]
    return pl.pallas_call(
        _norm_matmul_kernel,
        grid=(m // tm, n // tn),
        in_specs=[pl.BlockSpec((tm, k), lambda i, j: (i, 0)),
                  pl.BlockSpec((1, k), lambda i, j: (0, 0)),
                  pl.BlockSpec((k, tn), lambda i, j: (0, j))],
        out_specs=pl.BlockSpec((tm, tn), lambda i, j: (i, j)),
        out_shape=jax.ShapeDtypeStruct((m, n), out_dtype),
        scratch_shapes=[pltpu.VMEM((tm, k), BF16)],
        compiler_params=_cparams("parallel", "arbitrary"),
        name=name,
    )(x, nw, w)


def _matmul_res_kernel(a_ref, w_ref, r_ref, o_ref):
    o_ref[...] = r_ref[...] + _dot(a_ref[...], w_ref[...])


def _matmul_res(a, w, res, tm, tn, name):
    m, k = a.shape
    n = w.shape[1]
    return pl.pallas_call(
        _matmul_res_kernel,
        grid=(m // tm, n // tn),
        in_specs=[pl.BlockSpec((tm, k), lambda i, j: (i, 0)),
                  pl.BlockSpec((k, tn), lambda i, j: (0, j)),
                  pl.BlockSpec((tm, tn), lambda i, j: (i, j))],
        out_specs=pl.BlockSpec((tm, tn), lambda i, j: (i, j)),
        out_shape=jax.ShapeDtypeStruct((m, n), F32),
        compiler_params=_cparams("parallel", "parallel"),
        name=name,
    )(a, w, res)


def _ffn_up_kernel(x_ref, nw_ref, wg_ref, wu_ref, o_ref, u_scr):
    @pl.when(pl.program_id(1) == 0)
    def _():
        u_scr[...] = _rms_rows(x_ref[...], nw_ref[...]).astype(BF16)

    u = u_scr[...]
    g = _dot(u, wg_ref[...])
    o_ref[...] = (g * _sigmoid(g) * _dot(u, wu_ref[...])).astype(o_ref.dtype)


def _ffn_up(x, nw, wg, wu, tm, tn):
    m, k = x.shape
    n = wg.shape[1]
    return pl.pallas_call(
        _ffn_up_kernel,
        grid=(m // tm, n // tn),
        in_specs=[pl.BlockSpec((tm, k), lambda i, j: (i, 0)),
                  pl.BlockSpec((1, k), lambda i, j: (0, 0)),
                  pl.BlockSpec((k, tn), lambda i, j: (0, j)),
                  pl.BlockSpec((k, tn), lambda i, j: (0, j))],
        out_specs=pl.BlockSpec((tm, tn), lambda i, j: (i, j)),
        out_shape=jax.ShapeDtypeStruct((m, n), BF16),
        scratch_shapes=[pltpu.VMEM((tm, k), BF16)],
        compiler_params=_cparams("parallel", "arbitrary"),
        name="ffn_up",
    )(x, nw, wg, wu)


def _mix_kernel(ya_ref, yb_ref, ga_ref, gb_ref, wa_ref, wb_ref, o_ref):
    a = _dot(ya_ref[...], wa_ref[...])
    b = _dot(yb_ref[...], wb_ref[...])
    o_ref[...] = (_sigmoid(ga_ref[...]) * a + _sigmoid(gb_ref[...]) * b).astype(o_ref.dtype)


def _mix(ya, yb, proj, wa, wb, tm, tn):
    m, k = ya.shape
    n = wa.shape[1]
    nb = n // tn
    return pl.pallas_call(
        _mix_kernel,
        grid=(m // tm, nb),
        in_specs=[pl.BlockSpec((tm, k), lambda i, j: (i, 0)),
                  pl.BlockSpec((tm, k), lambda i, j: (i, 0)),
                  pl.BlockSpec((tm, tn), lambda i, j: (i, j)),
                  pl.BlockSpec((tm, tn), lambda i, j: (i, nb + j)),
                  pl.BlockSpec((k, tn), lambda i, j: (0, j)),
                  pl.BlockSpec((k, tn), lambda i, j: (0, j))],
        out_specs=pl.BlockSpec((tm, tn), lambda i, j: (i, j)),
        out_shape=jax.ShapeDtypeStruct((m, n), BF16),
        compiler_params=_cparams("parallel", "parallel"),
        name="mix",
    )(ya, yb, proj, proj, wa, wb)


def _split3(x):
    a = x.astype(BF16)
    r = x - a.astype(F32)
    b = r.astype(BF16)
    c = (r - b.astype(F32)).astype(BF16)
    return a, b, c


def _gdn_kernel(q_ref, k_ref, v_ref, z_ref, bd_ref, cwq_ref, cwk_ref, cwv_ref,
                alog_ref, dtb_ref, onw_ref, o_ref, s_scr, tail_scr, *, tb):
    hp = GDN_HEADS_PER_STEP
    C = GDN_CHUNK
    D = HEAD_DIM
    head0 = pl.program_id(1) * hp

    @pl.when(pl.program_id(2) == 0)
    def _():
        s_scr[...] = jnp.zeros_like(s_scr)
        tail_scr[...] = jnp.zeros_like(tail_scr)

    row8 = lax.broadcasted_iota(jnp.int32, (8, hp * D), 0)
    tails = []

    def conv_silu(u_ref, cw_ref, idx):
        u = u_ref[0]
        w = cw_ref[...]
        tail = tail_scr[idx]
        y = None
        ytop = None
        for s in (3, 2, 1):
            sh = pltpu.roll(u, s, axis=0)
            top = jnp.where(row8 < s, pltpu.roll(tail, s, axis=0), sh[0:8])
            wj = w[3 - s:4 - s]
            y = sh * wj if y is None else y + sh * wj
            ytop = top * wj if ytop is None else ytop + top * wj
        y = y + u * w[3:4]
        ytop = ytop + u[0:8] * w[3:4]
        tails.append(u[tb - 8:tb])
        y = jnp.concatenate([ytop, y[8:]], axis=0)
        return _silu(y)

    q_all = conv_silu(q_ref, cwq_ref, 0)
    k_all = conv_silu(k_ref, cwk_ref, 1)
    v_all = conv_silu(v_ref, cwv_ref, 2)

    bd = bd_ref[0]
    lane = lax.broadcasted_iota(jnp.int32, (tb, LANES), 1)
    beta_all = _sigmoid(bd)
    xg = bd + dtb_ref[...]
    softplus = jnp.maximum(xg, 0.0) + jnp.log1p(jnp.exp(-jnp.abs(xg)))
    g_all = -jnp.exp(alog_ref[...]) * softplus
    betas = [jnp.sum(jnp.where(lane == head0 + e, beta_all, 0.0), axis=-1, keepdims=True)
             for e in range(hp)]
    gs = [jnp.sum(jnp.where(lane == head0 + e + GDN_HEADS, g_all, 0.0), axis=-1, keepdims=True)
          for e in range(hp)]

    nc = tb // C
    ri = lax.broadcasted_iota(jnp.int32, (tb, tb), 0)
    ci = lax.broadcasted_iota(jnp.int32, (tb, tb), 1)
    same = (ri // C) == (ci // C)
    tril16 = jnp.where(same & (ri >= ci), 1.0, 0.0).astype(BF16)
    g_b = jnp.concatenate([jnp.broadcast_to(g, (tb, LANES)) for g in gs], axis=1)
    gc_all = sum(_dot(tril16, piece) for piece in _split3(g_b))

    iw = lax.broadcasted_iota(jnp.int32, (C, tb), 0)
    jw = lax.broadcasted_iota(jnp.int32, (C, tb), 1) % C
    incl_w = iw >= jw
    strict_w = iw > jw
    eye_w = jnp.where(iw == jw, 1.0, 0.0).astype(F32)
    low_half = (lax.broadcasted_iota(jnp.int32, (C, LANES), 1) < C)

    def block_diag(wide16):
        return jnp.where(same, jnp.concatenate([wide16] * nc, axis=0), jnp.zeros((), BF16))

    def pair_blocks(full):
        return jnp.where(low_half, full[:C], full[C:])

    E = range(hp)
    pairs = range(tb // LANES)

    def l2n(x):
        return x * lax.rsqrt(jnp.sum(x * x, axis=-1, keepdims=True) + RMS_EPS)

    qs = [l2n(q_all[:, e * D:(e + 1) * D]) * (D ** -0.5) for e in E]
    ks = [l2n(k_all[:, e * D:(e + 1) * D]) for e in E]
    vs = [v_all[:, e * D:(e + 1) * D] for e in E]
    gcs = [gc_all[:, e * D:(e + 1) * D] for e in E]
    g_col = [jnp.concatenate([pair_blocks(gc[p * LANES:(p + 1) * LANES]) for p in pairs], axis=1)
             for gc in gcs]
    g_row = [jnp.concatenate([gc[p * LANES:(p + 1) * LANES, :].T[0:1, :] for p in pairs], axis=1)
             for gc in gcs]
    decay = [jnp.where(incl_w, jnp.exp(jnp.where(incl_w, g_col[e] - g_row[e], 0.0)), 0.0) for e in E]
    k16 = [k.astype(BF16) for k in ks]
    q16 = [q.astype(BF16) for q in qs]
    kb = [ks[e] * betas[e] for e in E]
    kb16 = [x.astype(BF16) for x in kb]

    def pair_products(a16, b16):
        return jnp.concatenate([pair_blocks(_dot_nt(a16[p * LANES:(p + 1) * LANES],
                                                    b16[p * LANES:(p + 1) * LANES])) for p in pairs], axis=1)

    lmat = [jnp.where(strict_w, pair_products(kb16[e], k16[e]) * decay[e], 0.0) for e in E]
    amat = [jnp.where(incl_w, pair_products(q16[e], k16[e]) * decay[e], 0.0) for e in E]
    p16 = [(-lmat[e]).astype(BF16) for e in E]
    tinv = [eye_w - lmat[e] for e in E]
    pw = [_dot(p16[e], block_diag(p16[e])) for e in E]
    for _ in range(4):
        p16 = [pw[e].astype(BF16) for e in E]
        prod = [_dot(jnp.concatenate([p16[e], tinv[e].astype(BF16)], axis=0), block_diag(p16[e])) for e in E]
        pw = [prod[e][:C] for e in E]
        tinv = [tinv[e] + prod[e][C:] for e in E]
    tinv = [tinv[e] + _dot(tinv[e].astype(BF16), block_diag(pw[e].astype(BF16))) for e in E]
    eg = [jnp.exp(gc) for gc in gcs]
    rhs = [jnp.concatenate([kb[e] * eg[e], vs[e] * betas[e]], axis=1).astype(BF16) for e in E]
    wu = [_dot(block_diag(tinv[e].astype(BF16)), rhs[e]).astype(BF16) for e in E]
    au = [_dot(block_diag(amat[e].astype(BF16)), wu[e]) for e in E]
    q_eff = [(qs[e] * eg[e] - au[e][:, :D]).astype(BF16) for e in E]
    gl = [jnp.concatenate([jnp.broadcast_to(gc[c * C + C - 1:c * C + C, :], (C, LANES))
                           for c in range(nc)], axis=0) for gc in gcs]
    k_dec = [(ks[e] * jnp.exp(gl[e] - gcs[e])).astype(BF16) for e in E]
    pn = [[_dot_tn(k_dec[e][c * C:(c + 1) * C], wu[e][c * C:(c + 1) * C]) for c in range(nc)]
          for e in E]

    states = [s_scr[e] for e in E]
    onw = onw_ref[...]
    for c in range(nc):
        r0 = c * C
        lhs = [jnp.concatenate([pn[e][c][:, :D].astype(BF16), q_eff[e][r0:r0 + C]], axis=0) for e in E]
        res = [_dot(lhs[e], states[e].astype(BF16)) for e in E]
        outs = [_rms_rows(res[e][D:] + au[e][r0:r0 + C, D:], onw) for e in E]
        states = [states[e] * jnp.exp(gl[e][r0:r0 + 1, :]) - res[e][:D] + pn[e][c][:, D:] for e in E]
        zc = z_ref[0, r0:r0 + C, :]
        o_ref[0, r0:r0 + C, :] = (jnp.concatenate(outs, axis=1) * _silu(zc)).astype(o_ref.dtype)
    for e in range(hp):
        s_scr[e] = states[e]
    for idx in range(3):
        tail_scr[idx] = tails[idx]


def _gdn(proj3, conv_w, alog_row, dtb_row, onw, col0, bd_col, tb):
    b, t, _ = proj3.shape
    hp = GDN_HEADS_PER_STEP
    ng = GDN_HEADS // hp
    w = hp * HEAD_DIM

    def col(base):
        return pl.BlockSpec((1, tb, w), lambda bi, hi, ti: (bi, ti, base // hp + hi))

    def cw(base):
        return pl.BlockSpec((GDN_CONV, w), lambda bi, hi, ti: (0, base // hp + hi))

    assert col0 % hp == 0
    row = pl.BlockSpec((1, LANES), lambda bi, hi, ti: (0, 0))
    return pl.pallas_call(
        functools.partial(_gdn_kernel, tb=tb),
        grid=(b, ng, t // tb),
        in_specs=[col(col0), col(col0 + GDN_HEADS), col(col0 + 2 * GDN_HEADS), col(col0 + 3 * GDN_HEADS),
                  pl.BlockSpec((1, tb, LANES), lambda bi, hi, ti: (bi, ti, bd_col)),
                  cw(0), cw(GDN_HEADS), cw(2 * GDN_HEADS), row, row, row],
        out_specs=pl.BlockSpec((1, tb, w), lambda bi, hi, ti: (bi, ti, hi)),
        out_shape=jax.ShapeDtypeStruct((b, t, GDN_HEADS * HEAD_DIM), BF16),
        scratch_shapes=[pltpu.VMEM((hp, HEAD_DIM, HEAD_DIM), F32),
                        pltpu.VMEM((3, 8, w), F32)],
        compiler_params=_cparams("parallel", "parallel", "arbitrary"),
        name="gdn",
    )(proj3, proj3, proj3, proj3, proj3, conv_w, conv_w, conv_w, alog_row, dtb_row, onw)


def _bias_kernel(rel_ref, o_ref):
    h = pl.program_id(0)
    bs = MOBA_BLOCK
    max_exact = REL_BUCKETS // 2
    x = lax.broadcasted_iota(jnp.int32, (8, 2 * bs), 1)
    for d in range(BIAS_TILES):
        dist = jnp.maximum(d * bs + x - bs, 0)
        df = dist.astype(F32)
        log_ratio = jnp.log(jnp.maximum(df, float(max_exact)) / max_exact) / math.log(REL_MAX_DIST / max_exact)
        large = max_exact + (log_ratio * (REL_BUCKETS - max_exact)).astype(jnp.int32)
        large = jnp.minimum(large, REL_BUCKETS - 1)
        bucket = jnp.where(dist < max_exact, dist, large)
        row = jnp.zeros((8, 2 * bs), F32)
        for b in range(REL_BUCKETS):
            row = jnp.where(bucket == b, rel_ref[b, h], row)
        base = jnp.broadcast_to(row[0:1, :] * LOG2E, (bs, 2 * bs))
        o_ref[0, d] = pltpu.roll(base, 0, 1, stride=1, stride_axis=0)[:, bs:]


def _bias_tiles(rel_bias):
    bs = MOBA_BLOCK
    return pl.pallas_call(
        _bias_kernel,
        grid=(MOBA_HEADS,),
        in_specs=[pl.BlockSpec(memory_space=pltpu.SMEM)],
        out_specs=pl.BlockSpec((1, BIAS_TILES, bs, bs), lambda h: (h, 0, 0, 0)),
        out_shape=jax.ShapeDtypeStruct((MOBA_HEADS, BIAS_TILES, bs, bs), F32),
        compiler_params=_cparams("parallel"),
        name="bias",
    )(rel_bias)


def _moba_prep_kernel(q_ref, k_ref, v_ref, qw_ref, kw_ref, q16_ref, kn_ref, vt_ref, sel_ref, km_scr,
                      *, rows, nb):
    bs = MOBA_BLOCK
    nsub = rows // bs
    ti = pl.program_id(2)

    @pl.when(ti == 0)
    def _():
        km_scr[...] = jnp.zeros_like(km_scr)

    kn = _rms_rows(k_ref[0], kw_ref[...])
    kn_ref[0, 0] = kn.astype(BF16)
    for s in range(nsub):
        km_scr[pl.ds(ti * nsub + s, 1), :] = jnp.mean(kn[s * bs:(s + 1) * bs], axis=0, keepdims=True)
        vt_ref[0, 0, 0:HEAD_DIM, s * bs:(s + 1) * bs] = v_ref[0, s * bs:(s + 1) * bs, :].T.astype(BF16)
    pad_row = lax.broadcasted_iota(jnp.int32, (MOBA_V_ROWS - HEAD_DIM, rows), 0)
    vt_ref[0, 0, HEAD_DIM:, :] = jnp.where(pad_row == 0, 1.0, 0.0).astype(BF16)

    q = _rms_rows(q_ref[0], qw_ref[...])
    q16_ref[0, 0] = (q * ((HEAD_DIM ** -0.5) * LOG2E)).astype(BF16)

    blk = lax.broadcasted_iota(jnp.int32, (nb, rows), 0)
    own_blk = ti * nsub + lax.broadcasted_iota(jnp.int32, (nb, rows), 1) // bs
    blkf = blk.astype(F32)
    valid = blk < own_blk
    work = jnp.where(valid, _dot_nt(km_scr[...], q, HIGHEST), NEG_INF)
    picked = jnp.zeros((nb, rows), F32)
    for _ in range(MOBA_TOPK):
        best = jnp.max(work, axis=0, keepdims=True)
        first = jnp.min(jnp.where(work == best, blkf, float(nb)), axis=0, keepdims=True)
        hit = blkf == first
        picked = jnp.where(hit, 1.0, picked)
        work = jnp.where(hit, -jnp.inf, work)
    sel_ref[0, 0] = jnp.where(valid, picked, 0.0)


def _moba_prep(proj3, qw, kw, col0, rows):
    b, t, _ = proj3.shape
    hh, bs, dh = MOBA_HEADS, MOBA_BLOCK, HEAD_DIM
    nb = t // bs

    def col(base):
        return pl.BlockSpec((1, rows, LANES), lambda bi, hi, ti: (bi, ti, base + hi))

    row = pl.BlockSpec((1, LANES), lambda bi, hi, ti: (0, 0))
    return pl.pallas_call(
        functools.partial(_moba_prep_kernel, rows=rows, nb=nb),
        grid=(b, hh, t // rows),
        in_specs=[col(col0), col(col0 + hh), col(col0 + 2 * hh), row, row],
        out_specs=[pl.BlockSpec((1, 1, rows, dh), lambda bi, hi, ti: (bi, hi, ti, 0)),
                   pl.BlockSpec((1, 1, rows, dh), lambda bi, hi, ti: (bi, hi, ti, 0)),
                   pl.BlockSpec((1, 1, MOBA_V_ROWS, rows), lambda bi, hi, ti: (bi, hi, 0, ti)),
                   pl.BlockSpec((1, 1, nb, rows), lambda bi, hi, ti: (bi, hi, 0, ti))],
        out_shape=[jax.ShapeDtypeStruct((b, hh, t, dh), BF16),
                   jax.ShapeDtypeStruct((b, hh, t, dh), BF16),
                   jax.ShapeDtypeStruct((b, hh, MOBA_V_ROWS, t), BF16),
                   jax.ShapeDtypeStruct((b, hh, nb, t), F32)],
        scratch_shapes=[pltpu.VMEM((nb, dh), F32)],
        compiler_params=_cparams("parallel", "parallel", "arbitrary"),
        name="moba_prep",
    )(proj3, proj3, proj3, qw, kw)


def _moba_kernel(q_ref, kn_ref, vt_ref, sel_ref, bias_ref, o_ref, buf_a, buf_b, *, nb):
    i = pl.program_id(2)
    bs = MOBA_BLOCK
    ch = MOBA_TILES_PER_STEP
    E = range(MOBA_STREAMS)
    q16 = [q_ref[e, 0] for e in E]

    kk = lax.broadcasted_iota(jnp.int32, (bs, bs), 0)
    qq = lax.broadcasted_iota(jnp.int32, (bs, bs), 1)

    row0 = pl.multiple_of(i * bs, bs)
    own = [jnp.where(kk <= qq, _dot_nt(kn_ref[e, 0, pl.ds(row0, bs), :], q16[e]) + bias_ref[0, 0], NEG_INF)
           for e in E]
    m = [jnp.max(own[e], axis=0, keepdims=True) for e in E]
    acc = [_dot(vt_ref[e, 0, :, pl.ds(row0, bs)], jnp.exp2(own[e] - m[e]).astype(BF16)) for e in E]

    crows = ch * bs
    n_pairs = (i + 2 * ch - 1) // (2 * ch)
    last_chunk = nb // ch - 1

    def chunk_scores(buf, e, c):
        r = pl.multiple_of(c * crows, crows)
        raw = _dot_nt(kn_ref[e, 0, pl.ds(r, crows), :], q16[e])
        mx = None
        for t in range(ch):
            j = c * ch + t
            d = jnp.clip(i - j, 0, BIAS_TILES - 1)
            st = raw[t * bs:(t + 1) * bs, :] + bias_ref[0, d]
            st = jnp.where(sel_ref[e, 0, pl.ds(j, 1), :] > 0.0, st, NEG_INF)
            buf[e, t * bs:(t + 1) * bs, :] = st
            tmx = jnp.max(st, axis=0, keepdims=True)
            mx = tmx if mx is None else jnp.maximum(mx, tmx)
        return mx

    def absorb(buf, e, c, mx, m, acc):
        r = pl.multiple_of(c * crows, crows)
        m_new = jnp.maximum(m, mx)
        p = jnp.exp2(buf[e] - m_new)
        acc = jnp.exp2(m - m_new) * acc + _dot(vt_ref[e, 0, :, pl.ds(r, crows)], p.astype(BF16))
        return m_new, acc

    mx_a = tuple(chunk_scores(buf_a, e, 0) for e in E)

    def pair(c, carry):
        m, acc, mx_a = (list(x) for x in carry)
        mx_b = [chunk_scores(buf_b, e, 2 * c + 1) for e in E]
        for e in E:
            m[e], acc[e] = absorb(buf_a, e, 2 * c, mx_a[e], m[e], acc[e])
        mx_a = [chunk_scores(buf_a, e, jnp.minimum(2 * c + 2, last_chunk)) for e in E]
        for e in E:
            m[e], acc[e] = absorb(buf_b, e, 2 * c + 1, mx_b[e], m[e], acc[e])
        return tuple(m), tuple(acc), tuple(mx_a)

    m, acc, _ = lax.fori_loop(0, n_pairs, pair, (tuple(m), tuple(acc), mx_a))
    for e in E:
        o_ref[e] = (acc[e][:HEAD_DIM] / acc[e][HEAD_DIM:HEAD_DIM + 1]).T.astype(o_ref.dtype)


def _moba(q16, kn, vt, sel, bias):
    b, hh, t, dh = kn.shape
    bs = MOBA_BLOCK
    nb = t // bs
    ns = MOBA_STREAMS
    crows = MOBA_TILES_PER_STEP * bs
    assert b % ns == 0 and nb % (2 * MOBA_TILES_PER_STEP) == 0
    return pl.pallas_call(
        functools.partial(_moba_kernel, nb=nb),
        grid=(b // ns, hh, nb),
        in_specs=[pl.BlockSpec((ns, 1, bs, dh), lambda bi, hi, ti: (bi, hi, ti, 0)),
                  pl.BlockSpec((ns, 1, t, dh), lambda bi, hi, ti: (bi, hi, 0, 0)),
                  pl.BlockSpec((ns, 1, MOBA_V_ROWS, t), lambda bi, hi, ti: (bi, hi, 0, 0)),
                  pl.BlockSpec((ns, 1, nb, bs), lambda bi, hi, ti: (bi, hi, 0, ti)),
                  pl.BlockSpec((1, BIAS_TILES, bs, bs), lambda bi, hi, ti: (hi, 0, 0, 0))],
        out_specs=pl.BlockSpec((ns, bs, dh), lambda bi, hi, ti: (bi, ti, hi)),
        out_shape=jax.ShapeDtypeStruct((b, t, hh * dh), BF16),
        scratch_shapes=[pltpu.VMEM((ns, crows, bs), F32),
                        pltpu.VMEM((ns, crows, bs), F32)],
        compiler_params=_cparams("parallel", "parallel", "parallel"),
        name="moba",
    )(q16, kn, vt, sel, bias)


def _tile(n, want):
    t = min(n, want)
    assert n % t == 0, (n, want)
    return t


def kernel(x, norm_mix_w, w_in, conv_w, a_log, dt_bias, gdn_o_norm_w, q_norm_w, k_norm_w, rel_bias,
           w_branch_gdn, w_branch_moba, w_out, norm_ffn_w, w_ffn_gate, w_ffn_up, w_ffn_down):
    b, t, d = x.shape
    m = b * t
    gw = GDN_HEADS * HEAD_DIM
    mw = MOBA_HEADS * HEAD_DIM
    assert t % MOBA_PREP_ROWS == 0 and w_in.shape[0] == 1
    assert w_in.shape[2] == 4 * gw + 2 * GDN_HEADS + 3 * mw + 2 * d

    wi = w_in[0]
    o_bd = 4 * gw
    o_moba = o_bd + 2 * GDN_HEADS
    o_gate = o_moba + 3 * mw
    n_main = 2 * d + 4 * gw + 3 * mw
    tn_proj = 1280
    n_proj = -(-(n_main + LANES) // tn_proj) * tn_proj
    wi = lax.optimization_barrier(wi.astype(BF16))
    w_proj = jnp.concatenate(
        [wi[:, o_gate:], wi[:, :o_bd], wi[:, o_moba:o_gate], wi[:, o_bd:o_moba],
         jnp.zeros((d, n_proj - n_main - 2 * GDN_HEADS), BF16)], axis=1)
    col_gdn = (2 * d) // LANES
    col_moba = col_gdn + 4 * GDN_HEADS
    col_bd = col_moba + 3 * MOBA_HEADS

    x2 = x.reshape(m, d)
    proj = _norm_matmul(x2, norm_mix_w, w_proj, _tile(m, 1024), tn_proj, F32, "proj")
    proj3 = proj.reshape(b, t, n_proj)

    lane_pad = jnp.zeros((1, LANES - 2 * GDN_HEADS), F32)
    head_pad = jnp.zeros((1, GDN_HEADS), F32)
    alog_row = jnp.concatenate([head_pad, a_log, lane_pad], axis=1)
    dtb_row = jnp.concatenate([head_pad, dt_bias, lane_pad], axis=1)
    y_a = _gdn(proj3, conv_w[0], alog_row, dtb_row, gdn_o_norm_w, col_gdn, col_bd, _tile(t, 256))

    bias = _bias_tiles(rel_bias)
    q16, kn, vt, sel = _moba_prep(proj3, q_norm_w, k_norm_w, col_moba, MOBA_PREP_ROWS)
    y_b = _moba(q16, kn, vt, sel, bias)

    mix = _mix(y_a.reshape(m, gw), y_b.reshape(m, mw), proj,
               w_branch_gdn[0].astype(BF16), w_branch_moba[0].astype(BF16), _tile(m, 512), d)
    h1 = _matmul_res(mix, w_out[0].astype(BF16), x2, _tile(m, 512), d, "out")

    hid = _ffn_up(h1, norm_ffn_w, w_ffn_gate[0].astype(BF16), w_ffn_up[0].astype(BF16),
                  _tile(m, 1024), 512)
    h2 = _matmul_res(hid, w_ffn_down[0].astype(BF16), h1, _tile(m, 1024), 512, "ffn_down")
    return h2.reshape(b, t, d)
```

```python
import functools
import math

import jax
import jax.numpy as jnp
from jax import lax
from jax.experimental import pallas as pl
from jax.experimental.pallas import tpu as pltpu

F32 = jnp.float32
BF16 = jnp.bfloat16
HIGHEST = lax.Precision.HIGHEST

LANES = 128
HEAD_DIM = 128
GDN_HEADS = 8
GDN_CONV = 4
GDN_CHUNK = 64
GDN_HEADS_PER_STEP = 8
NEG_INF = -1e30
MOBA_HEADS = 8
MOBA_BLOCK = 256
MOBA_TOPK = 3
REL_BUCKETS = 32
REL_MAX_DIST = 2048
BIAS_TILES = 8
MOBA_TILES_PER_STEP = 2
MOBA_PREP_ROWS = 2048
MOBA_STREAMS = 4
MOBA_V_ROWS = HEAD_DIM + 16
MOBA_M_INIT = 0.1 * NEG_INF
LOG2E = math.log2(math.e)
RMS_EPS = 1e-6
V7X_VMEM_BYTES = 64 * 1024 * 1024
VMEM_LIMIT = V7X_VMEM_BYTES * 7 // 8

PROJ_TILE = (1024, 1280)
MIX_ROWS = 512
OUT_ROWS = 512
FFN_UP_TILE = (1024, 512)
FFN_DOWN_TILE = (1024, 512)
GDN_BLOCK = 256


def _cparams(*sem):
    return pltpu.CompilerParams(dimension_semantics=sem, vmem_limit_bytes=VMEM_LIMIT)


def _sigmoid(x):
    return 1.0 / (1.0 + jnp.exp(-x))


def _silu(x):
    h = 0.5 * x
    return h + h * jnp.tanh(h)


def _dot(a, b, precision=None):
    return jnp.dot(a, b, preferred_element_type=F32, precision=precision)


def _dot_nt(a, b, precision=None):
    return lax.dot_general(a, b, (((1,), (1,)), ((), ())),
                           preferred_element_type=F32, precision=precision)


def _dot_tn(a, b):
    return lax.dot_general(a, b, (((0,), (0,)), ((), ())), preferred_element_type=F32)


def _rms_rows(x, w):
    return x * lax.rsqrt(jnp.mean(x * x, axis=-1, keepdims=True) + RMS_EPS) * w


def _norm_matmul_kernel(x_ref, nw_ref, w_ref, o_ref, u_scr):
    @pl.when(pl.program_id(1) == 0)
    def _():
        u_scr[...] = _rms_rows(x_ref[...], nw_ref[...]).astype(BF16)

    o_ref[...] = _dot(u_scr[...], w_ref[...]).astype(o_ref.dtype)


def _norm_matmul(x, nw, w, tm, tn, out_dtype, name):
    m, k = x.shape
    n = w.shape[1]
    return pl.pallas_call(
        _norm_matmul_kernel,
        grid=(m // tm, n // tn),
        in_specs=[pl.BlockSpec((tm, k), lambda i, j: (i, 0)),
                  pl.BlockSpec((1, k), lambda i, j: (0, 0)),
                  pl.BlockSpec((k, tn), lambda i, j: (0, j))],
        out_specs=pl.BlockSpec((tm, tn), lambda i, j: (i, j)),
        out_shape=jax.ShapeDtypeStruct((m, n), out_dtype),
        scratch_shapes=[pltpu.VMEM((tm, k), BF16)],
        compiler_params=_cparams("parallel", "arbitrary"),
        name=name,
    )(x, nw, w)


def _matmul_res_kernel(a_ref, w_ref, r_ref, o_ref):
    o_ref[...] = r_ref[...] + _dot(a_ref[...], w_ref[...])


def _matmul_res(a, w, res, tm, tn, name):
    m, k = a.shape
    n = w.shape[1]
    return pl.pallas_call(
        _matmul_res_kernel,
        grid=(m // tm, n // tn),
        in_specs=[pl.BlockSpec((tm, k), lambda i, j: (i, 0)),
                  pl.BlockSpec((k, tn), lambda i, j: (0, j)),
                  pl.BlockSpec((tm, tn), lambda i, j: (i, j))],
        out_specs=pl.BlockSpec((tm, tn), lambda i, j: (i, j)),
        out_shape=jax.ShapeDtypeStruct((m, n), F32),
        compiler_params=_cparams("parallel", "parallel"),
        name=name,
    )(a, w, res)


def _ffn_up_kernel(x_ref, nw_ref, wg_ref, wu_ref, o_ref, u_scr):
    @pl.when(pl.program_id(1) == 0)
    def _():
        u_scr[...] = _rms_rows(x_ref[...], nw_ref[...]).astype(BF16)

    u = u_scr[...]
    g = _dot(u, wg_ref[...])
    o_ref[...] = (g * _sigmoid(g) * _dot(u, wu_ref[...])).astype(o_ref.dtype)


def _ffn_up(x, nw, wg, wu, tm, tn):
    m, k = x.shape
    n = wg.shape[1]
    return pl.pallas_call(
        _ffn_up_kernel,
        grid=(m // tm, n // tn),
        in_specs=[pl.BlockSpec((tm, k), lambda i, j: (i, 0)),
                  pl.BlockSpec((1, k), lambda i, j: (0, 0)),
                  pl.BlockSpec((k, tn), lambda i, j: (0, j)),
                  pl.BlockSpec((k, tn), lambda i, j: (0, j))],
        out_specs=pl.BlockSpec((tm, tn), lambda i, j: (i, j)),
        out_shape=jax.ShapeDtypeStruct((m, n), BF16),
        scratch_shapes=[pltpu.VMEM((tm, k), BF16)],
        compiler_params=_cparams("parallel", "arbitrary"),
        name="ffn_up",
    )(x, nw, wg, wu)


def _mix_kernel(ya_ref, yb_ref, ga_ref, gb_ref, wa_ref, wb_ref, o_ref):
    a = _dot(ya_ref[...], wa_ref[...])
    b = _dot(yb_ref[...], wb_ref[...])
    o_ref[...] = (_sigmoid(ga_ref[...]) * a + _sigmoid(gb_ref[...]) * b).astype(o_ref.dtype)


def _mix(ya, yb, proj, wa, wb, tm, tn):
    m, k = ya.shape
    n = wa.shape[1]
    nb = n // tn
    return pl.pallas_call(
        _mix_kernel,
        grid=(m // tm, nb),
        in_specs=[pl.BlockSpec((tm, k), lambda i, j: (i, 0)),
                  pl.BlockSpec((tm, k), lambda i, j: (i, 0)),
                  pl.BlockSpec((tm, tn), lambda i, j: (i, j)),
                  pl.BlockSpec((tm, tn), lambda i, j: (i, nb + j)),
                  pl.BlockSpec((k, tn), lambda i, j: (0, j)),
                  pl.BlockSpec((k, tn), lambda i, j: (0, j))],
        out_specs=pl.BlockSpec((tm, tn), lambda i, j: (i, j)),
        out_shape=jax.ShapeDtypeStruct((m, n), BF16),
        compiler_params=_cparams("parallel", "parallel"),
        name="mix",
    )(ya, yb, proj, proj, wa, wb)


def _split3(x):
    a = x.astype(BF16)
    r = x - a.astype(F32)
    b = r.astype(BF16)
    c = (r - b.astype(F32)).astype(BF16)
    return a, b, c


def _gdn_kernel(q_ref, k_ref, v_ref, z_ref, bd_ref, cwq_ref, cwk_ref, cwv_ref,
                alog_ref, dtb_ref, onw_ref, o_ref, s_scr, tail_scr, *, tb):
    hp = GDN_HEADS_PER_STEP
    C = GDN_CHUNK
    D = HEAD_DIM
    head0 = pl.program_id(1) * hp

    @pl.when(pl.program_id(2) == 0)
    def _():
        s_scr[...] = jnp.zeros_like(s_scr)
        tail_scr[...] = jnp.zeros_like(tail_scr)

    row8 = lax.broadcasted_iota(jnp.int32, (8, hp * D), 0)
    tails = []

    def conv_silu(u_ref, cw_ref, idx):
        u = u_ref[0]
        w = cw_ref[...]
        tail = tail_scr[idx]
        y = None
        ytop = None
        for s in (3, 2, 1):
            sh = pltpu.roll(u, s, axis=0)
            top = jnp.where(row8 < s, pltpu.roll(tail, s, axis=0), sh[0:8])
            wj = w[3 - s:4 - s]
            y = sh * wj if y is None else y + sh * wj
            ytop = top * wj if ytop is None else ytop + top * wj
        y = y + u * w[3:4]
        ytop = ytop + u[0:8] * w[3:4]
        tails.append(u[tb - 8:tb])
        y = jnp.concatenate([ytop, y[8:]], axis=0)
        return _silu(y)

    q_all = conv_silu(q_ref, cwq_ref, 0)
    k_all = conv_silu(k_ref, cwk_ref, 1)
    v_all = conv_silu(v_ref, cwv_ref, 2)

    bd = bd_ref[0]
    lane = lax.broadcasted_iota(jnp.int32, (tb, LANES), 1)
    beta_all = _sigmoid(bd)
    xg = bd + dtb_ref[...]
    softplus = jnp.maximum(xg, 0.0) + jnp.log1p(jnp.exp(-jnp.abs(xg)))
    g_all = -jnp.exp(alog_ref[...]) * softplus
    betas = [jnp.sum(jnp.where(lane == head0 + e, beta_all, 0.0), axis=-1, keepdims=True)
             for e in range(hp)]
    gs = [jnp.sum(jnp.where(lane == head0 + e + GDN_HEADS, g_all, 0.0), axis=-1, keepdims=True)
          for e in range(hp)]

    nc = tb // C
    ri = lax.broadcasted_iota(jnp.int32, (tb, tb), 0)
    ci = lax.broadcasted_iota(jnp.int32, (tb, tb), 1)
    same = (ri // C) == (ci // C)
    tril16 = jnp.where(same & (ri >= ci), 1.0, 0.0).astype(BF16)
    g_b = jnp.concatenate([jnp.broadcast_to(g, (tb, LANES)) for g in gs], axis=1)
    gc_all = sum(_dot(tril16, piece) for piece in _split3(g_b))

    iw = lax.broadcasted_iota(jnp.int32, (C, tb), 0)
    jw = lax.broadcasted_iota(jnp.int32, (C, tb), 1) % C
    incl_w = iw >= jw
    strict_w = iw > jw
    eye_w = jnp.where(iw == jw, 1.0, 0.0).astype(F32)
    low_half = (lax.broadcasted_iota(jnp.int32, (C, LANES), 1) < C)

    def block_diag(wide16):
        return jnp.where(same, jnp.concatenate([wide16] * nc, axis=0), jnp.zeros((), BF16))

    def pair_blocks(full):
        return jnp.where(low_half, full[:C], full[C:])

    E = range(hp)
    pairs = range(tb // LANES)

    def l2n(x):
        return x * lax.rsqrt(jnp.sum(x * x, axis=-1, keepdims=True) + RMS_EPS)

    qs = [l2n(q_all[:, e * D:(e + 1) * D]) * (D ** -0.5) for e in E]
    ks = [l2n(k_all[:, e * D:(e + 1) * D]) for e in E]
    vs = [v_all[:, e * D:(e + 1) * D] for e in E]
    gcs = [gc_all[:, e * D:(e + 1) * D] for e in E]
    g_col = [jnp.concatenate([pair_blocks(gc[p * LANES:(p + 1) * LANES]) for p in pairs], axis=1)
             for gc in gcs]
    g_row = [jnp.concatenate([gc[p * LANES:(p + 1) * LANES, :].T[0:1, :] for p in pairs], axis=1)
             for gc in gcs]
    decay = [jnp.where(incl_w, jnp.exp(jnp.where(incl_w, g_col[e] - g_row[e], 0.0)), 0.0) for e in E]
    k16 = [k.astype(BF16) for k in ks]
    q16 = [q.astype(BF16) for q in qs]
    kb = [ks[e] * betas[e] for e in E]
    kb16 = [x.astype(BF16) for x in kb]

    def pair_products(a16, b16):
        return jnp.concatenate([pair_blocks(_dot_nt(a16[p * LANES:(p + 1) * LANES],
                                                    b16[p * LANES:(p + 1) * LANES])) for p in pairs], axis=1)

    lmat = [jnp.where(strict_w, pair_products(kb16[e], k16[e]) * decay[e], 0.0) for e in E]
    amat = [jnp.where(incl_w, pair_products(q16[e], k16[e]) * decay[e], 0.0) for e in E]
    p16 = [(-lmat[e]).astype(BF16) for e in E]
    tinv = [eye_w - lmat[e] for e in E]
    pw = [_dot(p16[e], block_diag(p16[e])) for e in E]
    for _ in range(4):
        p16 = [pw[e].astype(BF16) for e in E]
        prod = [_dot(jnp.concatenate([p16[e], tinv[e].astype(BF16)], axis=0), block_diag(p16[e])) for e in E]
        pw = [prod[e][:C] for e in E]
        tinv = [tinv[e] + prod[e][C:] for e in E]
    tinv = [tinv[e] + _dot(tinv[e].astype(BF16), block_diag(pw[e].astype(BF16))) for e in E]
    eg = [jnp.exp(gc) for gc in gcs]
    rhs = [jnp.concatenate([kb[e] * eg[e], vs[e] * betas[e]], axis=1).astype(BF16) for e in E]
    wu = [_dot(block_diag(tinv[e].astype(BF16)), rhs[e]).astype(BF16) for e in E]
    au = [_dot(block_diag(amat[e].astype(BF16)), wu[e]) for e in E]
    q_eff = [(qs[e] * eg[e] - au[e][:, :D]).astype(BF16) for e in E]
    gl = [jnp.concatenate([jnp.broadcast_to(gc[c * C + C - 1:c * C + C, :], (C, LANES))
                           for c in range(nc)], axis=0) for gc in gcs]
    k_dec = [(ks[e] * jnp.exp(gl[e] - gcs[e])).astype(BF16) for e in E]
    pn = [[_dot_tn(k_dec[e][c * C:(c + 1) * C], wu[e][c * C:(c + 1) * C]) for c in range(nc)]
          for e in E]

    states = [s_scr[e] for e in E]
    onw = onw_ref[...]
    for c in range(nc):
        r0 = c * C
        lhs = [jnp.concatenate([pn[e][c][:, :D].astype(BF16), q_eff[e][r0:r0 + C]], axis=0) for e in E]
        res = [_dot(lhs[e], states[e].astype(BF16)) for e in E]
        outs = [_rms_rows(res[e][D:] + au[e][r0:r0 + C, D:], onw) for e in E]
        states = [states[e] * jnp.exp(gl[e][r0:r0 + 1, :]) - res[e][:D] + pn[e][c][:, D:] for e in E]
        zc = z_ref[0, r0:r0 + C, :]
        o_ref[0, r0:r0 + C, :] = (jnp.concatenate(outs, axis=1) * _silu(zc)).astype(o_ref.dtype)
    for e in range(hp):
        s_scr[e] = states[e]
    for idx in range(3):
        tail_scr[idx] = tails[idx]


def _gdn(proj3, conv_w, alog_row, dtb_row, onw, col0, bd_col, tb):
    b, t, _ = proj3.shape
    hp = GDN_HEADS_PER_STEP
    ng = GDN_HEADS // hp
    w = hp * HEAD_DIM

    def col(base):
        return pl.BlockSpec((1, tb, w), lambda bi, hi, ti: (bi, ti, base // hp + hi))

    def cw(base):
        return pl.BlockSpec((GDN_CONV, w), lambda bi, hi, ti: (0, base // hp + hi))

    assert col0 % hp == 0
    row = pl.BlockSpec((1, LANES), lambda bi, hi, ti: (0, 0))
    return pl.pallas_call(
        functools.partial(_gdn_kernel, tb=tb),
        grid=(b, ng, t // tb),
        in_specs=[col(col0), col(col0 + GDN_HEADS), col(col0 + 2 * GDN_HEADS), col(col0 + 3 * GDN_HEADS),
                  pl.BlockSpec((1, tb, LANES), lambda bi, hi, ti: (bi, ti, bd_col)),
                  cw(0), cw(GDN_HEADS), cw(2 * GDN_HEADS), row, row, row],
        out_specs=pl.BlockSpec((1, tb, w), lambda bi, hi, ti: (bi, ti, hi)),
        out_shape=jax.ShapeDtypeStruct((b, t, GDN_HEADS * HEAD_DIM), BF16),
        scratch_shapes=[pltpu.VMEM((hp, HEAD_DIM, HEAD_DIM), F32),
                        pltpu.VMEM((3, 8, w), F32)],
        compiler_params=_cparams("parallel", "parallel", "arbitrary"),
        name="gdn",
    )(proj3, proj3, proj3, proj3, proj3, conv_w, conv_w, conv_w, alog_row, dtb_row, onw)


def _bias_kernel(rel_ref, o_ref):
    h = pl.program_id(0)
    bs = MOBA_BLOCK
    max_exact = REL_BUCKETS // 2
    x = lax.broadcasted_iota(jnp.int32, (8, 2 * bs), 1)
    for d in range(BIAS_TILES):
        dist = jnp.maximum(d * bs + x - bs, 0)
        df = dist.astype(F32)
        log_ratio = jnp.log(jnp.maximum(df, float(max_exact)) / max_exact) / math.log(REL_MAX_DIST / max_exact)
        large = max_exact + (log_ratio * (REL_BUCKETS - max_exact)).astype(jnp.int32)
        large = jnp.minimum(large, REL_BUCKETS - 1)
        bucket = jnp.where(dist < max_exact, dist, large)
        row = jnp.zeros((8, 2 * bs), F32)
        for b in range(REL_BUCKETS):
            row = jnp.where(bucket == b, rel_ref[b, h], row)
        base = jnp.broadcast_to(row[0:1, :] * LOG2E, (bs, 2 * bs))
        tile = pltpu.roll(base, 0, 1, stride=1, stride_axis=0)[:, bs:]
        if d == 0:
            kk = lax.broadcasted_iota(jnp.int32, (bs, bs), 0)
            qq = lax.broadcasted_iota(jnp.int32, (bs, bs), 1)
            tile = jnp.where(kk <= qq, tile, NEG_INF)
        o_ref[0, d] = tile


def _bias_tiles(rel_bias):
    bs = MOBA_BLOCK
    max_exact = REL_BUCKETS // 2
    nearest = (BIAS_TILES - 1) * bs - (bs - 1)
    assert max_exact + int(math.log(nearest / max_exact) / math.log(REL_MAX_DIST / max_exact)
                           * (REL_BUCKETS - max_exact)) >= REL_BUCKETS - 1
    return pl.pallas_call(
        _bias_kernel,
        grid=(MOBA_HEADS,),
        in_specs=[pl.BlockSpec(memory_space=pltpu.SMEM)],
        out_specs=pl.BlockSpec((1, BIAS_TILES, bs, bs), lambda h: (h, 0, 0, 0)),
        out_shape=jax.ShapeDtypeStruct((MOBA_HEADS, BIAS_TILES, bs, bs), F32),
        compiler_params=_cparams("parallel"),
        name="bias",
    )(rel_bias)


def _moba_prep_kernel(q_ref, k_ref, v_ref, qw_ref, kw_ref, rel_ref, q16_ref, kn_ref, vt_ref, add_ref, km_scr,
                      *, rows, nb):
    bs = MOBA_BLOCK
    nsub = rows // bs
    ti = pl.program_id(2)

    @pl.when(ti == 0)
    def _():
        km_scr[...] = jnp.zeros_like(km_scr)

    kn = _rms_rows(k_ref[0], kw_ref[...])
    kn_ref[0, 0] = kn.astype(BF16)
    for s in range(nsub):
        km_scr[pl.ds(ti * nsub + s, 1), :] = jnp.mean(kn[s * bs:(s + 1) * bs], axis=0, keepdims=True)
        vt_ref[0, 0, 0:HEAD_DIM, s * bs:(s + 1) * bs] = v_ref[0, s * bs:(s + 1) * bs, :].T.astype(BF16)
    pad_row = lax.broadcasted_iota(jnp.int32, (MOBA_V_ROWS - HEAD_DIM, rows), 0)
    vt_ref[0, 0, HEAD_DIM:, :] = jnp.where(pad_row == 0, 1.0, 0.0).astype(BF16)

    q = _rms_rows(q_ref[0], qw_ref[...])
    q16_ref[0, 0] = (q * ((HEAD_DIM ** -0.5) * LOG2E)).astype(BF16)

    blk = lax.broadcasted_iota(jnp.int32, (nb, rows), 0)
    own_blk = ti * nsub + lax.broadcasted_iota(jnp.int32, (nb, rows), 1) // bs
    blkf = blk.astype(F32)
    valid = blk < own_blk
    work = jnp.where(valid, _dot_nt(km_scr[...], q, HIGHEST), NEG_INF)
    picked = jnp.zeros((nb, rows), F32)
    for _ in range(MOBA_TOPK):
        best = jnp.max(work, axis=0, keepdims=True)
        first = jnp.min(jnp.where(work == best, blkf, float(nb)), axis=0, keepdims=True)
        hit = blkf == first
        picked = jnp.where(hit, 1.0, picked)
        work = jnp.where(hit, -jnp.inf, work)
    far_bias = rel_ref[REL_BUCKETS - 1, pl.program_id(1)] * LOG2E
    attended = (valid & (picked > 0.0)) | (blk == own_blk)
    add_ref[0, 0] = jnp.where(attended, far_bias, NEG_INF)


def _moba_prep(proj3, qw, kw, rel_bias, col0, rows):
    b, t, _ = proj3.shape
    hh, bs, dh = MOBA_HEADS, MOBA_BLOCK, HEAD_DIM
    nb = t // bs

    def col(base):
        return pl.BlockSpec((1, rows, LANES), lambda bi, hi, ti: (bi, ti, base + hi))

    row = pl.BlockSpec((1, LANES), lambda bi, hi, ti: (0, 0))
    return pl.pallas_call(
        functools.partial(_moba_prep_kernel, rows=rows, nb=nb),
        grid=(b, hh, t // rows),
        in_specs=[col(col0), col(col0 + hh), col(col0 + 2 * hh), row, row,
                  pl.BlockSpec(memory_space=pltpu.SMEM)],
        out_specs=[pl.BlockSpec((1, 1, rows, dh), lambda bi, hi, ti: (bi, hi, ti, 0)),
                   pl.BlockSpec((1, 1, rows, dh), lambda bi, hi, ti: (bi, hi, ti, 0)),
                   pl.BlockSpec((1, 1, MOBA_V_ROWS, rows), lambda bi, hi, ti: (bi, hi, 0, ti)),
                   pl.BlockSpec((1, 1, nb, rows), lambda bi, hi, ti: (bi, hi, 0, ti))],
        out_shape=[jax.ShapeDtypeStruct((b, hh, t, dh), BF16),
                   jax.ShapeDtypeStruct((b, hh, t, dh), BF16),
                   jax.ShapeDtypeStruct((b, hh, MOBA_V_ROWS, t), BF16),
                   jax.ShapeDtypeStruct((b, hh, nb, t), F32)],
        scratch_shapes=[pltpu.VMEM((nb, dh), F32)],
        compiler_params=_cparams("parallel", "parallel", "arbitrary"),
        name="moba_prep",
    )(proj3, proj3, proj3, qw, kw, rel_bias)


def _moba_kernel(q_ref, kn_ref, vt_ref, add_ref, bias_ref, o_ref, buf_a, buf_b, *, nb):
    i = pl.program_id(2)
    bs = MOBA_BLOCK
    ch = MOBA_TILES_PER_STEP
    E = range(MOBA_STREAMS)
    q16 = [q_ref[e, 0] for e in E]

    crows = ch * bs
    n_pairs = (i + 2 * ch) // (2 * ch)
    last_chunk = nb // ch - 1

    def chunk_scores(buf, e, c, far):
        r = pl.multiple_of(c * crows, crows)
        raw = _dot_nt(kn_ref[e, 0, pl.ds(r, crows), :], q16[e])
        mx = None
        for t in range(ch):
            j = c * ch + t
            add = add_ref[e, 0, pl.ds(j, 1), :]
            if far:
                st = raw[t * bs:(t + 1) * bs, :] + add
            else:
                d = jnp.clip(i - j, 0, BIAS_TILES - 1)
                st = jnp.where(add > 0.5 * NEG_INF, raw[t * bs:(t + 1) * bs, :] + bias_ref[0, d], NEG_INF)
            buf[e, t * bs:(t + 1) * bs, :] = st
            tmx = jnp.max(st, axis=0, keepdims=True)
            mx = tmx if mx is None else jnp.maximum(mx, tmx)
        return mx

    def absorb(buf, e, c, mx, m, acc):
        r = pl.multiple_of(c * crows, crows)
        m_new = jnp.maximum(m, mx)
        p = jnp.exp2(buf[e] - m_new)
        acc = jnp.exp2(m - m_new) * acc + _dot(vt_ref[e, 0, :, pl.ds(r, crows)], p.astype(BF16))
        return m_new, acc

    def pair(far, c, carry):
        m, acc, mx_a = (list(x) for x in carry)
        mx_b = [chunk_scores(buf_b, e, 2 * c + 1, far) for e in E]
        for e in E:
            m[e], acc[e] = absorb(buf_a, e, 2 * c, mx_a[e], m[e], acc[e])
        mx_a = [chunk_scores(buf_a, e, jnp.minimum(2 * c + 2, last_chunk), far) for e in E]
        for e in E:
            m[e], acc[e] = absorb(buf_b, e, 2 * c + 1, mx_b[e], m[e], acc[e])
        return tuple(m), tuple(acc), tuple(mx_a)

    def step(c, carry):
        all_far = (2 * c + 3) * ch - 1 + (BIAS_TILES - 1) <= i
        return lax.cond(all_far, functools.partial(pair, True, c), functools.partial(pair, False, c), carry)

    m0 = tuple(jnp.full((1, bs), MOBA_M_INIT, F32) for e in E)
    acc0 = tuple(jnp.zeros((MOBA_V_ROWS, bs), F32) for e in E)
    mx_a = tuple(chunk_scores(buf_a, e, 0, False) for e in E)
    m, acc, _ = lax.fori_loop(0, n_pairs, step, (m0, acc0, mx_a))
    for e in E:
        o_ref[e] = (acc[e][:HEAD_DIM] / acc[e][HEAD_DIM:HEAD_DIM + 1]).T.astype(o_ref.dtype)


def _moba(q16, kn, vt, add, bias):
    b, hh, t, dh = kn.shape
    bs = MOBA_BLOCK
    nb = t // bs
    ns = MOBA_STREAMS
    crows = MOBA_TILES_PER_STEP * bs
    assert b % ns == 0 and nb % (2 * MOBA_TILES_PER_STEP) == 0
    return pl.pallas_call(
        functools.partial(_moba_kernel, nb=nb),
        grid=(b // ns, hh, nb),
        in_specs=[pl.BlockSpec((ns, 1, bs, dh), lambda bi, hi, ti: (bi, hi, ti, 0)),
                  pl.BlockSpec((ns, 1, t, dh), lambda bi, hi, ti: (bi, hi, 0, 0)),
                  pl.BlockSpec((ns, 1, MOBA_V_ROWS, t), lambda bi, hi, ti: (bi, hi, 0, 0)),
                  pl.BlockSpec((ns, 1, nb, bs), lambda bi, hi, ti: (bi, hi, 0, ti)),
                  pl.BlockSpec((1, BIAS_TILES, bs, bs), lambda bi, hi, ti: (hi, 0, 0, 0))],
        out_specs=pl.BlockSpec((ns, bs, dh), lambda bi, hi, ti: (bi, ti, hi)),
        out_shape=jax.ShapeDtypeStruct((b, t, hh * dh), BF16),
        scratch_shapes=[pltpu.VMEM((ns, crows, bs), F32),
                        pltpu.VMEM((ns, crows, bs), F32)],
        compiler_params=_cparams("parallel", "parallel", "parallel"),
        name="moba",
    )(q16, kn, vt, add, bias)


def _tile(n, want):
    t = min(n, want)
    assert n % t == 0, (n, want)
    return t


def kernel(x, norm_mix_w, w_in, conv_w, a_log, dt_bias, gdn_o_norm_w, q_norm_w, k_norm_w, rel_bias,
           w_branch_gdn, w_branch_moba, w_out, norm_ffn_w, w_ffn_gate, w_ffn_up, w_ffn_down):
    b, t, d = x.shape
    m = b * t
    gw = GDN_HEADS * HEAD_DIM
    mw = MOBA_HEADS * HEAD_DIM
    assert t % MOBA_PREP_ROWS == 0 and w_in.shape[0] == 1
    assert w_in.shape[2] == 4 * gw + 2 * GDN_HEADS + 3 * mw + 2 * d

    wi = w_in[0]
    o_bd = 4 * gw
    o_moba = o_bd + 2 * GDN_HEADS
    o_gate = o_moba + 3 * mw
    n_main = 2 * d + 4 * gw + 3 * mw
    tn_proj = PROJ_TILE[1]
    n_proj = -(-(n_main + LANES) // tn_proj) * tn_proj
    wi = lax.optimization_barrier(wi.astype(BF16))
    w_proj = jnp.concatenate(
        [wi[:, o_gate:], wi[:, :o_bd], wi[:, o_moba:o_gate], wi[:, o_bd:o_moba],
         jnp.zeros((d, n_proj - n_main - 2 * GDN_HEADS), BF16)], axis=1)
    col_gdn = (2 * d) // LANES
    col_moba = col_gdn + 4 * GDN_HEADS
    col_bd = col_moba + 3 * MOBA_HEADS

    x2 = x.reshape(m, d)
    proj = _norm_matmul(x2, norm_mix_w, w_proj, _tile(m, PROJ_TILE[0]), tn_proj, F32, "proj")
    proj3 = proj.reshape(b, t, n_proj)

    lane_pad = jnp.zeros((1, LANES - 2 * GDN_HEADS), F32)
    head_pad = jnp.zeros((1, GDN_HEADS), F32)
    alog_row = jnp.concatenate([head_pad, a_log, lane_pad], axis=1)
    dtb_row = jnp.concatenate([head_pad, dt_bias, lane_pad], axis=1)
    y_a = _gdn(proj3, conv_w[0], alog_row, dtb_row, gdn_o_norm_w, col_gdn, col_bd, _tile(t, GDN_BLOCK))

    bias = _bias_tiles(rel_bias)
    q16, kn, vt, add = _moba_prep(proj3, q_norm_w, k_norm_w, rel_bias, col_moba, MOBA_PREP_ROWS)
    y_b = _moba(q16, kn, vt, add, bias)

    mix = _mix(y_a.reshape(m, gw), y_b.reshape(m, mw), proj,
               w_branch_gdn[0].astype(BF16), w_branch_moba[0].astype(BF16), _tile(m, MIX_ROWS), d)
    h1 = _matmul_res(mix, w_out[0].astype(BF16), x2, _tile(m, OUT_ROWS), d, "out")

    hid = _ffn_up(h1, norm_ffn_w, w_ffn_gate[0].astype(BF16), w_ffn_up[0].astype(BF16),
                  _tile(m, FFN_UP_TILE[0]), FFN_UP_TILE[1])
    h2 = _matmul_res(hid, w_ffn_down[0].astype(BF16), h1, _tile(m, FFN_DOWN_TILE[0]), FFN_DOWN_TILE[1],
                     "ffn_down")
    return h2.reshape(b, t, d)
```

```python
import functools
import math

import jax
import jax.numpy as jnp
from jax import lax
from jax.experimental import pallas as pl
from jax.experimental.pallas import tpu as pltpu

F32 = jnp.float32
BF16 = jnp.bfloat16
HIGHEST = lax.Precision.HIGHEST

LANES = 128
HEAD_DIM = 128
GDN_HEADS = 8
GDN_CONV = 4
GDN_CHUNK = 64
GDN_HEADS_PER_STEP = 8
NEG_INF = -1e30
MOBA_HEADS = 8
MOBA_BLOCK = 256
MOBA_TOPK = 3
REL_BUCKETS = 32
REL_MAX_DIST = 2048
BIAS_TILES = 8
MOBA_TILES_PER_STEP = 2
MOBA_PREP_ROWS = 2048
MOBA_STREAMS = 4
MOBA_V_ROWS = HEAD_DIM + 16
MOBA_M_INIT = 0.1 * NEG_INF
LOG2E = math.log2(math.e)
RMS_EPS = 1e-6
V7X_VMEM_BYTES = 64 * 1024 * 1024
VMEM_LIMIT = V7X_VMEM_BYTES * 7 // 8

PROJ_TILE = (1024, 1280)
MIX_ROWS = 512
OUT_ROWS = 512
FFN_UP_TILE = (1024, 512)
FFN_DOWN_TILE = (1024, 512)
GDN_BLOCK = 256
REGROUP_ROWS = 128


def _cparams(*sem):
    return pltpu.CompilerParams(dimension_semantics=sem, vmem_limit_bytes=VMEM_LIMIT)


def _sigmoid(x):
    return 1.0 / (1.0 + jnp.exp(-x))


def _silu(x):
    h = 0.5 * x
    return h + h * jnp.tanh(h)


def _dot(a, b, precision=None):
    return jnp.dot(a, b, preferred_element_type=F32, precision=precision)


def _dot_nt(a, b, precision=None):
    return lax.dot_general(a, b, (((1,), (1,)), ((), ())),
                           preferred_element_type=F32, precision=precision)


def _dot_tn(a, b):
    return lax.dot_general(a, b, (((0,), (0,)), ((), ())), preferred_element_type=F32)


def _rms_rows(x, w):
    return x * lax.rsqrt(jnp.mean(x * x, axis=-1, keepdims=True) + RMS_EPS) * w


def _regroup_kernel(w_ref, o_ref, *, cuts):
    o_bd, o_moba, o_gate = cuts
    x = w_ref[...]
    parts = [x[:, o_gate:], x[:, :o_bd], x[:, o_moba:o_gate], x[:, o_bd:o_moba]]
    used = sum(p.shape[1] for p in parts)
    parts.append(jnp.zeros((x.shape[0], o_ref.shape[1] - used), x.dtype))
    o_ref[...] = jnp.concatenate(parts, axis=1).astype(o_ref.dtype)


def _regroup_weights(w, cuts, n_out):
    k, n_in = w.shape
    rows = _tile(k, REGROUP_ROWS)
    return pl.pallas_call(
        functools.partial(_regroup_kernel, cuts=cuts),
        grid=(k // rows,),
        in_specs=[pl.BlockSpec((rows, n_in), lambda i: (i, 0))],
        out_specs=pl.BlockSpec((rows, n_out), lambda i: (i, 0)),
        out_shape=jax.ShapeDtypeStruct((k, n_out), BF16),
        compiler_params=_cparams("parallel"),
        name="regroup",
    )(w)


def _norm_matmul_kernel(x_ref, nw_ref, w_ref, o_ref, u_scr):
    @pl.when(pl.program_id(1) == 0)
    def _():
        u_scr[...] = _rms_rows(x_ref[...], nw_ref[...]).astype(BF16)

    o_ref[...] = _dot(u_scr[...], w_ref[...]).astype(o_ref.dtype)


def _norm_matmul(x, nw, w, tm, tn, out_dtype, name):
    m, k = x.shape
    n = w.shape[1]
    return pl.pallas_call(
        _norm_matmul_kernel,
        grid=(m // tm, n // tn),
        in_specs=[pl.BlockSpec((tm, k), lambda i, j: (i, 0)),
                  pl.BlockSpec((1, k), lambda i, j: (0, 0)),
                  pl.BlockSpec((k, tn), lambda i, j: (0, j))],
        out_specs=pl.BlockSpec((tm, tn), lambda i, j: (i, j)),
        out_shape=jax.ShapeDtypeStruct((m, n), out_dtype),
        scratch_shapes=[pltpu.VMEM((tm, k), BF16)],
        compiler_params=_cparams("parallel", "arbitrary"),
        name=name,
    )(x, nw, w)


def _matmul_res_kernel(a_ref, w_ref, r_ref, o_ref):
    o_ref[...] = r_ref[...] + _dot(a_ref[...], w_ref[...])


def _matmul_res(a, w, res, tm, tn, name):
    m, k = a.shape
    n = w.shape[1]
    return pl.pallas_call(
        _matmul_res_kernel,
        grid=(m // tm, n // tn),
        in_specs=[pl.BlockSpec((tm, k), lambda i, j: (i, 0)),
                  pl.BlockSpec((k, tn), lambda i, j: (0, j)),
                  pl.BlockSpec((tm, tn), lambda i, j: (i, j))],
        out_specs=pl.BlockSpec((tm, tn), lambda i, j: (i, j)),
        out_shape=jax.ShapeDtypeStruct((m, n), F32),
        compiler_params=_cparams("parallel", "parallel"),
        name=name,
    )(a, w, res)


def _ffn_up_kernel(x_ref, nw_ref, wg_ref, wu_ref, o_ref, u_scr):
    @pl.when(pl.program_id(1) == 0)
    def _():
        u_scr[...] = _rms_rows(x_ref[...], nw_ref[...]).astype(BF16)

    u = u_scr[...]
    g = _dot(u, wg_ref[...])
    o_ref[...] = (g * _sigmoid(g) * _dot(u, wu_ref[...])).astype(o_ref.dtype)


def _ffn_up(x, nw, wg, wu, tm, tn):
    m, k = x.shape
    n = wg.shape[1]
    return pl.pallas_call(
        _ffn_up_kernel,
        grid=(m // tm, n // tn),
        in_specs=[pl.BlockSpec((tm, k), lambda i, j: (i, 0)),
                  pl.BlockSpec((1, k), lambda i, j: (0, 0)),
                  pl.BlockSpec((k, tn), lambda i, j: (0, j)),
                  pl.BlockSpec((k, tn), lambda i, j: (0, j))],
        out_specs=pl.BlockSpec((tm, tn), lambda i, j: (i, j)),
        out_shape=jax.ShapeDtypeStruct((m, n), BF16),
        scratch_shapes=[pltpu.VMEM((tm, k), BF16)],
        compiler_params=_cparams("parallel", "arbitrary"),
        name="ffn_up",
    )(x, nw, wg, wu)


def _mix_kernel(ya_ref, yb_ref, ga_ref, gb_ref, wa_ref, wb_ref, o_ref):
    a = _dot(ya_ref[...], wa_ref[...])
    b = _dot(yb_ref[...], wb_ref[...])
    o_ref[...] = (_sigmoid(ga_ref[...]) * a + _sigmoid(gb_ref[...]) * b).astype(o_ref.dtype)


def _mix(ya, yb, proj, wa, wb, tm, tn):
    m, k = ya.shape
    n = wa.shape[1]
    nb = n // tn
    return pl.pallas_call(
        _mix_kernel,
        grid=(m // tm, nb),
        in_specs=[pl.BlockSpec((tm, k), lambda i, j: (i, 0)),
                  pl.BlockSpec((tm, k), lambda i, j: (i, 0)),
                  pl.BlockSpec((tm, tn), lambda i, j: (i, j)),
                  pl.BlockSpec((tm, tn), lambda i, j: (i, nb + j)),
                  pl.BlockSpec((k, tn), lambda i, j: (0, j)),
                  pl.BlockSpec((k, tn), lambda i, j: (0, j))],
        out_specs=pl.BlockSpec((tm, tn), lambda i, j: (i, j)),
        out_shape=jax.ShapeDtypeStruct((m, n), BF16),
        compiler_params=_cparams("parallel", "parallel"),
        name="mix",
    )(ya, yb, proj, proj, wa, wb)


def _split3(x):
    a = x.astype(BF16)
    r = x - a.astype(F32)
    b = r.astype(BF16)
    c = (r - b.astype(F32)).astype(BF16)
    return a, b, c


def _gdn_kernel(q_ref, k_ref, v_ref, z_ref, bd_ref, cwq_ref, cwk_ref, cwv_ref,
                alog_ref, dtb_ref, onw_ref, o_ref, s_scr, tail_scr, *, tb):
    hp = GDN_HEADS_PER_STEP
    C = GDN_CHUNK
    D = HEAD_DIM
    head0 = pl.program_id(1) * hp

    @pl.when(pl.program_id(2) == 0)
    def _():
        s_scr[...] = jnp.zeros_like(s_scr)
        tail_scr[...] = jnp.zeros_like(tail_scr)

    row8 = lax.broadcasted_iota(jnp.int32, (8, hp * D), 0)
    tails = []

    def conv_silu(u_ref, cw_ref, idx):
        u = u_ref[0]
        w = cw_ref[...]
        tail = tail_scr[idx]
        y = None
        ytop = None
        for s in (3, 2, 1):
            sh = pltpu.roll(u, s, axis=0)
            top = jnp.where(row8 < s, pltpu.roll(tail, s, axis=0), sh[0:8])
            wj = w[3 - s:4 - s]
            y = sh * wj if y is None else y + sh * wj
            ytop = top * wj if ytop is None else ytop + top * wj
        y = y + u * w[3:4]
        ytop = ytop + u[0:8] * w[3:4]
        tails.append(u[tb - 8:tb])
        y = jnp.concatenate([ytop, y[8:]], axis=0)
        return _silu(y)

    q_all = conv_silu(q_ref, cwq_ref, 0)
    k_all = conv_silu(k_ref, cwk_ref, 1)
    v_all = conv_silu(v_ref, cwv_ref, 2)

    bd = bd_ref[0]
    lane = lax.broadcasted_iota(jnp.int32, (tb, LANES), 1)
    beta_all = _sigmoid(bd)
    xg = bd + dtb_ref[...]
    softplus = jnp.maximum(xg, 0.0) + jnp.log1p(jnp.exp(-jnp.abs(xg)))
    g_all = -jnp.exp(alog_ref[...]) * softplus
    betas = [jnp.sum(jnp.where(lane == head0 + e, beta_all, 0.0), axis=-1, keepdims=True)
             for e in range(hp)]
    gs = [jnp.sum(jnp.where(lane == head0 + e + GDN_HEADS, g_all, 0.0), axis=-1, keepdims=True)
          for e in range(hp)]

    nc = tb // C
    ri = lax.broadcasted_iota(jnp.int32, (tb, tb), 0)
    ci = lax.broadcasted_iota(jnp.int32, (tb, tb), 1)
    same = (ri // C) == (ci // C)
    tril16 = jnp.where(same & (ri >= ci), 1.0, 0.0).astype(BF16)
    g_b = jnp.concatenate([jnp.broadcast_to(g, (tb, LANES)) for g in gs], axis=1)
    gc_all = sum(_dot(tril16, piece) for piece in _split3(g_b))

    iw = lax.broadcasted_iota(jnp.int32, (C, tb), 0)
    jw = lax.broadcasted_iota(jnp.int32, (C, tb), 1) % C
    incl_w = iw >= jw
    strict_w = iw > jw
    eye_w = jnp.where(iw == jw, 1.0, 0.0).astype(F32)
    low_half = (lax.broadcasted_iota(jnp.int32, (C, LANES), 1) < C)

    def block_diag(wide16):
        return jnp.where(same, jnp.concatenate([wide16] * nc, axis=0), jnp.zeros((), BF16))

    def pair_blocks(full):
        return jnp.where(low_half, full[:C], full[C:])

    E = range(hp)
    pairs = range(tb // LANES)

    def l2n(x):
        return x * lax.rsqrt(jnp.sum(x * x, axis=-1, keepdims=True) + RMS_EPS)

    qs = [l2n(q_all[:, e * D:(e + 1) * D]) * (D ** -0.5) for e in E]
    ks = [l2n(k_all[:, e * D:(e + 1) * D]) for e in E]
    vs = [v_all[:, e * D:(e + 1) * D] for e in E]
    gcs = [gc_all[:, e * D:(e + 1) * D] for e in E]
    g_col = [jnp.concatenate([pair_blocks(gc[p * LANES:(p + 1) * LANES]) for p in pairs], axis=1)
             for gc in gcs]
    g_row = [jnp.concatenate([gc[p * LANES:(p + 1) * LANES, :].T[0:1, :] for p in pairs], axis=1)
             for gc in gcs]
    decay = [jnp.where(incl_w, jnp.exp(jnp.where(incl_w, g_col[e] - g_row[e], 0.0)), 0.0) for e in E]
    k16 = [k.astype(BF16) for k in ks]
    q16 = [q.astype(BF16) for q in qs]
    kb = [ks[e] * betas[e] for e in E]
    kb16 = [x.astype(BF16) for x in kb]

    def pair_products(a16, b16):
        return jnp.concatenate([pair_blocks(_dot_nt(a16[p * LANES:(p + 1) * LANES],
                                                    b16[p * LANES:(p + 1) * LANES])) for p in pairs], axis=1)

    lmat = [jnp.where(strict_w, pair_products(kb16[e], k16[e]) * decay[e], 0.0) for e in E]
    amat = [jnp.where(incl_w, pair_products(q16[e], k16[e]) * decay[e], 0.0) for e in E]
    p16 = [(-lmat[e]).astype(BF16) for e in E]
    tinv = [eye_w - lmat[e] for e in E]
    pw = [_dot(p16[e], block_diag(p16[e])) for e in E]
    for _ in range(4):
        p16 = [pw[e].astype(BF16) for e in E]
        prod = [_dot(jnp.concatenate([p16[e], tinv[e].astype(BF16)], axis=0), block_diag(p16[e])) for e in E]
        pw = [prod[e][:C] for e in E]
        tinv = [tinv[e] + prod[e][C:] for e in E]
    tinv = [tinv[e] + _dot(tinv[e].astype(BF16), block_diag(pw[e].astype(BF16))) for e in E]
    eg = [jnp.exp(gc) for gc in gcs]
    rhs = [jnp.concatenate([kb[e] * eg[e], vs[e] * betas[e]], axis=1).astype(BF16) for e in E]
    wu = [_dot(block_diag(tinv[e].astype(BF16)), rhs[e]).astype(BF16) for e in E]
    au = [_dot(block_diag(amat[e].astype(BF16)), wu[e]) for e in E]
    q_eff = [(qs[e] * eg[e] - au[e][:, :D]).astype(BF16) for e in E]
    gl = [jnp.concatenate([jnp.broadcast_to(gc[c * C + C - 1:c * C + C, :], (C, LANES))
                           for c in range(nc)], axis=0) for gc in gcs]
    k_dec = [(ks[e] * jnp.exp(gl[e] - gcs[e])).astype(BF16) for e in E]
    pn = [[_dot_tn(k_dec[e][c * C:(c + 1) * C], wu[e][c * C:(c + 1) * C]) for c in range(nc)]
          for e in E]

    states = [s_scr[e] for e in E]
    onw = onw_ref[...]
    for c in range(nc):
        r0 = c * C
        lhs = [jnp.concatenate([pn[e][c][:, :D].astype(BF16), q_eff[e][r0:r0 + C]], axis=0) for e in E]
        res = [_dot(lhs[e], states[e].astype(BF16)) for e in E]
        outs = [_rms_rows(res[e][D:] + au[e][r0:r0 + C, D:], onw) for e in E]
        states = [states[e] * jnp.exp(gl[e][r0:r0 + 1, :]) - res[e][:D] + pn[e][c][:, D:] for e in E]
        zc = z_ref[0, r0:r0 + C, :]
        o_ref[0, r0:r0 + C, :] = (jnp.concatenate(outs, axis=1) * _silu(zc)).astype(o_ref.dtype)
    for e in range(hp):
        s_scr[e] = states[e]
    for idx in range(3):
        tail_scr[idx] = tails[idx]


def _gdn(proj3, conv_w, alog_row, dtb_row, onw, col0, bd_col, tb):
    b, t, _ = proj3.shape
    hp = GDN_HEADS_PER_STEP
    ng = GDN_HEADS // hp
    w = hp * HEAD_DIM

    def col(base):
        return pl.BlockSpec((1, tb, w), lambda bi, hi, ti: (bi, ti, base // hp + hi))

    def cw(base):
        return pl.BlockSpec((GDN_CONV, w), lambda bi, hi, ti: (0, base // hp + hi))

    assert col0 % hp == 0
    row = pl.BlockSpec((1, LANES), lambda bi, hi, ti: (0, 0))
    return pl.pallas_call(
        functools.partial(_gdn_kernel, tb=tb),
        grid=(b, ng, t // tb),
        in_specs=[col(col0), col(col0 + GDN_HEADS), col(col0 + 2 * GDN_HEADS), col(col0 + 3 * GDN_HEADS),
                  pl.BlockSpec((1, tb, LANES), lambda bi, hi, ti: (bi, ti, bd_col)),
                  cw(0), cw(GDN_HEADS), cw(2 * GDN_HEADS), row, row, row],
        out_specs=pl.BlockSpec((1, tb, w), lambda bi, hi, ti: (bi, ti, hi)),
        out_shape=jax.ShapeDtypeStruct((b, t, GDN_HEADS * HEAD_DIM), BF16),
        scratch_shapes=[pltpu.VMEM((hp, HEAD_DIM, HEAD_DIM), F32),
                        pltpu.VMEM((3, 8, w), F32)],
        compiler_params=_cparams("parallel", "parallel", "arbitrary"),
        name="gdn",
    )(proj3, proj3, proj3, proj3, proj3, conv_w, conv_w, conv_w, alog_row, dtb_row, onw)


def _bias_kernel(rel_ref, o_ref):
    h = pl.program_id(0)
    bs = MOBA_BLOCK
    max_exact = REL_BUCKETS // 2
    x = lax.broadcasted_iota(jnp.int32, (8, 2 * bs), 1)
    for d in range(BIAS_TILES):
        dist = jnp.maximum(d * bs + x - bs, 0)
        df = dist.astype(F32)
        log_ratio = jnp.log(jnp.maximum(df, float(max_exact)) / max_exact) / math.log(REL_MAX_DIST / max_exact)
        large = max_exact + (log_ratio * (REL_BUCKETS - max_exact)).astype(jnp.int32)
        large = jnp.minimum(large, REL_BUCKETS - 1)
        bucket = jnp.where(dist < max_exact, dist, large)
        row = jnp.zeros((8, 2 * bs), F32)
        for b in range(REL_BUCKETS):
            row = jnp.where(bucket == b, rel_ref[b, h], row)
        base = jnp.broadcast_to(row[0:1, :] * LOG2E, (bs, 2 * bs))
        tile = pltpu.roll(base, 0, 1, stride=1, stride_axis=0)[:, bs:]
        if d == 0:
            kk = lax.broadcasted_iota(jnp.int32, (bs, bs), 0)
            qq = lax.broadcasted_iota(jnp.int32, (bs, bs), 1)
            tile = jnp.where(kk <= qq, tile, NEG_INF)
        o_ref[0, d] = tile


def _bias_tiles(rel_bias):
    bs = MOBA_BLOCK
    max_exact = REL_BUCKETS // 2
    nearest = (BIAS_TILES - 1) * bs - (bs - 1)
    assert max_exact + int(math.log(nearest / max_exact) / math.log(REL_MAX_DIST / max_exact)
                           * (REL_BUCKETS - max_exact)) >= REL_BUCKETS - 1
    return pl.pallas_call(
        _bias_kernel,
        grid=(MOBA_HEADS,),
        in_specs=[pl.BlockSpec(memory_space=pltpu.SMEM)],
        out_specs=pl.BlockSpec((1, BIAS_TILES, bs, bs), lambda h: (h, 0, 0, 0)),
        out_shape=jax.ShapeDtypeStruct((MOBA_HEADS, BIAS_TILES, bs, bs), F32),
        compiler_params=_cparams("parallel"),
        name="bias",
    )(rel_bias)


def _moba_prep_kernel(q_ref, k_ref, v_ref, qw_ref, kw_ref, rel_ref, q16_ref, kn_ref, vt_ref, add_ref, km_scr,
                      *, rows, nb):
    bs = MOBA_BLOCK
    nsub = rows // bs
    ti = pl.program_id(2)

    @pl.when(ti == 0)
    def _():
        km_scr[...] = jnp.zeros_like(km_scr)

    kn = _rms_rows(k_ref[0], kw_ref[...])
    kn_ref[0, 0] = kn.astype(BF16)
    for s in range(nsub):
        km_scr[pl.ds(ti * nsub + s, 1), :] = jnp.mean(kn[s * bs:(s + 1) * bs], axis=0, keepdims=True)
        vt_ref[0, 0, 0:HEAD_DIM, s * bs:(s + 1) * bs] = v_ref[0, s * bs:(s + 1) * bs, :].T.astype(BF16)
    pad_row = lax.broadcasted_iota(jnp.int32, (MOBA_V_ROWS - HEAD_DIM, rows), 0)
    vt_ref[0, 0, HEAD_DIM:, :] = jnp.where(pad_row == 0, 1.0, 0.0).astype(BF16)

    q = _rms_rows(q_ref[0], qw_ref[...])
    q16_ref[0, 0] = (q * ((HEAD_DIM ** -0.5) * LOG2E)).astype(BF16)

    blk = lax.broadcasted_iota(jnp.int32, (nb, rows), 0)
    own_blk = ti * nsub + lax.broadcasted_iota(jnp.int32, (nb, rows), 1) // bs
    blkf = blk.astype(F32)
    valid = blk < own_blk
    work = jnp.where(valid, _dot_nt(km_scr[...], q, HIGHEST), NEG_INF)
    picked = jnp.zeros((nb, rows), F32)
    for _ in range(MOBA_TOPK):
        best = jnp.max(work, axis=0, keepdims=True)
        first = jnp.min(jnp.where(work == best, blkf, float(nb)), axis=0, keepdims=True)
        hit = blkf == first
        picked = jnp.where(hit, 1.0, picked)
        work = jnp.where(hit, -jnp.inf, work)
    far_bias = rel_ref[REL_BUCKETS - 1, pl.program_id(1)] * LOG2E
    attended = (valid & (picked > 0.0)) | (blk == own_blk)
    add_ref[0, 0] = jnp.where(attended, far_bias, NEG_INF)


def _moba_prep(proj3, qw, kw, rel_bias, col0, rows):
    b, t, _ = proj3.shape
    hh, bs, dh = MOBA_HEADS, MOBA_BLOCK, HEAD_DIM
    nb = t // bs

    def col(base):
        return pl.BlockSpec((1, rows, LANES), lambda bi, hi, ti: (bi, ti, base + hi))

    row = pl.BlockSpec((1, LANES), lambda bi, hi, ti: (0, 0))
    return pl.pallas_call(
        functools.partial(_moba_prep_kernel, rows=rows, nb=nb),
        grid=(b, hh, t // rows),
        in_specs=[col(col0), col(col0 + hh), col(col0 + 2 * hh), row, row,
                  pl.BlockSpec(memory_space=pltpu.SMEM)],
        out_specs=[pl.BlockSpec((1, 1, rows, dh), lambda bi, hi, ti: (bi, hi, ti, 0)),
                   pl.BlockSpec((1, 1, rows, dh), lambda bi, hi, ti: (bi, hi, ti, 0)),
                   pl.BlockSpec((1, 1, MOBA_V_ROWS, rows), lambda bi, hi, ti: (bi, hi, 0, ti)),
                   pl.BlockSpec((1, 1, nb, rows), lambda bi, hi, ti: (bi, hi, 0, ti))],
        out_shape=[jax.ShapeDtypeStruct((b, hh, t, dh), BF16),
                   jax.ShapeDtypeStruct((b, hh, t, dh), BF16),
                   jax.ShapeDtypeStruct((b, hh, MOBA_V_ROWS, t), BF16),
                   jax.ShapeDtypeStruct((b, hh, nb, t), F32)],
        scratch_shapes=[pltpu.VMEM((nb, dh), F32)],
        compiler_params=_cparams("parallel", "parallel", "arbitrary"),
        name="moba_prep",
    )(proj3, proj3, proj3, qw, kw, rel_bias)


def _moba_kernel(q_ref, kn_ref, vt_ref, add_ref, bias_ref, o_ref, buf_a, buf_b, *, nb):
    i = pl.program_id(2)
    bs = MOBA_BLOCK
    ch = MOBA_TILES_PER_STEP
    E = range(MOBA_STREAMS)
    q16 = [q_ref[e, 0] for e in E]

    crows = ch * bs
    n_pairs = (i + 2 * ch) // (2 * ch)
    last_chunk = nb // ch - 1

    def chunk_scores(buf, e, c, far):
        r = pl.multiple_of(c * crows, crows)
        raw = _dot_nt(kn_ref[e, 0, pl.ds(r, crows), :], q16[e])
        mx = None
        for t in range(ch):
            j = c * ch + t
            add = add_ref[e, 0, pl.ds(j, 1), :]
            if far:
                st = raw[t * bs:(t + 1) * bs, :] + add
            else:
                d = jnp.clip(i - j, 0, BIAS_TILES - 1)
                st = jnp.where(add > 0.5 * NEG_INF, raw[t * bs:(t + 1) * bs, :] + bias_ref[0, d], NEG_INF)
            buf[e, t * bs:(t + 1) * bs, :] = st
            tmx = jnp.max(st, axis=0, keepdims=True)
            mx = tmx if mx is None else jnp.maximum(mx, tmx)
        return mx

    def absorb(buf, e, c, mx, m, acc):
        r = pl.multiple_of(c * crows, crows)
        m_new = jnp.maximum(m, mx)
        p = jnp.exp2(buf[e] - m_new)
        acc = jnp.exp2(m - m_new) * acc + _dot(vt_ref[e, 0, :, pl.ds(r, crows)], p.astype(BF16))
        return m_new, acc

    def pair(far, c, carry):
        m, acc, mx_a = (list(x) for x in carry)
        mx_b = [chunk_scores(buf_b, e, 2 * c + 1, far) for e in E]
        for e in E:
            m[e], acc[e] = absorb(buf_a, e, 2 * c, mx_a[e], m[e], acc[e])
        mx_a = [chunk_scores(buf_a, e, jnp.minimum(2 * c + 2, last_chunk), far) for e in E]
        for e in E:
            m[e], acc[e] = absorb(buf_b, e, 2 * c + 1, mx_b[e], m[e], acc[e])
        return tuple(m), tuple(acc), tuple(mx_a)

    def step(c, carry):
        all_far = (2 * c + 3) * ch - 1 + (BIAS_TILES - 1) <= i
        return lax.cond(all_far, functools.partial(pair, True, c), functools.partial(pair, False, c), carry)

    m0 = tuple(jnp.full((1, bs), MOBA_M_INIT, F32) for e in E)
    acc0 = tuple(jnp.zeros((MOBA_V_ROWS, bs), F32) for e in E)
    mx_a = tuple(chunk_scores(buf_a, e, 0, False) for e in E)
    m, acc, _ = lax.fori_loop(0, n_pairs, step, (m0, acc0, mx_a))
    for e in E:
        o_ref[e] = (acc[e][:HEAD_DIM] / acc[e][HEAD_DIM:HEAD_DIM + 1]).T.astype(o_ref.dtype)


def _moba(q16, kn, vt, add, bias):
    b, hh, t, dh = kn.shape
    bs = MOBA_BLOCK
    nb = t // bs
    ns = MOBA_STREAMS
    crows = MOBA_TILES_PER_STEP * bs
    assert b % ns == 0 and nb % (2 * MOBA_TILES_PER_STEP) == 0
    return pl.pallas_call(
        functools.partial(_moba_kernel, nb=nb),
        grid=(b // ns, hh, nb),
        in_specs=[pl.BlockSpec((ns, 1, bs, dh), lambda bi, hi, ti: (bi, hi, ti, 0)),
                  pl.BlockSpec((ns, 1, t, dh), lambda bi, hi, ti: (bi, hi, 0, 0)),
                  pl.BlockSpec((ns, 1, MOBA_V_ROWS, t), lambda bi, hi, ti: (bi, hi, 0, 0)),
                  pl.BlockSpec((ns, 1, nb, bs), lambda bi, hi, ti: (bi, hi, 0, ti)),
                  pl.BlockSpec((1, BIAS_TILES, bs, bs), lambda bi, hi, ti: (hi, 0, 0, 0))],
        out_specs=pl.BlockSpec((ns, bs, dh), lambda bi, hi, ti: (bi, ti, hi)),
        out_shape=jax.ShapeDtypeStruct((b, t, hh * dh), BF16),
        scratch_shapes=[pltpu.VMEM((ns, crows, bs), F32),
                        pltpu.VMEM((ns, crows, bs), F32)],
        compiler_params=_cparams("parallel", "parallel", "parallel"),
        name="moba",
    )(q16, kn, vt, add, bias)


def _tile(n, want):
    t = min(n, want)
    assert n % t == 0, (n, want)
    return t


def kernel(x, norm_mix_w, w_in, conv_w, a_log, dt_bias, gdn_o_norm_w, q_norm_w, k_norm_w, rel_bias,
           w_branch_gdn, w_branch_moba, w_out, norm_ffn_w, w_ffn_gate, w_ffn_up, w_ffn_down):
    b, t, d = x.shape
    m = b * t
    gw = GDN_HEADS * HEAD_DIM
    mw = MOBA_HEADS * HEAD_DIM
    assert t % MOBA_PREP_ROWS == 0 and w_in.shape[0] == 1
    assert w_in.shape[2] == 4 * gw + 2 * GDN_HEADS + 3 * mw + 2 * d

    wi = w_in[0]
    o_bd = 4 * gw
    o_moba = o_bd + 2 * GDN_HEADS
    o_gate = o_moba + 3 * mw
    n_main = 2 * d + 4 * gw + 3 * mw
    tn_proj = PROJ_TILE[1]
    n_proj = -(-(n_main + LANES) // tn_proj) * tn_proj
    w_proj = _regroup_weights(wi, (o_bd, o_moba, o_gate), n_proj)
    col_gdn = (2 * d) // LANES
    col_moba = col_gdn + 4 * GDN_HEADS
    col_bd = col_moba + 3 * MOBA_HEADS

    x2 = x.reshape(m, d)
    proj = _norm_matmul(x2, norm_mix_w, w_proj, _tile(m, PROJ_TILE[0]), tn_proj, F32, "proj")
    proj3 = proj.reshape(b, t, n_proj)

    lane_pad = jnp.zeros((1, LANES - 2 * GDN_HEADS), F32)
    head_pad = jnp.zeros((1, GDN_HEADS), F32)
    alog_row = jnp.concatenate([head_pad, a_log, lane_pad], axis=1)
    dtb_row = jnp.concatenate([head_pad, dt_bias, lane_pad], axis=1)
    y_a = _gdn(proj3, conv_w[0], alog_row, dtb_row, gdn_o_norm_w, col_gdn, col_bd, _tile(t, GDN_BLOCK))

    bias = _bias_tiles(rel_bias)
    q16, kn, vt, add = _moba_prep(proj3, q_norm_w, k_norm_w, rel_bias, col_moba, MOBA_PREP_ROWS)
    y_b = _moba(q16, kn, vt, add, bias)

    mix = _mix(y_a.reshape(m, gw), y_b.reshape(m, mw), proj,
               w_branch_gdn[0].astype(BF16), w_branch_moba[0].astype(BF16), _tile(m, MIX_ROWS), d)
    h1 = _matmul_res(mix, w_out[0].astype(BF16), x2, _tile(m, OUT_ROWS), d, "out")

    hid = _ffn_up(h1, norm_ffn_w, w_ffn_gate[0].astype(BF16), w_ffn_up[0].astype(BF16),
                  _tile(m, FFN_UP_TILE[0]), FFN_UP_TILE[1])
    h2 = _matmul_res(hid, w_ffn_down[0].astype(BF16), h1, _tile(m, FFN_DOWN_TILE[0]), FFN_DOWN_TILE[1],
                     "ffn_down")
    return h2.reshape(b, t, d)
```

```python
import functools
import math

import jax
import jax.numpy as jnp
from jax import lax
from jax.experimental import pallas as pl
from jax.experimental.pallas import tpu as pltpu

F32 = jnp.float32
BF16 = jnp.bfloat16
HIGHEST = lax.Precision.HIGHEST

LANES = 128
HEAD_DIM = 128
GDN_HEADS = 8
GDN_CONV = 4
GDN_CHUNK = 64
GDN_HEADS_PER_STEP = 8
NEG_INF = -1e30
MOBA_HEADS = 8
MOBA_BLOCK = 256
MOBA_TOPK = 3
REL_BUCKETS = 32
REL_MAX_DIST = 2048
BIAS_TILES = 8
MOBA_TILES_PER_STEP = 2
MOBA_PREP_ROWS = 2048
MOBA_STREAMS = 4
MOBA_V_ROWS = HEAD_DIM + 16
MOBA_M_INIT = 0.1 * NEG_INF
LOG2E = math.log2(math.e)
RMS_EPS = 1e-6
V7X_VMEM_BYTES = 64 * 1024 * 1024
VMEM_LIMIT = V7X_VMEM_BYTES * 7 // 8

PROJ_TILE = (1024, 1280)
MIX_ROWS = 512
OUT_ROWS = 512
FFN_UP_TILE = (1024, 512)
FFN_DOWN_TILE = (1024, 512)
GDN_BLOCK = 256
REGROUP_ROWS = 128


def _cparams(*sem):
    return pltpu.CompilerParams(dimension_semantics=sem, vmem_limit_bytes=VMEM_LIMIT)


def _sigmoid(x):
    return 1.0 / (1.0 + jnp.exp(-x))


def _silu(x):
    h = 0.5 * x
    return h + h * jnp.tanh(h)


def _dot(a, b, precision=None):
    return jnp.dot(a, b, preferred_element_type=F32, precision=precision)


def _dot_nt(a, b, precision=None):
    return lax.dot_general(a, b, (((1,), (1,)), ((), ())),
                           preferred_element_type=F32, precision=precision)


def _dot_tn(a, b):
    return lax.dot_general(a, b, (((0,), (0,)), ((), ())), preferred_element_type=F32)


def _rms_rows(x, w):
    return x * lax.rsqrt(jnp.mean(x * x, axis=-1, keepdims=True) + RMS_EPS) * w


def _regroup_kernel(w_ref, o_ref, *, cuts):
    o_bd, o_moba, o_gate = cuts
    x = w_ref[...]
    parts = [x[:, o_gate:], x[:, :o_bd], x[:, o_moba:o_gate], x[:, o_bd:o_moba]]
    used = sum(p.shape[1] for p in parts)
    parts.append(jnp.zeros((x.shape[0], o_ref.shape[1] - used), x.dtype))
    o_ref[...] = jnp.concatenate(parts, axis=1).astype(o_ref.dtype)


def _regroup_weights(w, cuts, n_out):
    _, k, n_in = w.shape
    rows = _tile(k, REGROUP_ROWS)
    return pl.pallas_call(
        functools.partial(_regroup_kernel, cuts=cuts),
        grid=(k // rows,),
        in_specs=[pl.BlockSpec((None, rows, n_in), lambda i: (0, i, 0))],
        out_specs=pl.BlockSpec((rows, n_out), lambda i: (i, 0)),
        out_shape=jax.ShapeDtypeStruct((k, n_out), BF16),
        compiler_params=_cparams("parallel"),
        name="regroup",
    )(w)


def _norm_matmul_kernel(x_ref, nw_ref, w_ref, o_ref, u_scr):
    @pl.when(pl.program_id(1) == 0)
    def _():
        u_scr[...] = _rms_rows(x_ref[...], nw_ref[...]).astype(BF16)

    o_ref[...] = _dot(u_scr[...], w_ref[...]).astype(o_ref.dtype)


def _norm_matmul(x, nw, w, tm, tn, out_dtype, name):
    m, k = x.shape
    n = w.shape[1]
    return pl.pallas_call(
        _norm_matmul_kernel,
        grid=(m // tm, n // tn),
        in_specs=[pl.BlockSpec((tm, k), lambda i, j: (i, 0)),
                  pl.BlockSpec((1, k), lambda i, j: (0, 0)),
                  pl.BlockSpec((k, tn), lambda i, j: (0, j))],
        out_specs=pl.BlockSpec((tm, tn), lambda i, j: (i, j)),
        out_shape=jax.ShapeDtypeStruct((m, n), out_dtype),
        scratch_shapes=[pltpu.VMEM((tm, k), BF16)],
        compiler_params=_cparams("parallel", "arbitrary"),
        name=name,
    )(x, nw, w)


def _matmul_res_kernel(a_ref, w_ref, r_ref, o_ref):
    o_ref[...] = r_ref[...] + _dot(a_ref[...], w_ref[...])


def _matmul_res(a, w, res, tm, tn, name):
    m, k = a.shape
    n = w.shape[1]
    return pl.pallas_call(
        _matmul_res_kernel,
        grid=(m // tm, n // tn),
        in_specs=[pl.BlockSpec((tm, k), lambda i, j: (i, 0)),
                  pl.BlockSpec((k, tn), lambda i, j: (0, j)),
                  pl.BlockSpec((tm, tn), lambda i, j: (i, j))],
        out_specs=pl.BlockSpec((tm, tn), lambda i, j: (i, j)),
        out_shape=jax.ShapeDtypeStruct((m, n), F32),
        compiler_params=_cparams("parallel", "parallel"),
        name=name,
    )(a, w, res)


def _ffn_up_kernel(x_ref, nw_ref, wg_ref, wu_ref, o_ref, u_scr):
    @pl.when(pl.program_id(1) == 0)
    def _():
        u_scr[...] = _rms_rows(x_ref[...], nw_ref[...]).astype(BF16)

    u = u_scr[...]
    g = _dot(u, wg_ref[...])
    o_ref[...] = (g * _sigmoid(g) * _dot(u, wu_ref[...])).astype(o_ref.dtype)


def _ffn_up(x, nw, wg, wu, tm, tn):
    m, k = x.shape
    n = wg.shape[1]
    return pl.pallas_call(
        _ffn_up_kernel,
        grid=(m // tm, n // tn),
        in_specs=[pl.BlockSpec((tm, k), lambda i, j: (i, 0)),
                  pl.BlockSpec((1, k), lambda i, j: (0, 0)),
                  pl.BlockSpec((k, tn), lambda i, j: (0, j)),
                  pl.BlockSpec((k, tn), lambda i, j: (0, j))],
        out_specs=pl.BlockSpec((tm, tn), lambda i, j: (i, j)),
        out_shape=jax.ShapeDtypeStruct((m, n), BF16),
        scratch_shapes=[pltpu.VMEM((tm, k), BF16)],
        compiler_params=_cparams("parallel", "arbitrary"),
        name="ffn_up",
    )(x, nw, wg, wu)


def _mix_kernel(ya_ref, yb_ref, ga_ref, gb_ref, wa_ref, wb_ref, o_ref):
    a = _dot(ya_ref[...], wa_ref[...])
    b = _dot(yb_ref[...], wb_ref[...])
    o_ref[...] = (_sigmoid(ga_ref[...]) * a + _sigmoid(gb_ref[...]) * b).astype(o_ref.dtype)


def _mix(ya, yb, proj, wa, wb, tm, tn):
    m, k = ya.shape
    n = wa.shape[1]
    nb = n // tn
    return pl.pallas_call(
        _mix_kernel,
        grid=(m // tm, nb),
        in_specs=[pl.BlockSpec((tm, k), lambda i, j: (i, 0)),
                  pl.BlockSpec((tm, k), lambda i, j: (i, 0)),
                  pl.BlockSpec((tm, tn), lambda i, j: (i, j)),
                  pl.BlockSpec((tm, tn), lambda i, j: (i, nb + j)),
                  pl.BlockSpec((k, tn), lambda i, j: (0, j)),
                  pl.BlockSpec((k, tn), lambda i, j: (0, j))],
        out_specs=pl.BlockSpec((tm, tn), lambda i, j: (i, j)),
        out_shape=jax.ShapeDtypeStruct((m, n), BF16),
        compiler_params=_cparams("parallel", "parallel"),
        name="mix",
    )(ya, yb, proj, proj, wa, wb)


def _split3(x):
    a = x.astype(BF16)
    r = x - a.astype(F32)
    b = r.astype(BF16)
    c = (r - b.astype(F32)).astype(BF16)
    return a, b, c


def _gdn_kernel(q_ref, k_ref, v_ref, z_ref, bd_ref, cwq_ref, cwk_ref, cwv_ref,
                alog_ref, dtb_ref, onw_ref, o_ref, s_scr, tail_scr, *, tb):
    hp = GDN_HEADS_PER_STEP
    C = GDN_CHUNK
    D = HEAD_DIM
    head0 = pl.program_id(1) * hp

    @pl.when(pl.program_id(2) == 0)
    def _():
        s_scr[...] = jnp.zeros_like(s_scr)
        tail_scr[...] = jnp.zeros_like(tail_scr)

    row8 = lax.broadcasted_iota(jnp.int32, (8, hp * D), 0)
    tails = []

    def conv_silu(u_ref, cw_ref, idx):
        u = u_ref[0]
        w = cw_ref[...]
        tail = tail_scr[idx]
        y = None
        ytop = None
        for s in (3, 2, 1):
            sh = pltpu.roll(u, s, axis=0)
            top = jnp.where(row8 < s, pltpu.roll(tail, s, axis=0), sh[0:8])
            wj = w[3 - s:4 - s]
            y = sh * wj if y is None else y + sh * wj
            ytop = top * wj if ytop is None else ytop + top * wj
        y = y + u * w[3:4]
        ytop = ytop + u[0:8] * w[3:4]
        tails.append(u[tb - 8:tb])
        y = jnp.concatenate([ytop, y[8:]], axis=0)
        return _silu(y)

    q_all = conv_silu(q_ref, cwq_ref, 0)
    k_all = conv_silu(k_ref, cwk_ref, 1)
    v_all = conv_silu(v_ref, cwv_ref, 2)

    bd = bd_ref[0]
    lane = lax.broadcasted_iota(jnp.int32, (tb, LANES), 1)
    beta_all = _sigmoid(bd)
    xg = bd + dtb_ref[...]
    softplus = jnp.maximum(xg, 0.0) + jnp.log1p(jnp.exp(-jnp.abs(xg)))
    g_all = -jnp.exp(alog_ref[...]) * softplus
    betas = [jnp.sum(jnp.where(lane == head0 + e, beta_all, 0.0), axis=-1, keepdims=True)
             for e in range(hp)]
    gs = [jnp.sum(jnp.where(lane == head0 + e + GDN_HEADS, g_all, 0.0), axis=-1, keepdims=True)
          for e in range(hp)]

    nc = tb // C
    ri = lax.broadcasted_iota(jnp.int32, (tb, tb), 0)
    ci = lax.broadcasted_iota(jnp.int32, (tb, tb), 1)
    same = (ri // C) == (ci // C)
    tril16 = jnp.where(same & (ri >= ci), 1.0, 0.0).astype(BF16)
    g_b = jnp.concatenate([jnp.broadcast_to(g, (tb, LANES)) for g in gs], axis=1)
    gc_all = sum(_dot(tril16, piece) for piece in _split3(g_b))

    iw = lax.broadcasted_iota(jnp.int32, (C, tb), 0)
    jw = lax.broadcasted_iota(jnp.int32, (C, tb), 1) % C
    incl_w = iw >= jw
    strict_w = iw > jw
    eye_w = jnp.where(iw == jw, 1.0, 0.0).astype(F32)
    low_half = (lax.broadcasted_iota(jnp.int32, (C, LANES), 1) < C)

    def block_diag(wide16):
        return jnp.where(same, jnp.concatenate([wide16] * nc, axis=0), jnp.zeros((), BF16))

    def pair_blocks(full):
        return jnp.where(low_half, full[:C], full[C:])

    E = range(hp)
    pairs = range(tb // LANES)

    def l2n(x):
        return x * lax.rsqrt(jnp.sum(x * x, axis=-1, keepdims=True) + RMS_EPS)

    qs = [l2n(q_all[:, e * D:(e + 1) * D]) * (D ** -0.5) for e in E]
    ks = [l2n(k_all[:, e * D:(e + 1) * D]) for e in E]
    vs = [v_all[:, e * D:(e + 1) * D] for e in E]
    gcs = [gc_all[:, e * D:(e + 1) * D] for e in E]
    g_col = [jnp.concatenate([pair_blocks(gc[p * LANES:(p + 1) * LANES]) for p in pairs], axis=1)
             for gc in gcs]
    g_row = [jnp.concatenate([gc[p * LANES:(p + 1) * LANES, :].T[0:1, :] for p in pairs], axis=1)
             for gc in gcs]
    decay = [jnp.where(incl_w, jnp.exp(jnp.where(incl_w, g_col[e] - g_row[e], 0.0)), 0.0) for e in E]
    k16 = [k.astype(BF16) for k in ks]
    q16 = [q.astype(BF16) for q in qs]
    kb = [ks[e] * betas[e] for e in E]
    kb16 = [x.astype(BF16) for x in kb]

    def pair_products(a16, b16):
        return jnp.concatenate([pair_blocks(_dot_nt(a16[p * LANES:(p + 1) * LANES],
                                                    b16[p * LANES:(p + 1) * LANES])) for p in pairs], axis=1)

    lmat = [jnp.where(strict_w, pair_products(kb16[e], k16[e]) * decay[e], 0.0) for e in E]
    amat = [jnp.where(incl_w, pair_products(q16[e], k16[e]) * decay[e], 0.0) for e in E]
    p16 = [(-lmat[e]).astype(BF16) for e in E]
    tinv = [eye_w - lmat[e] for e in E]
    pw = [_dot(p16[e], block_diag(p16[e])) for e in E]
    for _ in range(4):
        p16 = [pw[e].astype(BF16) for e in E]
        prod = [_dot(jnp.concatenate([p16[e], tinv[e].astype(BF16)], axis=0), block_diag(p16[e])) for e in E]
        pw = [prod[e][:C] for e in E]
        tinv = [tinv[e] + prod[e][C:] for e in E]
    tinv = [tinv[e] + _dot(tinv[e].astype(BF16), block_diag(pw[e].astype(BF16))) for e in E]
    eg = [jnp.exp(gc) for gc in gcs]
    rhs = [jnp.concatenate([kb[e] * eg[e], vs[e] * betas[e]], axis=1).astype(BF16) for e in E]
    wu = [_dot(block_diag(tinv[e].astype(BF16)), rhs[e]).astype(BF16) for e in E]
    au = [_dot(block_diag(amat[e].astype(BF16)), wu[e]) for e in E]
    q_eff = [(qs[e] * eg[e] - au[e][:, :D]).astype(BF16) for e in E]
    gl = [jnp.concatenate([jnp.broadcast_to(gc[c * C + C - 1:c * C + C, :], (C, LANES))
                           for c in range(nc)], axis=0) for gc in gcs]
    k_dec = [(ks[e] * jnp.exp(gl[e] - gcs[e])).astype(BF16) for e in E]
    pn = [[_dot_tn(k_dec[e][c * C:(c + 1) * C], wu[e][c * C:(c + 1) * C]) for c in range(nc)]
          for e in E]

    states = [s_scr[e] for e in E]
    onw = onw_ref[...]
    for c in range(nc):
        r0 = c * C
        lhs = [jnp.concatenate([pn[e][c][:, :D].astype(BF16), q_eff[e][r0:r0 + C]], axis=0) for e in E]
        res = [_dot(lhs[e], states[e].astype(BF16)) for e in E]
        outs = [_rms_rows(res[e][D:] + au[e][r0:r0 + C, D:], onw) for e in E]
        states = [states[e] * jnp.exp(gl[e][r0:r0 + 1, :]) - res[e][:D] + pn[e][c][:, D:] for e in E]
        zc = z_ref[0, r0:r0 + C, :]
        o_ref[0, r0:r0 + C, :] = (jnp.concatenate(outs, axis=1) * _silu(zc)).astype(o_ref.dtype)
    for e in range(hp):
        s_scr[e] = states[e]
    for idx in range(3):
        tail_scr[idx] = tails[idx]


def _gdn(proj3, conv_w, alog_row, dtb_row, onw, col0, bd_col, tb):
    b, t, _ = proj3.shape
    hp = GDN_HEADS_PER_STEP
    ng = GDN_HEADS // hp
    w = hp * HEAD_DIM

    def col(base):
        return pl.BlockSpec((1, tb, w), lambda bi, hi, ti: (bi, ti, base // hp + hi))

    def cw(base):
        return pl.BlockSpec((GDN_CONV, w), lambda bi, hi, ti: (0, base // hp + hi))

    assert col0 % hp == 0
    row = pl.BlockSpec((1, LANES), lambda bi, hi, ti: (0, 0))
    return pl.pallas_call(
        functools.partial(_gdn_kernel, tb=tb),
        grid=(b, ng, t // tb),
        in_specs=[col(col0), col(col0 + GDN_HEADS), col(col0 + 2 * GDN_HEADS), col(col0 + 3 * GDN_HEADS),
                  pl.BlockSpec((1, tb, LANES), lambda bi, hi, ti: (bi, ti, bd_col)),
                  cw(0), cw(GDN_HEADS), cw(2 * GDN_HEADS), row, row, row],
        out_specs=pl.BlockSpec((1, tb, w), lambda bi, hi, ti: (bi, ti, hi)),
        out_shape=jax.ShapeDtypeStruct((b, t, GDN_HEADS * HEAD_DIM), BF16),
        scratch_shapes=[pltpu.VMEM((hp, HEAD_DIM, HEAD_DIM), F32),
                        pltpu.VMEM((3, 8, w), F32)],
        compiler_params=_cparams("parallel", "parallel", "arbitrary"),
        name="gdn",
    )(proj3, proj3, proj3, proj3, proj3, conv_w, conv_w, conv_w, alog_row, dtb_row, onw)


def _bias_kernel(rel_ref, o_ref):
    h = pl.program_id(0)
    bs = MOBA_BLOCK
    max_exact = REL_BUCKETS // 2
    x = lax.broadcasted_iota(jnp.int32, (8, 2 * bs), 1)
    for d in range(BIAS_TILES):
        dist = jnp.maximum(d * bs + x - bs, 0)
        df = dist.astype(F32)
        log_ratio = jnp.log(jnp.maximum(df, float(max_exact)) / max_exact) / math.log(REL_MAX_DIST / max_exact)
        large = max_exact + (log_ratio * (REL_BUCKETS - max_exact)).astype(jnp.int32)
        large = jnp.minimum(large, REL_BUCKETS - 1)
        bucket = jnp.where(dist < max_exact, dist, large)
        row = jnp.zeros((8, 2 * bs), F32)
        for b in range(REL_BUCKETS):
            row = jnp.where(bucket == b, rel_ref[b, h], row)
        base = jnp.broadcast_to(row[0:1, :] * LOG2E, (bs, 2 * bs))
        tile = pltpu.roll(base, 0, 1, stride=1, stride_axis=0)[:, bs:]
        if d == 0:
            kk = lax.broadcasted_iota(jnp.int32, (bs, bs), 0)
            qq = lax.broadcasted_iota(jnp.int32, (bs, bs), 1)
            tile = jnp.where(kk <= qq, tile, NEG_INF)
        o_ref[0, d] = tile


def _bias_tiles(rel_bias):
    bs = MOBA_BLOCK
    max_exact = REL_BUCKETS // 2
    nearest = (BIAS_TILES - 1) * bs - (bs - 1)
    assert max_exact + int(math.log(nearest / max_exact) / math.log(REL_MAX_DIST / max_exact)
                           * (REL_BUCKETS - max_exact)) >= REL_BUCKETS - 1
    return pl.pallas_call(
        _bias_kernel,
        grid=(MOBA_HEADS,),
        in_specs=[pl.BlockSpec(memory_space=pltpu.SMEM)],
        out_specs=pl.BlockSpec((1, BIAS_TILES, bs, bs), lambda h: (h, 0, 0, 0)),
        out_shape=jax.ShapeDtypeStruct((MOBA_HEADS, BIAS_TILES, bs, bs), F32),
        compiler_params=_cparams("parallel"),
        name="bias",
    )(rel_bias)


def _moba_prep_kernel(q_ref, k_ref, v_ref, qw_ref, kw_ref, rel_ref, q16_ref, kn_ref, vt_ref, add_ref, km_scr,
                      *, rows, nb):
    bs = MOBA_BLOCK
    nsub = rows // bs
    ti = pl.program_id(2)

    @pl.when(ti == 0)
    def _():
        km_scr[...] = jnp.zeros_like(km_scr)

    kn = _rms_rows(k_ref[0], kw_ref[...])
    kn_ref[0, 0] = kn.astype(BF16)
    for s in range(nsub):
        km_scr[pl.ds(ti * nsub + s, 1), :] = jnp.mean(kn[s * bs:(s + 1) * bs], axis=0, keepdims=True)
        vt_ref[0, 0, 0:HEAD_DIM, s * bs:(s + 1) * bs] = v_ref[0, s * bs:(s + 1) * bs, :].T.astype(BF16)
    pad_row = lax.broadcasted_iota(jnp.int32, (MOBA_V_ROWS - HEAD_DIM, rows), 0)
    vt_ref[0, 0, HEAD_DIM:, :] = jnp.where(pad_row == 0, 1.0, 0.0).astype(BF16)

    q = _rms_rows(q_ref[0], qw_ref[...])
    q16_ref[0, 0] = (q * ((HEAD_DIM ** -0.5) * LOG2E)).astype(BF16)

    blk = lax.broadcasted_iota(jnp.int32, (nb, rows), 0)
    own_blk = ti * nsub + lax.broadcasted_iota(jnp.int32, (nb, rows), 1) // bs
    blkf = blk.astype(F32)
    valid = blk < own_blk
    work = jnp.where(valid, _dot_nt(km_scr[...], q, HIGHEST), NEG_INF)
    picked = jnp.zeros((nb, rows), F32)
    for _ in range(MOBA_TOPK):
        best = jnp.max(work, axis=0, keepdims=True)
        first = jnp.min(jnp.where(work == best, blkf, float(nb)), axis=0, keepdims=True)
        hit = blkf == first
        picked = jnp.where(hit, 1.0, picked)
        work = jnp.where(hit, -jnp.inf, work)
    far_bias = rel_ref[REL_BUCKETS - 1, pl.program_id(1)] * LOG2E
    attended = (valid & (picked > 0.0)) | (blk == own_blk)
    add_ref[0, 0] = jnp.where(attended, far_bias, NEG_INF)


def _moba_prep(proj3, qw, kw, rel_bias, col0, rows):
    b, t, _ = proj3.shape
    hh, bs, dh = MOBA_HEADS, MOBA_BLOCK, HEAD_DIM
    nb = t // bs

    def col(base):
        return pl.BlockSpec((1, rows, LANES), lambda bi, hi, ti: (bi, ti, base + hi))

    row = pl.BlockSpec((1, LANES), lambda bi, hi, ti: (0, 0))
    return pl.pallas_call(
        functools.partial(_moba_prep_kernel, rows=rows, nb=nb),
        grid=(b, hh, t // rows),
        in_specs=[col(col0), col(col0 + hh), col(col0 + 2 * hh), row, row,
                  pl.BlockSpec(memory_space=pltpu.SMEM)],
        out_specs=[pl.BlockSpec((1, 1, rows, dh), lambda bi, hi, ti: (bi, hi, ti, 0)),
                   pl.BlockSpec((1, 1, rows, dh), lambda bi, hi, ti: (bi, hi, ti, 0)),
                   pl.BlockSpec((1, 1, MOBA_V_ROWS, rows), lambda bi, hi, ti: (bi, hi, 0, ti)),
                   pl.BlockSpec((1, 1, nb, rows), lambda bi, hi, ti: (bi, hi, 0, ti))],
        out_shape=[jax.ShapeDtypeStruct((b, hh, t, dh), BF16),
                   jax.ShapeDtypeStruct((b, hh, t, dh), BF16),
                   jax.ShapeDtypeStruct((b, hh, MOBA_V_ROWS, t), BF16),
                   jax.ShapeDtypeStruct((b, hh, nb, t), F32)],
        scratch_shapes=[pltpu.VMEM((nb, dh), F32)],
        compiler_params=_cparams("parallel", "parallel", "arbitrary"),
        name="moba_prep",
    )(proj3, proj3, proj3, qw, kw, rel_bias)


def _moba_kernel(q_ref, kn_ref, vt_ref, add_ref, bias_ref, o_ref, buf_a, buf_b, *, nb):
    i = pl.program_id(2)
    bs = MOBA_BLOCK
    ch = MOBA_TILES_PER_STEP
    E = range(MOBA_STREAMS)
    q16 = [q_ref[e, 0] for e in E]

    crows = ch * bs
    n_pairs = (i + 2 * ch) // (2 * ch)
    last_chunk = nb // ch - 1

    def chunk_scores(buf, e, c, far):
        r = pl.multiple_of(c * crows, crows)
        raw = _dot_nt(kn_ref[e, 0, pl.ds(r, crows), :], q16[e])
        mx = None
        for t in range(ch):
            j = c * ch + t
            add = add_ref[e, 0, pl.ds(j, 1), :]
            if far:
                st = raw[t * bs:(t + 1) * bs, :] + add
            else:
                d = jnp.clip(i - j, 0, BIAS_TILES - 1)
                st = jnp.where(add > 0.5 * NEG_INF, raw[t * bs:(t + 1) * bs, :] + bias_ref[0, d], NEG_INF)
            buf[e, t * bs:(t + 1) * bs, :] = st
            tmx = jnp.max(st, axis=0, keepdims=True)
            mx = tmx if mx is None else jnp.maximum(mx, tmx)
        return mx

    def absorb(buf, e, c, mx, m, acc):
        r = pl.multiple_of(c * crows, crows)
        m_new = jnp.maximum(m, mx)
        p = jnp.exp2(buf[e] - m_new)
        acc = jnp.exp2(m - m_new) * acc + _dot(vt_ref[e, 0, :, pl.ds(r, crows)], p.astype(BF16))
        return m_new, acc

    def pair(far, c, carry):
        m, acc, mx_a = (list(x) for x in carry)
        mx_b = [chunk_scores(buf_b, e, 2 * c + 1, far) for e in E]
        for e in E:
            m[e], acc[e] = absorb(buf_a, e, 2 * c, mx_a[e], m[e], acc[e])
        mx_a = [chunk_scores(buf_a, e, jnp.minimum(2 * c + 2, last_chunk), far) for e in E]
        for e in E:
            m[e], acc[e] = absorb(buf_b, e, 2 * c + 1, mx_b[e], m[e], acc[e])
        return tuple(m), tuple(acc), tuple(mx_a)

    def step(c, carry):
        all_far = (2 * c + 3) * ch - 1 + (BIAS_TILES - 1) <= i
        return lax.cond(all_far, functools.partial(pair, True, c), functools.partial(pair, False, c), carry)

    m0 = tuple(jnp.full((1, bs), MOBA_M_INIT, F32) for e in E)
    acc0 = tuple(jnp.zeros((MOBA_V_ROWS, bs), F32) for e in E)
    mx_a = tuple(chunk_scores(buf_a, e, 0, False) for e in E)
    m, acc, _ = lax.fori_loop(0, n_pairs, step, (m0, acc0, mx_a))
    for e in E:
        o_ref[e] = (acc[e][:HEAD_DIM] / acc[e][HEAD_DIM:HEAD_DIM + 1]).T.astype(o_ref.dtype)


def _moba(q16, kn, vt, add, bias):
    b, hh, t, dh = kn.shape
    bs = MOBA_BLOCK
    nb = t // bs
    ns = MOBA_STREAMS
    crows = MOBA_TILES_PER_STEP * bs
    assert b % ns == 0 and nb % (2 * MOBA_TILES_PER_STEP) == 0
    return pl.pallas_call(
        functools.partial(_moba_kernel, nb=nb),
        grid=(b // ns, hh, nb),
        in_specs=[pl.BlockSpec((ns, 1, bs, dh), lambda bi, hi, ti: (bi, hi, ti, 0)),
                  pl.BlockSpec((ns, 1, t, dh), lambda bi, hi, ti: (bi, hi, 0, 0)),
                  pl.BlockSpec((ns, 1, MOBA_V_ROWS, t), lambda bi, hi, ti: (bi, hi, 0, 0)),
                  pl.BlockSpec((ns, 1, nb, bs), lambda bi, hi, ti: (bi, hi, 0, ti)),
                  pl.BlockSpec((1, BIAS_TILES, bs, bs), lambda bi, hi, ti: (hi, 0, 0, 0))],
        out_specs=pl.BlockSpec((ns, bs, dh), lambda bi, hi, ti: (bi, ti, hi)),
        out_shape=jax.ShapeDtypeStruct((b, t, hh * dh), BF16),
        scratch_shapes=[pltpu.VMEM((ns, crows, bs), F32),
                        pltpu.VMEM((ns, crows, bs), F32)],
        compiler_params=_cparams("parallel", "parallel", "parallel"),
        name="moba",
    )(q16, kn, vt, add, bias)


def _tile(n, want):
    t = min(n, want)
    assert n % t == 0, (n, want)
    return t


def kernel(x, norm_mix_w, w_in, conv_w, a_log, dt_bias, gdn_o_norm_w, q_norm_w, k_norm_w, rel_bias,
           w_branch_gdn, w_branch_moba, w_out, norm_ffn_w, w_ffn_gate, w_ffn_up, w_ffn_down):
    b, t, d = x.shape
    m = b * t
    gw = GDN_HEADS * HEAD_DIM
    mw = MOBA_HEADS * HEAD_DIM
    assert t % MOBA_PREP_ROWS == 0 and w_in.shape[0] == 1
    assert w_in.shape[2] == 4 * gw + 2 * GDN_HEADS + 3 * mw + 2 * d

    o_bd = 4 * gw
    o_moba = o_bd + 2 * GDN_HEADS
    o_gate = o_moba + 3 * mw
    n_main = 2 * d + 4 * gw + 3 * mw
    tn_proj = PROJ_TILE[1]
    n_proj = -(-(n_main + LANES) // tn_proj) * tn_proj
    w_proj = _regroup_weights(w_in, (o_bd, o_moba, o_gate), n_proj)
    col_gdn = (2 * d) // LANES
    col_moba = col_gdn + 4 * GDN_HEADS
    col_bd = col_moba + 3 * MOBA_HEADS

    x2 = x.reshape(m, d)
    proj = _norm_matmul(x2, norm_mix_w, w_proj, _tile(m, PROJ_TILE[0]), tn_proj, F32, "proj")
    proj3 = proj.reshape(b, t, n_proj)

    lane_pad = jnp.zeros((1, LANES - 2 * GDN_HEADS), F32)
    head_pad = jnp.zeros((1, GDN_HEADS), F32)
    alog_row = jnp.concatenate([head_pad, a_log, lane_pad], axis=1)
    dtb_row = jnp.concatenate([head_pad, dt_bias, lane_pad], axis=1)
    y_a = _gdn(proj3, conv_w[0], alog_row, dtb_row, gdn_o_norm_w, col_gdn, col_bd, _tile(t, GDN_BLOCK))

    bias = _bias_tiles(rel_bias)
    q16, kn, vt, add = _moba_prep(proj3, q_norm_w, k_norm_w, rel_bias, col_moba, MOBA_PREP_ROWS)
    y_b = _moba(q16, kn, vt, add, bias)

    mix = _mix(y_a.reshape(m, gw), y_b.reshape(m, mw), proj,
               w_branch_gdn[0].astype(BF16), w_branch_moba[0].astype(BF16), _tile(m, MIX_ROWS), d)
    h1 = _matmul_res(mix, w_out[0].astype(BF16), x2, _tile(m, OUT_ROWS), d, "out")

    hid = _ffn_up(h1, norm_ffn_w, w_ffn_gate[0].astype(BF16), w_ffn_up[0].astype(BF16),
                  _tile(m, FFN_UP_TILE[0]), FFN_UP_TILE[1])
    h2 = _matmul_res(hid, w_ffn_down[0].astype(BF16), h1, _tile(m, FFN_DOWN_TILE[0]), FFN_DOWN_TILE[1],
                     "ffn_down")
    return h2.reshape(b, t, d)
```

```python
import functools
import math

import jax
import jax.numpy as jnp
from jax import lax
from jax.experimental import pallas as pl
from jax.experimental.pallas import tpu as pltpu

F32 = jnp.float32
BF16 = jnp.bfloat16
HIGHEST = lax.Precision.HIGHEST

LANES = 128
HEAD_DIM = 128
GDN_HEADS = 8
GDN_CONV = 4
GDN_CHUNK = 64
GDN_INV_BASE = 8
GDN_HEADS_PER_STEP = 8
NEG_INF = -1e30
MOBA_HEADS = 8
MOBA_BLOCK = 256
MOBA_TOPK = 3
REL_BUCKETS = 32
REL_MAX_DIST = 2048
BIAS_TILES = 8
MOBA_TILES_PER_STEP = 2
MOBA_PREP_ROWS = 2048
MOBA_STREAMS = 4
MOBA_V_ROWS = HEAD_DIM + 16
MOBA_M_INIT = 0.1 * NEG_INF
LOG2E = math.log2(math.e)
RMS_EPS = 1e-6
V7X_VMEM_BYTES = 64 * 1024 * 1024
VMEM_LIMIT = V7X_VMEM_BYTES * 7 // 8

PROJ_TILE = (1024, 1280)
MIX_ROWS = 512
OUT_ROWS = 512
FFN_UP_TILE = (1024, 512)
FFN_DOWN_TILE = (1024, 512)
GDN_BLOCK = 256
REGROUP_ROWS = 128


def _cparams(*sem):
    return pltpu.CompilerParams(dimension_semantics=sem, vmem_limit_bytes=VMEM_LIMIT)


def _sigmoid(x):
    return 0.5 + 0.5 * jnp.tanh(0.5 * x)


def _silu(x):
    h = 0.5 * x
    return h + h * jnp.tanh(h)


def _dot(a, b, precision=None):
    return jnp.dot(a, b, preferred_element_type=F32, precision=precision)


def _dot_nt(a, b, precision=None):
    return lax.dot_general(a, b, (((1,), (1,)), ((), ())),
                           preferred_element_type=F32, precision=precision)


def _dot_tn(a, b):
    return lax.dot_general(a, b, (((0,), (0,)), ((), ())), preferred_element_type=F32)


def _rms_rows(x, w):
    return x * lax.rsqrt(jnp.mean(x * x, axis=-1, keepdims=True) + RMS_EPS) * w


def _regroup_kernel(w_ref, o_ref, *, cuts):
    o_bd, o_moba, o_gate = cuts
    x = w_ref[...]
    parts = [x[:, o_gate:], x[:, :o_bd], x[:, o_moba:o_gate], x[:, o_bd:o_moba]]
    used = sum(p.shape[1] for p in parts)
    parts.append(jnp.zeros((x.shape[0], o_ref.shape[1] - used), x.dtype))
    o_ref[...] = jnp.concatenate(parts, axis=1).astype(o_ref.dtype)


def _regroup_weights(w, cuts, n_out):
    _, k, n_in = w.shape
    rows = _tile(k, REGROUP_ROWS)
    return pl.pallas_call(
        functools.partial(_regroup_kernel, cuts=cuts),
        grid=(k // rows,),
        in_specs=[pl.BlockSpec((None, rows, n_in), lambda i: (0, i, 0))],
        out_specs=pl.BlockSpec((rows, n_out), lambda i: (i, 0)),
        out_shape=jax.ShapeDtypeStruct((k, n_out), BF16),
        compiler_params=_cparams("parallel"),
        name="regroup",
    )(w)


def _norm_matmul_kernel(x_ref, nw_ref, w_ref, o_ref, u_scr):
    @pl.when(pl.program_id(1) == 0)
    def _():
        u_scr[...] = _rms_rows(x_ref[...], nw_ref[...]).astype(BF16)

    o_ref[...] = _dot(u_scr[...], w_ref[...]).astype(o_ref.dtype)


def _norm_matmul(x, nw, w, tm, tn, out_dtype, name):
    m, k = x.shape
    n = w.shape[1]
    assert m % tm == 0 and n % tn == 0
    return pl.pallas_call(
        _norm_matmul_kernel,
        grid=(m // tm, n // tn),
        in_specs=[pl.BlockSpec((tm, k), lambda i, j: (i, 0)),
                  pl.BlockSpec((1, k), lambda i, j: (0, 0)),
                  pl.BlockSpec((k, tn), lambda i, j: (0, j))],
        out_specs=pl.BlockSpec((tm, tn), lambda i, j: (i, j)),
        out_shape=jax.ShapeDtypeStruct((m, n), out_dtype),
        scratch_shapes=[pltpu.VMEM((tm, k), BF16)],
        compiler_params=_cparams("parallel", "arbitrary"),
        name=name,
    )(x, nw, w)


def _matmul_res_kernel(a_ref, w_ref, r_ref, o_ref):
    o_ref[...] = r_ref[...] + _dot(a_ref[...], w_ref[...])


def _matmul_res(a, w, res, tm, tn, name):
    m, k = a.shape
    n = w.shape[1]
    assert m % tm == 0 and n % tn == 0
    return pl.pallas_call(
        _matmul_res_kernel,
        grid=(m // tm, n // tn),
        in_specs=[pl.BlockSpec((tm, k), lambda i, j: (i, 0)),
                  pl.BlockSpec((k, tn), lambda i, j: (0, j)),
                  pl.BlockSpec((tm, tn), lambda i, j: (i, j))],
        out_specs=pl.BlockSpec((tm, tn), lambda i, j: (i, j)),
        out_shape=jax.ShapeDtypeStruct((m, n), F32),
        compiler_params=_cparams("parallel", "parallel"),
        name=name,
    )(a, w, res)


def _ffn_up_kernel(x_ref, nw_ref, wg_ref, wu_ref, o_ref, u_scr):
    @pl.when(pl.program_id(1) == 0)
    def _():
        u_scr[...] = _rms_rows(x_ref[...], nw_ref[...]).astype(BF16)

    u = u_scr[...]
    g = _dot(u, wg_ref[...])
    o_ref[...] = (_silu(g) * _dot(u, wu_ref[...])).astype(o_ref.dtype)


def _ffn_up(x, nw, wg, wu, tm, tn):
    m, k = x.shape
    n = wg.shape[1]
    assert m % tm == 0 and n % tn == 0
    return pl.pallas_call(
        _ffn_up_kernel,
        grid=(m // tm, n // tn),
        in_specs=[pl.BlockSpec((tm, k), lambda i, j: (i, 0)),
                  pl.BlockSpec((1, k), lambda i, j: (0, 0)),
                  pl.BlockSpec((k, tn), lambda i, j: (0, j)),
                  pl.BlockSpec((k, tn), lambda i, j: (0, j))],
        out_specs=pl.BlockSpec((tm, tn), lambda i, j: (i, j)),
        out_shape=jax.ShapeDtypeStruct((m, n), BF16),
        scratch_shapes=[pltpu.VMEM((tm, k), BF16)],
        compiler_params=_cparams("parallel", "arbitrary"),
        name="ffn_up",
    )(x, nw, wg, wu)


def _mix_kernel(ya_ref, yb_ref, ga_ref, gb_ref, wa_ref, wb_ref, o_ref):
    a = _dot(ya_ref[...], wa_ref[...])
    b = _dot(yb_ref[...], wb_ref[...])
    o_ref[...] = (_sigmoid(ga_ref[...]) * a + _sigmoid(gb_ref[...]) * b).astype(o_ref.dtype)


def _mix(ya, yb, proj, wa, wb, tm, tn):
    m, k = ya.shape
    n = wa.shape[1]
    nb = n // tn
    assert m % tm == 0 and n % tn == 0
    return pl.pallas_call(
        _mix_kernel,
        grid=(m // tm, nb),
        in_specs=[pl.BlockSpec((tm, k), lambda i, j: (i, 0)),
                  pl.BlockSpec((tm, k), lambda i, j: (i, 0)),
                  pl.BlockSpec((tm, tn), lambda i, j: (i, j)),
                  pl.BlockSpec((tm, tn), lambda i, j: (i, nb + j)),
                  pl.BlockSpec((k, tn), lambda i, j: (0, j)),
                  pl.BlockSpec((k, tn), lambda i, j: (0, j))],
        out_specs=pl.BlockSpec((tm, tn), lambda i, j: (i, j)),
        out_shape=jax.ShapeDtypeStruct((m, n), BF16),
        compiler_params=_cparams("parallel", "parallel"),
        name="mix",
    )(ya, yb, proj, proj, wa, wb)


def _split3(x):
    a = x.astype(BF16)
    r = x - a.astype(F32)
    b = r.astype(BF16)
    c = (r - b.astype(F32)).astype(BF16)
    return a, b, c


def _gdn_kernel(q_ref, k_ref, v_ref, z_ref, bd_ref, cwq_ref, cwk_ref, cwv_ref,
                alog_ref, dtb_ref, onw_ref, o_ref, s_scr, tail_scr, *, tb):
    hp = GDN_HEADS_PER_STEP
    C = GDN_CHUNK
    D = HEAD_DIM
    head0 = pl.program_id(1) * hp

    @pl.when(pl.program_id(2) == 0)
    def _():
        s_scr[...] = jnp.zeros_like(s_scr)
        tail_scr[...] = jnp.zeros_like(tail_scr)

    row8 = lax.broadcasted_iota(jnp.int32, (8, hp * D), 0)
    tails = []

    def conv_silu(u_ref, cw_ref, idx):
        u = u_ref[0]
        w = cw_ref[...]
        tail = tail_scr[idx]
        y = None
        ytop = None
        for s in (3, 2, 1):
            sh = pltpu.roll(u, s, axis=0)
            top = jnp.where(row8 < s, pltpu.roll(tail, s, axis=0), sh[0:8])
            wj = w[3 - s:4 - s]
            y = sh * wj if y is None else y + sh * wj
            ytop = top * wj if ytop is None else ytop + top * wj
        y = y + u * w[3:4]
        ytop = ytop + u[0:8] * w[3:4]
        tails.append(u[tb - 8:tb])
        y = jnp.concatenate([ytop, y[8:]], axis=0)
        return _silu(y)

    q_all = conv_silu(q_ref, cwq_ref, 0)
    k_all = conv_silu(k_ref, cwk_ref, 1)
    v_all = conv_silu(v_ref, cwv_ref, 2)

    bd = bd_ref[0]
    lane = lax.broadcasted_iota(jnp.int32, (tb, LANES), 1)
    beta_all = _sigmoid(bd)
    xg = bd + dtb_ref[...]
    softplus = jnp.maximum(xg, 0.0) + jnp.log1p(jnp.exp(-jnp.abs(xg)))
    g_all = -jnp.exp(alog_ref[...]) * softplus
    betas = [jnp.sum(jnp.where(lane == head0 + e, beta_all, 0.0), axis=-1, keepdims=True)
             for e in range(hp)]
    gs = [jnp.sum(jnp.where(lane == head0 + e + GDN_HEADS, g_all, 0.0), axis=-1, keepdims=True)
          for e in range(hp)]

    nc = tb // C
    ri = lax.broadcasted_iota(jnp.int32, (tb, tb), 0)
    ci = lax.broadcasted_iota(jnp.int32, (tb, tb), 1)
    same = (ri // C) == (ci // C)
    tril16 = jnp.where(same & (ri >= ci), 1.0, 0.0).astype(BF16)
    g_b = jnp.concatenate([jnp.broadcast_to(g, (tb, LANES)) for g in gs], axis=1)
    gc_all = sum(_dot(tril16, piece) for piece in _split3(g_b))

    iw = lax.broadcasted_iota(jnp.int32, (C, tb), 0)
    jw = lax.broadcasted_iota(jnp.int32, (C, tb), 1) % C
    incl_w = iw >= jw
    strict_w = iw > jw
    eye_w = jnp.where(iw == jw, 1.0, 0.0).astype(F32)
    low_half = (lax.broadcasted_iota(jnp.int32, (C, LANES), 1) < C)

    def block_diag(wide16):
        return jnp.where(same, jnp.concatenate([wide16] * nc, axis=0), jnp.zeros((), BF16))

    def pair_blocks(full):
        return jnp.where(low_half, full[:C], full[C:])

    E = range(hp)
    pairs = range(tb // LANES)

    def l2n(x):
        return x * lax.rsqrt(jnp.sum(x * x, axis=-1, keepdims=True) + RMS_EPS)

    qs = [l2n(q_all[:, e * D:(e + 1) * D]) * (D ** -0.5) for e in E]
    ks = [l2n(k_all[:, e * D:(e + 1) * D]) for e in E]
    vs = [v_all[:, e * D:(e + 1) * D] for e in E]
    gcs = [gc_all[:, e * D:(e + 1) * D] for e in E]
    g_col = [jnp.concatenate([pair_blocks(gc[p * LANES:(p + 1) * LANES]) for p in pairs], axis=1)
             for gc in gcs]
    g_row = [jnp.concatenate([gc[p * LANES:(p + 1) * LANES, :].T[0:1, :] for p in pairs], axis=1)
             for gc in gcs]
    decay = [jnp.where(incl_w, jnp.exp(jnp.where(incl_w, g_col[e] - g_row[e], 0.0)), 0.0) for e in E]
    k16 = [k.astype(BF16) for k in ks]
    q16 = [q.astype(BF16) for q in qs]
    kb = [ks[e] * betas[e] for e in E]
    kb16 = [x.astype(BF16) for x in kb]

    def pair_products(a16, b16):
        return jnp.concatenate([pair_blocks(_dot_nt(a16[p * LANES:(p + 1) * LANES],
                                                    b16[p * LANES:(p + 1) * LANES])) for p in pairs], axis=1)

    lmat = [jnp.where(strict_w, pair_products(kb16[e], k16[e]) * decay[e], 0.0) for e in E]
    amat = [jnp.where(incl_w, pair_products(q16[e], k16[e]) * decay[e], 0.0) for e in E]
    base = GDN_INV_BASE
    blk_i, blk_j = iw // base, jw // base
    neg_d16 = [jnp.where(blk_i == blk_j, -lmat[e], 0.0).astype(BF16) for e in E]
    tinv = [eye_w + neg_d16[e].astype(F32) for e in E]
    pw = [_dot(neg_d16[e], block_diag(neg_d16[e])) for e in E]
    span = 2
    while 2 * span < base:
        p16 = [pw[e].astype(BF16) for e in E]
        prod = [_dot(jnp.concatenate([p16[e], tinv[e].astype(BF16)], axis=0), block_diag(p16[e])) for e in E]
        pw = [prod[e][:C] for e in E]
        tinv = [tinv[e] + prod[e][C:] for e in E]
        span *= 2
    tinv = [tinv[e] + _dot(tinv[e].astype(BF16), block_diag(pw[e].astype(BF16))) for e in E]
    s = base
    while s < C:
        below_left = (iw // (2 * s) == jw // (2 * s)) & ((iw // s) % 2 == 1) & ((jw // s) % 2 == 0)
        c16 = [jnp.where(below_left, lmat[e], 0.0).astype(BF16) for e in E]
        cx = [_dot(c16[e], block_diag(tinv[e].astype(BF16))) for e in E]
        tinv = [tinv[e] - _dot(tinv[e].astype(BF16), block_diag(cx[e].astype(BF16))) for e in E]
        s *= 2
    eg = [jnp.exp(gc) for gc in gcs]
    rhs = [jnp.concatenate([kb[e] * eg[e], vs[e] * betas[e]], axis=1).astype(BF16) for e in E]
    wu = [_dot(block_diag(tinv[e].astype(BF16)), rhs[e]).astype(BF16) for e in E]
    au = [_dot(block_diag(amat[e].astype(BF16)), wu[e]) for e in E]
    q_eff = [(qs[e] * eg[e] - au[e][:, :D]).astype(BF16) for e in E]
    gl = [jnp.concatenate([jnp.broadcast_to(gc[c * C + C - 1:c * C + C, :], (C, LANES))
                           for c in range(nc)], axis=0) for gc in gcs]
    k_dec = [(ks[e] * jnp.exp(gl[e] - gcs[e])).astype(BF16) for e in E]
    pn = [[_dot_tn(k_dec[e][c * C:(c + 1) * C], wu[e][c * C:(c + 1) * C]) for c in range(nc)]
          for e in E]

    states = [s_scr[e] for e in E]
    onw = onw_ref[...]
    for c in range(nc):
        r0 = c * C
        lhs = [jnp.concatenate([pn[e][c][:, :D].astype(BF16), q_eff[e][r0:r0 + C]], axis=0) for e in E]
        res = [_dot(lhs[e], states[e].astype(BF16)) for e in E]
        outs = [_rms_rows(res[e][D:] + au[e][r0:r0 + C, D:], onw) for e in E]
        states = [states[e] * jnp.exp(gl[e][r0:r0 + 1, :]) - res[e][:D] + pn[e][c][:, D:] for e in E]
        zc = z_ref[0, r0:r0 + C, :]
        o_ref[0, r0:r0 + C, :] = (jnp.concatenate(outs, axis=1) * _silu(zc)).astype(o_ref.dtype)
    for e in range(hp):
        s_scr[e] = states[e]
    for idx in range(3):
        tail_scr[idx] = tails[idx]


def _gdn(proj3, conv_w, alog_row, dtb_row, onw, col0, bd_col, tb):
    b, t, _ = proj3.shape
    hp = GDN_HEADS_PER_STEP
    ng = GDN_HEADS // hp
    w = hp * HEAD_DIM

    def col(base):
        return pl.BlockSpec((1, tb, w), lambda bi, hi, ti: (bi, ti, base // hp + hi))

    def cw(base):
        return pl.BlockSpec((GDN_CONV, w), lambda bi, hi, ti: (0, base // hp + hi))

    assert col0 % hp == 0
    row = pl.BlockSpec((1, LANES), lambda bi, hi, ti: (0, 0))
    return pl.pallas_call(
        functools.partial(_gdn_kernel, tb=tb),
        grid=(b, ng, t // tb),
        in_specs=[col(col0), col(col0 + GDN_HEADS), col(col0 + 2 * GDN_HEADS), col(col0 + 3 * GDN_HEADS),
                  pl.BlockSpec((1, tb, LANES), lambda bi, hi, ti: (bi, ti, bd_col)),
                  cw(0), cw(GDN_HEADS), cw(2 * GDN_HEADS), row, row, row],
        out_specs=pl.BlockSpec((1, tb, w), lambda bi, hi, ti: (bi, ti, hi)),
        out_shape=jax.ShapeDtypeStruct((b, t, GDN_HEADS * HEAD_DIM), BF16),
        scratch_shapes=[pltpu.VMEM((hp, HEAD_DIM, HEAD_DIM), F32),
                        pltpu.VMEM((3, 8, w), F32)],
        compiler_params=_cparams("parallel", "parallel", "arbitrary"),
        name="gdn",
    )(proj3, proj3, proj3, proj3, proj3, conv_w, conv_w, conv_w, alog_row, dtb_row, onw)


def _bias_kernel(rel_ref, o_ref):
    h = pl.program_id(0)
    bs = MOBA_BLOCK
    max_exact = REL_BUCKETS // 2
    x = lax.broadcasted_iota(jnp.int32, (8, 2 * bs), 1)
    for d in range(BIAS_TILES):
        dist = jnp.maximum(d * bs + x - bs, 0)
        df = dist.astype(F32)
        log_ratio = jnp.log(jnp.maximum(df, float(max_exact)) / max_exact) / math.log(REL_MAX_DIST / max_exact)
        large = max_exact + (log_ratio * (REL_BUCKETS - max_exact)).astype(jnp.int32)
        large = jnp.minimum(large, REL_BUCKETS - 1)
        bucket = jnp.where(dist < max_exact, dist, large)
        row = jnp.zeros((8, 2 * bs), F32)
        for b in range(REL_BUCKETS):
            row = jnp.where(bucket == b, rel_ref[b, h], row)
        base = jnp.broadcast_to(row[0:1, :] * LOG2E, (bs, 2 * bs))
        tile = pltpu.roll(base, 0, 1, stride=1, stride_axis=0)[:, bs:]
        if d == 0:
            kk = lax.broadcasted_iota(jnp.int32, (bs, bs), 0)
            qq = lax.broadcasted_iota(jnp.int32, (bs, bs), 1)
            tile = jnp.where(kk <= qq, tile, NEG_INF)
        o_ref[0, d] = tile


def _bias_tiles(rel_bias):
    bs = MOBA_BLOCK
    max_exact = REL_BUCKETS // 2
    nearest = (BIAS_TILES - 1) * bs - (bs - 1)
    assert max_exact + int(math.log(nearest / max_exact) / math.log(REL_MAX_DIST / max_exact)
                           * (REL_BUCKETS - max_exact)) >= REL_BUCKETS - 1
    return pl.pallas_call(
        _bias_kernel,
        grid=(MOBA_HEADS,),
        in_specs=[pl.BlockSpec(memory_space=pltpu.SMEM)],
        out_specs=pl.BlockSpec((1, BIAS_TILES, bs, bs), lambda h: (h, 0, 0, 0)),
        out_shape=jax.ShapeDtypeStruct((MOBA_HEADS, BIAS_TILES, bs, bs), F32),
        compiler_params=_cparams("parallel"),
        name="bias",
    )(rel_bias)


def _moba_prep_kernel(q_ref, k_ref, v_ref, qw_ref, kw_ref, rel_ref, q16_ref, kn_ref, vt_ref, add_ref, km_scr,
                      *, rows, nb):
    bs = MOBA_BLOCK
    nsub = rows // bs
    ti = pl.program_id(2)

    @pl.when(ti == 0)
    def _():
        km_scr[...] = jnp.zeros_like(km_scr)

    kn = _rms_rows(k_ref[0], kw_ref[...])
    kn_ref[0, 0] = kn.astype(BF16)
    for s in range(nsub):
        km_scr[pl.ds(ti * nsub + s, 1), :] = jnp.mean(kn[s * bs:(s + 1) * bs], axis=0, keepdims=True)
        vt_ref[0, 0, 0:HEAD_DIM, s * bs:(s + 1) * bs] = v_ref[0, s * bs:(s + 1) * bs, :].T.astype(BF16)
    pad_row = lax.broadcasted_iota(jnp.int32, (MOBA_V_ROWS - HEAD_DIM, rows), 0)
    vt_ref[0, 0, HEAD_DIM:, :] = jnp.where(pad_row == 0, 1.0, 0.0).astype(BF16)

    q = _rms_rows(q_ref[0], qw_ref[...])
    q16_ref[0, 0] = (q * ((HEAD_DIM ** -0.5) * LOG2E)).astype(BF16)

    blk = lax.broadcasted_iota(jnp.int32, (nb, rows), 0)
    own_blk = ti * nsub + lax.broadcasted_iota(jnp.int32, (nb, rows), 1) // bs
    blkf = blk.astype(F32)
    valid = blk < own_blk
    work = jnp.where(valid, _dot_nt(km_scr[...], q, HIGHEST), NEG_INF)
    picked = jnp.zeros((nb, rows), F32)
    for _ in range(MOBA_TOPK):
        best = jnp.max(work, axis=0, keepdims=True)
        first = jnp.min(jnp.where(work == best, blkf, float(nb)), axis=0, keepdims=True)
        hit = blkf == first
        picked = jnp.where(hit, 1.0, picked)
        work = jnp.where(hit, -jnp.inf, work)
    far_bias = rel_ref[REL_BUCKETS - 1, pl.program_id(1)] * LOG2E
    attended = (valid & (picked > 0.0)) | (blk == own_blk)
    add_ref[0, 0] = jnp.where(attended, far_bias, NEG_INF)


def _moba_prep(proj3, qw, kw, rel_bias, col0, rows):
    b, t, _ = proj3.shape
    hh, bs, dh = MOBA_HEADS, MOBA_BLOCK, HEAD_DIM
    nb = t // bs

    def col(base):
        return pl.BlockSpec((1, rows, LANES), lambda bi, hi, ti: (bi, ti, base + hi))

    row = pl.BlockSpec((1, LANES), lambda bi, hi, ti: (0, 0))
    return pl.pallas_call(
        functools.partial(_moba_prep_kernel, rows=rows, nb=nb),
        grid=(b, hh, t // rows),
        in_specs=[col(col0), col(col0 + hh), col(col0 + 2 * hh), row, row,
                  pl.BlockSpec(memory_space=pltpu.SMEM)],
        out_specs=[pl.BlockSpec((1, 1, rows, dh), lambda bi, hi, ti: (bi, hi, ti, 0)),
                   pl.BlockSpec((1, 1, rows, dh), lambda bi, hi, ti: (bi, hi, ti, 0)),
                   pl.BlockSpec((1, 1, MOBA_V_ROWS, rows), lambda bi, hi, ti: (bi, hi, 0, ti)),
                   pl.BlockSpec((1, 1, nb, rows), lambda bi, hi, ti: (bi, hi, 0, ti))],
        out_shape=[jax.ShapeDtypeStruct((b, hh, t, dh), BF16),
                   jax.ShapeDtypeStruct((b, hh, t, dh), BF16),
                   jax.ShapeDtypeStruct((b, hh, MOBA_V_ROWS, t), BF16),
                   jax.ShapeDtypeStruct((b, hh, nb, t), F32)],
        scratch_shapes=[pltpu.VMEM((nb, dh), F32)],
        compiler_params=_cparams("parallel", "parallel", "arbitrary"),
        name="moba_prep",
    )(proj3, proj3, proj3, qw, kw, rel_bias)


def _moba_kernel(q_ref, kn_ref, vt_ref, add_ref, bias_ref, o_ref, buf_a, buf_b, *, nb):
    i = pl.program_id(2)
    bs = MOBA_BLOCK
    ch = MOBA_TILES_PER_STEP
    E = range(MOBA_STREAMS)
    q16 = [q_ref[e, 0] for e in E]

    crows = ch * bs
    n_pairs = (i + 2 * ch) // (2 * ch)
    last_chunk = nb // ch - 1

    def chunk_scores(buf, e, c, far):
        r = pl.multiple_of(c * crows, crows)
        raw = _dot_nt(kn_ref[e, 0, pl.ds(r, crows), :], q16[e])
        mx = None
        for t in range(ch):
            j = c * ch + t
            add = add_ref[e, 0, pl.ds(j, 1), :]
            if far:
                st = raw[t * bs:(t + 1) * bs, :] + add
            else:
                d = jnp.clip(i - j, 0, BIAS_TILES - 1)
                st = jnp.where(add > 0.5 * NEG_INF, raw[t * bs:(t + 1) * bs, :] + bias_ref[0, d], NEG_INF)
            buf[e, t * bs:(t + 1) * bs, :] = st
            tmx = jnp.max(st, axis=0, keepdims=True)
            mx = tmx if mx is None else jnp.maximum(mx, tmx)
        return mx

    def absorb(buf, e, c, mx, m, acc):
        r = pl.multiple_of(c * crows, crows)
        m_new = jnp.maximum(m, mx)
        p = jnp.exp2(buf[e] - m_new)
        acc = jnp.exp2(m - m_new) * acc + _dot(vt_ref[e, 0, :, pl.ds(r, crows)], p.astype(BF16))
        return m_new, acc

    def pair(far, c, carry):
        m, acc, mx_a = (list(x) for x in carry)
        mx_b = [chunk_scores(buf_b, e, 2 * c + 1, far) for e in E]
        for e in E:
            m[e], acc[e] = absorb(buf_a, e, 2 * c, mx_a[e], m[e], acc[e])
        mx_a = [chunk_scores(buf_a, e, jnp.minimum(2 * c + 2, last_chunk), far) for e in E]
        for e in E:
            m[e], acc[e] = absorb(buf_b, e, 2 * c + 1, mx_b[e], m[e], acc[e])
        return tuple(m), tuple(acc), tuple(mx_a)

    def step(c, carry):
        all_far = (2 * c + 3) * ch - 1 + (BIAS_TILES - 1) <= i
        return lax.cond(all_far, functools.partial(pair, True, c), functools.partial(pair, False, c), carry)

    m0 = tuple(jnp.full((1, bs), MOBA_M_INIT, F32) for e in E)
    acc0 = tuple(jnp.zeros((MOBA_V_ROWS, bs), F32) for e in E)
    mx_a = tuple(chunk_scores(buf_a, e, 0, False) for e in E)
    m, acc, _ = lax.fori_loop(0, n_pairs, step, (m0, acc0, mx_a))
    for e in E:
        o_ref[e] = (acc[e][:HEAD_DIM] / acc[e][HEAD_DIM:HEAD_DIM + 1]).T.astype(o_ref.dtype)


def _moba(q16, kn, vt, add, bias):
    b, hh, t, dh = kn.shape
    bs = MOBA_BLOCK
    nb = t // bs
    ns = MOBA_STREAMS
    crows = MOBA_TILES_PER_STEP * bs
    assert b % ns == 0 and nb % (2 * MOBA_TILES_PER_STEP) == 0
    return pl.pallas_call(
        functools.partial(_moba_kernel, nb=nb),
        grid=(b // ns, hh, nb),
        in_specs=[pl.BlockSpec((ns, 1, bs, dh), lambda bi, hi, ti: (bi, hi, ti, 0)),
                  pl.BlockSpec((ns, 1, t, dh), lambda bi, hi, ti: (bi, hi, 0, 0)),
                  pl.BlockSpec((ns, 1, MOBA_V_ROWS, t), lambda bi, hi, ti: (bi, hi, 0, 0)),
                  pl.BlockSpec((ns, 1, nb, bs), lambda bi, hi, ti: (bi, hi, 0, ti)),
                  pl.BlockSpec((1, BIAS_TILES, bs, bs), lambda bi, hi, ti: (hi, 0, 0, 0))],
        out_specs=pl.BlockSpec((ns, bs, dh), lambda bi, hi, ti: (bi, ti, hi)),
        out_shape=jax.ShapeDtypeStruct((b, t, hh * dh), BF16),
        scratch_shapes=[pltpu.VMEM((ns, crows, bs), F32),
                        pltpu.VMEM((ns, crows, bs), F32)],
        compiler_params=_cparams("parallel", "parallel", "parallel"),
        name="moba",
    )(q16, kn, vt, add, bias)


def _tile(n, want):
    t = min(n, want)
    assert n % t == 0, (n, want)
    return t


def kernel(x, norm_mix_w, w_in, conv_w, a_log, dt_bias, gdn_o_norm_w, q_norm_w, k_norm_w, rel_bias,
           w_branch_gdn, w_branch_moba, w_out, norm_ffn_w, w_ffn_gate, w_ffn_up, w_ffn_down):
    b, t, d = x.shape
    m = b * t
    gw = GDN_HEADS * HEAD_DIM
    mw = MOBA_HEADS * HEAD_DIM
    assert t % MOBA_PREP_ROWS == 0 and w_in.shape[0] == 1
    assert w_in.shape[2] == 4 * gw + 2 * GDN_HEADS + 3 * mw + 2 * d

    o_bd = 4 * gw
    o_moba = o_bd + 2 * GDN_HEADS
    o_gate = o_moba + 3 * mw
    n_main = 2 * d + 4 * gw + 3 * mw
    tn_proj = PROJ_TILE[1]
    n_proj = -(-(n_main + LANES) // tn_proj) * tn_proj
    w_proj = _regroup_weights(w_in, (o_bd, o_moba, o_gate), n_proj)
    col_gdn = (2 * d) // LANES
    col_moba = col_gdn + 4 * GDN_HEADS
    col_bd = col_moba + 3 * MOBA_HEADS

    x2 = x.reshape(m, d)
    proj = _norm_matmul(x2, norm_mix_w, w_proj, _tile(m, PROJ_TILE[0]), tn_proj, F32, "proj")
    proj3 = proj.reshape(b, t, n_proj)

    lane_pad = jnp.zeros((1, LANES - 2 * GDN_HEADS), F32)
    head_pad = jnp.zeros((1, GDN_HEADS), F32)
    alog_row = jnp.concatenate([head_pad, a_log, lane_pad], axis=1)
    dtb_row = jnp.concatenate([head_pad, dt_bias, lane_pad], axis=1)
    y_a = _gdn(proj3, conv_w[0], alog_row, dtb_row, gdn_o_norm_w, col_gdn, col_bd, _tile(t, GDN_BLOCK))

    bias = _bias_tiles(rel_bias)
    q16, kn, vt, add = _moba_prep(proj3, q_norm_w, k_norm_w, rel_bias, col_moba, MOBA_PREP_ROWS)
    y_b = _moba(q16, kn, vt, add, bias)

    mix = _mix(y_a.reshape(m, gw), y_b.reshape(m, mw), proj,
               w_branch_gdn[0].astype(BF16), w_branch_moba[0].astype(BF16), _tile(m, MIX_ROWS), d)
    h1 = _matmul_res(mix, w_out[0].astype(BF16), x2, _tile(m, OUT_ROWS), d, "out")

    hid = _ffn_up(h1, norm_ffn_w, w_ffn_gate[0].astype(BF16), w_ffn_up[0].astype(BF16),
                  _tile(m, FFN_UP_TILE[0]), FFN_UP_TILE[1])
    h2 = _matmul_res(hid, w_ffn_down[0].astype(BF16), h1, _tile(m, FFN_DOWN_TILE[0]), FFN_DOWN_TILE[1],
                     "ffn_down")
    return h2.reshape(b, t, d)
```

```python
import functools
import math

import jax
import jax.numpy as jnp
from jax import lax
from jax.experimental import pallas as pl
from jax.experimental.pallas import tpu as pltpu

F32 = jnp.float32
BF16 = jnp.bfloat16
HIGHEST = lax.Precision.HIGHEST

LANES = 128
HEAD_DIM = 128
GDN_HEADS = 8
GDN_CONV = 4
GDN_CHUNK = 64
GDN_INV_BASE = 8
GDN_HEADS_PER_STEP = 8
NEG_INF = -1e30
MOBA_HEADS = 8
MOBA_BLOCK = 256
MOBA_TOPK = 3
REL_BUCKETS = 32
REL_MAX_DIST = 2048
BIAS_TILES = 8
MOBA_TILES_PER_STEP = 2
MOBA_PREP_ROWS = 2048
MOBA_STREAMS = 4
MOBA_V_ROWS = HEAD_DIM + 16
MOBA_M_INIT = 0.1 * NEG_INF
LOG2E = math.log2(math.e)
RMS_EPS = 1e-6
V7X_VMEM_BYTES = 64 * 1024 * 1024
VMEM_LIMIT = V7X_VMEM_BYTES * 7 // 8

PROJ_TILE = (1024, 1280)
MIX_ROWS = 512
OUT_ROWS = 512
FFN_UP_TILE = (1024, 512)
FFN_DOWN_TILE = (1024, 512)
GDN_BLOCK = 256
REGROUP_ROWS = 128


def _cparams(*sem):
    return pltpu.CompilerParams(dimension_semantics=sem, vmem_limit_bytes=VMEM_LIMIT)


def _sigmoid(x):
    return 0.5 + 0.5 * jnp.tanh(0.5 * x)


def _silu(x):
    h = 0.5 * x
    return h + h * jnp.tanh(h)


def _dot(a, b, precision=None):
    return jnp.dot(a, b, preferred_element_type=F32, precision=precision)


def _dot_nt(a, b, precision=None):
    return lax.dot_general(a, b, (((1,), (1,)), ((), ())),
                           preferred_element_type=F32, precision=precision)


def _dot_tn(a, b):
    return lax.dot_general(a, b, (((0,), (0,)), ((), ())), preferred_element_type=F32)


def _rms_rows(x, w):
    return x * lax.rsqrt(jnp.mean(x * x, axis=-1, keepdims=True) + RMS_EPS) * w


def _regroup_kernel(w_ref, o_ref, *, cuts):
    o_bd, o_moba, o_gate = cuts
    x = w_ref[...]
    parts = [x[:, o_gate:], x[:, :o_bd], x[:, o_moba:o_gate], x[:, o_bd:o_moba]]
    used = sum(p.shape[1] for p in parts)
    parts.append(jnp.zeros((x.shape[0], o_ref.shape[1] - used), x.dtype))
    o_ref[...] = jnp.concatenate(parts, axis=1).astype(o_ref.dtype)


def _regroup_weights(w, cuts, n_out):
    _, k, n_in = w.shape
    rows = _tile(k, REGROUP_ROWS)
    return pl.pallas_call(
        functools.partial(_regroup_kernel, cuts=cuts),
        grid=(k // rows,),
        in_specs=[pl.BlockSpec((None, rows, n_in), lambda i: (0, i, 0))],
        out_specs=pl.BlockSpec((rows, n_out), lambda i: (i, 0)),
        out_shape=jax.ShapeDtypeStruct((k, n_out), BF16),
        compiler_params=_cparams("parallel"),
        name="regroup",
    )(w)


def _norm_matmul_kernel(x_ref, nw_ref, w_ref, o_ref, u_scr):
    @pl.when(pl.program_id(1) == 0)
    def _():
        u_scr[...] = _rms_rows(x_ref[...], nw_ref[...]).astype(BF16)

    o_ref[...] = _dot(u_scr[...], w_ref[...]).astype(o_ref.dtype)


def _norm_matmul(x, nw, w, tm, tn, out_dtype, name):
    m, k = x.shape
    n = w.shape[1]
    assert m % tm == 0 and n % tn == 0
    return pl.pallas_call(
        _norm_matmul_kernel,
        grid=(m // tm, n // tn),
        in_specs=[pl.BlockSpec((tm, k), lambda i, j: (i, 0)),
                  pl.BlockSpec((1, k), lambda i, j: (0, 0)),
                  pl.BlockSpec((k, tn), lambda i, j: (0, j))],
        out_specs=pl.BlockSpec((tm, tn), lambda i, j: (i, j)),
        out_shape=jax.ShapeDtypeStruct((m, n), out_dtype),
        scratch_shapes=[pltpu.VMEM((tm, k), BF16)],
        compiler_params=_cparams("parallel", "arbitrary"),
        name=name,
    )(x, nw, w)


def _matmul_res_kernel(a_ref, w_ref, r_ref, o_ref):
    o_ref[...] = r_ref[...] + _dot(a_ref[...], w_ref[...])


def _matmul_res(a, w, res, tm, tn, name):
    m, k = a.shape
    n = w.shape[1]
    assert m % tm == 0 and n % tn == 0
    return pl.pallas_call(
        _matmul_res_kernel,
        grid=(m // tm, n // tn),
        in_specs=[pl.BlockSpec((tm, k), lambda i, j: (i, 0)),
                  pl.BlockSpec((k, tn), lambda i, j: (0, j)),
                  pl.BlockSpec((tm, tn), lambda i, j: (i, j))],
        out_specs=pl.BlockSpec((tm, tn), lambda i, j: (i, j)),
        out_shape=jax.ShapeDtypeStruct((m, n), F32),
        compiler_params=_cparams("parallel", "parallel"),
        name=name,
    )(a, w, res)


def _ffn_up_kernel(x_ref, nw_ref, wg_ref, wu_ref, o_ref, u_scr):
    @pl.when(pl.program_id(1) == 0)
    def _():
        u_scr[...] = _rms_rows(x_ref[...], nw_ref[...]).astype(BF16)

    u = u_scr[...]
    g = _dot(u, wg_ref[...])
    o_ref[...] = (_silu(g) * _dot(u, wu_ref[...])).astype(o_ref.dtype)


def _ffn_up(x, nw, wg, wu, tm, tn):
    m, k = x.shape
    n = wg.shape[1]
    assert m % tm == 0 and n % tn == 0
    return pl.pallas_call(
        _ffn_up_kernel,
        grid=(m // tm, n // tn),
        in_specs=[pl.BlockSpec((tm, k), lambda i, j: (i, 0)),
                  pl.BlockSpec((1, k), lambda i, j: (0, 0)),
                  pl.BlockSpec((k, tn), lambda i, j: (0, j)),
                  pl.BlockSpec((k, tn), lambda i, j: (0, j))],
        out_specs=pl.BlockSpec((tm, tn), lambda i, j: (i, j)),
        out_shape=jax.ShapeDtypeStruct((m, n), BF16),
        scratch_shapes=[pltpu.VMEM((tm, k), BF16)],
        compiler_params=_cparams("parallel", "arbitrary"),
        name="ffn_up",
    )(x, nw, wg, wu)


def _mix_kernel(ya_ref, yb_ref, ga_ref, gb_ref, wa_ref, wb_ref, o_ref):
    a = _dot(ya_ref[...], wa_ref[...])
    b = _dot(yb_ref[...], wb_ref[...])
    o_ref[...] = (_sigmoid(ga_ref[...]) * a + _sigmoid(gb_ref[...]) * b).astype(o_ref.dtype)


def _mix(ya, yb, proj, wa, wb, tm, tn):
    m, k = ya.shape
    n = wa.shape[1]
    nb = n // tn
    assert m % tm == 0 and n % tn == 0
    return pl.pallas_call(
        _mix_kernel,
        grid=(m // tm, nb),
        in_specs=[pl.BlockSpec((tm, k), lambda i, j: (i, 0)),
                  pl.BlockSpec((tm, k), lambda i, j: (i, 0)),
                  pl.BlockSpec((tm, tn), lambda i, j: (i, j)),
                  pl.BlockSpec((tm, tn), lambda i, j: (i, nb + j)),
                  pl.BlockSpec((k, tn), lambda i, j: (0, j)),
                  pl.BlockSpec((k, tn), lambda i, j: (0, j))],
        out_specs=pl.BlockSpec((tm, tn), lambda i, j: (i, j)),
        out_shape=jax.ShapeDtypeStruct((m, n), BF16),
        compiler_params=_cparams("parallel", "parallel"),
        name="mix",
    )(ya, yb, proj, proj, wa, wb)


def _split3(x):
    a = x.astype(BF16)
    r = x - a.astype(F32)
    b = r.astype(BF16)
    c = (r - b.astype(F32)).astype(BF16)
    return a, b, c


def _gdn_kernel(q_ref, k_ref, v_ref, z_ref, bd_ref, cwq_ref, cwk_ref, cwv_ref,
                alog_ref, dtb_ref, onw_ref, o_ref, s_scr, tail_scr, *, tb):
    hp = GDN_HEADS_PER_STEP
    C = GDN_CHUNK
    D = HEAD_DIM
    head0 = pl.program_id(1) * hp

    @pl.when(pl.program_id(2) == 0)
    def _():
        s_scr[...] = jnp.zeros_like(s_scr)
        tail_scr[...] = jnp.zeros_like(tail_scr)

    row8 = lax.broadcasted_iota(jnp.int32, (8, hp * D), 0)
    tails = []

    def conv_silu(u_ref, cw_ref, idx):
        u = u_ref[0]
        w = cw_ref[...]
        tail = tail_scr[idx]
        y = None
        ytop = None
        for s in (3, 2, 1):
            sh = pltpu.roll(u, s, axis=0)
            top = jnp.where(row8 < s, pltpu.roll(tail, s, axis=0), sh[0:8])
            wj = w[3 - s:4 - s]
            y = sh * wj if y is None else y + sh * wj
            ytop = top * wj if ytop is None else ytop + top * wj
        y = y + u * w[3:4]
        ytop = ytop + u[0:8] * w[3:4]
        tails.append(u[tb - 8:tb])
        y = jnp.concatenate([ytop, y[8:]], axis=0)
        return _silu(y)

    q_all = conv_silu(q_ref, cwq_ref, 0)
    k_all = conv_silu(k_ref, cwk_ref, 1)
    v_all = conv_silu(v_ref, cwv_ref, 2)

    bd = bd_ref[0]
    lane = lax.broadcasted_iota(jnp.int32, (tb, LANES), 1)
    beta_all = _sigmoid(bd)
    xg = bd + dtb_ref[...]
    softplus = jnp.maximum(xg, 0.0) + jnp.log1p(jnp.exp(-jnp.abs(xg)))
    g_all = -jnp.exp(alog_ref[...]) * softplus
    betas = [jnp.sum(jnp.where(lane == head0 + e, beta_all, 0.0), axis=-1, keepdims=True)
             for e in range(hp)]
    gs = [jnp.sum(jnp.where(lane == head0 + e + GDN_HEADS, g_all, 0.0), axis=-1, keepdims=True)
          for e in range(hp)]

    nc = tb // C
    ri = lax.broadcasted_iota(jnp.int32, (tb, tb), 0)
    ci = lax.broadcasted_iota(jnp.int32, (tb, tb), 1)
    same = (ri // C) == (ci // C)
    tril16 = jnp.where(same & (ri >= ci), 1.0, 0.0).astype(BF16)
    g_b = jnp.concatenate([jnp.broadcast_to(g, (tb, LANES)) for g in gs], axis=1)
    gc_all = sum(_dot(tril16, piece) for piece in _split3(g_b))

    iw = lax.broadcasted_iota(jnp.int32, (C, tb), 0)
    jw = lax.broadcasted_iota(jnp.int32, (C, tb), 1) % C
    incl_w = iw >= jw
    strict_w = iw > jw
    eye_w = jnp.where(iw == jw, 1.0, 0.0).astype(F32)
    low_half = (lax.broadcasted_iota(jnp.int32, (C, LANES), 1) < C)

    def block_diag(wide16):
        return jnp.where(same, jnp.concatenate([wide16] * nc, axis=0), jnp.zeros((), BF16))

    def pair_blocks(full):
        return jnp.where(low_half, full[:C], full[C:])

    E = range(hp)
    pairs = range(tb // LANES)

    def l2n(x):
        return x * lax.rsqrt(jnp.sum(x * x, axis=-1, keepdims=True) + RMS_EPS)

    qs = [l2n(q_all[:, e * D:(e + 1) * D]) * (D ** -0.5) for e in E]
    ks = [l2n(k_all[:, e * D:(e + 1) * D]) for e in E]
    vs = [v_all[:, e * D:(e + 1) * D] for e in E]
    gcs = [gc_all[:, e * D:(e + 1) * D] for e in E]
    g_col = [jnp.concatenate([pair_blocks(gc[p * LANES:(p + 1) * LANES]) for p in pairs], axis=1)
             for gc in gcs]
    g_row = [jnp.concatenate([gc[p * LANES:(p + 1) * LANES, :].T[0:1, :] for p in pairs], axis=1)
             for gc in gcs]
    decay = [jnp.where(incl_w, jnp.exp(jnp.where(incl_w, g_col[e] - g_row[e], 0.0)), 0.0) for e in E]
    k16 = [k.astype(BF16) for k in ks]
    q16 = [q.astype(BF16) for q in qs]
    kb = [ks[e] * betas[e] for e in E]
    kb16 = [x.astype(BF16) for x in kb]

    def pair_products(a16, b16):
        return jnp.concatenate([pair_blocks(_dot_nt(a16[p * LANES:(p + 1) * LANES],
                                                    b16[p * LANES:(p + 1) * LANES])) for p in pairs], axis=1)

    lmat = [jnp.where(strict_w, pair_products(kb16[e], k16[e]) * decay[e], 0.0) for e in E]
    amat = [jnp.where(incl_w, pair_products(q16[e], k16[e]) * decay[e], 0.0) for e in E]
    base = GDN_INV_BASE
    blk_i, blk_j = iw // base, jw // base
    neg_d16 = [jnp.where(blk_i == blk_j, -lmat[e], 0.0).astype(BF16) for e in E]
    tinv = [eye_w + neg_d16[e].astype(F32) for e in E]
    pw = [_dot(neg_d16[e], block_diag(neg_d16[e])) for e in E]
    span = 2
    while 2 * span < base:
        p16 = [pw[e].astype(BF16) for e in E]
        prod = [_dot(jnp.concatenate([p16[e], tinv[e].astype(BF16)], axis=0), block_diag(p16[e])) for e in E]
        pw = [prod[e][:C] for e in E]
        tinv = [tinv[e] + prod[e][C:] for e in E]
        span *= 2
    tinv = [tinv[e] + _dot(tinv[e].astype(BF16), block_diag(pw[e].astype(BF16))) for e in E]
    s = base
    while s < C:
        below_left = (iw // (2 * s) == jw // (2 * s)) & ((iw // s) % 2 == 1) & ((jw // s) % 2 == 0)
        c16 = [jnp.where(below_left, lmat[e], 0.0).astype(BF16) for e in E]
        cx = [_dot(c16[e], block_diag(tinv[e].astype(BF16))) for e in E]
        tinv = [tinv[e] - _dot(tinv[e].astype(BF16), block_diag(cx[e].astype(BF16))) for e in E]
        s *= 2
    eg = [jnp.exp(gc) for gc in gcs]
    rhs = [jnp.concatenate([kb[e] * eg[e], vs[e] * betas[e]], axis=1).astype(BF16) for e in E]
    wu = [_dot(block_diag(tinv[e].astype(BF16)), rhs[e]).astype(BF16) for e in E]
    au = [_dot(block_diag(amat[e].astype(BF16)), wu[e]) for e in E]
    q_eff = [(qs[e] * eg[e] - au[e][:, :D]).astype(BF16) for e in E]
    gl = [jnp.concatenate([jnp.broadcast_to(gc[c * C + C - 1:c * C + C, :], (C, LANES))
                           for c in range(nc)], axis=0) for gc in gcs]
    k_dec = [(ks[e] * jnp.exp(gl[e] - gcs[e])).astype(BF16) for e in E]
    pn = [[_dot_tn(k_dec[e][c * C:(c + 1) * C], wu[e][c * C:(c + 1) * C]) for c in range(nc)]
          for e in E]

    states = [s_scr[e] for e in E]
    onw = onw_ref[...]
    for c in range(nc):
        r0 = c * C
        lhs = [jnp.concatenate([pn[e][c][:, :D].astype(BF16), q_eff[e][r0:r0 + C]], axis=0) for e in E]
        res = [_dot(lhs[e], states[e].astype(BF16)) for e in E]
        outs = [_rms_rows(res[e][D:] + au[e][r0:r0 + C, D:], onw) for e in E]
        states = [states[e] * jnp.exp(gl[e][r0:r0 + 1, :]) - res[e][:D] + pn[e][c][:, D:] for e in E]
        zc = z_ref[0, r0:r0 + C, :]
        o_ref[0, r0:r0 + C, :] = (jnp.concatenate(outs, axis=1) * _silu(zc)).astype(o_ref.dtype)
    for e in range(hp):
        s_scr[e] = states[e]
    for idx in range(3):
        tail_scr[idx] = tails[idx]


def _gdn(proj3, conv_w, alog_row, dtb_row, onw, col0, bd_col, tb):
    b, t, _ = proj3.shape
    hp = GDN_HEADS_PER_STEP
    ng = GDN_HEADS // hp
    w = hp * HEAD_DIM

    def col(base):
        return pl.BlockSpec((1, tb, w), lambda bi, hi, ti: (bi, ti, base // hp + hi))

    def cw(base):
        return pl.BlockSpec((GDN_CONV, w), lambda bi, hi, ti: (0, base // hp + hi))

    assert col0 % hp == 0
    row = pl.BlockSpec((1, LANES), lambda bi, hi, ti: (0, 0))
    return pl.pallas_call(
        functools.partial(_gdn_kernel, tb=tb),
        grid=(b, ng, t // tb),
        in_specs=[col(col0), col(col0 + GDN_HEADS), col(col0 + 2 * GDN_HEADS), col(col0 + 3 * GDN_HEADS),
                  pl.BlockSpec((1, tb, LANES), lambda bi, hi, ti: (bi, ti, bd_col)),
                  cw(0), cw(GDN_HEADS), cw(2 * GDN_HEADS), row, row, row],
        out_specs=pl.BlockSpec((1, tb, w), lambda bi, hi, ti: (bi, ti, hi)),
        out_shape=jax.ShapeDtypeStruct((b, t, GDN_HEADS * HEAD_DIM), BF16),
        scratch_shapes=[pltpu.VMEM((hp, HEAD_DIM, HEAD_DIM), F32),
                        pltpu.VMEM((3, 8, w), F32)],
        compiler_params=_cparams("parallel", "parallel", "arbitrary"),
        name="gdn",
    )(proj3, proj3, proj3, proj3, proj3, conv_w, conv_w, conv_w, alog_row, dtb_row, onw)


def _bias_kernel(rel_ref, o_ref):
    h = pl.program_id(0)
    bs = MOBA_BLOCK
    max_exact = REL_BUCKETS // 2
    x = lax.broadcasted_iota(jnp.int32, (8, 2 * bs), 1)
    for d in range(BIAS_TILES):
        dist = jnp.maximum(d * bs + x - bs, 0)
        df = dist.astype(F32)
        log_ratio = jnp.log(jnp.maximum(df, float(max_exact)) / max_exact) / math.log(REL_MAX_DIST / max_exact)
        large = max_exact + (log_ratio * (REL_BUCKETS - max_exact)).astype(jnp.int32)
        large = jnp.minimum(large, REL_BUCKETS - 1)
        bucket = jnp.where(dist < max_exact, dist, large)
        row = jnp.zeros((8, 2 * bs), F32)
        for b in range(REL_BUCKETS):
            row = jnp.where(bucket == b, rel_ref[b, h], row)
        base = jnp.broadcast_to(row[0:1, :] * LOG2E, (bs, 2 * bs))
        tile = pltpu.roll(base, 0, 1, stride=1, stride_axis=0)[:, bs:]
        if d == 0:
            kk = lax.broadcasted_iota(jnp.int32, (bs, bs), 0)
            qq = lax.broadcasted_iota(jnp.int32, (bs, bs), 1)
            tile = jnp.where(kk <= qq, tile, NEG_INF)
        o_ref[0, d] = tile


def _bias_tiles(rel_bias):
    bs = MOBA_BLOCK
    max_exact = REL_BUCKETS // 2
    nearest = (BIAS_TILES - 1) * bs - (bs - 1)
    assert max_exact + int(math.log(nearest / max_exact) / math.log(REL_MAX_DIST / max_exact)
                           * (REL_BUCKETS - max_exact)) >= REL_BUCKETS - 1
    return pl.pallas_call(
        _bias_kernel,
        grid=(MOBA_HEADS,),
        in_specs=[pl.BlockSpec(memory_space=pltpu.SMEM)],
        out_specs=pl.BlockSpec((1, BIAS_TILES, bs, bs), lambda h: (h, 0, 0, 0)),
        out_shape=jax.ShapeDtypeStruct((MOBA_HEADS, BIAS_TILES, bs, bs), F32),
        compiler_params=_cparams("parallel"),
        name="bias",
    )(rel_bias)


def _moba_prep_kernel(q_ref, k_ref, v_ref, qw_ref, kw_ref, rel_ref, q16_ref, kn_ref, vt_ref, add_ref, km_scr,
                      *, rows, nb):
    bs = MOBA_BLOCK
    nsub = rows // bs
    ti = pl.program_id(2)

    @pl.when(ti == 0)
    def _():
        km_scr[...] = jnp.zeros_like(km_scr)

    kn = _rms_rows(k_ref[0], kw_ref[...])
    kn_ref[0, 0] = kn.astype(BF16)
    for s in range(nsub):
        km_scr[pl.ds(ti * nsub + s, 1), :] = jnp.mean(kn[s * bs:(s + 1) * bs], axis=0, keepdims=True)
        vt_ref[0, 0, 0:HEAD_DIM, s * bs:(s + 1) * bs] = v_ref[0, s * bs:(s + 1) * bs, :].T.astype(BF16)
    pad_row = lax.broadcasted_iota(jnp.int32, (MOBA_V_ROWS - HEAD_DIM, rows), 0)
    vt_ref[0, 0, HEAD_DIM:, :] = jnp.where(pad_row == 0, 1.0, 0.0).astype(BF16)

    q = _rms_rows(q_ref[0], qw_ref[...])
    q16_ref[0, 0] = (q * ((HEAD_DIM ** -0.5) * LOG2E)).astype(BF16)

    blk = lax.broadcasted_iota(jnp.int32, (nb, rows), 0)
    own_blk = ti * nsub + lax.broadcasted_iota(jnp.int32, (nb, rows), 1) // bs
    blkf = blk.astype(F32)
    valid = blk < own_blk
    work = jnp.where(valid, _dot_nt(km_scr[...], q, HIGHEST), NEG_INF)
    picked = jnp.zeros((nb, rows), F32)
    for _ in range(MOBA_TOPK):
        best = jnp.max(work, axis=0, keepdims=True)
        first = jnp.min(jnp.where(work == best, blkf, float(nb)), axis=0, keepdims=True)
        hit = blkf == first
        picked = jnp.where(hit, 1.0, picked)
        work = jnp.where(hit, -jnp.inf, work)
    far_bias = rel_ref[REL_BUCKETS - 1, pl.program_id(1)] * LOG2E
    attended = (valid & (picked > 0.0)) | (blk == own_blk)
    add_ref[0, 0] = jnp.where(attended, far_bias, NEG_INF)


def _moba_prep(proj3, qw, kw, rel_bias, col0, rows):
    b, t, _ = proj3.shape
    hh, bs, dh = MOBA_HEADS, MOBA_BLOCK, HEAD_DIM
    nb = t // bs

    def col(base):
        return pl.BlockSpec((1, rows, LANES), lambda bi, hi, ti: (bi, ti, base + hi))

    row = pl.BlockSpec((1, LANES), lambda bi, hi, ti: (0, 0))
    return pl.pallas_call(
        functools.partial(_moba_prep_kernel, rows=rows, nb=nb),
        grid=(b, hh, t // rows),
        in_specs=[col(col0), col(col0 + hh), col(col0 + 2 * hh), row, row,
                  pl.BlockSpec(memory_space=pltpu.SMEM)],
        out_specs=[pl.BlockSpec((1, 1, rows, dh), lambda bi, hi, ti: (bi, hi, ti, 0)),
                   pl.BlockSpec((1, 1, rows, dh), lambda bi, hi, ti: (bi, hi, ti, 0)),
                   pl.BlockSpec((1, 1, MOBA_V_ROWS, rows), lambda bi, hi, ti: (bi, hi, 0, ti)),
                   pl.BlockSpec((1, 1, nb, rows), lambda bi, hi, ti: (bi, hi, 0, ti))],
        out_shape=[jax.ShapeDtypeStruct((b, hh, t, dh), BF16),
                   jax.ShapeDtypeStruct((b, hh, t, dh), BF16),
                   jax.ShapeDtypeStruct((b, hh, MOBA_V_ROWS, t), BF16),
                   jax.ShapeDtypeStruct((b, hh, nb, t), F32)],
        scratch_shapes=[pltpu.VMEM((nb, dh), F32)],
        compiler_params=_cparams("parallel", "parallel", "arbitrary"),
        name="moba_prep",
    )(proj3, proj3, proj3, qw, kw, rel_bias)


def _moba_kernel(q_ref, kn_ref, vt_ref, add_ref, bias_ref, o_ref, buf_a, buf_b, *, nb):
    i = pl.program_id(2)
    bs = MOBA_BLOCK
    ch = MOBA_TILES_PER_STEP
    E = range(MOBA_STREAMS)
    q16 = [q_ref[e, 0] for e in E]

    crows = ch * bs
    n_pairs = (i + 2 * ch) // (2 * ch)
    last_chunk = nb // ch - 1

    def chunk_scores(buf, e, c, far):
        r = pl.multiple_of(c * crows, crows)
        raw = _dot_nt(kn_ref[e, 0, pl.ds(r, crows), :], q16[e])
        mx = None
        for t in range(ch):
            j = c * ch + t
            add = add_ref[e, 0, pl.ds(j, 1), :]
            if far:
                st = raw[t * bs:(t + 1) * bs, :] + add
            else:
                d = jnp.clip(i - j, 0, BIAS_TILES - 1)
                st = jnp.where(add > 0.5 * NEG_INF, raw[t * bs:(t + 1) * bs, :] + bias_ref[0, d], NEG_INF)
            buf[e, t * bs:(t + 1) * bs, :] = st
            tmx = jnp.max(st, axis=0, keepdims=True)
            mx = tmx if mx is None else jnp.maximum(mx, tmx)
        return mx

    def absorb(buf, e, c, mx, m, acc):
        r = pl.multiple_of(c * crows, crows)
        m_new = jnp.maximum(m, mx)
        p = jnp.exp2(buf[e] - m_new)
        acc = jnp.exp2(m - m_new) * acc + _dot(vt_ref[e, 0, :, pl.ds(r, crows)], p.astype(BF16))
        return m_new, acc

    def pair(far, c, carry):
        m, acc, mx_a = (list(x) for x in carry)
        mx_b = [chunk_scores(buf_b, e, 2 * c + 1, far) for e in E]
        for e in E:
            m[e], acc[e] = absorb(buf_a, e, 2 * c, mx_a[e], m[e], acc[e])
        mx_a = [chunk_scores(buf_a, e, jnp.minimum(2 * c + 2, last_chunk), far) for e in E]
        for e in E:
            m[e], acc[e] = absorb(buf_b, e, 2 * c + 1, mx_b[e], m[e], acc[e])
        return tuple(m), tuple(acc), tuple(mx_a)

    def step(c, carry):
        all_far = (2 * c + 3) * ch - 1 + (BIAS_TILES - 1) <= i
        return lax.cond(all_far, functools.partial(pair, True, c), functools.partial(pair, False, c), carry)

    m0 = tuple(jnp.full((1, bs), MOBA_M_INIT, F32) for e in E)
    acc0 = tuple(jnp.zeros((MOBA_V_ROWS, bs), F32) for e in E)
    mx_a = tuple(chunk_scores(buf_a, e, 0, False) for e in E)
    m, acc, _ = lax.fori_loop(0, n_pairs, step, (m0, acc0, mx_a))
    for e in E:
        o_ref[e] = (acc[e][:HEAD_DIM] / acc[e][HEAD_DIM:HEAD_DIM + 1]).T.astype(o_ref.dtype)


def _moba(q16, kn, vt, add, bias):
    b, hh, t, dh = kn.shape
    bs = MOBA_BLOCK
    nb = t // bs
    ns = MOBA_STREAMS
    crows = MOBA_TILES_PER_STEP * bs
    assert b % ns == 0 and nb % (2 * MOBA_TILES_PER_STEP) == 0
    return pl.pallas_call(
        functools.partial(_moba_kernel, nb=nb),
        grid=(b // ns, hh, nb),
        in_specs=[pl.BlockSpec((ns, 1, bs, dh), lambda bi, hi, ti: (bi, hi, ti, 0)),
                  pl.BlockSpec((ns, 1, t, dh), lambda bi, hi, ti: (bi, hi, 0, 0)),
                  pl.BlockSpec((ns, 1, MOBA_V_ROWS, t), lambda bi, hi, ti: (bi, hi, 0, 0)),
                  pl.BlockSpec((ns, 1, nb, bs), lambda bi, hi, ti: (bi, hi, 0, ti)),
                  pl.BlockSpec((1, BIAS_TILES, bs, bs), lambda bi, hi, ti: (hi, 0, 0, 0))],
        out_specs=pl.BlockSpec((ns, bs, dh), lambda bi, hi, ti: (bi, ti, hi)),
        out_shape=jax.ShapeDtypeStruct((b, t, hh * dh), BF16),
        scratch_shapes=[pltpu.VMEM((ns, crows, bs), F32),
                        pltpu.VMEM((ns, crows, bs), F32)],
        compiler_params=_cparams("parallel", "parallel", "parallel"),
        name="moba",
    )(q16, kn, vt, add, bias)


def _tile(n, want):
    t = min(n, want)
    assert n % t == 0, (n, want)
    return t


def kernel(x, norm_mix_w, w_in, conv_w, a_log, dt_bias, gdn_o_norm_w, q_norm_w, k_norm_w, rel_bias,
           w_branch_gdn, w_branch_moba, w_out, norm_ffn_w, w_ffn_gate, w_ffn_up, w_ffn_down):
    b, t, d = x.shape
    m = b * t
    gw = GDN_HEADS * HEAD_DIM
    mw = MOBA_HEADS * HEAD_DIM
    assert t % MOBA_PREP_ROWS == 0 and w_in.shape[0] == 1
    assert w_in.shape[2] == 4 * gw + 2 * GDN_HEADS + 3 * mw + 2 * d

    o_bd = 4 * gw
    o_moba = o_bd + 2 * GDN_HEADS
    o_gate = o_moba + 3 * mw
    n_main = 2 * d + 4 * gw + 3 * mw
    tn_proj = PROJ_TILE[1]
    n_proj = -(-(n_main + LANES) // tn_proj) * tn_proj
    w_proj = _regroup_weights(w_in.astype(BF16), (o_bd, o_moba, o_gate), n_proj)
    col_gdn = (2 * d) // LANES
    col_moba = col_gdn + 4 * GDN_HEADS
    col_bd = col_moba + 3 * MOBA_HEADS

    x2 = x.reshape(m, d)
    proj = _norm_matmul(x2, norm_mix_w, w_proj, _tile(m, PROJ_TILE[0]), tn_proj, F32, "proj")
    proj3 = proj.reshape(b, t, n_proj)

    lane_pad = jnp.zeros((1, LANES - 2 * GDN_HEADS), F32)
    head_pad = jnp.zeros((1, GDN_HEADS), F32)
    alog_row = jnp.concatenate([head_pad, a_log, lane_pad], axis=1)
    dtb_row = jnp.concatenate([head_pad, dt_bias, lane_pad], axis=1)
    y_a = _gdn(proj3, conv_w[0], alog_row, dtb_row, gdn_o_norm_w, col_gdn, col_bd, _tile(t, GDN_BLOCK))

    bias = _bias_tiles(rel_bias)
    q16, kn, vt, add = _moba_prep(proj3, q_norm_w, k_norm_w, rel_bias, col_moba, MOBA_PREP_ROWS)
    y_b = _moba(q16, kn, vt, add, bias)

    mix = _mix(y_a.reshape(m, gw), y_b.reshape(m, mw), proj,
               w_branch_gdn[0].astype(BF16), w_branch_moba[0].astype(BF16), _tile(m, MIX_ROWS), d)
    h1 = _matmul_res(mix, w_out[0].astype(BF16), x2, _tile(m, OUT_ROWS), d, "out")

    hid = _ffn_up(h1, norm_ffn_w, w_ffn_gate[0].astype(BF16), w_ffn_up[0].astype(BF16),
                  _tile(m, FFN_UP_TILE[0]), FFN_UP_TILE[1])
    h2 = _matmul_res(hid, w_ffn_down[0].astype(BF16), h1, _tile(m, FFN_DOWN_TILE[0]), FFN_DOWN_TILE[1],
                     "ffn_down")
    return h2.reshape(b, t, d)
```

```python
import functools
import math

import jax
import jax.numpy as jnp
from jax import lax
from jax.experimental import pallas as pl
from jax.experimental.pallas import tpu as pltpu

F32 = jnp.float32
BF16 = jnp.bfloat16
HIGHEST = lax.Precision.HIGHEST

LANES = 128
HEAD_DIM = 128
GDN_HEADS = 8
GDN_CONV = 4
GDN_CHUNK = 64
GDN_INV_BASE = 8
GDN_HEADS_PER_STEP = 8
NEG_INF = -1e30
MOBA_HEADS = 8
MOBA_BLOCK = 256
MOBA_TOPK = 3
REL_BUCKETS = 32
REL_MAX_DIST = 2048
BIAS_TILES = 8
MOBA_TILES_PER_STEP = 1
MOBA_PREP_ROWS = 2048
MOBA_STREAMS = 4
MOBA_V_ROWS = HEAD_DIM + 16
MOBA_M_INIT = 0.1 * NEG_INF
LOG2E = math.log2(math.e)
RMS_EPS = 1e-6
V7X_VMEM_BYTES = 64 * 1024 * 1024
VMEM_LIMIT = V7X_VMEM_BYTES * 7 // 8

PROJ_TILE = (1024, 1280)
MIX_ROWS = 512
OUT_ROWS = 512
FFN_UP_TILE = (1024, 512)
FFN_DOWN_TILE = (1024, 512)
GDN_BLOCK = 256
REGROUP_ROWS = 128


def _cparams(*sem):
    return pltpu.CompilerParams(dimension_semantics=sem, vmem_limit_bytes=VMEM_LIMIT)


def _sigmoid(x):
    return 0.5 + 0.5 * jnp.tanh(0.5 * x)


def _silu(x):
    h = 0.5 * x
    return h + h * jnp.tanh(h)


def _dot(a, b, precision=None):
    return jnp.dot(a, b, preferred_element_type=F32, precision=precision)


def _dot_nt(a, b, precision=None):
    return lax.dot_general(a, b, (((1,), (1,)), ((), ())),
                           preferred_element_type=F32, precision=precision)


def _dot_tn(a, b):
    return lax.dot_general(a, b, (((0,), (0,)), ((), ())), preferred_element_type=F32)


def _rms_rows(x, w):
    return x * lax.rsqrt(jnp.mean(x * x, axis=-1, keepdims=True) + RMS_EPS) * w


def _regroup_kernel(w_ref, o_ref, *, cuts):
    o_bd, o_moba, o_gate = cuts
    x = w_ref[...]
    parts = [x[:, o_gate:], x[:, :o_bd], x[:, o_moba:o_gate], x[:, o_bd:o_moba]]
    used = sum(p.shape[1] for p in parts)
    parts.append(jnp.zeros((x.shape[0], o_ref.shape[1] - used), x.dtype))
    o_ref[...] = jnp.concatenate(parts, axis=1).astype(o_ref.dtype)


def _regroup_weights(w, cuts, n_out):
    _, k, n_in = w.shape
    rows = _tile(k, REGROUP_ROWS)
    return pl.pallas_call(
        functools.partial(_regroup_kernel, cuts=cuts),
        grid=(k // rows,),
        in_specs=[pl.BlockSpec((None, rows, n_in), lambda i: (0, i, 0))],
        out_specs=pl.BlockSpec((rows, n_out), lambda i: (i, 0)),
        out_shape=jax.ShapeDtypeStruct((k, n_out), BF16),
        compiler_params=_cparams("parallel"),
        name="regroup",
    )(w)


def _norm_matmul_kernel(x_ref, nw_ref, w_ref, o_ref, u_scr):
    @pl.when(pl.program_id(1) == 0)
    def _():
        u_scr[...] = _rms_rows(x_ref[...], nw_ref[...]).astype(BF16)

    o_ref[...] = _dot(u_scr[...], w_ref[...]).astype(o_ref.dtype)


def _norm_matmul(x, nw, w, tm, tn, out_dtype, name):
    m, k = x.shape
    n = w.shape[1]
    assert m % tm == 0 and n % tn == 0
    return pl.pallas_call(
        _norm_matmul_kernel,
        grid=(m // tm, n // tn),
        in_specs=[pl.BlockSpec((tm, k), lambda i, j: (i, 0)),
                  pl.BlockSpec((1, k), lambda i, j: (0, 0)),
                  pl.BlockSpec((k, tn), lambda i, j: (0, j))],
        out_specs=pl.BlockSpec((tm, tn), lambda i, j: (i, j)),
        out_shape=jax.ShapeDtypeStruct((m, n), out_dtype),
        scratch_shapes=[pltpu.VMEM((tm, k), BF16)],
        compiler_params=_cparams("parallel", "arbitrary"),
        name=name,
    )(x, nw, w)


def _matmul_res_kernel(a_ref, w_ref, r_ref, o_ref):
    o_ref[...] = r_ref[...] + _dot(a_ref[...], w_ref[...])


def _matmul_res(a, w, res, tm, tn, name):
    m, k = a.shape
    n = w.shape[1]
    assert m % tm == 0 and n % tn == 0
    return pl.pallas_call(
        _matmul_res_kernel,
        grid=(m // tm, n // tn),
        in_specs=[pl.BlockSpec((tm, k), lambda i, j: (i, 0)),
                  pl.BlockSpec((k, tn), lambda i, j: (0, j)),
                  pl.BlockSpec((tm, tn), lambda i, j: (i, j))],
        out_specs=pl.BlockSpec((tm, tn), lambda i, j: (i, j)),
        out_shape=jax.ShapeDtypeStruct((m, n), F32),
        compiler_params=_cparams("parallel", "parallel"),
        name=name,
    )(a, w, res)


def _ffn_up_kernel(x_ref, nw_ref, wg_ref, wu_ref, o_ref, u_scr):
    @pl.when(pl.program_id(1) == 0)
    def _():
        u_scr[...] = _rms_rows(x_ref[...], nw_ref[...]).astype(BF16)

    u = u_scr[...]
    g = _dot(u, wg_ref[...])
    o_ref[...] = (_silu(g) * _dot(u, wu_ref[...])).astype(o_ref.dtype)


def _ffn_up(x, nw, wg, wu, tm, tn):
    m, k = x.shape
    n = wg.shape[1]
    assert m % tm == 0 and n % tn == 0
    return pl.pallas_call(
        _ffn_up_kernel,
        grid=(m // tm, n // tn),
        in_specs=[pl.BlockSpec((tm, k), lambda i, j: (i, 0)),
                  pl.BlockSpec((1, k), lambda i, j: (0, 0)),
                  pl.BlockSpec((k, tn), lambda i, j: (0, j)),
                  pl.BlockSpec((k, tn), lambda i, j: (0, j))],
        out_specs=pl.BlockSpec((tm, tn), lambda i, j: (i, j)),
        out_shape=jax.ShapeDtypeStruct((m, n), BF16),
        scratch_shapes=[pltpu.VMEM((tm, k), BF16)],
        compiler_params=_cparams("parallel", "arbitrary"),
        name="ffn_up",
    )(x, nw, wg, wu)


def _mix_kernel(ya_ref, yb_ref, ga_ref, gb_ref, wa_ref, wb_ref, o_ref):
    a = _dot(ya_ref[...], wa_ref[...])
    b = _dot(yb_ref[...], wb_ref[...])
    o_ref[...] = (_sigmoid(ga_ref[...]) * a + _sigmoid(gb_ref[...]) * b).astype(o_ref.dtype)


def _mix(ya, yb, proj, wa, wb, tm, tn):
    m, k = ya.shape
    n = wa.shape[1]
    nb = n // tn
    assert m % tm == 0 and n % tn == 0
    return pl.pallas_call(
        _mix_kernel,
        grid=(m // tm, nb),
        in_specs=[pl.BlockSpec((tm, k), lambda i, j: (i, 0)),
                  pl.BlockSpec((tm, k), lambda i, j: (i, 0)),
                  pl.BlockSpec((tm, tn), lambda i, j: (i, j)),
                  pl.BlockSpec((tm, tn), lambda i, j: (i, nb + j)),
                  pl.BlockSpec((k, tn), lambda i, j: (0, j)),
                  pl.BlockSpec((k, tn), lambda i, j: (0, j))],
        out_specs=pl.BlockSpec((tm, tn), lambda i, j: (i, j)),
        out_shape=jax.ShapeDtypeStruct((m, n), BF16),
        compiler_params=_cparams("parallel", "parallel"),
        name="mix",
    )(ya, yb, proj, proj, wa, wb)


def _split3(x):
    a = x.astype(BF16)
    r = x - a.astype(F32)
    b = r.astype(BF16)
    c = (r - b.astype(F32)).astype(BF16)
    return a, b, c


def _gdn_kernel(q_ref, k_ref, v_ref, z_ref, bd_ref, cwq_ref, cwk_ref, cwv_ref,
                alog_ref, dtb_ref, onw_ref, o_ref, s_scr, tail_scr, *, tb):
    hp = GDN_HEADS_PER_STEP
    C = GDN_CHUNK
    D = HEAD_DIM
    head0 = pl.program_id(1) * hp

    @pl.when(pl.program_id(2) == 0)
    def _():
        s_scr[...] = jnp.zeros_like(s_scr)
        tail_scr[...] = jnp.zeros_like(tail_scr)

    row8 = lax.broadcasted_iota(jnp.int32, (8, hp * D), 0)
    tails = []

    def conv_silu(u_ref, cw_ref, idx):
        u = u_ref[0]
        w = cw_ref[...]
        tail = tail_scr[idx]
        y = None
        ytop = None
        for s in (3, 2, 1):
            sh = pltpu.roll(u, s, axis=0)
            top = jnp.where(row8 < s, pltpu.roll(tail, s, axis=0), sh[0:8])
            wj = w[3 - s:4 - s]
            y = sh * wj if y is None else y + sh * wj
            ytop = top * wj if ytop is None else ytop + top * wj
        y = y + u * w[3:4]
        ytop = ytop + u[0:8] * w[3:4]
        tails.append(u[tb - 8:tb])
        y = jnp.concatenate([ytop, y[8:]], axis=0)
        return _silu(y)

    q_all = conv_silu(q_ref, cwq_ref, 0)
    k_all = conv_silu(k_ref, cwk_ref, 1)
    v_all = conv_silu(v_ref, cwv_ref, 2)

    bd = bd_ref[0]
    lane = lax.broadcasted_iota(jnp.int32, (tb, LANES), 1)
    beta_all = _sigmoid(bd)
    xg = bd + dtb_ref[...]
    softplus = jnp.maximum(xg, 0.0) + jnp.log1p(jnp.exp(-jnp.abs(xg)))
    g_all = -jnp.exp(alog_ref[...]) * softplus
    betas = [jnp.sum(jnp.where(lane == head0 + e, beta_all, 0.0), axis=-1, keepdims=True)
             for e in range(hp)]
    gs = [jnp.sum(jnp.where(lane == head0 + e + GDN_HEADS, g_all, 0.0), axis=-1, keepdims=True)
          for e in range(hp)]

    nc = tb // C
    ri = lax.broadcasted_iota(jnp.int32, (tb, tb), 0)
    ci = lax.broadcasted_iota(jnp.int32, (tb, tb), 1)
    same = (ri // C) == (ci // C)
    tril16 = jnp.where(same & (ri >= ci), 1.0, 0.0).astype(BF16)
    g_b = jnp.concatenate([jnp.broadcast_to(g, (tb, LANES)) for g in gs], axis=1)
    gc_all = sum(_dot(tril16, piece) for piece in _split3(g_b))

    iw = lax.broadcasted_iota(jnp.int32, (C, tb), 0)
    jw = lax.broadcasted_iota(jnp.int32, (C, tb), 1) % C
    incl_w = iw >= jw
    strict_w = iw > jw
    eye_w = jnp.where(iw == jw, 1.0, 0.0).astype(F32)
    low_half = (lax.broadcasted_iota(jnp.int32, (C, LANES), 1) < C)

    def block_diag(wide16):
        return jnp.where(same, jnp.concatenate([wide16] * nc, axis=0), jnp.zeros((), BF16))

    def pair_blocks(full):
        return jnp.where(low_half, full[:C], full[C:])

    E = range(hp)
    pairs = range(tb // LANES)

    def l2n(x):
        return x * lax.rsqrt(jnp.sum(x * x, axis=-1, keepdims=True) + RMS_EPS)

    qs = [l2n(q_all[:, e * D:(e + 1) * D]) * (D ** -0.5) for e in E]
    ks = [l2n(k_all[:, e * D:(e + 1) * D]) for e in E]
    vs = [v_all[:, e * D:(e + 1) * D] for e in E]
    gcs = [gc_all[:, e * D:(e + 1) * D] for e in E]
    g_col = [jnp.concatenate([pair_blocks(gc[p * LANES:(p + 1) * LANES]) for p in pairs], axis=1)
             for gc in gcs]
    g_row = [jnp.concatenate([gc[p * LANES:(p + 1) * LANES, :].T[0:1, :] for p in pairs], axis=1)
             for gc in gcs]
    decay = [jnp.where(incl_w, jnp.exp(jnp.where(incl_w, g_col[e] - g_row[e], 0.0)), 0.0) for e in E]
    k16 = [k.astype(BF16) for k in ks]
    q16 = [q.astype(BF16) for q in qs]
    kb = [ks[e] * betas[e] for e in E]
    kb16 = [x.astype(BF16) for x in kb]

    def pair_products(a16, b16):
        return jnp.concatenate([pair_blocks(_dot_nt(a16[p * LANES:(p + 1) * LANES],
                                                    b16[p * LANES:(p + 1) * LANES])) for p in pairs], axis=1)

    lmat = [jnp.where(strict_w, pair_products(kb16[e], k16[e]) * decay[e], 0.0) for e in E]
    amat = [jnp.where(incl_w, pair_products(q16[e], k16[e]) * decay[e], 0.0) for e in E]
    base = GDN_INV_BASE
    blk_i, blk_j = iw // base, jw // base
    neg_d16 = [jnp.where(blk_i == blk_j, -lmat[e], 0.0).astype(BF16) for e in E]
    tinv = [eye_w + neg_d16[e].astype(F32) for e in E]
    pw = [_dot(neg_d16[e], block_diag(neg_d16[e])) for e in E]
    span = 2
    while 2 * span < base:
        p16 = [pw[e].astype(BF16) for e in E]
        prod = [_dot(jnp.concatenate([p16[e], tinv[e].astype(BF16)], axis=0), block_diag(p16[e])) for e in E]
        pw = [prod[e][:C] for e in E]
        tinv = [tinv[e] + prod[e][C:] for e in E]
        span *= 2
    tinv = [tinv[e] + _dot(tinv[e].astype(BF16), block_diag(pw[e].astype(BF16))) for e in E]
    s = base
    while s < C:
        below_left = (iw // (2 * s) == jw // (2 * s)) & ((iw // s) % 2 == 1) & ((jw // s) % 2 == 0)
        c16 = [jnp.where(below_left, lmat[e], 0.0).astype(BF16) for e in E]
        cx = [_dot(c16[e], block_diag(tinv[e].astype(BF16))) for e in E]
        tinv = [tinv[e] - _dot(tinv[e].astype(BF16), block_diag(cx[e].astype(BF16))) for e in E]
        s *= 2
    eg = [jnp.exp(gc) for gc in gcs]
    rhs = [jnp.concatenate([kb[e] * eg[e], vs[e] * betas[e]], axis=1).astype(BF16) for e in E]
    wu = [_dot(block_diag(tinv[e].astype(BF16)), rhs[e]).astype(BF16) for e in E]
    au = [_dot(block_diag(amat[e].astype(BF16)), wu[e]) for e in E]
    q_eff = [(qs[e] * eg[e] - au[e][:, :D]).astype(BF16) for e in E]
    gl = [jnp.concatenate([jnp.broadcast_to(gc[c * C + C - 1:c * C + C, :], (C, LANES))
                           for c in range(nc)], axis=0) for gc in gcs]
    k_dec = [(ks[e] * jnp.exp(gl[e] - gcs[e])).astype(BF16) for e in E]
    pn = [[_dot_tn(k_dec[e][c * C:(c + 1) * C], wu[e][c * C:(c + 1) * C]) for c in range(nc)]
          for e in E]

    states = [s_scr[e] for e in E]
    onw = onw_ref[...]
    for c in range(nc):
        r0 = c * C
        lhs = [jnp.concatenate([pn[e][c][:, :D].astype(BF16), q_eff[e][r0:r0 + C]], axis=0) for e in E]
        res = [_dot(lhs[e], states[e].astype(BF16)) for e in E]
        outs = [_rms_rows(res[e][D:] + au[e][r0:r0 + C, D:], onw) for e in E]
        states = [states[e] * jnp.exp(gl[e][r0:r0 + 1, :]) - res[e][:D] + pn[e][c][:, D:] for e in E]
        zc = z_ref[0, r0:r0 + C, :]
        o_ref[0, r0:r0 + C, :] = (jnp.concatenate(outs, axis=1) * _silu(zc)).astype(o_ref.dtype)
    for e in range(hp):
        s_scr[e] = states[e]
    for idx in range(3):
        tail_scr[idx] = tails[idx]


def _gdn(proj3, conv_w, alog_row, dtb_row, onw, col0, bd_col, tb):
    b, t, _ = proj3.shape
    hp = GDN_HEADS_PER_STEP
    ng = GDN_HEADS // hp
    w = hp * HEAD_DIM

    def col(base):
        return pl.BlockSpec((1, tb, w), lambda bi, hi, ti: (bi, ti, base // hp + hi))

    def cw(base):
        return pl.BlockSpec((GDN_CONV, w), lambda bi, hi, ti: (0, base // hp + hi))

    assert col0 % hp == 0
    row = pl.BlockSpec((1, LANES), lambda bi, hi, ti: (0, 0))
    return pl.pallas_call(
        functools.partial(_gdn_kernel, tb=tb),
        grid=(b, ng, t // tb),
        in_specs=[col(col0), col(col0 + GDN_HEADS), col(col0 + 2 * GDN_HEADS), col(col0 + 3 * GDN_HEADS),
                  pl.BlockSpec((1, tb, LANES), lambda bi, hi, ti: (bi, ti, bd_col)),
                  cw(0), cw(GDN_HEADS), cw(2 * GDN_HEADS), row, row, row],
        out_specs=pl.BlockSpec((1, tb, w), lambda bi, hi, ti: (bi, ti, hi)),
        out_shape=jax.ShapeDtypeStruct((b, t, GDN_HEADS * HEAD_DIM), BF16),
        scratch_shapes=[pltpu.VMEM((hp, HEAD_DIM, HEAD_DIM), F32),
                        pltpu.VMEM((3, 8, w), F32)],
        compiler_params=_cparams("parallel", "parallel", "arbitrary"),
        name="gdn",
    )(proj3, proj3, proj3, proj3, proj3, conv_w, conv_w, conv_w, alog_row, dtb_row, onw)


def _bias_kernel(rel_ref, o_ref):
    h = pl.program_id(0)
    bs = MOBA_BLOCK
    max_exact = REL_BUCKETS // 2
    x = lax.broadcasted_iota(jnp.int32, (8, 2 * bs), 1)
    for d in range(BIAS_TILES):
        dist = jnp.maximum(d * bs + x - bs, 0)
        df = dist.astype(F32)
        log_ratio = jnp.log(jnp.maximum(df, float(max_exact)) / max_exact) / math.log(REL_MAX_DIST / max_exact)
        large = max_exact + (log_ratio * (REL_BUCKETS - max_exact)).astype(jnp.int32)
        large = jnp.minimum(large, REL_BUCKETS - 1)
        bucket = jnp.where(dist < max_exact, dist, large)
        row = jnp.zeros((8, 2 * bs), F32)
        for b in range(REL_BUCKETS):
            row = jnp.where(bucket == b, rel_ref[b, h], row)
        base = jnp.broadcast_to(row[0:1, :] * LOG2E, (bs, 2 * bs))
        tile = pltpu.roll(base, 0, 1, stride=1, stride_axis=0)[:, bs:]
        if d == 0:
            kk = lax.broadcasted_iota(jnp.int32, (bs, bs), 0)
            qq = lax.broadcasted_iota(jnp.int32, (bs, bs), 1)
            tile = jnp.where(kk <= qq, tile, NEG_INF)
        o_ref[0, d] = tile


def _bias_tiles(rel_bias):
    bs = MOBA_BLOCK
    max_exact = REL_BUCKETS // 2
    nearest = (BIAS_TILES - 1) * bs - (bs - 1)
    assert max_exact + int(math.log(nearest / max_exact) / math.log(REL_MAX_DIST / max_exact)
                           * (REL_BUCKETS - max_exact)) >= REL_BUCKETS - 1
    return pl.pallas_call(
        _bias_kernel,
        grid=(MOBA_HEADS,),
        in_specs=[pl.BlockSpec(memory_space=pltpu.SMEM)],
        out_specs=pl.BlockSpec((1, BIAS_TILES, bs, bs), lambda h: (h, 0, 0, 0)),
        out_shape=jax.ShapeDtypeStruct((MOBA_HEADS, BIAS_TILES, bs, bs), F32),
        compiler_params=_cparams("parallel"),
        name="bias",
    )(rel_bias)


def _moba_prep_kernel(q_ref, k_ref, v_ref, qw_ref, kw_ref, rel_ref, q16_ref, kn_ref, vt_ref, add_ref, km_scr,
                      *, rows, nb):
    bs = MOBA_BLOCK
    nsub = rows // bs
    ti = pl.program_id(2)

    @pl.when(ti == 0)
    def _():
        km_scr[...] = jnp.zeros_like(km_scr)

    kn = _rms_rows(k_ref[0], kw_ref[...])
    kn_ref[0, 0] = kn.astype(BF16)
    for s in range(nsub):
        km_scr[pl.ds(ti * nsub + s, 1), :] = jnp.mean(kn[s * bs:(s + 1) * bs], axis=0, keepdims=True)
        vt_ref[0, 0, 0:HEAD_DIM, s * bs:(s + 1) * bs] = v_ref[0, s * bs:(s + 1) * bs, :].T.astype(BF16)
    pad_row = lax.broadcasted_iota(jnp.int32, (MOBA_V_ROWS - HEAD_DIM, rows), 0)
    vt_ref[0, 0, HEAD_DIM:, :] = jnp.where(pad_row == 0, 1.0, 0.0).astype(BF16)

    q = _rms_rows(q_ref[0], qw_ref[...])
    q16_ref[0, 0] = (q * ((HEAD_DIM ** -0.5) * LOG2E)).astype(BF16)

    blk = lax.broadcasted_iota(jnp.int32, (nb, rows), 0)
    own_blk = ti * nsub + lax.broadcasted_iota(jnp.int32, (nb, rows), 1) // bs
    blkf = blk.astype(F32)
    valid = blk < own_blk
    work = jnp.where(valid, _dot_nt(km_scr[...], q, HIGHEST), NEG_INF)
    picked = jnp.zeros((nb, rows), F32)
    for _ in range(MOBA_TOPK):
        best = jnp.max(work, axis=0, keepdims=True)
        first = jnp.min(jnp.where(work == best, blkf, float(nb)), axis=0, keepdims=True)
        hit = blkf == first
        picked = jnp.where(hit, 1.0, picked)
        work = jnp.where(hit, -jnp.inf, work)
    far_bias = rel_ref[REL_BUCKETS - 1, pl.program_id(1)] * LOG2E
    attended = (valid & (picked > 0.0)) | (blk == own_blk)
    add_ref[0, 0] = jnp.where(attended, far_bias, NEG_INF)


def _moba_prep(proj3, qw, kw, rel_bias, col0, rows):
    b, t, _ = proj3.shape
    hh, bs, dh = MOBA_HEADS, MOBA_BLOCK, HEAD_DIM
    nb = t // bs

    def col(base):
        return pl.BlockSpec((1, rows, LANES), lambda bi, hi, ti: (bi, ti, base + hi))

    row = pl.BlockSpec((1, LANES), lambda bi, hi, ti: (0, 0))
    return pl.pallas_call(
        functools.partial(_moba_prep_kernel, rows=rows, nb=nb),
        grid=(b, hh, t // rows),
        in_specs=[col(col0), col(col0 + hh), col(col0 + 2 * hh), row, row,
                  pl.BlockSpec(memory_space=pltpu.SMEM)],
        out_specs=[pl.BlockSpec((1, 1, rows, dh), lambda bi, hi, ti: (bi, hi, ti, 0)),
                   pl.BlockSpec((1, 1, rows, dh), lambda bi, hi, ti: (bi, hi, ti, 0)),
                   pl.BlockSpec((1, 1, MOBA_V_ROWS, rows), lambda bi, hi, ti: (bi, hi, 0, ti)),
                   pl.BlockSpec((1, 1, nb, rows), lambda bi, hi, ti: (bi, hi, 0, ti))],
        out_shape=[jax.ShapeDtypeStruct((b, hh, t, dh), BF16),
                   jax.ShapeDtypeStruct((b, hh, t, dh), BF16),
                   jax.ShapeDtypeStruct((b, hh, MOBA_V_ROWS, t), BF16),
                   jax.ShapeDtypeStruct((b, hh, nb, t), F32)],
        scratch_shapes=[pltpu.VMEM((nb, dh), F32)],
        compiler_params=_cparams("parallel", "parallel", "arbitrary"),
        name="moba_prep",
    )(proj3, proj3, proj3, qw, kw, rel_bias)


def _moba_kernel(q_ref, kn_ref, vt_ref, add_ref, bias_ref, o_ref, buf_a, buf_b, *, nb):
    i = pl.program_id(2)
    bs = MOBA_BLOCK
    ch = MOBA_TILES_PER_STEP
    E = range(MOBA_STREAMS)
    q16 = [q_ref[e, 0] for e in E]

    crows = ch * bs
    n_pairs = (i + 2 * ch) // (2 * ch)
    last_chunk = nb // ch - 1

    def chunk_scores(buf, e, c, far):
        r = pl.multiple_of(c * crows, crows)
        raw = _dot_nt(kn_ref[e, 0, pl.ds(r, crows), :], q16[e])
        mx = None
        for t in range(ch):
            j = c * ch + t
            add = add_ref[e, 0, pl.ds(j, 1), :]
            if far:
                st = raw[t * bs:(t + 1) * bs, :] + add
            else:
                d = jnp.clip(i - j, 0, BIAS_TILES - 1)
                st = jnp.where(add > 0.5 * NEG_INF, raw[t * bs:(t + 1) * bs, :] + bias_ref[0, d], NEG_INF)
            buf[e, t * bs:(t + 1) * bs, :] = st
            tmx = jnp.max(st, axis=0, keepdims=True)
            mx = tmx if mx is None else jnp.maximum(mx, tmx)
        return mx

    def absorb(buf, e, c, mx, m, acc):
        r = pl.multiple_of(c * crows, crows)
        m_new = jnp.maximum(m, mx)
        p = jnp.exp2(buf[e] - m_new)
        acc = jnp.exp2(m - m_new) * acc + _dot(vt_ref[e, 0, :, pl.ds(r, crows)], p.astype(BF16))
        return m_new, acc

    def pair(far, c, carry):
        m, acc, mx_a = (list(x) for x in carry)
        mx_b = [chunk_scores(buf_b, e, 2 * c + 1, far) for e in E]
        for e in E:
            m[e], acc[e] = absorb(buf_a, e, 2 * c, mx_a[e], m[e], acc[e])
        mx_a = [chunk_scores(buf_a, e, jnp.minimum(2 * c + 2, last_chunk), far) for e in E]
        for e in E:
            m[e], acc[e] = absorb(buf_b, e, 2 * c + 1, mx_b[e], m[e], acc[e])
        return tuple(m), tuple(acc), tuple(mx_a)

    def step(c, carry):
        all_far = (2 * c + 3) * ch - 1 + (BIAS_TILES - 1) <= i
        return lax.cond(all_far, functools.partial(pair, True, c), functools.partial(pair, False, c), carry)

    m0 = tuple(jnp.full((1, bs), MOBA_M_INIT, F32) for e in E)
    acc0 = tuple(jnp.zeros((MOBA_V_ROWS, bs), F32) for e in E)
    mx_a = tuple(chunk_scores(buf_a, e, 0, False) for e in E)
    m, acc, _ = lax.fori_loop(0, n_pairs, step, (m0, acc0, mx_a))
    for e in E:
        o_ref[e] = (acc[e][:HEAD_DIM] / acc[e][HEAD_DIM:HEAD_DIM + 1]).T.astype(o_ref.dtype)


def _moba(q16, kn, vt, add, bias):
    b, hh, t, dh = kn.shape
    bs = MOBA_BLOCK
    nb = t // bs
    ns = MOBA_STREAMS
    crows = MOBA_TILES_PER_STEP * bs
    assert b % ns == 0 and nb % (2 * MOBA_TILES_PER_STEP) == 0
    return pl.pallas_call(
        functools.partial(_moba_kernel, nb=nb),
        grid=(b // ns, hh, nb),
        in_specs=[pl.BlockSpec((ns, 1, bs, dh), lambda bi, hi, ti: (bi, hi, ti, 0)),
                  pl.BlockSpec((ns, 1, t, dh), lambda bi, hi, ti: (bi, hi, 0, 0)),
                  pl.BlockSpec((ns, 1, MOBA_V_ROWS, t), lambda bi, hi, ti: (bi, hi, 0, 0)),
                  pl.BlockSpec((ns, 1, nb, bs), lambda bi, hi, ti: (bi, hi, 0, ti)),
                  pl.BlockSpec((1, BIAS_TILES, bs, bs), lambda bi, hi, ti: (hi, 0, 0, 0))],
        out_specs=pl.BlockSpec((ns, bs, dh), lambda bi, hi, ti: (bi, ti, hi)),
        out_shape=jax.ShapeDtypeStruct((b, t, hh * dh), BF16),
        scratch_shapes=[pltpu.VMEM((ns, crows, bs), F32),
                        pltpu.VMEM((ns, crows, bs), F32)],
        compiler_params=_cparams("parallel", "parallel", "parallel"),
        name="moba",
    )(q16, kn, vt, add, bias)


def _tile(n, want):
    t = min(n, want)
    assert n % t == 0, (n, want)
    return t


def kernel(x, norm_mix_w, w_in, conv_w, a_log, dt_bias, gdn_o_norm_w, q_norm_w, k_norm_w, rel_bias,
           w_branch_gdn, w_branch_moba, w_out, norm_ffn_w, w_ffn_gate, w_ffn_up, w_ffn_down):
    b, t, d = x.shape
    m = b * t
    gw = GDN_HEADS * HEAD_DIM
    mw = MOBA_HEADS * HEAD_DIM
    assert t % MOBA_PREP_ROWS == 0 and w_in.shape[0] == 1
    assert w_in.shape[2] == 4 * gw + 2 * GDN_HEADS + 3 * mw + 2 * d

    o_bd = 4 * gw
    o_moba = o_bd + 2 * GDN_HEADS
    o_gate = o_moba + 3 * mw
    n_main = 2 * d + 4 * gw + 3 * mw
    tn_proj = PROJ_TILE[1]
    n_proj = -(-(n_main + LANES) // tn_proj) * tn_proj
    w_proj = _regroup_weights(w_in, (o_bd, o_moba, o_gate), n_proj)
    col_gdn = (2 * d) // LANES
    col_moba = col_gdn + 4 * GDN_HEADS
    col_bd = col_moba + 3 * MOBA_HEADS

    x2 = x.reshape(m, d)
    proj = _norm_matmul(x2, norm_mix_w, w_proj, _tile(m, PROJ_TILE[0]), tn_proj, F32, "proj")
    proj3 = proj.reshape(b, t, n_proj)

    lane_pad = jnp.zeros((1, LANES - 2 * GDN_HEADS), F32)
    head_pad = jnp.zeros((1, GDN_HEADS), F32)
    alog_row = jnp.concatenate([head_pad, a_log, lane_pad], axis=1)
    dtb_row = jnp.concatenate([head_pad, dt_bias, lane_pad], axis=1)
    y_a = _gdn(proj3, conv_w[0], alog_row, dtb_row, gdn_o_norm_w, col_gdn, col_bd, _tile(t, GDN_BLOCK))

    bias = _bias_tiles(rel_bias)
    q16, kn, vt, add = _moba_prep(proj3, q_norm_w, k_norm_w, rel_bias, col_moba, MOBA_PREP_ROWS)
    y_b = _moba(q16, kn, vt, add, bias)

    mix = _mix(y_a.reshape(m, gw), y_b.reshape(m, mw), proj,
               w_branch_gdn[0].astype(BF16), w_branch_moba[0].astype(BF16), _tile(m, MIX_ROWS), d)
    h1 = _matmul_res(mix, w_out[0].astype(BF16), x2, _tile(m, OUT_ROWS), d, "out")

    hid = _ffn_up(h1, norm_ffn_w, w_ffn_gate[0].astype(BF16), w_ffn_up[0].astype(BF16),
                  _tile(m, FFN_UP_TILE[0]), FFN_UP_TILE[1])
    h2 = _matmul_res(hid, w_ffn_down[0].astype(BF16), h1, _tile(m, FFN_DOWN_TILE[0]), FFN_DOWN_TILE[1],
                     "ffn_down")
    return h2.reshape(b, t, d)
```

```python
import functools
import math

import jax
import jax.numpy as jnp
from jax import lax
from jax.experimental import pallas as pl
from jax.experimental.pallas import tpu as pltpu

F32 = jnp.float32
BF16 = jnp.bfloat16
HIGHEST = lax.Precision.HIGHEST

LANES = 128
HEAD_DIM = 128
GDN_HEADS = 8
GDN_CONV = 4
GDN_CHUNK = 64
GDN_INV_BASE = 8
GDN_HEADS_PER_STEP = 8
NEG_INF = -1e30
MOBA_HEADS = 8
MOBA_BLOCK = 256
MOBA_TOPK = 3
REL_BUCKETS = 32
REL_MAX_DIST = 2048
BIAS_TILES = 8
MOBA_TILES_PER_STEP = 4
MOBA_PREP_ROWS = 2048
MOBA_STREAMS = 4
MOBA_V_ROWS = HEAD_DIM + 16
MOBA_M_INIT = 0.1 * NEG_INF
LOG2E = math.log2(math.e)
RMS_EPS = 1e-6
V7X_VMEM_BYTES = 64 * 1024 * 1024
VMEM_LIMIT = V7X_VMEM_BYTES * 7 // 8

PROJ_TILE = (1024, 1280)
MIX_ROWS = 512
OUT_ROWS = 512
FFN_UP_TILE = (1024, 512)
FFN_DOWN_TILE = (1024, 512)
GDN_BLOCK = 256
REGROUP_ROWS = 128


def _cparams(*sem):
    return pltpu.CompilerParams(dimension_semantics=sem, vmem_limit_bytes=VMEM_LIMIT)


def _sigmoid(x):
    return 0.5 + 0.5 * jnp.tanh(0.5 * x)


def _silu(x):
    h = 0.5 * x
    return h + h * jnp.tanh(h)


def _dot(a, b, precision=None):
    return jnp.dot(a, b, preferred_element_type=F32, precision=precision)


def _dot_nt(a, b, precision=None):
    return lax.dot_general(a, b, (((1,), (1,)), ((), ())),
                           preferred_element_type=F32, precision=precision)


def _dot_tn(a, b):
    return lax.dot_general(a, b, (((0,), (0,)), ((), ())), preferred_element_type=F32)


def _rms_rows(x, w):
    return x * lax.rsqrt(jnp.mean(x * x, axis=-1, keepdims=True) + RMS_EPS) * w


def _regroup_kernel(w_ref, o_ref, *, cuts):
    o_bd, o_moba, o_gate = cuts
    x = w_ref[...]
    parts = [x[:, o_gate:], x[:, :o_bd], x[:, o_moba:o_gate], x[:, o_bd:o_moba]]
    used = sum(p.shape[1] for p in parts)
    parts.append(jnp.zeros((x.shape[0], o_ref.shape[1] - used), x.dtype))
    o_ref[...] = jnp.concatenate(parts, axis=1).astype(o_ref.dtype)


def _regroup_weights(w, cuts, n_out):
    _, k, n_in = w.shape
    rows = _tile(k, REGROUP_ROWS)
    return pl.pallas_call(
        functools.partial(_regroup_kernel, cuts=cuts),
        grid=(k // rows,),
        in_specs=[pl.BlockSpec((None, rows, n_in), lambda i: (0, i, 0))],
        out_specs=pl.BlockSpec((rows, n_out), lambda i: (i, 0)),
        out_shape=jax.ShapeDtypeStruct((k, n_out), BF16),
        compiler_params=_cparams("parallel"),
        name="regroup",
    )(w)


def _norm_matmul_kernel(x_ref, nw_ref, w_ref, o_ref, u_scr):
    @pl.when(pl.program_id(1) == 0)
    def _():
        u_scr[...] = _rms_rows(x_ref[...], nw_ref[...]).astype(BF16)

    o_ref[...] = _dot(u_scr[...], w_ref[...]).astype(o_ref.dtype)


def _norm_matmul(x, nw, w, tm, tn, out_dtype, name):
    m, k = x.shape
    n = w.shape[1]
    assert m % tm == 0 and n % tn == 0
    return pl.pallas_call(
        _norm_matmul_kernel,
        grid=(m // tm, n // tn),
        in_specs=[pl.BlockSpec((tm, k), lambda i, j: (i, 0)),
                  pl.BlockSpec((1, k), lambda i, j: (0, 0)),
                  pl.BlockSpec((k, tn), lambda i, j: (0, j))],
        out_specs=pl.BlockSpec((tm, tn), lambda i, j: (i, j)),
        out_shape=jax.ShapeDtypeStruct((m, n), out_dtype),
        scratch_shapes=[pltpu.VMEM((tm, k), BF16)],
        compiler_params=_cparams("parallel", "arbitrary"),
        name=name,
    )(x, nw, w)


def _matmul_res_kernel(a_ref, w_ref, r_ref, o_ref):
    o_ref[...] = r_ref[...] + _dot(a_ref[...], w_ref[...])


def _matmul_res(a, w, res, tm, tn, name):
    m, k = a.shape
    n = w.shape[1]
    assert m % tm == 0 and n % tn == 0
    return pl.pallas_call(
        _matmul_res_kernel,
        grid=(m // tm, n // tn),
        in_specs=[pl.BlockSpec((tm, k), lambda i, j: (i, 0)),
                  pl.BlockSpec((k, tn), lambda i, j: (0, j)),
                  pl.BlockSpec((tm, tn), lambda i, j: (i, j))],
        out_specs=pl.BlockSpec((tm, tn), lambda i, j: (i, j)),
        out_shape=jax.ShapeDtypeStruct((m, n), F32),
        compiler_params=_cparams("parallel", "parallel"),
        name=name,
    )(a, w, res)


def _ffn_up_kernel(x_ref, nw_ref, wg_ref, wu_ref, o_ref, u_scr):
    @pl.when(pl.program_id(1) == 0)
    def _():
        u_scr[...] = _rms_rows(x_ref[...], nw_ref[...]).astype(BF16)

    u = u_scr[...]
    g = _dot(u, wg_ref[...])
    o_ref[...] = (_silu(g) * _dot(u, wu_ref[...])).astype(o_ref.dtype)


def _ffn_up(x, nw, wg, wu, tm, tn):
    m, k = x.shape
    n = wg.shape[1]
    assert m % tm == 0 and n % tn == 0
    return pl.pallas_call(
        _ffn_up_kernel,
        grid=(m // tm, n // tn),
        in_specs=[pl.BlockSpec((tm, k), lambda i, j: (i, 0)),
                  pl.BlockSpec((1, k), lambda i, j: (0, 0)),
                  pl.BlockSpec((k, tn), lambda i, j: (0, j)),
                  pl.BlockSpec((k, tn), lambda i, j: (0, j))],
        out_specs=pl.BlockSpec((tm, tn), lambda i, j: (i, j)),
        out_shape=jax.ShapeDtypeStruct((m, n), BF16),
        scratch_shapes=[pltpu.VMEM((tm, k), BF16)],
        compiler_params=_cparams("parallel", "arbitrary"),
        name="ffn_up",
    )(x, nw, wg, wu)


def _mix_kernel(ya_ref, yb_ref, ga_ref, gb_ref, wa_ref, wb_ref, o_ref):
    a = _dot(ya_ref[...], wa_ref[...])
    b = _dot(yb_ref[...], wb_ref[...])
    o_ref[...] = (_sigmoid(ga_ref[...]) * a + _sigmoid(gb_ref[...]) * b).astype(o_ref.dtype)


def _mix(ya, yb, proj, wa, wb, tm, tn):
    m, k = ya.shape
    n = wa.shape[1]
    nb = n // tn
    assert m % tm == 0 and n % tn == 0
    return pl.pallas_call(
        _mix_kernel,
        grid=(m // tm, nb),
        in_specs=[pl.BlockSpec((tm, k), lambda i, j: (i, 0)),
                  pl.BlockSpec((tm, k), lambda i, j: (i, 0)),
                  pl.BlockSpec((tm, tn), lambda i, j: (i, j)),
                  pl.BlockSpec((tm, tn), lambda i, j: (i, nb + j)),
                  pl.BlockSpec((k, tn), lambda i, j: (0, j)),
                  pl.BlockSpec((k, tn), lambda i, j: (0, j))],
        out_specs=pl.BlockSpec((tm, tn), lambda i, j: (i, j)),
        out_shape=jax.ShapeDtypeStruct((m, n), BF16),
        compiler_params=_cparams("parallel", "parallel"),
        name="mix",
    )(ya, yb, proj, proj, wa, wb)


def _split3(x):
    a = x.astype(BF16)
    r = x - a.astype(F32)
    b = r.astype(BF16)
    c = (r - b.astype(F32)).astype(BF16)
    return a, b, c


def _gdn_kernel(q_ref, k_ref, v_ref, z_ref, bd_ref, cwq_ref, cwk_ref, cwv_ref,
                alog_ref, dtb_ref, onw_ref, o_ref, s_scr, tail_scr, *, tb):
    hp = GDN_HEADS_PER_STEP
    C = GDN_CHUNK
    D = HEAD_DIM
    head0 = pl.program_id(1) * hp

    @pl.when(pl.program_id(2) == 0)
    def _():
        s_scr[...] = jnp.zeros_like(s_scr)
        tail_scr[...] = jnp.zeros_like(tail_scr)

    row8 = lax.broadcasted_iota(jnp.int32, (8, hp * D), 0)
    tails = []

    def conv_silu(u_ref, cw_ref, idx):
        u = u_ref[0]
        w = cw_ref[...]
        tail = tail_scr[idx]
        y = None
        ytop = None
        for s in (3, 2, 1):
            sh = pltpu.roll(u, s, axis=0)
            top = jnp.where(row8 < s, pltpu.roll(tail, s, axis=0), sh[0:8])
            wj = w[3 - s:4 - s]
            y = sh * wj if y is None else y + sh * wj
            ytop = top * wj if ytop is None else ytop + top * wj
        y = y + u * w[3:4]
        ytop = ytop + u[0:8] * w[3:4]
        tails.append(u[tb - 8:tb])
        y = jnp.concatenate([ytop, y[8:]], axis=0)
        return _silu(y)

    q_all = conv_silu(q_ref, cwq_ref, 0)
    k_all = conv_silu(k_ref, cwk_ref, 1)
    v_all = conv_silu(v_ref, cwv_ref, 2)

    bd = bd_ref[0]
    lane = lax.broadcasted_iota(jnp.int32, (tb, LANES), 1)
    beta_all = _sigmoid(bd)
    xg = bd + dtb_ref[...]
    softplus = jnp.maximum(xg, 0.0) + jnp.log1p(jnp.exp(-jnp.abs(xg)))
    g_all = -jnp.exp(alog_ref[...]) * softplus
    betas = [jnp.sum(jnp.where(lane == head0 + e, beta_all, 0.0), axis=-1, keepdims=True)
             for e in range(hp)]
    gs = [jnp.sum(jnp.where(lane == head0 + e + GDN_HEADS, g_all, 0.0), axis=-1, keepdims=True)
          for e in range(hp)]

    nc = tb // C
    ri = lax.broadcasted_iota(jnp.int32, (tb, tb), 0)
    ci = lax.broadcasted_iota(jnp.int32, (tb, tb), 1)
    same = (ri // C) == (ci // C)
    tril16 = jnp.where(same & (ri >= ci), 1.0, 0.0).astype(BF16)
    g_b = jnp.concatenate([jnp.broadcast_to(g, (tb, LANES)) for g in gs], axis=1)
    gc_all = sum(_dot(tril16, piece) for piece in _split3(g_b))

    iw = lax.broadcasted_iota(jnp.int32, (C, tb), 0)
    jw = lax.broadcasted_iota(jnp.int32, (C, tb), 1) % C
    incl_w = iw >= jw
    strict_w = iw > jw
    eye_w = jnp.where(iw == jw, 1.0, 0.0).astype(F32)
    low_half = (lax.broadcasted_iota(jnp.int32, (C, LANES), 1) < C)

    def block_diag(wide16):
        return jnp.where(same, jnp.concatenate([wide16] * nc, axis=0), jnp.zeros((), BF16))

    def pair_blocks(full):
        return jnp.where(low_half, full[:C], full[C:])

    E = range(hp)
    pairs = range(tb // LANES)

    def l2n(x):
        return x * lax.rsqrt(jnp.sum(x * x, axis=-1, keepdims=True) + RMS_EPS)

    qs = [l2n(q_all[:, e * D:(e + 1) * D]) * (D ** -0.5) for e in E]
    ks = [l2n(k_all[:, e * D:(e + 1) * D]) for e in E]
    vs = [v_all[:, e * D:(e + 1) * D] for e in E]
    gcs = [gc_all[:, e * D:(e + 1) * D] for e in E]
    g_col = [jnp.concatenate([pair_blocks(gc[p * LANES:(p + 1) * LANES]) for p in pairs], axis=1)
             for gc in gcs]
    g_row = [jnp.concatenate([gc[p * LANES:(p + 1) * LANES, :].T[0:1, :] for p in pairs], axis=1)
             for gc in gcs]
    decay = [jnp.where(incl_w, jnp.exp(jnp.where(incl_w, g_col[e] - g_row[e], 0.0)), 0.0) for e in E]
    k16 = [k.astype(BF16) for k in ks]
    q16 = [q.astype(BF16) for q in qs]
    kb = [ks[e] * betas[e] for e in E]
    kb16 = [x.astype(BF16) for x in kb]

    def pair_products(a16, b16):
        return jnp.concatenate([pair_blocks(_dot_nt(a16[p * LANES:(p + 1) * LANES],
                                                    b16[p * LANES:(p + 1) * LANES])) for p in pairs], axis=1)

    lmat = [jnp.where(strict_w, pair_products(kb16[e], k16[e]) * decay[e], 0.0) for e in E]
    amat = [jnp.where(incl_w, pair_products(q16[e], k16[e]) * decay[e], 0.0) for e in E]
    base = GDN_INV_BASE
    blk_i, blk_j = iw // base, jw // base
    neg_d16 = [jnp.where(blk_i == blk_j, -lmat[e], 0.0).astype(BF16) for e in E]
    tinv = [eye_w + neg_d16[e].astype(F32) for e in E]
    pw = [_dot(neg_d16[e], block_diag(neg_d16[e])) for e in E]
    span = 2
    while 2 * span < base:
        p16 = [pw[e].astype(BF16) for e in E]
        prod = [_dot(jnp.concatenate([p16[e], tinv[e].astype(BF16)], axis=0), block_diag(p16[e])) for e in E]
        pw = [prod[e][:C] for e in E]
        tinv = [tinv[e] + prod[e][C:] for e in E]
        span *= 2
    tinv = [tinv[e] + _dot(tinv[e].astype(BF16), block_diag(pw[e].astype(BF16))) for e in E]
    s = base
    while s < C:
        below_left = (iw // (2 * s) == jw // (2 * s)) & ((iw // s) % 2 == 1) & ((jw // s) % 2 == 0)
        c16 = [jnp.where(below_left, lmat[e], 0.0).astype(BF16) for e in E]
        cx = [_dot(c16[e], block_diag(tinv[e].astype(BF16))) for e in E]
        tinv = [tinv[e] - _dot(tinv[e].astype(BF16), block_diag(cx[e].astype(BF16))) for e in E]
        s *= 2
    eg = [jnp.exp(gc) for gc in gcs]
    rhs = [jnp.concatenate([kb[e] * eg[e], vs[e] * betas[e]], axis=1).astype(BF16) for e in E]
    wu = [_dot(block_diag(tinv[e].astype(BF16)), rhs[e]).astype(BF16) for e in E]
    au = [_dot(block_diag(amat[e].astype(BF16)), wu[e]) for e in E]
    q_eff = [(qs[e] * eg[e] - au[e][:, :D]).astype(BF16) for e in E]
    gl = [jnp.concatenate([jnp.broadcast_to(gc[c * C + C - 1:c * C + C, :], (C, LANES))
                           for c in range(nc)], axis=0) for gc in gcs]
    k_dec = [(ks[e] * jnp.exp(gl[e] - gcs[e])).astype(BF16) for e in E]
    pn = [[_dot_tn(k_dec[e][c * C:(c + 1) * C], wu[e][c * C:(c + 1) * C]) for c in range(nc)]
          for e in E]

    states = [s_scr[e] for e in E]
    onw = onw_ref[...]
    for c in range(nc):
        r0 = c * C
        lhs = [jnp.concatenate([pn[e][c][:, :D].astype(BF16), q_eff[e][r0:r0 + C]], axis=0) for e in E]
        res = [_dot(lhs[e], states[e].astype(BF16)) for e in E]
        outs = [_rms_rows(res[e][D:] + au[e][r0:r0 + C, D:], onw) for e in E]
        states = [states[e] * jnp.exp(gl[e][r0:r0 + 1, :]) - res[e][:D] + pn[e][c][:, D:] for e in E]
        zc = z_ref[0, r0:r0 + C, :]
        o_ref[0, r0:r0 + C, :] = (jnp.concatenate(outs, axis=1) * _silu(zc)).astype(o_ref.dtype)
    for e in range(hp):
        s_scr[e] = states[e]
    for idx in range(3):
        tail_scr[idx] = tails[idx]


def _gdn(proj3, conv_w, alog_row, dtb_row, onw, col0, bd_col, tb):
    b, t, _ = proj3.shape
    hp = GDN_HEADS_PER_STEP
    ng = GDN_HEADS // hp
    w = hp * HEAD_DIM

    def col(base):
        return pl.BlockSpec((1, tb, w), lambda bi, hi, ti: (bi, ti, base // hp + hi))

    def cw(base):
        return pl.BlockSpec((GDN_CONV, w), lambda bi, hi, ti: (0, base // hp + hi))

    assert col0 % hp == 0
    row = pl.BlockSpec((1, LANES), lambda bi, hi, ti: (0, 0))
    return pl.pallas_call(
        functools.partial(_gdn_kernel, tb=tb),
        grid=(b, ng, t // tb),
        in_specs=[col(col0), col(col0 + GDN_HEADS), col(col0 + 2 * GDN_HEADS), col(col0 + 3 * GDN_HEADS),
                  pl.BlockSpec((1, tb, LANES), lambda bi, hi, ti: (bi, ti, bd_col)),
                  cw(0), cw(GDN_HEADS), cw(2 * GDN_HEADS), row, row, row],
        out_specs=pl.BlockSpec((1, tb, w), lambda bi, hi, ti: (bi, ti, hi)),
        out_shape=jax.ShapeDtypeStruct((b, t, GDN_HEADS * HEAD_DIM), BF16),
        scratch_shapes=[pltpu.VMEM((hp, HEAD_DIM, HEAD_DIM), F32),
                        pltpu.VMEM((3, 8, w), F32)],
        compiler_params=_cparams("parallel", "parallel", "arbitrary"),
        name="gdn",
    )(proj3, proj3, proj3, proj3, proj3, conv_w, conv_w, conv_w, alog_row, dtb_row, onw)


def _bias_kernel(rel_ref, o_ref):
    h = pl.program_id(0)
    bs = MOBA_BLOCK
    max_exact = REL_BUCKETS // 2
    x = lax.broadcasted_iota(jnp.int32, (8, 2 * bs), 1)
    for d in range(BIAS_TILES):
        dist = jnp.maximum(d * bs + x - bs, 0)
        df = dist.astype(F32)
        log_ratio = jnp.log(jnp.maximum(df, float(max_exact)) / max_exact) / math.log(REL_MAX_DIST / max_exact)
        large = max_exact + (log_ratio * (REL_BUCKETS - max_exact)).astype(jnp.int32)
        large = jnp.minimum(large, REL_BUCKETS - 1)
        bucket = jnp.where(dist < max_exact, dist, large)
        row = jnp.zeros((8, 2 * bs), F32)
        for b in range(REL_BUCKETS):
            row = jnp.where(bucket == b, rel_ref[b, h], row)
        base = jnp.broadcast_to(row[0:1, :] * LOG2E, (bs, 2 * bs))
        tile = pltpu.roll(base, 0, 1, stride=1, stride_axis=0)[:, bs:]
        if d == 0:
            kk = lax.broadcasted_iota(jnp.int32, (bs, bs), 0)
            qq = lax.broadcasted_iota(jnp.int32, (bs, bs), 1)
            tile = jnp.where(kk <= qq, tile, NEG_INF)
        o_ref[0, d] = tile


def _bias_tiles(rel_bias):
    bs = MOBA_BLOCK
    max_exact = REL_BUCKETS // 2
    nearest = (BIAS_TILES - 1) * bs - (bs - 1)
    assert max_exact + int(math.log(nearest / max_exact) / math.log(REL_MAX_DIST / max_exact)
                           * (REL_BUCKETS - max_exact)) >= REL_BUCKETS - 1
    return pl.pallas_call(
        _bias_kernel,
        grid=(MOBA_HEADS,),
        in_specs=[pl.BlockSpec(memory_space=pltpu.SMEM)],
        out_specs=pl.BlockSpec((1, BIAS_TILES, bs, bs), lambda h: (h, 0, 0, 0)),
        out_shape=jax.ShapeDtypeStruct((MOBA_HEADS, BIAS_TILES, bs, bs), F32),
        compiler_params=_cparams("parallel"),
        name="bias",
    )(rel_bias)


def _moba_prep_kernel(q_ref, k_ref, v_ref, qw_ref, kw_ref, rel_ref, q16_ref, kn_ref, vt_ref, add_ref, km_scr,
                      *, rows, nb):
    bs = MOBA_BLOCK
    nsub = rows // bs
    ti = pl.program_id(2)

    @pl.when(ti == 0)
    def _():
        km_scr[...] = jnp.zeros_like(km_scr)

    kn = _rms_rows(k_ref[0], kw_ref[...])
    kn_ref[0, 0] = kn.astype(BF16)
    for s in range(nsub):
        km_scr[pl.ds(ti * nsub + s, 1), :] = jnp.mean(kn[s * bs:(s + 1) * bs], axis=0, keepdims=True)
        vt_ref[0, 0, 0:HEAD_DIM, s * bs:(s + 1) * bs] = v_ref[0, s * bs:(s + 1) * bs, :].T.astype(BF16)
    pad_row = lax.broadcasted_iota(jnp.int32, (MOBA_V_ROWS - HEAD_DIM, rows), 0)
    vt_ref[0, 0, HEAD_DIM:, :] = jnp.where(pad_row == 0, 1.0, 0.0).astype(BF16)

    q = _rms_rows(q_ref[0], qw_ref[...])
    q16_ref[0, 0] = (q * ((HEAD_DIM ** -0.5) * LOG2E)).astype(BF16)

    blk = lax.broadcasted_iota(jnp.int32, (nb, rows), 0)
    own_blk = ti * nsub + lax.broadcasted_iota(jnp.int32, (nb, rows), 1) // bs
    blkf = blk.astype(F32)
    valid = blk < own_blk
    work = jnp.where(valid, _dot_nt(km_scr[...], q, HIGHEST), NEG_INF)
    picked = jnp.zeros((nb, rows), F32)
    for _ in range(MOBA_TOPK):
        best = jnp.max(work, axis=0, keepdims=True)
        first = jnp.min(jnp.where(work == best, blkf, float(nb)), axis=0, keepdims=True)
        hit = blkf == first
        picked = jnp.where(hit, 1.0, picked)
        work = jnp.where(hit, -jnp.inf, work)
    far_bias = rel_ref[REL_BUCKETS - 1, pl.program_id(1)] * LOG2E
    attended = (valid & (picked > 0.0)) | (blk == own_blk)
    add_ref[0, 0] = jnp.where(attended, far_bias, NEG_INF)


def _moba_prep(proj3, qw, kw, rel_bias, col0, rows):
    b, t, _ = proj3.shape
    hh, bs, dh = MOBA_HEADS, MOBA_BLOCK, HEAD_DIM
    nb = t // bs

    def col(base):
        return pl.BlockSpec((1, rows, LANES), lambda bi, hi, ti: (bi, ti, base + hi))

    row = pl.BlockSpec((1, LANES), lambda bi, hi, ti: (0, 0))
    return pl.pallas_call(
        functools.partial(_moba_prep_kernel, rows=rows, nb=nb),
        grid=(b, hh, t // rows),
        in_specs=[col(col0), col(col0 + hh), col(col0 + 2 * hh), row, row,
                  pl.BlockSpec(memory_space=pltpu.SMEM)],
        out_specs=[pl.BlockSpec((1, 1, rows, dh), lambda bi, hi, ti: (bi, hi, ti, 0)),
                   pl.BlockSpec((1, 1, rows, dh), lambda bi, hi, ti: (bi, hi, ti, 0)),
                   pl.BlockSpec((1, 1, MOBA_V_ROWS, rows), lambda bi, hi, ti: (bi, hi, 0, ti)),
                   pl.BlockSpec((1, 1, nb, rows), lambda bi, hi, ti: (bi, hi, 0, ti))],
        out_shape=[jax.ShapeDtypeStruct((b, hh, t, dh), BF16),
                   jax.ShapeDtypeStruct((b, hh, t, dh), BF16),
                   jax.ShapeDtypeStruct((b, hh, MOBA_V_ROWS, t), BF16),
                   jax.ShapeDtypeStruct((b, hh, nb, t), F32)],
        scratch_shapes=[pltpu.VMEM((nb, dh), F32)],
        compiler_params=_cparams("parallel", "parallel", "arbitrary"),
        name="moba_prep",
    )(proj3, proj3, proj3, qw, kw, rel_bias)


def _moba_kernel(q_ref, kn_ref, vt_ref, add_ref, bias_ref, o_ref, buf_a, buf_b, *, nb):
    i = pl.program_id(2)
    bs = MOBA_BLOCK
    ch = MOBA_TILES_PER_STEP
    E = range(MOBA_STREAMS)
    q16 = [q_ref[e, 0] for e in E]

    crows = ch * bs
    n_pairs = (i + 2 * ch) // (2 * ch)
    last_chunk = nb // ch - 1

    def chunk_scores(buf, e, c, far):
        r = pl.multiple_of(c * crows, crows)
        raw = _dot_nt(kn_ref[e, 0, pl.ds(r, crows), :], q16[e])
        mx = None
        for t in range(ch):
            j = c * ch + t
            add = add_ref[e, 0, pl.ds(j, 1), :]
            if far:
                st = raw[t * bs:(t + 1) * bs, :] + add
            else:
                d = jnp.clip(i - j, 0, BIAS_TILES - 1)
                st = jnp.where(add > 0.5 * NEG_INF, raw[t * bs:(t + 1) * bs, :] + bias_ref[0, d], NEG_INF)
            buf[e, t * bs:(t + 1) * bs, :] = st
            tmx = jnp.max(st, axis=0, keepdims=True)
            mx = tmx if mx is None else jnp.maximum(mx, tmx)
        return mx

    def absorb(buf, e, c, mx, m, acc):
        r = pl.multiple_of(c * crows, crows)
        m_new = jnp.maximum(m, mx)
        p = jnp.exp2(buf[e] - m_new)
        acc = jnp.exp2(m - m_new) * acc + _dot(vt_ref[e, 0, :, pl.ds(r, crows)], p.astype(BF16))
        return m_new, acc

    def pair(far, c, carry):
        m, acc, mx_a = (list(x) for x in carry)
        mx_b = [chunk_scores(buf_b, e, 2 * c + 1, far) for e in E]
        for e in E:
            m[e], acc[e] = absorb(buf_a, e, 2 * c, mx_a[e], m[e], acc[e])
        mx_a = [chunk_scores(buf_a, e, jnp.minimum(2 * c + 2, last_chunk), far) for e in E]
        for e in E:
            m[e], acc[e] = absorb(buf_b, e, 2 * c + 1, mx_b[e], m[e], acc[e])
        return tuple(m), tuple(acc), tuple(mx_a)

    def step(c, carry):
        all_far = (2 * c + 3) * ch - 1 + (BIAS_TILES - 1) <= i
        return lax.cond(all_far, functools.partial(pair, True, c), functools.partial(pair, False, c), carry)

    m0 = tuple(jnp.full((1, bs), MOBA_M_INIT, F32) for e in E)
    acc0 = tuple(jnp.zeros((MOBA_V_ROWS, bs), F32) for e in E)
    mx_a = tuple(chunk_scores(buf_a, e, 0, False) for e in E)
    m, acc, _ = lax.fori_loop(0, n_pairs, step, (m0, acc0, mx_a))
    for e in E:
        o_ref[e] = (acc[e][:HEAD_DIM] / acc[e][HEAD_DIM:HEAD_DIM + 1]).T.astype(o_ref.dtype)


def _moba(q16, kn, vt, add, bias):
    b, hh, t, dh = kn.shape
    bs = MOBA_BLOCK
    nb = t // bs
    ns = MOBA_STREAMS
    crows = MOBA_TILES_PER_STEP * bs
    assert b % ns == 0 and nb % (2 * MOBA_TILES_PER_STEP) == 0
    return pl.pallas_call(
        functools.partial(_moba_kernel, nb=nb),
        grid=(b // ns, hh, nb),
        in_specs=[pl.BlockSpec((ns, 1, bs, dh), lambda bi, hi, ti: (bi, hi, ti, 0)),
                  pl.BlockSpec((ns, 1, t, dh), lambda bi, hi, ti: (bi, hi, 0, 0)),
                  pl.BlockSpec((ns, 1, MOBA_V_ROWS, t), lambda bi, hi, ti: (bi, hi, 0, 0)),
                  pl.BlockSpec((ns, 1, nb, bs), lambda bi, hi, ti: (bi, hi, 0, ti)),
                  pl.BlockSpec((1, BIAS_TILES, bs, bs), lambda bi, hi, ti: (hi, 0, 0, 0))],
        out_specs=pl.BlockSpec((ns, bs, dh), lambda bi, hi, ti: (bi, ti, hi)),
        out_shape=jax.ShapeDtypeStruct((b, t, hh * dh), BF16),
        scratch_shapes=[pltpu.VMEM((ns, crows, bs), F32),
                        pltpu.VMEM((ns, crows, bs), F32)],
        compiler_params=_cparams("parallel", "parallel", "parallel"),
        name="moba",
    )(q16, kn, vt, add, bias)


def _tile(n, want):
    t = min(n, want)
    assert n % t == 0, (n, want)
    return t


def kernel(x, norm_mix_w, w_in, conv_w, a_log, dt_bias, gdn_o_norm_w, q_norm_w, k_norm_w, rel_bias,
           w_branch_gdn, w_branch_moba, w_out, norm_ffn_w, w_ffn_gate, w_ffn_up, w_ffn_down):
    b, t, d = x.shape
    m = b * t
    gw = GDN_HEADS * HEAD_DIM
    mw = MOBA_HEADS * HEAD_DIM
    assert t % MOBA_PREP_ROWS == 0 and w_in.shape[0] == 1
    assert w_in.shape[2] == 4 * gw + 2 * GDN_HEADS + 3 * mw + 2 * d

    o_bd = 4 * gw
    o_moba = o_bd + 2 * GDN_HEADS
    o_gate = o_moba + 3 * mw
    n_main = 2 * d + 4 * gw + 3 * mw
    tn_proj = PROJ_TILE[1]
    n_proj = -(-(n_main + LANES) // tn_proj) * tn_proj
    w_proj = _regroup_weights(w_in, (o_bd, o_moba, o_gate), n_proj)
    col_gdn = (2 * d) // LANES
    col_moba = col_gdn + 4 * GDN_HEADS
    col_bd = col_moba + 3 * MOBA_HEADS

    x2 = x.reshape(m, d)
    proj = _norm_matmul(x2, norm_mix_w, w_proj, _tile(m, PROJ_TILE[0]), tn_proj, F32, "proj")
    proj3 = proj.reshape(b, t, n_proj)

    lane_pad = jnp.zeros((1, LANES - 2 * GDN_HEADS), F32)
    head_pad = jnp.zeros((1, GDN_HEADS), F32)
    alog_row = jnp.concatenate([head_pad, a_log, lane_pad], axis=1)
    dtb_row = jnp.concatenate([head_pad, dt_bias, lane_pad], axis=1)
    y_a = _gdn(proj3, conv_w[0], alog_row, dtb_row, gdn_o_norm_w, col_gdn, col_bd, _tile(t, GDN_BLOCK))

    bias = _bias_tiles(rel_bias)
    q16, kn, vt, add = _moba_prep(proj3, q_norm_w, k_norm_w, rel_bias, col_moba, MOBA_PREP_ROWS)
    y_b = _moba(q16, kn, vt, add, bias)

    mix = _mix(y_a.reshape(m, gw), y_b.reshape(m, mw), proj,
               w_branch_gdn[0].astype(BF16), w_branch_moba[0].astype(BF16), _tile(m, MIX_ROWS), d)
    h1 = _matmul_res(mix, w_out[0].astype(BF16), x2, _tile(m, OUT_ROWS), d, "out")

    hid = _ffn_up(h1, norm_ffn_w, w_ffn_gate[0].astype(BF16), w_ffn_up[0].astype(BF16),
                  _tile(m, FFN_UP_TILE[0]), FFN_UP_TILE[1])
    h2 = _matmul_res(hid, w_ffn_down[0].astype(BF16), h1, _tile(m, FFN_DOWN_TILE[0]), FFN_DOWN_TILE[1],
                     "ffn_down")
    return h2.reshape(b, t, d)
```

```python
import functools
import math

import jax
import jax.numpy as jnp
from jax import lax
from jax.experimental import pallas as pl
from jax.experimental.pallas import tpu as pltpu

F32 = jnp.float32
BF16 = jnp.bfloat16
HIGHEST = lax.Precision.HIGHEST

LANES = 128
HEAD_DIM = 128
GDN_HEADS = 8
GDN_CONV = 4
GDN_CHUNK = 64
GDN_INV_BASE = 8
GDN_HEADS_PER_STEP = 8
NEG_INF = -1e30
MOBA_HEADS = 8
MOBA_BLOCK = 256
MOBA_TOPK = 3
REL_BUCKETS = 32
REL_MAX_DIST = 2048
BIAS_TILES = 8
MOBA_TILES_PER_STEP = 4
MOBA_PREP_ROWS = 2048
MOBA_STREAMS = 2
MOBA_V_ROWS = HEAD_DIM + 16
MOBA_M_INIT = 0.1 * NEG_INF
LOG2E = math.log2(math.e)
RMS_EPS = 1e-6
V7X_VMEM_BYTES = 64 * 1024 * 1024
VMEM_LIMIT = V7X_VMEM_BYTES * 7 // 8

PROJ_TILE = (1024, 1280)
MIX_ROWS = 512
OUT_ROWS = 512
FFN_UP_TILE = (1024, 512)
FFN_DOWN_TILE = (1024, 512)
GDN_BLOCK = 256
REGROUP_ROWS = 128


def _cparams(*sem):
    return pltpu.CompilerParams(dimension_semantics=sem, vmem_limit_bytes=VMEM_LIMIT)


def _sigmoid(x):
    return 0.5 + 0.5 * jnp.tanh(0.5 * x)


def _silu(x):
    h = 0.5 * x
    return h + h * jnp.tanh(h)


def _dot(a, b, precision=None):
    return jnp.dot(a, b, preferred_element_type=F32, precision=precision)


def _dot_nt(a, b, precision=None):
    return lax.dot_general(a, b, (((1,), (1,)), ((), ())),
                           preferred_element_type=F32, precision=precision)


def _dot_tn(a, b):
    return lax.dot_general(a, b, (((0,), (0,)), ((), ())), preferred_element_type=F32)


def _rms_rows(x, w):
    return x * lax.rsqrt(jnp.mean(x * x, axis=-1, keepdims=True) + RMS_EPS) * w


def _regroup_kernel(w_ref, o_ref, *, cuts):
    o_bd, o_moba, o_gate = cuts
    x = w_ref[...]
    parts = [x[:, o_gate:], x[:, :o_bd], x[:, o_moba:o_gate], x[:, o_bd:o_moba]]
    used = sum(p.shape[1] for p in parts)
    parts.append(jnp.zeros((x.shape[0], o_ref.shape[1] - used), x.dtype))
    o_ref[...] = jnp.concatenate(parts, axis=1).astype(o_ref.dtype)


def _regroup_weights(w, cuts, n_out):
    _, k, n_in = w.shape
    rows = _tile(k, REGROUP_ROWS)
    return pl.pallas_call(
        functools.partial(_regroup_kernel, cuts=cuts),
        grid=(k // rows,),
        in_specs=[pl.BlockSpec((None, rows, n_in), lambda i: (0, i, 0))],
        out_specs=pl.BlockSpec((rows, n_out), lambda i: (i, 0)),
        out_shape=jax.ShapeDtypeStruct((k, n_out), BF16),
        compiler_params=_cparams("parallel"),
        name="regroup",
    )(w)


def _norm_matmul_kernel(x_ref, nw_ref, w_ref, o_ref, u_scr):
    @pl.when(pl.program_id(1) == 0)
    def _():
        u_scr[...] = _rms_rows(x_ref[...], nw_ref[...]).astype(BF16)

    o_ref[...] = _dot(u_scr[...], w_ref[...]).astype(o_ref.dtype)


def _norm_matmul(x, nw, w, tm, tn, out_dtype, name):
    m, k = x.shape
    n = w.shape[1]
    assert m % tm == 0 and n % tn == 0
    return pl.pallas_call(
        _norm_matmul_kernel,
        grid=(m // tm, n // tn),
        in_specs=[pl.BlockSpec((tm, k), lambda i, j: (i, 0)),
                  pl.BlockSpec((1, k), lambda i, j: (0, 0)),
                  pl.BlockSpec((k, tn), lambda i, j: (0, j))],
        out_specs=pl.BlockSpec((tm, tn), lambda i, j: (i, j)),
        out_shape=jax.ShapeDtypeStruct((m, n), out_dtype),
        scratch_shapes=[pltpu.VMEM((tm, k), BF16)],
        compiler_params=_cparams("parallel", "arbitrary"),
        name=name,
    )(x, nw, w)


def _matmul_res_kernel(a_ref, w_ref, r_ref, o_ref):
    o_ref[...] = r_ref[...] + _dot(a_ref[...], w_ref[...])


def _matmul_res(a, w, res, tm, tn, name):
    m, k = a.shape
    n = w.shape[1]
    assert m % tm == 0 and n % tn == 0
    return pl.pallas_call(
        _matmul_res_kernel,
        grid=(m // tm, n // tn),
        in_specs=[pl.BlockSpec((tm, k), lambda i, j: (i, 0)),
                  pl.BlockSpec((k, tn), lambda i, j: (0, j)),
                  pl.BlockSpec((tm, tn), lambda i, j: (i, j))],
        out_specs=pl.BlockSpec((tm, tn), lambda i, j: (i, j)),
        out_shape=jax.ShapeDtypeStruct((m, n), F32),
        compiler_params=_cparams("parallel", "parallel"),
        name=name,
    )(a, w, res)


def _ffn_up_kernel(x_ref, nw_ref, wg_ref, wu_ref, o_ref, u_scr):
    @pl.when(pl.program_id(1) == 0)
    def _():
        u_scr[...] = _rms_rows(x_ref[...], nw_ref[...]).astype(BF16)

    u = u_scr[...]
    g = _dot(u, wg_ref[...])
    o_ref[...] = (_silu(g) * _dot(u, wu_ref[...])).astype(o_ref.dtype)


def _ffn_up(x, nw, wg, wu, tm, tn):
    m, k = x.shape
    n = wg.shape[1]
    assert m % tm == 0 and n % tn == 0
    return pl.pallas_call(
        _ffn_up_kernel,
        grid=(m // tm, n // tn),
        in_specs=[pl.BlockSpec((tm, k), lambda i, j: (i, 0)),
                  pl.BlockSpec((1, k), lambda i, j: (0, 0)),
                  pl.BlockSpec((k, tn), lambda i, j: (0, j)),
                  pl.BlockSpec((k, tn), lambda i, j: (0, j))],
        out_specs=pl.BlockSpec((tm, tn), lambda i, j: (i, j)),
        out_shape=jax.ShapeDtypeStruct((m, n), BF16),
        scratch_shapes=[pltpu.VMEM((tm, k), BF16)],
        compiler_params=_cparams("parallel", "arbitrary"),
        name="ffn_up",
    )(x, nw, wg, wu)


def _mix_kernel(ya_ref, yb_ref, ga_ref, gb_ref, wa_ref, wb_ref, o_ref):
    a = _dot(ya_ref[...], wa_ref[...])
    b = _dot(yb_ref[...], wb_ref[...])
    o_ref[...] = (_sigmoid(ga_ref[...]) * a + _sigmoid(gb_ref[...]) * b).astype(o_ref.dtype)


def _mix(ya, yb, proj, wa, wb, tm, tn):
    m, k = ya.shape
    n = wa.shape[1]
    nb = n // tn
    assert m % tm == 0 and n % tn == 0
    return pl.pallas_call(
        _mix_kernel,
        grid=(m // tm, nb),
        in_specs=[pl.BlockSpec((tm, k), lambda i, j: (i, 0)),
                  pl.BlockSpec((tm, k), lambda i, j: (i, 0)),
                  pl.BlockSpec((tm, tn), lambda i, j: (i, j)),
                  pl.BlockSpec((tm, tn), lambda i, j: (i, nb + j)),
                  pl.BlockSpec((k, tn), lambda i, j: (0, j)),
                  pl.BlockSpec((k, tn), lambda i, j: (0, j))],
        out_specs=pl.BlockSpec((tm, tn), lambda i, j: (i, j)),
        out_shape=jax.ShapeDtypeStruct((m, n), BF16),
        compiler_params=_cparams("parallel", "parallel"),
        name="mix",
    )(ya, yb, proj, proj, wa, wb)


def _split3(x):
    a = x.astype(BF16)
    r = x - a.astype(F32)
    b = r.astype(BF16)
    c = (r - b.astype(F32)).astype(BF16)
    return a, b, c


def _gdn_kernel(q_ref, k_ref, v_ref, z_ref, bd_ref, cwq_ref, cwk_ref, cwv_ref,
                alog_ref, dtb_ref, onw_ref, o_ref, s_scr, tail_scr, *, tb):
    hp = GDN_HEADS_PER_STEP
    C = GDN_CHUNK
    D = HEAD_DIM
    head0 = pl.program_id(1) * hp

    @pl.when(pl.program_id(2) == 0)
    def _():
        s_scr[...] = jnp.zeros_like(s_scr)
        tail_scr[...] = jnp.zeros_like(tail_scr)

    row8 = lax.broadcasted_iota(jnp.int32, (8, hp * D), 0)
    tails = []

    def conv_silu(u_ref, cw_ref, idx):
        u = u_ref[0]
        w = cw_ref[...]
        tail = tail_scr[idx]
        y = None
        ytop = None
        for s in (3, 2, 1):
            sh = pltpu.roll(u, s, axis=0)
            top = jnp.where(row8 < s, pltpu.roll(tail, s, axis=0), sh[0:8])
            wj = w[3 - s:4 - s]
            y = sh * wj if y is None else y + sh * wj
            ytop = top * wj if ytop is None else ytop + top * wj
        y = y + u * w[3:4]
        ytop = ytop + u[0:8] * w[3:4]
        tails.append(u[tb - 8:tb])
        y = jnp.concatenate([ytop, y[8:]], axis=0)
        return _silu(y)

    q_all = conv_silu(q_ref, cwq_ref, 0)
    k_all = conv_silu(k_ref, cwk_ref, 1)
    v_all = conv_silu(v_ref, cwv_ref, 2)

    bd = bd_ref[0]
    lane = lax.broadcasted_iota(jnp.int32, (tb, LANES), 1)
    beta_all = _sigmoid(bd)
    xg = bd + dtb_ref[...]
    softplus = jnp.maximum(xg, 0.0) + jnp.log1p(jnp.exp(-jnp.abs(xg)))
    g_all = -jnp.exp(alog_ref[...]) * softplus
    betas = [jnp.sum(jnp.where(lane == head0 + e, beta_all, 0.0), axis=-1, keepdims=True)
             for e in range(hp)]
    gs = [jnp.sum(jnp.where(lane == head0 + e + GDN_HEADS, g_all, 0.0), axis=-1, keepdims=True)
          for e in range(hp)]

    nc = tb // C
    ri = lax.broadcasted_iota(jnp.int32, (tb, tb), 0)
    ci = lax.broadcasted_iota(jnp.int32, (tb, tb), 1)
    same = (ri // C) == (ci // C)
    tril16 = jnp.where(same & (ri >= ci), 1.0, 0.0).astype(BF16)
    g_b = jnp.concatenate([jnp.broadcast_to(g, (tb, LANES)) for g in gs], axis=1)
    gc_all = sum(_dot(tril16, piece) for piece in _split3(g_b))

    iw = lax.broadcasted_iota(jnp.int32, (C, tb), 0)
    jw = lax.broadcasted_iota(jnp.int32, (C, tb), 1) % C
    incl_w = iw >= jw
    strict_w = iw > jw
    eye_w = jnp.where(iw == jw, 1.0, 0.0).astype(F32)
    low_half = (lax.broadcasted_iota(jnp.int32, (C, LANES), 1) < C)

    def block_diag(wide16):
        return jnp.where(same, jnp.concatenate([wide16] * nc, axis=0), jnp.zeros((), BF16))

    def pair_blocks(full):
        return jnp.where(low_half, full[:C], full[C:])

    E = range(hp)
    pairs = range(tb // LANES)

    def l2n(x):
        return x * lax.rsqrt(jnp.sum(x * x, axis=-1, keepdims=True) + RMS_EPS)

    qs = [l2n(q_all[:, e * D:(e + 1) * D]) * (D ** -0.5) for e in E]
    ks = [l2n(k_all[:, e * D:(e + 1) * D]) for e in E]
    vs = [v_all[:, e * D:(e + 1) * D] for e in E]
    gcs = [gc_all[:, e * D:(e + 1) * D] for e in E]
    g_col = [jnp.concatenate([pair_blocks(gc[p * LANES:(p + 1) * LANES]) for p in pairs], axis=1)
             for gc in gcs]
    g_row = [jnp.concatenate([gc[p * LANES:(p + 1) * LANES, :].T[0:1, :] for p in pairs], axis=1)
             for gc in gcs]
    decay = [jnp.where(incl_w, jnp.exp(jnp.where(incl_w, g_col[e] - g_row[e], 0.0)), 0.0) for e in E]
    k16 = [k.astype(BF16) for k in ks]
    q16 = [q.astype(BF16) for q in qs]
    kb = [ks[e] * betas[e] for e in E]
    kb16 = [x.astype(BF16) for x in kb]

    def pair_products(a16, b16):
        return jnp.concatenate([pair_blocks(_dot_nt(a16[p * LANES:(p + 1) * LANES],
                                                    b16[p * LANES:(p + 1) * LANES])) for p in pairs], axis=1)

    lmat = [jnp.where(strict_w, pair_products(kb16[e], k16[e]) * decay[e], 0.0) for e in E]
    amat = [jnp.where(incl_w, pair_products(q16[e], k16[e]) * decay[e], 0.0) for e in E]
    base = GDN_INV_BASE
    blk_i, blk_j = iw // base, jw // base
    neg_d16 = [jnp.where(blk_i == blk_j, -lmat[e], 0.0).astype(BF16) for e in E]
    tinv = [eye_w + neg_d16[e].astype(F32) for e in E]
    pw = [_dot(neg_d16[e], block_diag(neg_d16[e])) for e in E]
    span = 2
    while 2 * span < base:
        p16 = [pw[e].astype(BF16) for e in E]
        prod = [_dot(jnp.concatenate([p16[e], tinv[e].astype(BF16)], axis=0), block_diag(p16[e])) for e in E]
        pw = [prod[e][:C] for e in E]
        tinv = [tinv[e] + prod[e][C:] for e in E]
        span *= 2
    tinv = [tinv[e] + _dot(tinv[e].astype(BF16), block_diag(pw[e].astype(BF16))) for e in E]
    s = base
    while s < C:
        below_left = (iw // (2 * s) == jw // (2 * s)) & ((iw // s) % 2 == 1) & ((jw // s) % 2 == 0)
        c16 = [jnp.where(below_left, lmat[e], 0.0).astype(BF16) for e in E]
        cx = [_dot(c16[e], block_diag(tinv[e].astype(BF16))) for e in E]
        tinv = [tinv[e] - _dot(tinv[e].astype(BF16), block_diag(cx[e].astype(BF16))) for e in E]
        s *= 2
    eg = [jnp.exp(gc) for gc in gcs]
    rhs = [jnp.concatenate([kb[e] * eg[e], vs[e] * betas[e]], axis=1).astype(BF16) for e in E]
    wu = [_dot(block_diag(tinv[e].astype(BF16)), rhs[e]).astype(BF16) for e in E]
    au = [_dot(block_diag(amat[e].astype(BF16)), wu[e]) for e in E]
    q_eff = [(qs[e] * eg[e] - au[e][:, :D]).astype(BF16) for e in E]
    gl = [jnp.concatenate([jnp.broadcast_to(gc[c * C + C - 1:c * C + C, :], (C, LANES))
                           for c in range(nc)], axis=0) for gc in gcs]
    k_dec = [(ks[e] * jnp.exp(gl[e] - gcs[e])).astype(BF16) for e in E]
    pn = [[_dot_tn(k_dec[e][c * C:(c + 1) * C], wu[e][c * C:(c + 1) * C]) for c in range(nc)]
          for e in E]

    states = [s_scr[e] for e in E]
    onw = onw_ref[...]
    for c in range(nc):
        r0 = c * C
        lhs = [jnp.concatenate([pn[e][c][:, :D].astype(BF16), q_eff[e][r0:r0 + C]], axis=0) for e in E]
        res = [_dot(lhs[e], states[e].astype(BF16)) for e in E]
        outs = [_rms_rows(res[e][D:] + au[e][r0:r0 + C, D:], onw) for e in E]
        states = [states[e] * jnp.exp(gl[e][r0:r0 + 1, :]) - res[e][:D] + pn[e][c][:, D:] for e in E]
        zc = z_ref[0, r0:r0 + C, :]
        o_ref[0, r0:r0 + C, :] = (jnp.concatenate(outs, axis=1) * _silu(zc)).astype(o_ref.dtype)
    for e in range(hp):
        s_scr[e] = states[e]
    for idx in range(3):
        tail_scr[idx] = tails[idx]


def _gdn(proj3, conv_w, alog_row, dtb_row, onw, col0, bd_col, tb):
    b, t, _ = proj3.shape
    hp = GDN_HEADS_PER_STEP
    ng = GDN_HEADS // hp
    w = hp * HEAD_DIM

    def col(base):
        return pl.BlockSpec((1, tb, w), lambda bi, hi, ti: (bi, ti, base // hp + hi))

    def cw(base):
        return pl.BlockSpec((GDN_CONV, w), lambda bi, hi, ti: (0, base // hp + hi))

    assert col0 % hp == 0
    row = pl.BlockSpec((1, LANES), lambda bi, hi, ti: (0, 0))
    return pl.pallas_call(
        functools.partial(_gdn_kernel, tb=tb),
        grid=(b, ng, t // tb),
        in_specs=[col(col0), col(col0 + GDN_HEADS), col(col0 + 2 * GDN_HEADS), col(col0 + 3 * GDN_HEADS),
                  pl.BlockSpec((1, tb, LANES), lambda bi, hi, ti: (bi, ti, bd_col)),
                  cw(0), cw(GDN_HEADS), cw(2 * GDN_HEADS), row, row, row],
        out_specs=pl.BlockSpec((1, tb, w), lambda bi, hi, ti: (bi, ti, hi)),
        out_shape=jax.ShapeDtypeStruct((b, t, GDN_HEADS * HEAD_DIM), BF16),
        scratch_shapes=[pltpu.VMEM((hp, HEAD_DIM, HEAD_DIM), F32),
                        pltpu.VMEM((3, 8, w), F32)],
        compiler_params=_cparams("parallel", "parallel", "arbitrary"),
        name="gdn",
    )(proj3, proj3, proj3, proj3, proj3, conv_w, conv_w, conv_w, alog_row, dtb_row, onw)


def _bias_kernel(rel_ref, o_ref):
    h = pl.program_id(0)
    bs = MOBA_BLOCK
    max_exact = REL_BUCKETS // 2
    x = lax.broadcasted_iota(jnp.int32, (8, 2 * bs), 1)
    for d in range(BIAS_TILES):
        dist = jnp.maximum(d * bs + x - bs, 0)
        df = dist.astype(F32)
        log_ratio = jnp.log(jnp.maximum(df, float(max_exact)) / max_exact) / math.log(REL_MAX_DIST / max_exact)
        large = max_exact + (log_ratio * (REL_BUCKETS - max_exact)).astype(jnp.int32)
        large = jnp.minimum(large, REL_BUCKETS - 1)
        bucket = jnp.where(dist < max_exact, dist, large)
        row = jnp.zeros((8, 2 * bs), F32)
        for b in range(REL_BUCKETS):
            row = jnp.where(bucket == b, rel_ref[b, h], row)
        base = jnp.broadcast_to(row[0:1, :] * LOG2E, (bs, 2 * bs))
        tile = pltpu.roll(base, 0, 1, stride=1, stride_axis=0)[:, bs:]
        if d == 0:
            kk = lax.broadcasted_iota(jnp.int32, (bs, bs), 0)
            qq = lax.broadcasted_iota(jnp.int32, (bs, bs), 1)
            tile = jnp.where(kk <= qq, tile, NEG_INF)
        o_ref[0, d] = tile


def _bias_tiles(rel_bias):
    bs = MOBA_BLOCK
    max_exact = REL_BUCKETS // 2
    nearest = (BIAS_TILES - 1) * bs - (bs - 1)
    assert max_exact + int(math.log(nearest / max_exact) / math.log(REL_MAX_DIST / max_exact)
                           * (REL_BUCKETS - max_exact)) >= REL_BUCKETS - 1
    return pl.pallas_call(
        _bias_kernel,
        grid=(MOBA_HEADS,),
        in_specs=[pl.BlockSpec(memory_space=pltpu.SMEM)],
        out_specs=pl.BlockSpec((1, BIAS_TILES, bs, bs), lambda h: (h, 0, 0, 0)),
        out_shape=jax.ShapeDtypeStruct((MOBA_HEADS, BIAS_TILES, bs, bs), F32),
        compiler_params=_cparams("parallel"),
        name="bias",
    )(rel_bias)


def _moba_prep_kernel(q_ref, k_ref, v_ref, qw_ref, kw_ref, rel_ref, q16_ref, kn_ref, vt_ref, add_ref, km_scr,
                      *, rows, nb):
    bs = MOBA_BLOCK
    nsub = rows // bs
    ti = pl.program_id(2)

    @pl.when(ti == 0)
    def _():
        km_scr[...] = jnp.zeros_like(km_scr)

    kn = _rms_rows(k_ref[0], kw_ref[...])
    kn_ref[0, 0] = kn.astype(BF16)
    for s in range(nsub):
        km_scr[pl.ds(ti * nsub + s, 1), :] = jnp.mean(kn[s * bs:(s + 1) * bs], axis=0, keepdims=True)
        vt_ref[0, 0, 0:HEAD_DIM, s * bs:(s + 1) * bs] = v_ref[0, s * bs:(s + 1) * bs, :].T.astype(BF16)
    pad_row = lax.broadcasted_iota(jnp.int32, (MOBA_V_ROWS - HEAD_DIM, rows), 0)
    vt_ref[0, 0, HEAD_DIM:, :] = jnp.where(pad_row == 0, 1.0, 0.0).astype(BF16)

    q = _rms_rows(q_ref[0], qw_ref[...])
    q16_ref[0, 0] = (q * ((HEAD_DIM ** -0.5) * LOG2E)).astype(BF16)

    blk = lax.broadcasted_iota(jnp.int32, (nb, rows), 0)
    own_blk = ti * nsub + lax.broadcasted_iota(jnp.int32, (nb, rows), 1) // bs
    blkf = blk.astype(F32)
    valid = blk < own_blk
    work = jnp.where(valid, _dot_nt(km_scr[...], q, HIGHEST), NEG_INF)
    picked = jnp.zeros((nb, rows), F32)
    for _ in range(MOBA_TOPK):
        best = jnp.max(work, axis=0, keepdims=True)
        first = jnp.min(jnp.where(work == best, blkf, float(nb)), axis=0, keepdims=True)
        hit = blkf == first
        picked = jnp.where(hit, 1.0, picked)
        work = jnp.where(hit, -jnp.inf, work)
    far_bias = rel_ref[REL_BUCKETS - 1, pl.program_id(1)] * LOG2E
    attended = (valid & (picked > 0.0)) | (blk == own_blk)
    add_ref[0, 0] = jnp.where(attended, far_bias, NEG_INF)


def _moba_prep(proj3, qw, kw, rel_bias, col0, rows):
    b, t, _ = proj3.shape
    hh, bs, dh = MOBA_HEADS, MOBA_BLOCK, HEAD_DIM
    nb = t // bs

    def col(base):
        return pl.BlockSpec((1, rows, LANES), lambda bi, hi, ti: (bi, ti, base + hi))

    row = pl.BlockSpec((1, LANES), lambda bi, hi, ti: (0, 0))
    return pl.pallas_call(
        functools.partial(_moba_prep_kernel, rows=rows, nb=nb),
        grid=(b, hh, t // rows),
        in_specs=[col(col0), col(col0 + hh), col(col0 + 2 * hh), row, row,
                  pl.BlockSpec(memory_space=pltpu.SMEM)],
        out_specs=[pl.BlockSpec((1, 1, rows, dh), lambda bi, hi, ti: (bi, hi, ti, 0)),
                   pl.BlockSpec((1, 1, rows, dh), lambda bi, hi, ti: (bi, hi, ti, 0)),
                   pl.BlockSpec((1, 1, MOBA_V_ROWS, rows), lambda bi, hi, ti: (bi, hi, 0, ti)),
                   pl.BlockSpec((1, 1, nb, rows), lambda bi, hi, ti: (bi, hi, 0, ti))],
        out_shape=[jax.ShapeDtypeStruct((b, hh, t, dh), BF16),
                   jax.ShapeDtypeStruct((b, hh, t, dh), BF16),
                   jax.ShapeDtypeStruct((b, hh, MOBA_V_ROWS, t), BF16),
                   jax.ShapeDtypeStruct((b, hh, nb, t), F32)],
        scratch_shapes=[pltpu.VMEM((nb, dh), F32)],
        compiler_params=_cparams("parallel", "parallel", "arbitrary"),
        name="moba_prep",
    )(proj3, proj3, proj3, qw, kw, rel_bias)


def _moba_kernel(q_ref, kn_ref, vt_ref, add_ref, bias_ref, o_ref, buf_a, buf_b, *, nb):
    i = pl.program_id(2)
    bs = MOBA_BLOCK
    ch = MOBA_TILES_PER_STEP
    E = range(MOBA_STREAMS)
    q16 = [q_ref[e, 0] for e in E]

    crows = ch * bs
    n_pairs = (i + 2 * ch) // (2 * ch)
    last_chunk = nb // ch - 1

    def chunk_scores(buf, e, c, far):
        r = pl.multiple_of(c * crows, crows)
        raw = _dot_nt(kn_ref[e, 0, pl.ds(r, crows), :], q16[e])
        mx = None
        for t in range(ch):
            j = c * ch + t
            add = add_ref[e, 0, pl.ds(j, 1), :]
            if far:
                st = raw[t * bs:(t + 1) * bs, :] + add
            else:
                d = jnp.clip(i - j, 0, BIAS_TILES - 1)
                st = jnp.where(add > 0.5 * NEG_INF, raw[t * bs:(t + 1) * bs, :] + bias_ref[0, d], NEG_INF)
            buf[e, t * bs:(t + 1) * bs, :] = st
            tmx = jnp.max(st, axis=0, keepdims=True)
            mx = tmx if mx is None else jnp.maximum(mx, tmx)
        return mx

    def absorb(buf, e, c, mx, m, acc):
        r = pl.multiple_of(c * crows, crows)
        m_new = jnp.maximum(m, mx)
        p = jnp.exp2(buf[e] - m_new)
        acc = jnp.exp2(m - m_new) * acc + _dot(vt_ref[e, 0, :, pl.ds(r, crows)], p.astype(BF16))
        return m_new, acc

    def pair(far, c, carry):
        m, acc, mx_a = (list(x) for x in carry)
        mx_b = [chunk_scores(buf_b, e, 2 * c + 1, far) for e in E]
        for e in E:
            m[e], acc[e] = absorb(buf_a, e, 2 * c, mx_a[e], m[e], acc[e])
        mx_a = [chunk_scores(buf_a, e, jnp.minimum(2 * c + 2, last_chunk), far) for e in E]
        for e in E:
            m[e], acc[e] = absorb(buf_b, e, 2 * c + 1, mx_b[e], m[e], acc[e])
        return tuple(m), tuple(acc), tuple(mx_a)

    def step(c, carry):
        all_far = (2 * c + 3) * ch - 1 + (BIAS_TILES - 1) <= i
        return lax.cond(all_far, functools.partial(pair, True, c), functools.partial(pair, False, c), carry)

    m0 = tuple(jnp.full((1, bs), MOBA_M_INIT, F32) for e in E)
    acc0 = tuple(jnp.zeros((MOBA_V_ROWS, bs), F32) for e in E)
    mx_a = tuple(chunk_scores(buf_a, e, 0, False) for e in E)
    m, acc, _ = lax.fori_loop(0, n_pairs, step, (m0, acc0, mx_a))
    for e in E:
        o_ref[e] = (acc[e][:HEAD_DIM] / acc[e][HEAD_DIM:HEAD_DIM + 1]).T.astype(o_ref.dtype)


def _moba(q16, kn, vt, add, bias):
    b, hh, t, dh = kn.shape
    bs = MOBA_BLOCK
    nb = t // bs
    ns = MOBA_STREAMS
    crows = MOBA_TILES_PER_STEP * bs
    assert b % ns == 0 and nb % (2 * MOBA_TILES_PER_STEP) == 0
    return pl.pallas_call(
        functools.partial(_moba_kernel, nb=nb),
        grid=(b // ns, hh, nb),
        in_specs=[pl.BlockSpec((ns, 1, bs, dh), lambda bi, hi, ti: (bi, hi, ti, 0)),
                  pl.BlockSpec((ns, 1, t, dh), lambda bi, hi, ti: (bi, hi, 0, 0)),
                  pl.BlockSpec((ns, 1, MOBA_V_ROWS, t), lambda bi, hi, ti: (bi, hi, 0, 0)),
                  pl.BlockSpec((ns, 1, nb, bs), lambda bi, hi, ti: (bi, hi, 0, ti)),
                  pl.BlockSpec((1, BIAS_TILES, bs, bs), lambda bi, hi, ti: (hi, 0, 0, 0))],
        out_specs=pl.BlockSpec((ns, bs, dh), lambda bi, hi, ti: (bi, ti, hi)),
        out_shape=jax.ShapeDtypeStruct((b, t, hh * dh), BF16),
        scratch_shapes=[pltpu.VMEM((ns, crows, bs), F32),
                        pltpu.VMEM((ns, crows, bs), F32)],
        compiler_params=_cparams("parallel", "parallel", "parallel"),
        name="moba",
    )(q16, kn, vt, add, bias)


def _tile(n, want):
    t = min(n, want)
    assert n % t == 0, (n, want)
    return t


def kernel(x, norm_mix_w, w_in, conv_w, a_log, dt_bias, gdn_o_norm_w, q_norm_w, k_norm_w, rel_bias,
           w_branch_gdn, w_branch_moba, w_out, norm_ffn_w, w_ffn_gate, w_ffn_up, w_ffn_down):
    b, t, d = x.shape
    m = b * t
    gw = GDN_HEADS * HEAD_DIM
    mw = MOBA_HEADS * HEAD_DIM
    assert t % MOBA_PREP_ROWS == 0 and w_in.shape[0] == 1
    assert w_in.shape[2] == 4 * gw + 2 * GDN_HEADS + 3 * mw + 2 * d

    o_bd = 4 * gw
    o_moba = o_bd + 2 * GDN_HEADS
    o_gate = o_moba + 3 * mw
    n_main = 2 * d + 4 * gw + 3 * mw
    tn_proj = PROJ_TILE[1]
    n_proj = -(-(n_main + LANES) // tn_proj) * tn_proj
    w_proj = _regroup_weights(w_in, (o_bd, o_moba, o_gate), n_proj)
    col_gdn = (2 * d) // LANES
    col_moba = col_gdn + 4 * GDN_HEADS
    col_bd = col_moba + 3 * MOBA_HEADS

    x2 = x.reshape(m, d)
    proj = _norm_matmul(x2, norm_mix_w, w_proj, _tile(m, PROJ_TILE[0]), tn_proj, F32, "proj")
    proj3 = proj.reshape(b, t, n_proj)

    lane_pad = jnp.zeros((1, LANES - 2 * GDN_HEADS), F32)
    head_pad = jnp.zeros((1, GDN_HEADS), F32)
    alog_row = jnp.concatenate([head_pad, a_log, lane_pad], axis=1)
    dtb_row = jnp.concatenate([head_pad, dt_bias, lane_pad], axis=1)
    y_a = _gdn(proj3, conv_w[0], alog_row, dtb_row, gdn_o_norm_w, col_gdn, col_bd, _tile(t, GDN_BLOCK))

    bias = _bias_tiles(rel_bias)
    q16, kn, vt, add = _moba_prep(proj3, q_norm_w, k_norm_w, rel_bias, col_moba, MOBA_PREP_ROWS)
    y_b = _moba(q16, kn, vt, add, bias)

    mix = _mix(y_a.reshape(m, gw), y_b.reshape(m, mw), proj,
               w_branch_gdn[0].astype(BF16), w_branch_moba[0].astype(BF16), _tile(m, MIX_ROWS), d)
    h1 = _matmul_res(mix, w_out[0].astype(BF16), x2, _tile(m, OUT_ROWS), d, "out")

    hid = _ffn_up(h1, norm_ffn_w, w_ffn_gate[0].astype(BF16), w_ffn_up[0].astype(BF16),
                  _tile(m, FFN_UP_TILE[0]), FFN_UP_TILE[1])
    h2 = _matmul_res(hid, w_ffn_down[0].astype(BF16), h1, _tile(m, FFN_DOWN_TILE[0]), FFN_DOWN_TILE[1],
                     "ffn_down")
    return h2.reshape(b, t, d)
```

```python
import functools
import math

import jax
import jax.numpy as jnp
from jax import lax
from jax.experimental import pallas as pl
from jax.experimental.pallas import tpu as pltpu

F32 = jnp.float32
BF16 = jnp.bfloat16
HIGHEST = lax.Precision.HIGHEST

LANES = 128
HEAD_DIM = 128
GDN_HEADS = 8
GDN_CONV = 4
GDN_CHUNK = 64
GDN_INV_BASE = 8
GDN_HEADS_PER_STEP = 8
NEG_INF = -1e30
MOBA_HEADS = 8
MOBA_BLOCK = 256
MOBA_TOPK = 3
REL_BUCKETS = 32
REL_MAX_DIST = 2048
BIAS_TILES = 8
MOBA_TILES_PER_STEP = 4
MOBA_PREP_ROWS = 2048
MOBA_STREAMS = 4
MOBA_V_ROWS = HEAD_DIM + 16
MOBA_M_INIT = 0.1 * NEG_INF
LOG2E = math.log2(math.e)
RMS_EPS = 1e-6
V7X_VMEM_BYTES = 64 * 1024 * 1024
VMEM_LIMIT = V7X_VMEM_BYTES * 7 // 8

PROJ_TILE = (1024, 1920)
MIX_ROWS = 512
OUT_ROWS = 512
FFN_UP_TILE = (1024, 512)
FFN_DOWN_TILE = (1024, 512)
GDN_BLOCK = 256
REGROUP_ROWS = 128


def _cparams(*sem):
    return pltpu.CompilerParams(dimension_semantics=sem, vmem_limit_bytes=VMEM_LIMIT)


def _sigmoid(x):
    return 0.5 + 0.5 * jnp.tanh(0.5 * x)


def _silu(x):
    h = 0.5 * x
    return h + h * jnp.tanh(h)


def _dot(a, b, precision=None):
    return jnp.dot(a, b, preferred_element_type=F32, precision=precision)


def _dot_nt(a, b, precision=None):
    return lax.dot_general(a, b, (((1,), (1,)), ((), ())),
                           preferred_element_type=F32, precision=precision)


def _dot_tn(a, b):
    return lax.dot_general(a, b, (((0,), (0,)), ((), ())), preferred_element_type=F32)


def _rms_rows(x, w):
    return x * lax.rsqrt(jnp.mean(x * x, axis=-1, keepdims=True) + RMS_EPS) * w


def _regroup_kernel(w_ref, o_ref, *, cuts):
    o_bd, o_moba, o_gate = cuts
    x = w_ref[...]
    parts = [x[:, o_gate:], x[:, :o_bd], x[:, o_moba:o_gate], x[:, o_bd:o_moba]]
    used = sum(p.shape[1] for p in parts)
    parts.append(jnp.zeros((x.shape[0], o_ref.shape[1] - used), x.dtype))
    o_ref[...] = jnp.concatenate(parts, axis=1).astype(o_ref.dtype)


def _regroup_weights(w, cuts, n_out):
    _, k, n_in = w.shape
    rows = _tile(k, REGROUP_ROWS)
    return pl.pallas_call(
        functools.partial(_regroup_kernel, cuts=cuts),
        grid=(k // rows,),
        in_specs=[pl.BlockSpec((None, rows, n_in), lambda i: (0, i, 0))],
        out_specs=pl.BlockSpec((rows, n_out), lambda i: (i, 0)),
        out_shape=jax.ShapeDtypeStruct((k, n_out), BF16),
        compiler_params=_cparams("parallel"),
        name="regroup",
    )(w)


def _norm_matmul_kernel(x_ref, nw_ref, w_ref, o_ref, u_scr):
    @pl.when(pl.program_id(1) == 0)
    def _():
        u_scr[...] = _rms_rows(x_ref[...], nw_ref[...]).astype(BF16)

    o_ref[...] = _dot(u_scr[...], w_ref[...]).astype(o_ref.dtype)


def _norm_matmul(x, nw, w, tm, tn, out_dtype, name):
    m, k = x.shape
    n = w.shape[1]
    assert m % tm == 0 and n % tn == 0
    return pl.pallas_call(
        _norm_matmul_kernel,
        grid=(m // tm, n // tn),
        in_specs=[pl.BlockSpec((tm, k), lambda i, j: (i, 0)),
                  pl.BlockSpec((1, k), lambda i, j: (0, 0)),
                  pl.BlockSpec((k, tn), lambda i, j: (0, j))],
        out_specs=pl.BlockSpec((tm, tn), lambda i, j: (i, j)),
        out_shape=jax.ShapeDtypeStruct((m, n), out_dtype),
        scratch_shapes=[pltpu.VMEM((tm, k), BF16)],
        compiler_params=_cparams("parallel", "arbitrary"),
        name=name,
    )(x, nw, w)


def _matmul_res_kernel(a_ref, w_ref, r_ref, o_ref):
    o_ref[...] = r_ref[...] + _dot(a_ref[...], w_ref[...])


def _matmul_res(a, w, res, tm, tn, name):
    m, k = a.shape
    n = w.shape[1]
    assert m % tm == 0 and n % tn == 0
    return pl.pallas_call(
        _matmul_res_kernel,
        grid=(m // tm, n // tn),
        in_specs=[pl.BlockSpec((tm, k), lambda i, j: (i, 0)),
                  pl.BlockSpec((k, tn), lambda i, j: (0, j)),
                  pl.BlockSpec((tm, tn), lambda i, j: (i, j))],
        out_specs=pl.BlockSpec((tm, tn), lambda i, j: (i, j)),
        out_shape=jax.ShapeDtypeStruct((m, n), F32),
        compiler_params=_cparams("parallel", "parallel"),
        name=name,
    )(a, w, res)


def _ffn_up_kernel(x_ref, nw_ref, wg_ref, wu_ref, o_ref, u_scr):
    @pl.when(pl.program_id(1) == 0)
    def _():
        u_scr[...] = _rms_rows(x_ref[...], nw_ref[...]).astype(BF16)

    u = u_scr[...]
    g = _dot(u, wg_ref[...])
    o_ref[...] = (_silu(g) * _dot(u, wu_ref[...])).astype(o_ref.dtype)


def _ffn_up(x, nw, wg, wu, tm, tn):
    m, k = x.shape
    n = wg.shape[1]
    assert m % tm == 0 and n % tn == 0
    return pl.pallas_call(
        _ffn_up_kernel,
        grid=(m // tm, n // tn),
        in_specs=[pl.BlockSpec((tm, k), lambda i, j: (i, 0)),
                  pl.BlockSpec((1, k), lambda i, j: (0, 0)),
                  pl.BlockSpec((k, tn), lambda i, j: (0, j)),
                  pl.BlockSpec((k, tn), lambda i, j: (0, j))],
        out_specs=pl.BlockSpec((tm, tn), lambda i, j: (i, j)),
        out_shape=jax.ShapeDtypeStruct((m, n), BF16),
        scratch_shapes=[pltpu.VMEM((tm, k), BF16)],
        compiler_params=_cparams("parallel", "arbitrary"),
        name="ffn_up",
    )(x, nw, wg, wu)


def _mix_kernel(ya_ref, yb_ref, ga_ref, gb_ref, wa_ref, wb_ref, o_ref):
    a = _dot(ya_ref[...], wa_ref[...])
    b = _dot(yb_ref[...], wb_ref[...])
    o_ref[...] = (_sigmoid(ga_ref[...]) * a + _sigmoid(gb_ref[...]) * b).astype(o_ref.dtype)


def _mix(ya, yb, proj, wa, wb, tm, tn):
    m, k = ya.shape
    n = wa.shape[1]
    nb = n // tn
    assert m % tm == 0 and n % tn == 0
    return pl.pallas_call(
        _mix_kernel,
        grid=(m // tm, nb),
        in_specs=[pl.BlockSpec((tm, k), lambda i, j: (i, 0)),
                  pl.BlockSpec((tm, k), lambda i, j: (i, 0)),
                  pl.BlockSpec((tm, tn), lambda i, j: (i, j)),
                  pl.BlockSpec((tm, tn), lambda i, j: (i, nb + j)),
                  pl.BlockSpec((k, tn), lambda i, j: (0, j)),
                  pl.BlockSpec((k, tn), lambda i, j: (0, j))],
        out_specs=pl.BlockSpec((tm, tn), lambda i, j: (i, j)),
        out_shape=jax.ShapeDtypeStruct((m, n), BF16),
        compiler_params=_cparams("parallel", "parallel"),
        name="mix",
    )(ya, yb, proj, proj, wa, wb)


def _split3(x):
    a = x.astype(BF16)
    r = x - a.astype(F32)
    b = r.astype(BF16)
    c = (r - b.astype(F32)).astype(BF16)
    return a, b, c


def _gdn_kernel(q_ref, k_ref, v_ref, z_ref, bd_ref, cwq_ref, cwk_ref, cwv_ref,
                alog_ref, dtb_ref, onw_ref, o_ref, s_scr, tail_scr, *, tb):
    hp = GDN_HEADS_PER_STEP
    C = GDN_CHUNK
    D = HEAD_DIM
    head0 = pl.program_id(1) * hp

    @pl.when(pl.program_id(2) == 0)
    def _():
        s_scr[...] = jnp.zeros_like(s_scr)
        tail_scr[...] = jnp.zeros_like(tail_scr)

    row8 = lax.broadcasted_iota(jnp.int32, (8, hp * D), 0)
    tails = []

    def conv_silu(u_ref, cw_ref, idx):
        u = u_ref[0]
        w = cw_ref[...]
        tail = tail_scr[idx]
        y = None
        ytop = None
        for s in (3, 2, 1):
            sh = pltpu.roll(u, s, axis=0)
            top = jnp.where(row8 < s, pltpu.roll(tail, s, axis=0), sh[0:8])
            wj = w[3 - s:4 - s]
            y = sh * wj if y is None else y + sh * wj
            ytop = top * wj if ytop is None else ytop + top * wj
        y = y + u * w[3:4]
        ytop = ytop + u[0:8] * w[3:4]
        tails.append(u[tb - 8:tb])
        y = jnp.concatenate([ytop, y[8:]], axis=0)
        return _silu(y)

    q_all = conv_silu(q_ref, cwq_ref, 0)
    k_all = conv_silu(k_ref, cwk_ref, 1)
    v_all = conv_silu(v_ref, cwv_ref, 2)

    bd = bd_ref[0]
    lane = lax.broadcasted_iota(jnp.int32, (tb, LANES), 1)
    beta_all = _sigmoid(bd)
    xg = bd + dtb_ref[...]
    softplus = jnp.maximum(xg, 0.0) + jnp.log1p(jnp.exp(-jnp.abs(xg)))
    g_all = -jnp.exp(alog_ref[...]) * softplus
    betas = [jnp.sum(jnp.where(lane == head0 + e, beta_all, 0.0), axis=-1, keepdims=True)
             for e in range(hp)]
    gs = [jnp.sum(jnp.where(lane == head0 + e + GDN_HEADS, g_all, 0.0), axis=-1, keepdims=True)
          for e in range(hp)]

    nc = tb // C
    ri = lax.broadcasted_iota(jnp.int32, (tb, tb), 0)
    ci = lax.broadcasted_iota(jnp.int32, (tb, tb), 1)
    same = (ri // C) == (ci // C)
    tril16 = jnp.where(same & (ri >= ci), 1.0, 0.0).astype(BF16)
    g_b = jnp.concatenate([jnp.broadcast_to(g, (tb, LANES)) for g in gs], axis=1)
    gc_all = sum(_dot(tril16, piece) for piece in _split3(g_b))

    iw = lax.broadcasted_iota(jnp.int32, (C, tb), 0)
    jw = lax.broadcasted_iota(jnp.int32, (C, tb), 1) % C
    incl_w = iw >= jw
    strict_w = iw > jw
    eye_w = jnp.where(iw == jw, 1.0, 0.0).astype(F32)
    low_half = (lax.broadcasted_iota(jnp.int32, (C, LANES), 1) < C)

    def block_diag(wide16):
        return jnp.where(same, jnp.concatenate([wide16] * nc, axis=0), jnp.zeros((), BF16))

    def pair_blocks(full):
        return jnp.where(low_half, full[:C], full[C:])

    E = range(hp)
    pairs = range(tb // LANES)

    def l2n(x):
        return x * lax.rsqrt(jnp.sum(x * x, axis=-1, keepdims=True) + RMS_EPS)

    qs = [l2n(q_all[:, e * D:(e + 1) * D]) * (D ** -0.5) for e in E]
    ks = [l2n(k_all[:, e * D:(e + 1) * D]) for e in E]
    vs = [v_all[:, e * D:(e + 1) * D] for e in E]
    gcs = [gc_all[:, e * D:(e + 1) * D] for e in E]
    g_col = [jnp.concatenate([pair_blocks(gc[p * LANES:(p + 1) * LANES]) for p in pairs], axis=1)
             for gc in gcs]
    g_row = [jnp.concatenate([gc[p * LANES:(p + 1) * LANES, :].T[0:1, :] for p in pairs], axis=1)
             for gc in gcs]
    decay = [jnp.where(incl_w, jnp.exp(jnp.where(incl_w, g_col[e] - g_row[e], 0.0)), 0.0) for e in E]
    k16 = [k.astype(BF16) for k in ks]
    q16 = [q.astype(BF16) for q in qs]
    kb = [ks[e] * betas[e] for e in E]
    kb16 = [x.astype(BF16) for x in kb]

    def pair_products(a16, b16):
        return jnp.concatenate([pair_blocks(_dot_nt(a16[p * LANES:(p + 1) * LANES],
                                                    b16[p * LANES:(p + 1) * LANES])) for p in pairs], axis=1)

    lmat = [jnp.where(strict_w, pair_products(kb16[e], k16[e]) * decay[e], 0.0) for e in E]
    amat = [jnp.where(incl_w, pair_products(q16[e], k16[e]) * decay[e], 0.0) for e in E]
    base = GDN_INV_BASE
    blk_i, blk_j = iw // base, jw // base
    neg_d16 = [jnp.where(blk_i == blk_j, -lmat[e], 0.0).astype(BF16) for e in E]
    tinv = [eye_w + neg_d16[e].astype(F32) for e in E]
    pw = [_dot(neg_d16[e], block_diag(neg_d16[e])) for e in E]
    span = 2
    while 2 * span < base:
        p16 = [pw[e].astype(BF16) for e in E]
        prod = [_dot(jnp.concatenate([p16[e], tinv[e].astype(BF16)], axis=0), block_diag(p16[e])) for e in E]
        pw = [prod[e][:C] for e in E]
        tinv = [tinv[e] + prod[e][C:] for e in E]
        span *= 2
    tinv = [tinv[e] + _dot(tinv[e].astype(BF16), block_diag(pw[e].astype(BF16))) for e in E]
    s = base
    while s < C:
        below_left = (iw // (2 * s) == jw // (2 * s)) & ((iw // s) % 2 == 1) & ((jw // s) % 2 == 0)
        c16 = [jnp.where(below_left, lmat[e], 0.0).astype(BF16) for e in E]
        cx = [_dot(c16[e], block_diag(tinv[e].astype(BF16))) for e in E]
        tinv = [tinv[e] - _dot(tinv[e].astype(BF16), block_diag(cx[e].astype(BF16))) for e in E]
        s *= 2
    eg = [jnp.exp(gc) for gc in gcs]
    rhs = [jnp.concatenate([kb[e] * eg[e], vs[e] * betas[e]], axis=1).astype(BF16) for e in E]
    wu = [_dot(block_diag(tinv[e].astype(BF16)), rhs[e]).astype(BF16) for e in E]
    au = [_dot(block_diag(amat[e].astype(BF16)), wu[e]) for e in E]
    q_eff = [(qs[e] * eg[e] - au[e][:, :D]).astype(BF16) for e in E]
    gl = [jnp.concatenate([jnp.broadcast_to(gc[c * C + C - 1:c * C + C, :], (C, LANES))
                           for c in range(nc)], axis=0) for gc in gcs]
    k_dec = [(ks[e] * jnp.exp(gl[e] - gcs[e])).astype(BF16) for e in E]
    pn = [[_dot_tn(k_dec[e][c * C:(c + 1) * C], wu[e][c * C:(c + 1) * C]) for c in range(nc)]
          for e in E]

    states = [s_scr[e] for e in E]
    onw = onw_ref[...]
    for c in range(nc):
        r0 = c * C
        lhs = [jnp.concatenate([pn[e][c][:, :D].astype(BF16), q_eff[e][r0:r0 + C]], axis=0) for e in E]
        res = [_dot(lhs[e], states[e].astype(BF16)) for e in E]
        outs = [_rms_rows(res[e][D:] + au[e][r0:r0 + C, D:], onw) for e in E]
        states = [states[e] * jnp.exp(gl[e][r0:r0 + 1, :]) - res[e][:D] + pn[e][c][:, D:] for e in E]
        zc = z_ref[0, r0:r0 + C, :]
        o_ref[0, r0:r0 + C, :] = (jnp.concatenate(outs, axis=1) * _silu(zc)).astype(o_ref.dtype)
    for e in range(hp):
        s_scr[e] = states[e]
    for idx in range(3):
        tail_scr[idx] = tails[idx]


def _gdn(proj3, conv_w, alog_row, dtb_row, onw, col0, bd_col, tb):
    b, t, _ = proj3.shape
    hp = GDN_HEADS_PER_STEP
    ng = GDN_HEADS // hp
    w = hp * HEAD_DIM

    def col(base):
        return pl.BlockSpec((1, tb, w), lambda bi, hi, ti: (bi, ti, base // hp + hi))

    def cw(base):
        return pl.BlockSpec((GDN_CONV, w), lambda bi, hi, ti: (0, base // hp + hi))

    assert col0 % hp == 0
    row = pl.BlockSpec((1, LANES), lambda bi, hi, ti: (0, 0))
    return pl.pallas_call(
        functools.partial(_gdn_kernel, tb=tb),
        grid=(b, ng, t // tb),
        in_specs=[col(col0), col(col0 + GDN_HEADS), col(col0 + 2 * GDN_HEADS), col(col0 + 3 * GDN_HEADS),
                  pl.BlockSpec((1, tb, LANES), lambda bi, hi, ti: (bi, ti, bd_col)),
                  cw(0), cw(GDN_HEADS), cw(2 * GDN_HEADS), row, row, row],
        out_specs=pl.BlockSpec((1, tb, w), lambda bi, hi, ti: (bi, ti, hi)),
        out_shape=jax.ShapeDtypeStruct((b, t, GDN_HEADS * HEAD_DIM), BF16),
        scratch_shapes=[pltpu.VMEM((hp, HEAD_DIM, HEAD_DIM), F32),
                        pltpu.VMEM((3, 8, w), F32)],
        compiler_params=_cparams("parallel", "parallel", "arbitrary"),
        name="gdn",
    )(proj3, proj3, proj3, proj3, proj3, conv_w, conv_w, conv_w, alog_row, dtb_row, onw)


def _bias_kernel(rel_ref, o_ref):
    h = pl.program_id(0)
    bs = MOBA_BLOCK
    max_exact = REL_BUCKETS // 2
    x = lax.broadcasted_iota(jnp.int32, (8, 2 * bs), 1)
    for d in range(BIAS_TILES):
        dist = jnp.maximum(d * bs + x - bs, 0)
        df = dist.astype(F32)
        log_ratio = jnp.log(jnp.maximum(df, float(max_exact)) / max_exact) / math.log(REL_MAX_DIST / max_exact)
        large = max_exact + (log_ratio * (REL_BUCKETS - max_exact)).astype(jnp.int32)
        large = jnp.minimum(large, REL_BUCKETS - 1)
        bucket = jnp.where(dist < max_exact, dist, large)
        row = jnp.zeros((8, 2 * bs), F32)
        for b in range(REL_BUCKETS):
            row = jnp.where(bucket == b, rel_ref[b, h], row)
        base = jnp.broadcast_to(row[0:1, :] * LOG2E, (bs, 2 * bs))
        tile = pltpu.roll(base, 0, 1, stride=1, stride_axis=0)[:, bs:]
        if d == 0:
            kk = lax.broadcasted_iota(jnp.int32, (bs, bs), 0)
            qq = lax.broadcasted_iota(jnp.int32, (bs, bs), 1)
            tile = jnp.where(kk <= qq, tile, NEG_INF)
        o_ref[0, d] = tile


def _bias_tiles(rel_bias):
    bs = MOBA_BLOCK
    max_exact = REL_BUCKETS // 2
    nearest = (BIAS_TILES - 1) * bs - (bs - 1)
    assert max_exact + int(math.log(nearest / max_exact) / math.log(REL_MAX_DIST / max_exact)
                           * (REL_BUCKETS - max_exact)) >= REL_BUCKETS - 1
    return pl.pallas_call(
        _bias_kernel,
        grid=(MOBA_HEADS,),
        in_specs=[pl.BlockSpec(memory_space=pltpu.SMEM)],
        out_specs=pl.BlockSpec((1, BIAS_TILES, bs, bs), lambda h: (h, 0, 0, 0)),
        out_shape=jax.ShapeDtypeStruct((MOBA_HEADS, BIAS_TILES, bs, bs), F32),
        compiler_params=_cparams("parallel"),
        name="bias",
    )(rel_bias)


def _moba_prep_kernel(q_ref, k_ref, v_ref, qw_ref, kw_ref, rel_ref, q16_ref, kn_ref, vt_ref, add_ref, km_scr,
                      *, rows, nb):
    bs = MOBA_BLOCK
    nsub = rows // bs
    ti = pl.program_id(2)

    @pl.when(ti == 0)
    def _():
        km_scr[...] = jnp.zeros_like(km_scr)

    kn = _rms_rows(k_ref[0], kw_ref[...])
    kn_ref[0, 0] = kn.astype(BF16)
    for s in range(nsub):
        km_scr[pl.ds(ti * nsub + s, 1), :] = jnp.mean(kn[s * bs:(s + 1) * bs], axis=0, keepdims=True)
        vt_ref[0, 0, 0:HEAD_DIM, s * bs:(s + 1) * bs] = v_ref[0, s * bs:(s + 1) * bs, :].T.astype(BF16)
    pad_row = lax.broadcasted_iota(jnp.int32, (MOBA_V_ROWS - HEAD_DIM, rows), 0)
    vt_ref[0, 0, HEAD_DIM:, :] = jnp.where(pad_row == 0, 1.0, 0.0).astype(BF16)

    q = _rms_rows(q_ref[0], qw_ref[...])
    q16_ref[0, 0] = (q * ((HEAD_DIM ** -0.5) * LOG2E)).astype(BF16)

    blk = lax.broadcasted_iota(jnp.int32, (nb, rows), 0)
    own_blk = ti * nsub + lax.broadcasted_iota(jnp.int32, (nb, rows), 1) // bs
    blkf = blk.astype(F32)
    valid = blk < own_blk
    work = jnp.where(valid, _dot_nt(km_scr[...], q, HIGHEST), NEG_INF)
    picked = jnp.zeros((nb, rows), F32)
    for _ in range(MOBA_TOPK):
        best = jnp.max(work, axis=0, keepdims=True)
        first = jnp.min(jnp.where(work == best, blkf, float(nb)), axis=0, keepdims=True)
        hit = blkf == first
        picked = jnp.where(hit, 1.0, picked)
        work = jnp.where(hit, -jnp.inf, work)
    far_bias = rel_ref[REL_BUCKETS - 1, pl.program_id(1)] * LOG2E
    attended = (valid & (picked > 0.0)) | (blk == own_blk)
    add_ref[0, 0] = jnp.where(attended, far_bias, NEG_INF)


def _moba_prep(proj3, qw, kw, rel_bias, col0, rows):
    b, t, _ = proj3.shape
    hh, bs, dh = MOBA_HEADS, MOBA_BLOCK, HEAD_DIM
    nb = t // bs

    def col(base):
        return pl.BlockSpec((1, rows, LANES), lambda bi, hi, ti: (bi, ti, base + hi))

    row = pl.BlockSpec((1, LANES), lambda bi, hi, ti: (0, 0))
    return pl.pallas_call(
        functools.partial(_moba_prep_kernel, rows=rows, nb=nb),
        grid=(b, hh, t // rows),
        in_specs=[col(col0), col(col0 + hh), col(col0 + 2 * hh), row, row,
                  pl.BlockSpec(memory_space=pltpu.SMEM)],
        out_specs=[pl.BlockSpec((1, 1, rows, dh), lambda bi, hi, ti: (bi, hi, ti, 0)),
                   pl.BlockSpec((1, 1, rows, dh), lambda bi, hi, ti: (bi, hi, ti, 0)),
                   pl.BlockSpec((1, 1, MOBA_V_ROWS, rows), lambda bi, hi, ti: (bi, hi, 0, ti)),
                   pl.BlockSpec((1, 1, nb, rows), lambda bi, hi, ti: (bi, hi, 0, ti))],
        out_shape=[jax.ShapeDtypeStruct((b, hh, t, dh), BF16),
                   jax.ShapeDtypeStruct((b, hh, t, dh), BF16),
                   jax.ShapeDtypeStruct((b, hh, MOBA_V_ROWS, t), BF16),
                   jax.ShapeDtypeStruct((b, hh, nb, t), F32)],
        scratch_shapes=[pltpu.VMEM((nb, dh), F32)],
        compiler_params=_cparams("parallel", "parallel", "arbitrary"),
        name="moba_prep",
    )(proj3, proj3, proj3, qw, kw, rel_bias)


def _moba_kernel(q_ref, kn_ref, vt_ref, add_ref, bias_ref, o_ref, buf_a, buf_b, *, nb):
    i = pl.program_id(2)
    bs = MOBA_BLOCK
    ch = MOBA_TILES_PER_STEP
    E = range(MOBA_STREAMS)
    q16 = [q_ref[e, 0] for e in E]

    crows = ch * bs
    n_pairs = (i + 2 * ch) // (2 * ch)
    last_chunk = nb // ch - 1

    def chunk_scores(buf, e, c, far):
        r = pl.multiple_of(c * crows, crows)
        raw = _dot_nt(kn_ref[e, 0, pl.ds(r, crows), :], q16[e])
        mx = None
        for t in range(ch):
            j = c * ch + t
            add = add_ref[e, 0, pl.ds(j, 1), :]
            if far:
                st = raw[t * bs:(t + 1) * bs, :] + add
            else:
                d = jnp.clip(i - j, 0, BIAS_TILES - 1)
                st = jnp.where(add > 0.5 * NEG_INF, raw[t * bs:(t + 1) * bs, :] + bias_ref[0, d], NEG_INF)
            buf[e, t * bs:(t + 1) * bs, :] = st
            tmx = jnp.max(st, axis=0, keepdims=True)
            mx = tmx if mx is None else jnp.maximum(mx, tmx)
        return mx

    def absorb(buf, e, c, mx, m, acc):
        r = pl.multiple_of(c * crows, crows)
        m_new = jnp.maximum(m, mx)
        p = jnp.exp2(buf[e] - m_new)
        acc = jnp.exp2(m - m_new) * acc + _dot(vt_ref[e, 0, :, pl.ds(r, crows)], p.astype(BF16))
        return m_new, acc

    def pair(far, c, carry):
        m, acc, mx_a = (list(x) for x in carry)
        mx_b = [chunk_scores(buf_b, e, 2 * c + 1, far) for e in E]
        for e in E:
            m[e], acc[e] = absorb(buf_a, e, 2 * c, mx_a[e], m[e], acc[e])
        mx_a = [chunk_scores(buf_a, e, jnp.minimum(2 * c + 2, last_chunk), far) for e in E]
        for e in E:
            m[e], acc[e] = absorb(buf_b, e, 2 * c + 1, mx_b[e], m[e], acc[e])
        return tuple(m), tuple(acc), tuple(mx_a)

    def step(c, carry):
        all_far = (2 * c + 3) * ch - 1 + (BIAS_TILES - 1) <= i
        return lax.cond(all_far, functools.partial(pair, True, c), functools.partial(pair, False, c), carry)

    m0 = tuple(jnp.full((1, bs), MOBA_M_INIT, F32) for e in E)
    acc0 = tuple(jnp.zeros((MOBA_V_ROWS, bs), F32) for e in E)
    mx_a = tuple(chunk_scores(buf_a, e, 0, False) for e in E)
    m, acc, _ = lax.fori_loop(0, n_pairs, step, (m0, acc0, mx_a))
    for e in E:
        o_ref[e] = (acc[e][:HEAD_DIM] / acc[e][HEAD_DIM:HEAD_DIM + 1]).T.astype(o_ref.dtype)


def _moba(q16, kn, vt, add, bias):
    b, hh, t, dh = kn.shape
    bs = MOBA_BLOCK
    nb = t // bs
    ns = MOBA_STREAMS
    crows = MOBA_TILES_PER_STEP * bs
    assert b % ns == 0 and nb % (2 * MOBA_TILES_PER_STEP) == 0
    return pl.pallas_call(
        functools.partial(_moba_kernel, nb=nb),
        grid=(b // ns, hh, nb),
        in_specs=[pl.BlockSpec((ns, 1, bs, dh), lambda bi, hi, ti: (bi, hi, ti, 0)),
                  pl.BlockSpec((ns, 1, t, dh), lambda bi, hi, ti: (bi, hi, 0, 0)),
                  pl.BlockSpec((ns, 1, MOBA_V_ROWS, t), lambda bi, hi, ti: (bi, hi, 0, 0)),
                  pl.BlockSpec((ns, 1, nb, bs), lambda bi, hi, ti: (bi, hi, 0, ti)),
                  pl.BlockSpec((1, BIAS_TILES, bs, bs), lambda bi, hi, ti: (hi, 0, 0, 0))],
        out_specs=pl.BlockSpec((ns, bs, dh), lambda bi, hi, ti: (bi, ti, hi)),
        out_shape=jax.ShapeDtypeStruct((b, t, hh * dh), BF16),
        scratch_shapes=[pltpu.VMEM((ns, crows, bs), F32),
                        pltpu.VMEM((ns, crows, bs), F32)],
        compiler_params=_cparams("parallel", "parallel", "parallel"),
        name="moba",
    )(q16, kn, vt, add, bias)


def _tile(n, want):
    t = min(n, want)
    assert n % t == 0, (n, want)
    return t


def kernel(x, norm_mix_w, w_in, conv_w, a_log, dt_bias, gdn_o_norm_w, q_norm_w, k_norm_w, rel_bias,
           w_branch_gdn, w_branch_moba, w_out, norm_ffn_w, w_ffn_gate, w_ffn_up, w_ffn_down):
    b, t, d = x.shape
    m = b * t
    gw = GDN_HEADS * HEAD_DIM
    mw = MOBA_HEADS * HEAD_DIM
    assert t % MOBA_PREP_ROWS == 0 and w_in.shape[0] == 1
    assert w_in.shape[2] == 4 * gw + 2 * GDN_HEADS + 3 * mw + 2 * d

    o_bd = 4 * gw
    o_moba = o_bd + 2 * GDN_HEADS
    o_gate = o_moba + 3 * mw
    n_main = 2 * d + 4 * gw + 3 * mw
    tn_proj = PROJ_TILE[1]
    n_proj = -(-(n_main + LANES) // tn_proj) * tn_proj
    w_proj = _regroup_weights(w_in, (o_bd, o_moba, o_gate), n_proj)
    col_gdn = (2 * d) // LANES
    col_moba = col_gdn + 4 * GDN_HEADS
    col_bd = col_moba + 3 * MOBA_HEADS

    x2 = x.reshape(m, d)
    proj = _norm_matmul(x2, norm_mix_w, w_proj, _tile(m, PROJ_TILE[0]), tn_proj, F32, "proj")
    proj3 = proj.reshape(b, t, n_proj)

    lane_pad = jnp.zeros((1, LANES - 2 * GDN_HEADS), F32)
    head_pad = jnp.zeros((1, GDN_HEADS), F32)
    alog_row = jnp.concatenate([head_pad, a_log, lane_pad], axis=1)
    dtb_row = jnp.concatenate([head_pad, dt_bias, lane_pad], axis=1)
    y_a = _gdn(proj3, conv_w[0], alog_row, dtb_row, gdn_o_norm_w, col_gdn, col_bd, _tile(t, GDN_BLOCK))

    bias = _bias_tiles(rel_bias)
    q16, kn, vt, add = _moba_prep(proj3, q_norm_w, k_norm_w, rel_bias, col_moba, MOBA_PREP_ROWS)
    y_b = _moba(q16, kn, vt, add, bias)

    mix = _mix(y_a.reshape(m, gw), y_b.reshape(m, mw), proj,
               w_branch_gdn[0].astype(BF16), w_branch_moba[0].astype(BF16), _tile(m, MIX_ROWS), d)
    h1 = _matmul_res(mix, w_out[0].astype(BF16), x2, _tile(m, OUT_ROWS), d, "out")

    hid = _ffn_up(h1, norm_ffn_w, w_ffn_gate[0].astype(BF16), w_ffn_up[0].astype(BF16),
                  _tile(m, FFN_UP_TILE[0]), FFN_UP_TILE[1])
    h2 = _matmul_res(hid, w_ffn_down[0].astype(BF16), h1, _tile(m, FFN_DOWN_TILE[0]), FFN_DOWN_TILE[1],
                     "ffn_down")
    return h2.reshape(b, t, d)
```

```python
import functools
import math

import jax
import jax.numpy as jnp
from jax import lax
from jax.experimental import pallas as pl
from jax.experimental.pallas import tpu as pltpu

F32 = jnp.float32
BF16 = jnp.bfloat16
HIGHEST = lax.Precision.HIGHEST

LANES = 128
HEAD_DIM = 128
GDN_HEADS = 8
GDN_CONV = 4
GDN_CHUNK = 64
GDN_INV_BASE = 8
GDN_HEADS_PER_STEP = 8
NEG_INF = -1e30
MOBA_HEADS = 8
MOBA_BLOCK = 256
MOBA_TOPK = 3
REL_BUCKETS = 32
REL_MAX_DIST = 2048
BIAS_TILES = 8
MOBA_TILES_PER_STEP = 4
MOBA_PREP_ROWS = 4096
MOBA_STREAMS = 4
MOBA_V_ROWS = HEAD_DIM + 16
MOBA_M_INIT = 0.1 * NEG_INF
LOG2E = math.log2(math.e)
RMS_EPS = 1e-6
V7X_VMEM_BYTES = 64 * 1024 * 1024
VMEM_LIMIT = V7X_VMEM_BYTES * 7 // 8

PROJ_TILE = (1024, 1280)
MIX_ROWS = 512
OUT_ROWS = 512
FFN_UP_TILE = (1024, 512)
FFN_DOWN_TILE = (1024, 512)
GDN_BLOCK = 256
REGROUP_ROWS = 128


def _cparams(*sem):
    return pltpu.CompilerParams(dimension_semantics=sem, vmem_limit_bytes=VMEM_LIMIT)


def _sigmoid(x):
    return 0.5 + 0.5 * jnp.tanh(0.5 * x)


def _silu(x):
    h = 0.5 * x
    return h + h * jnp.tanh(h)


def _dot(a, b, precision=None):
    return jnp.dot(a, b, preferred_element_type=F32, precision=precision)


def _dot_nt(a, b, precision=None):
    return lax.dot_general(a, b, (((1,), (1,)), ((), ())),
                           preferred_element_type=F32, precision=precision)


def _dot_tn(a, b):
    return lax.dot_general(a, b, (((0,), (0,)), ((), ())), preferred_element_type=F32)


def _rms_rows(x, w):
    return x * lax.rsqrt(jnp.mean(x * x, axis=-1, keepdims=True) + RMS_EPS) * w


def _regroup_kernel(w_ref, o_ref, *, cuts):
    o_bd, o_moba, o_gate = cuts
    x = w_ref[...]
    parts = [x[:, o_gate:], x[:, :o_bd], x[:, o_moba:o_gate], x[:, o_bd:o_moba]]
    used = sum(p.shape[1] for p in parts)
    parts.append(jnp.zeros((x.shape[0], o_ref.shape[1] - used), x.dtype))
    o_ref[...] = jnp.concatenate(parts, axis=1).astype(o_ref.dtype)


def _regroup_weights(w, cuts, n_out):
    _, k, n_in = w.shape
    rows = _tile(k, REGROUP_ROWS)
    return pl.pallas_call(
        functools.partial(_regroup_kernel, cuts=cuts),
        grid=(k // rows,),
        in_specs=[pl.BlockSpec((None, rows, n_in), lambda i: (0, i, 0))],
        out_specs=pl.BlockSpec((rows, n_out), lambda i: (i, 0)),
        out_shape=jax.ShapeDtypeStruct((k, n_out), BF16),
        compiler_params=_cparams("parallel"),
        name="regroup",
    )(w)


def _norm_matmul_kernel(x_ref, nw_ref, w_ref, o_ref, u_scr):
    @pl.when(pl.program_id(1) == 0)
    def _():
        u_scr[...] = _rms_rows(x_ref[...], nw_ref[...]).astype(BF16)

    o_ref[...] = _dot(u_scr[...], w_ref[...]).astype(o_ref.dtype)


def _norm_matmul(x, nw, w, tm, tn, out_dtype, name):
    m, k = x.shape
    n = w.shape[1]
    assert m % tm == 0 and n % tn == 0
    return pl.pallas_call(
        _norm_matmul_kernel,
        grid=(m // tm, n // tn),
        in_specs=[pl.BlockSpec((tm, k), lambda i, j: (i, 0)),
                  pl.BlockSpec((1, k), lambda i, j: (0, 0)),
                  pl.BlockSpec((k, tn), lambda i, j: (0, j))],
        out_specs=pl.BlockSpec((tm, tn), lambda i, j: (i, j)),
        out_shape=jax.ShapeDtypeStruct((m, n), out_dtype),
        scratch_shapes=[pltpu.VMEM((tm, k), BF16)],
        compiler_params=_cparams("parallel", "arbitrary"),
        name=name,
    )(x, nw, w)


def _matmul_res_kernel(a_ref, w_ref, r_ref, o_ref):
    o_ref[...] = r_ref[...] + _dot(a_ref[...], w_ref[...])


def _matmul_res(a, w, res, tm, tn, name):
    m, k = a.shape
    n = w.shape[1]
    assert m % tm == 0 and n % tn == 0
    return pl.pallas_call(
        _matmul_res_kernel,
        grid=(m // tm, n // tn),
        in_specs=[pl.BlockSpec((tm, k), lambda i, j: (i, 0)),
                  pl.BlockSpec((k, tn), lambda i, j: (0, j)),
                  pl.BlockSpec((tm, tn), lambda i, j: (i, j))],
        out_specs=pl.BlockSpec((tm, tn), lambda i, j: (i, j)),
        out_shape=jax.ShapeDtypeStruct((m, n), F32),
        compiler_params=_cparams("parallel", "parallel"),
        name=name,
    )(a, w, res)


def _ffn_up_kernel(x_ref, nw_ref, wg_ref, wu_ref, o_ref, u_scr):
    @pl.when(pl.program_id(1) == 0)
    def _():
        u_scr[...] = _rms_rows(x_ref[...], nw_ref[...]).astype(BF16)

    u = u_scr[...]
    g = _dot(u, wg_ref[...])
    o_ref[...] = (_silu(g) * _dot(u, wu_ref[...])).astype(o_ref.dtype)


def _ffn_up(x, nw, wg, wu, tm, tn):
    m, k = x.shape
    n = wg.shape[1]
    assert m % tm == 0 and n % tn == 0
    return pl.pallas_call(
        _ffn_up_kernel,
        grid=(m // tm, n // tn),
        in_specs=[pl.BlockSpec((tm, k), lambda i, j: (i, 0)),
                  pl.BlockSpec((1, k), lambda i, j: (0, 0)),
                  pl.BlockSpec((k, tn), lambda i, j: (0, j)),
                  pl.BlockSpec((k, tn), lambda i, j: (0, j))],
        out_specs=pl.BlockSpec((tm, tn), lambda i, j: (i, j)),
        out_shape=jax.ShapeDtypeStruct((m, n), BF16),
        scratch_shapes=[pltpu.VMEM((tm, k), BF16)],
        compiler_params=_cparams("parallel", "arbitrary"),
        name="ffn_up",
    )(x, nw, wg, wu)


def _mix_kernel(ya_ref, yb_ref, ga_ref, gb_ref, wa_ref, wb_ref, o_ref):
    a = _dot(ya_ref[...], wa_ref[...])
    b = _dot(yb_ref[...], wb_ref[...])
    o_ref[...] = (_sigmoid(ga_ref[...]) * a + _sigmoid(gb_ref[...]) * b).astype(o_ref.dtype)


def _mix(ya, yb, proj, wa, wb, tm, tn):
    m, k = ya.shape
    n = wa.shape[1]
    nb = n // tn
    assert m % tm == 0 and n % tn == 0
    return pl.pallas_call(
        _mix_kernel,
        grid=(m // tm, nb),
        in_specs=[pl.BlockSpec((tm, k), lambda i, j: (i, 0)),
                  pl.BlockSpec((tm, k), lambda i, j: (i, 0)),
                  pl.BlockSpec((tm, tn), lambda i, j: (i, j)),
                  pl.BlockSpec((tm, tn), lambda i, j: (i, nb + j)),
                  pl.BlockSpec((k, tn), lambda i, j: (0, j)),
                  pl.BlockSpec((k, tn), lambda i, j: (0, j))],
        out_specs=pl.BlockSpec((tm, tn), lambda i, j: (i, j)),
        out_shape=jax.ShapeDtypeStruct((m, n), BF16),
        compiler_params=_cparams("parallel", "parallel"),
        name="mix",
    )(ya, yb, proj, proj, wa, wb)


def _split3(x):
    a = x.astype(BF16)
    r = x - a.astype(F32)
    b = r.astype(BF16)
    c = (r - b.astype(F32)).astype(BF16)
    return a, b, c


def _gdn_kernel(q_ref, k_ref, v_ref, z_ref, bd_ref, cwq_ref, cwk_ref, cwv_ref,
                alog_ref, dtb_ref, onw_ref, o_ref, s_scr, tail_scr, *, tb):
    hp = GDN_HEADS_PER_STEP
    C = GDN_CHUNK
    D = HEAD_DIM
    head0 = pl.program_id(1) * hp

    @pl.when(pl.program_id(2) == 0)
    def _():
        s_scr[...] = jnp.zeros_like(s_scr)
        tail_scr[...] = jnp.zeros_like(tail_scr)

    row8 = lax.broadcasted_iota(jnp.int32, (8, hp * D), 0)
    tails = []

    def conv_silu(u_ref, cw_ref, idx):
        u = u_ref[0]
        w = cw_ref[...]
        tail = tail_scr[idx]
        y = None
        ytop = None
        for s in (3, 2, 1):
            sh = pltpu.roll(u, s, axis=0)
            top = jnp.where(row8 < s, pltpu.roll(tail, s, axis=0), sh[0:8])
            wj = w[3 - s:4 - s]
            y = sh * wj if y is None else y + sh * wj
            ytop = top * wj if ytop is None else ytop + top * wj
        y = y + u * w[3:4]
        ytop = ytop + u[0:8] * w[3:4]
        tails.append(u[tb - 8:tb])
        y = jnp.concatenate([ytop, y[8:]], axis=0)
        return _silu(y)

    q_all = conv_silu(q_ref, cwq_ref, 0)
    k_all = conv_silu(k_ref, cwk_ref, 1)
    v_all = conv_silu(v_ref, cwv_ref, 2)

    bd = bd_ref[0]
    lane = lax.broadcasted_iota(jnp.int32, (tb, LANES), 1)
    beta_all = _sigmoid(bd)
    xg = bd + dtb_ref[...]
    softplus = jnp.maximum(xg, 0.0) + jnp.log1p(jnp.exp(-jnp.abs(xg)))
    g_all = -jnp.exp(alog_ref[...]) * softplus
    betas = [jnp.sum(jnp.where(lane == head0 + e, beta_all, 0.0), axis=-1, keepdims=True)
             for e in range(hp)]
    gs = [jnp.sum(jnp.where(lane == head0 + e + GDN_HEADS, g_all, 0.0), axis=-1, keepdims=True)
          for e in range(hp)]

    nc = tb // C
    ri = lax.broadcasted_iota(jnp.int32, (tb, tb), 0)
    ci = lax.broadcasted_iota(jnp.int32, (tb, tb), 1)
    same = (ri // C) == (ci // C)
    tril16 = jnp.where(same & (ri >= ci), 1.0, 0.0).astype(BF16)
    g_b = jnp.concatenate([jnp.broadcast_to(g, (tb, LANES)) for g in gs], axis=1)
    gc_all = sum(_dot(tril16, piece) for piece in _split3(g_b))

    iw = lax.broadcasted_iota(jnp.int32, (C, tb), 0)
    jw = lax.broadcasted_iota(jnp.int32, (C, tb), 1) % C
    incl_w = iw >= jw
    strict_w = iw > jw
    eye_w = jnp.where(iw == jw, 1.0, 0.0).astype(F32)
    low_half = (lax.broadcasted_iota(jnp.int32, (C, LANES), 1) < C)

    def block_diag(wide16):
        return jnp.where(same, jnp.concatenate([wide16] * nc, axis=0), jnp.zeros((), BF16))

    def pair_blocks(full):
        return jnp.where(low_half, full[:C], full[C:])

    E = range(hp)
    pairs = range(tb // LANES)

    def l2n(x):
        return x * lax.rsqrt(jnp.sum(x * x, axis=-1, keepdims=True) + RMS_EPS)

    qs = [l2n(q_all[:, e * D:(e + 1) * D]) * (D ** -0.5) for e in E]
    ks = [l2n(k_all[:, e * D:(e + 1) * D]) for e in E]
    vs = [v_all[:, e * D:(e + 1) * D] for e in E]
    gcs = [gc_all[:, e * D:(e + 1) * D] for e in E]
    g_col = [jnp.concatenate([pair_blocks(gc[p * LANES:(p + 1) * LANES]) for p in pairs], axis=1)
             for gc in gcs]
    g_row = [jnp.concatenate([gc[p * LANES:(p + 1) * LANES, :].T[0:1, :] for p in pairs], axis=1)
             for gc in gcs]
    decay = [jnp.where(incl_w, jnp.exp(jnp.where(incl_w, g_col[e] - g_row[e], 0.0)), 0.0) for e in E]
    k16 = [k.astype(BF16) for k in ks]
    q16 = [q.astype(BF16) for q in qs]
    kb = [ks[e] * betas[e] for e in E]
    kb16 = [x.astype(BF16) for x in kb]

    def pair_products(a16, b16):
        return jnp.concatenate([pair_blocks(_dot_nt(a16[p * LANES:(p + 1) * LANES],
                                                    b16[p * LANES:(p + 1) * LANES])) for p in pairs], axis=1)

    lmat = [jnp.where(strict_w, pair_products(kb16[e], k16[e]) * decay[e], 0.0) for e in E]
    amat = [jnp.where(incl_w, pair_products(q16[e], k16[e]) * decay[e], 0.0) for e in E]
    base = GDN_INV_BASE
    blk_i, blk_j = iw // base, jw // base
    neg_d16 = [jnp.where(blk_i == blk_j, -lmat[e], 0.0).astype(BF16) for e in E]
    tinv = [eye_w + neg_d16[e].astype(F32) for e in E]
    pw = [_dot(neg_d16[e], block_diag(neg_d16[e])) for e in E]
    span = 2
    while 2 * span < base:
        p16 = [pw[e].astype(BF16) for e in E]
        prod = [_dot(jnp.concatenate([p16[e], tinv[e].astype(BF16)], axis=0), block_diag(p16[e])) for e in E]
        pw = [prod[e][:C] for e in E]
        tinv = [tinv[e] + prod[e][C:] for e in E]
        span *= 2
    tinv = [tinv[e] + _dot(tinv[e].astype(BF16), block_diag(pw[e].astype(BF16))) for e in E]
    s = base
    while s < C:
        below_left = (iw // (2 * s) == jw // (2 * s)) & ((iw // s) % 2 == 1) & ((jw // s) % 2 == 0)
        c16 = [jnp.where(below_left, lmat[e], 0.0).astype(BF16) for e in E]
        cx = [_dot(c16[e], block_diag(tinv[e].astype(BF16))) for e in E]
        tinv = [tinv[e] - _dot(tinv[e].astype(BF16), block_diag(cx[e].astype(BF16))) for e in E]
        s *= 2
    eg = [jnp.exp(gc) for gc in gcs]
    rhs = [jnp.concatenate([kb[e] * eg[e], vs[e] * betas[e]], axis=1).astype(BF16) for e in E]
    wu = [_dot(block_diag(tinv[e].astype(BF16)), rhs[e]).astype(BF16) for e in E]
    au = [_dot(block_diag(amat[e].astype(BF16)), wu[e]) for e in E]
    q_eff = [(qs[e] * eg[e] - au[e][:, :D]).astype(BF16) for e in E]
    gl = [jnp.concatenate([jnp.broadcast_to(gc[c * C + C - 1:c * C + C, :], (C, LANES))
                           for c in range(nc)], axis=0) for gc in gcs]
    k_dec = [(ks[e] * jnp.exp(gl[e] - gcs[e])).astype(BF16) for e in E]
    pn = [[_dot_tn(k_dec[e][c * C:(c + 1) * C], wu[e][c * C:(c + 1) * C]) for c in range(nc)]
          for e in E]

    states = [s_scr[e] for e in E]
    onw = onw_ref[...]
    for c in range(nc):
        r0 = c * C
        lhs = [jnp.concatenate([pn[e][c][:, :D].astype(BF16), q_eff[e][r0:r0 + C]], axis=0) for e in E]
        res = [_dot(lhs[e], states[e].astype(BF16)) for e in E]
        outs = [_rms_rows(res[e][D:] + au[e][r0:r0 + C, D:], onw) for e in E]
        states = [states[e] * jnp.exp(gl[e][r0:r0 + 1, :]) - res[e][:D] + pn[e][c][:, D:] for e in E]
        zc = z_ref[0, r0:r0 + C, :]
        o_ref[0, r0:r0 + C, :] = (jnp.concatenate(outs, axis=1) * _silu(zc)).astype(o_ref.dtype)
    for e in range(hp):
        s_scr[e] = states[e]
    for idx in range(3):
        tail_scr[idx] = tails[idx]


def _gdn(proj3, conv_w, alog_row, dtb_row, onw, col0, bd_col, tb):
    b, t, _ = proj3.shape
    hp = GDN_HEADS_PER_STEP
    ng = GDN_HEADS // hp
    w = hp * HEAD_DIM

    def col(base):
        return pl.BlockSpec((1, tb, w), lambda bi, hi, ti: (bi, ti, base // hp + hi))

    def cw(base):
        return pl.BlockSpec((GDN_CONV, w), lambda bi, hi, ti: (0, base // hp + hi))

    assert col0 % hp == 0
    row = pl.BlockSpec((1, LANES), lambda bi, hi, ti: (0, 0))
    return pl.pallas_call(
        functools.partial(_gdn_kernel, tb=tb),
        grid=(b, ng, t // tb),
        in_specs=[col(col0), col(col0 + GDN_HEADS), col(col0 + 2 * GDN_HEADS), col(col0 + 3 * GDN_HEADS),
                  pl.BlockSpec((1, tb, LANES), lambda bi, hi, ti: (bi, ti, bd_col)),
                  cw(0), cw(GDN_HEADS), cw(2 * GDN_HEADS), row, row, row],
        out_specs=pl.BlockSpec((1, tb, w), lambda bi, hi, ti: (bi, ti, hi)),
        out_shape=jax.ShapeDtypeStruct((b, t, GDN_HEADS * HEAD_DIM), BF16),
        scratch_shapes=[pltpu.VMEM((hp, HEAD_DIM, HEAD_DIM), F32),
                        pltpu.VMEM((3, 8, w), F32)],
        compiler_params=_cparams("parallel", "parallel", "arbitrary"),
        name="gdn",
    )(proj3, proj3, proj3, proj3, proj3, conv_w, conv_w, conv_w, alog_row, dtb_row, onw)


def _bias_kernel(rel_ref, o_ref):
    h = pl.program_id(0)
    bs = MOBA_BLOCK
    max_exact = REL_BUCKETS // 2
    x = lax.broadcasted_iota(jnp.int32, (8, 2 * bs), 1)
    for d in range(BIAS_TILES):
        dist = jnp.maximum(d * bs + x - bs, 0)
        df = dist.astype(F32)
        log_ratio = jnp.log(jnp.maximum(df, float(max_exact)) / max_exact) / math.log(REL_MAX_DIST / max_exact)
        large = max_exact + (log_ratio * (REL_BUCKETS - max_exact)).astype(jnp.int32)
        large = jnp.minimum(large, REL_BUCKETS - 1)
        bucket = jnp.where(dist < max_exact, dist, large)
        row = jnp.zeros((8, 2 * bs), F32)
        for b in range(REL_BUCKETS):
            row = jnp.where(bucket == b, rel_ref[b, h], row)
        base = jnp.broadcast_to(row[0:1, :] * LOG2E, (bs, 2 * bs))
        tile = pltpu.roll(base, 0, 1, stride=1, stride_axis=0)[:, bs:]
        if d == 0:
            kk = lax.broadcasted_iota(jnp.int32, (bs, bs), 0)
            qq = lax.broadcasted_iota(jnp.int32, (bs, bs), 1)
            tile = jnp.where(kk <= qq, tile, NEG_INF)
        o_ref[0, d] = tile


def _bias_tiles(rel_bias):
    bs = MOBA_BLOCK
    max_exact = REL_BUCKETS // 2
    nearest = (BIAS_TILES - 1) * bs - (bs - 1)
    assert max_exact + int(math.log(nearest / max_exact) / math.log(REL_MAX_DIST / max_exact)
                           * (REL_BUCKETS - max_exact)) >= REL_BUCKETS - 1
    return pl.pallas_call(
        _bias_kernel,
        grid=(MOBA_HEADS,),
        in_specs=[pl.BlockSpec(memory_space=pltpu.SMEM)],
        out_specs=pl.BlockSpec((1, BIAS_TILES, bs, bs), lambda h: (h, 0, 0, 0)),
        out_shape=jax.ShapeDtypeStruct((MOBA_HEADS, BIAS_TILES, bs, bs), F32),
        compiler_params=_cparams("parallel"),
        name="bias",
    )(rel_bias)


def _moba_prep_kernel(q_ref, k_ref, v_ref, qw_ref, kw_ref, rel_ref, q16_ref, kn_ref, vt_ref, add_ref, km_scr,
                      *, rows, nb):
    bs = MOBA_BLOCK
    nsub = rows // bs
    ti = pl.program_id(2)

    @pl.when(ti == 0)
    def _():
        km_scr[...] = jnp.zeros_like(km_scr)

    kn = _rms_rows(k_ref[0], kw_ref[...])
    kn_ref[0, 0] = kn.astype(BF16)
    for s in range(nsub):
        km_scr[pl.ds(ti * nsub + s, 1), :] = jnp.mean(kn[s * bs:(s + 1) * bs], axis=0, keepdims=True)
        vt_ref[0, 0, 0:HEAD_DIM, s * bs:(s + 1) * bs] = v_ref[0, s * bs:(s + 1) * bs, :].T.astype(BF16)
    pad_row = lax.broadcasted_iota(jnp.int32, (MOBA_V_ROWS - HEAD_DIM, rows), 0)
    vt_ref[0, 0, HEAD_DIM:, :] = jnp.where(pad_row == 0, 1.0, 0.0).astype(BF16)

    q = _rms_rows(q_ref[0], qw_ref[...])
    q16_ref[0, 0] = (q * ((HEAD_DIM ** -0.5) * LOG2E)).astype(BF16)

    blk = lax.broadcasted_iota(jnp.int32, (nb, rows), 0)
    own_blk = ti * nsub + lax.broadcasted_iota(jnp.int32, (nb, rows), 1) // bs
    blkf = blk.astype(F32)
    valid = blk < own_blk
    work = jnp.where(valid, _dot_nt(km_scr[...], q, HIGHEST), NEG_INF)
    picked = jnp.zeros((nb, rows), F32)
    for _ in range(MOBA_TOPK):
        best = jnp.max(work, axis=0, keepdims=True)
        first = jnp.min(jnp.where(work == best, blkf, float(nb)), axis=0, keepdims=True)
        hit = blkf == first
        picked = jnp.where(hit, 1.0, picked)
        work = jnp.where(hit, -jnp.inf, work)
    far_bias = rel_ref[REL_BUCKETS - 1, pl.program_id(1)] * LOG2E
    attended = (valid & (picked > 0.0)) | (blk == own_blk)
    add_ref[0, 0] = jnp.where(attended, far_bias, NEG_INF)


def _moba_prep(proj3, qw, kw, rel_bias, col0, rows):
    b, t, _ = proj3.shape
    hh, bs, dh = MOBA_HEADS, MOBA_BLOCK, HEAD_DIM
    nb = t // bs

    def col(base):
        return pl.BlockSpec((1, rows, LANES), lambda bi, hi, ti: (bi, ti, base + hi))

    row = pl.BlockSpec((1, LANES), lambda bi, hi, ti: (0, 0))
    return pl.pallas_call(
        functools.partial(_moba_prep_kernel, rows=rows, nb=nb),
        grid=(b, hh, t // rows),
        in_specs=[col(col0), col(col0 + hh), col(col0 + 2 * hh), row, row,
                  pl.BlockSpec(memory_space=pltpu.SMEM)],
        out_specs=[pl.BlockSpec((1, 1, rows, dh), lambda bi, hi, ti: (bi, hi, ti, 0)),
                   pl.BlockSpec((1, 1, rows, dh), lambda bi, hi, ti: (bi, hi, ti, 0)),
                   pl.BlockSpec((1, 1, MOBA_V_ROWS, rows), lambda bi, hi, ti: (bi, hi, 0, ti)),
                   pl.BlockSpec((1, 1, nb, rows), lambda bi, hi, ti: (bi, hi, 0, ti))],
        out_shape=[jax.ShapeDtypeStruct((b, hh, t, dh), BF16),
                   jax.ShapeDtypeStruct((b, hh, t, dh), BF16),
                   jax.ShapeDtypeStruct((b, hh, MOBA_V_ROWS, t), BF16),
                   jax.ShapeDtypeStruct((b, hh, nb, t), F32)],
        scratch_shapes=[pltpu.VMEM((nb, dh), F32)],
        compiler_params=_cparams("parallel", "parallel", "arbitrary"),
        name="moba_prep",
    )(proj3, proj3, proj3, qw, kw, rel_bias)


def _moba_kernel(q_ref, kn_ref, vt_ref, add_ref, bias_ref, o_ref, buf_a, buf_b, *, nb):
    i = pl.program_id(2)
    bs = MOBA_BLOCK
    ch = MOBA_TILES_PER_STEP
    E = range(MOBA_STREAMS)
    q16 = [q_ref[e, 0] for e in E]

    crows = ch * bs
    n_pairs = (i + 2 * ch) // (2 * ch)
    last_chunk = nb // ch - 1

    def chunk_scores(buf, e, c, far):
        r = pl.multiple_of(c * crows, crows)
        raw = _dot_nt(kn_ref[e, 0, pl.ds(r, crows), :], q16[e])
        mx = None
        for t in range(ch):
            j = c * ch + t
            add = add_ref[e, 0, pl.ds(j, 1), :]
            if far:
                st = raw[t * bs:(t + 1) * bs, :] + add
            else:
                d = jnp.clip(i - j, 0, BIAS_TILES - 1)
                st = jnp.where(add > 0.5 * NEG_INF, raw[t * bs:(t + 1) * bs, :] + bias_ref[0, d], NEG_INF)
            buf[e, t * bs:(t + 1) * bs, :] = st
            tmx = jnp.max(st, axis=0, keepdims=True)
            mx = tmx if mx is None else jnp.maximum(mx, tmx)
        return mx

    def absorb(buf, e, c, mx, m, acc):
        r = pl.multiple_of(c * crows, crows)
        m_new = jnp.maximum(m, mx)
        p = jnp.exp2(buf[e] - m_new)
        acc = jnp.exp2(m - m_new) * acc + _dot(vt_ref[e, 0, :, pl.ds(r, crows)], p.astype(BF16))
        return m_new, acc

    def pair(far, c, carry):
        m, acc, mx_a = (list(x) for x in carry)
        mx_b = [chunk_scores(buf_b, e, 2 * c + 1, far) for e in E]
        for e in E:
            m[e], acc[e] = absorb(buf_a, e, 2 * c, mx_a[e], m[e], acc[e])
        mx_a = [chunk_scores(buf_a, e, jnp.minimum(2 * c + 2, last_chunk), far) for e in E]
        for e in E:
            m[e], acc[e] = absorb(buf_b, e, 2 * c + 1, mx_b[e], m[e], acc[e])
        return tuple(m), tuple(acc), tuple(mx_a)

    def step(c, carry):
        all_far = (2 * c + 3) * ch - 1 + (BIAS_TILES - 1) <= i
        return lax.cond(all_far, functools.partial(pair, True, c), functools.partial(pair, False, c), carry)

    m0 = tuple(jnp.full((1, bs), MOBA_M_INIT, F32) for e in E)
    acc0 = tuple(jnp.zeros((MOBA_V_ROWS, bs), F32) for e in E)
    mx_a = tuple(chunk_scores(buf_a, e, 0, False) for e in E)
    m, acc, _ = lax.fori_loop(0, n_pairs, step, (m0, acc0, mx_a))
    for e in E:
        o_ref[e] = (acc[e][:HEAD_DIM] / acc[e][HEAD_DIM:HEAD_DIM + 1]).T.astype(o_ref.dtype)


def _moba(q16, kn, vt, add, bias):
    b, hh, t, dh = kn.shape
    bs = MOBA_BLOCK
    nb = t // bs
    ns = MOBA_STREAMS
    crows = MOBA_TILES_PER_STEP * bs
    assert b % ns == 0 and nb % (2 * MOBA_TILES_PER_STEP) == 0
    return pl.pallas_call(
        functools.partial(_moba_kernel, nb=nb),
        grid=(b // ns, hh, nb),
        in_specs=[pl.BlockSpec((ns, 1, bs, dh), lambda bi, hi, ti: (bi, hi, ti, 0)),
                  pl.BlockSpec((ns, 1, t, dh), lambda bi, hi, ti: (bi, hi, 0, 0)),
                  pl.BlockSpec((ns, 1, MOBA_V_ROWS, t), lambda bi, hi, ti: (bi, hi, 0, 0)),
                  pl.BlockSpec((ns, 1, nb, bs), lambda bi, hi, ti: (bi, hi, 0, ti)),
                  pl.BlockSpec((1, BIAS_TILES, bs, bs), lambda bi, hi, ti: (hi, 0, 0, 0))],
        out_specs=pl.BlockSpec((ns, bs, dh), lambda bi, hi, ti: (bi, ti, hi)),
        out_shape=jax.ShapeDtypeStruct((b, t, hh * dh), BF16),
        scratch_shapes=[pltpu.VMEM((ns, crows, bs), F32),
                        pltpu.VMEM((ns, crows, bs), F32)],
        compiler_params=_cparams("parallel", "parallel", "parallel"),
        name="moba",
    )(q16, kn, vt, add, bias)


def _tile(n, want):
    t = min(n, want)
    assert n % t == 0, (n, want)
    return t


def kernel(x, norm_mix_w, w_in, conv_w, a_log, dt_bias, gdn_o_norm_w, q_norm_w, k_norm_w, rel_bias,
           w_branch_gdn, w_branch_moba, w_out, norm_ffn_w, w_ffn_gate, w_ffn_up, w_ffn_down):
    b, t, d = x.shape
    m = b * t
    gw = GDN_HEADS * HEAD_DIM
    mw = MOBA_HEADS * HEAD_DIM
    assert t % MOBA_PREP_ROWS == 0 and w_in.shape[0] == 1
    assert w_in.shape[2] == 4 * gw + 2 * GDN_HEADS + 3 * mw + 2 * d

    o_bd = 4 * gw
    o_moba = o_bd + 2 * GDN_HEADS
    o_gate = o_moba + 3 * mw
    n_main = 2 * d + 4 * gw + 3 * mw
    tn_proj = PROJ_TILE[1]
    n_proj = -(-(n_main + LANES) // tn_proj) * tn_proj
    w_proj = _regroup_weights(w_in, (o_bd, o_moba, o_gate), n_proj)
    col_gdn = (2 * d) // LANES
    col_moba = col_gdn + 4 * GDN_HEADS
    col_bd = col_moba + 3 * MOBA_HEADS

    x2 = x.reshape(m, d)
    proj = _norm_matmul(x2, norm_mix_w, w_proj, _tile(m, PROJ_TILE[0]), tn_proj, F32, "proj")
    proj3 = proj.reshape(b, t, n_proj)

    lane_pad = jnp.zeros((1, LANES - 2 * GDN_HEADS), F32)
    head_pad = jnp.zeros((1, GDN_HEADS), F32)
    alog_row = jnp.concatenate([head_pad, a_log, lane_pad], axis=1)
    dtb_row = jnp.concatenate([head_pad, dt_bias, lane_pad], axis=1)
    y_a = _gdn(proj3, conv_w[0], alog_row, dtb_row, gdn_o_norm_w, col_gdn, col_bd, _tile(t, GDN_BLOCK))

    bias = _bias_tiles(rel_bias)
    q16, kn, vt, add = _moba_prep(proj3, q_norm_w, k_norm_w, rel_bias, col_moba, MOBA_PREP_ROWS)
    y_b = _moba(q16, kn, vt, add, bias)

    mix = _mix(y_a.reshape(m, gw), y_b.reshape(m, mw), proj,
               w_branch_gdn[0].astype(BF16), w_branch_moba[0].astype(BF16), _tile(m, MIX_ROWS), d)
    h1 = _matmul_res(mix, w_out[0].astype(BF16), x2, _tile(m, OUT_ROWS), d, "out")

    hid = _ffn_up(h1, norm_ffn_w, w_ffn_gate[0].astype(BF16), w_ffn_up[0].astype(BF16),
                  _tile(m, FFN_UP_TILE[0]), FFN_UP_TILE[1])
    h2 = _matmul_res(hid, w_ffn_down[0].astype(BF16), h1, _tile(m, FFN_DOWN_TILE[0]), FFN_DOWN_TILE[1],
                     "ffn_down")
    return h2.reshape(b, t, d)
```

```python
import functools
import math

import jax
import jax.numpy as jnp
from jax import lax
from jax.experimental import pallas as pl
from jax.experimental.pallas import tpu as pltpu

F32 = jnp.float32
BF16 = jnp.bfloat16
HIGHEST = lax.Precision.HIGHEST

LANES = 128
HEAD_DIM = 128
GDN_HEADS = 8
GDN_CONV = 4
GDN_CHUNK = 64
GDN_INV_BASE = 8
GDN_HEADS_PER_STEP = 8
NEG_INF = -1e30
MOBA_HEADS = 8
MOBA_BLOCK = 256
MOBA_TOPK = 3
REL_BUCKETS = 32
REL_MAX_DIST = 2048
BIAS_TILES = 8
MOBA_TILES_PER_STEP = 4
MOBA_PREP_ROWS = 8192
MOBA_STREAMS = 4
MOBA_V_ROWS = HEAD_DIM + 16
MOBA_M_INIT = 0.1 * NEG_INF
LOG2E = math.log2(math.e)
RMS_EPS = 1e-6
V7X_VMEM_BYTES = 64 * 1024 * 1024
VMEM_LIMIT = V7X_VMEM_BYTES * 7 // 8

PROJ_TILE = (1024, 1280)
MIX_ROWS = 512
OUT_ROWS = 512
FFN_UP_TILE = (1024, 512)
FFN_DOWN_TILE = (1024, 512)
GDN_BLOCK = 256
REGROUP_ROWS = 128


def _cparams(*sem):
    return pltpu.CompilerParams(dimension_semantics=sem, vmem_limit_bytes=VMEM_LIMIT)


def _sigmoid(x):
    return 0.5 + 0.5 * jnp.tanh(0.5 * x)


def _silu(x):
    h = 0.5 * x
    return h + h * jnp.tanh(h)


def _dot(a, b, precision=None):
    return jnp.dot(a, b, preferred_element_type=F32, precision=precision)


def _dot_nt(a, b, precision=None):
    return lax.dot_general(a, b, (((1,), (1,)), ((), ())),
                           preferred_element_type=F32, precision=precision)


def _dot_tn(a, b):
    return lax.dot_general(a, b, (((0,), (0,)), ((), ())), preferred_element_type=F32)


def _rms_rows(x, w):
    return x * lax.rsqrt(jnp.mean(x * x, axis=-1, keepdims=True) + RMS_EPS) * w


def _regroup_kernel(w_ref, o_ref, *, cuts):
    o_bd, o_moba, o_gate = cuts
    x = w_ref[...]
    parts = [x[:, o_gate:], x[:, :o_bd], x[:, o_moba:o_gate], x[:, o_bd:o_moba]]
    used = sum(p.shape[1] for p in parts)
    parts.append(jnp.zeros((x.shape[0], o_ref.shape[1] - used), x.dtype))
    o_ref[...] = jnp.concatenate(parts, axis=1).astype(o_ref.dtype)


def _regroup_weights(w, cuts, n_out):
    _, k, n_in = w.shape
    rows = _tile(k, REGROUP_ROWS)
    return pl.pallas_call(
        functools.partial(_regroup_kernel, cuts=cuts),
        grid=(k // rows,),
        in_specs=[pl.BlockSpec((None, rows, n_in), lambda i: (0, i, 0))],
        out_specs=pl.BlockSpec((rows, n_out), lambda i: (i, 0)),
        out_shape=jax.ShapeDtypeStruct((k, n_out), BF16),
        compiler_params=_cparams("parallel"),
        name="regroup",
    )(w)


def _norm_matmul_kernel(x_ref, nw_ref, w_ref, o_ref, u_scr):
    @pl.when(pl.program_id(1) == 0)
    def _():
        u_scr[...] = _rms_rows(x_ref[...], nw_ref[...]).astype(BF16)

    o_ref[...] = _dot(u_scr[...], w_ref[...]).astype(o_ref.dtype)


def _norm_matmul(x, nw, w, tm, tn, out_dtype, name):
    m, k = x.shape
    n = w.shape[1]
    assert m % tm == 0 and n % tn == 0
    return pl.pallas_call(
        _norm_matmul_kernel,
        grid=(m // tm, n // tn),
        in_specs=[pl.BlockSpec((tm, k), lambda i, j: (i, 0)),
                  pl.BlockSpec((1, k), lambda i, j: (0, 0)),
                  pl.BlockSpec((k, tn), lambda i, j: (0, j))],
        out_specs=pl.BlockSpec((tm, tn), lambda i, j: (i, j)),
        out_shape=jax.ShapeDtypeStruct((m, n), out_dtype),
        scratch_shapes=[pltpu.VMEM((tm, k), BF16)],
        compiler_params=_cparams("parallel", "arbitrary"),
        name=name,
    )(x, nw, w)


def _matmul_res_kernel(a_ref, w_ref, r_ref, o_ref):
    o_ref[...] = r_ref[...] + _dot(a_ref[...], w_ref[...])


def _matmul_res(a, w, res, tm, tn, name):
    m, k = a.shape
    n = w.shape[1]
    assert m % tm == 0 and n % tn == 0
    return pl.pallas_call(
        _matmul_res_kernel,
        grid=(m // tm, n // tn),
        in_specs=[pl.BlockSpec((tm, k), lambda i, j: (i, 0)),
                  pl.BlockSpec((k, tn), lambda i, j: (0, j)),
                  pl.BlockSpec((tm, tn), lambda i, j: (i, j))],
        out_specs=pl.BlockSpec((tm, tn), lambda i, j: (i, j)),
        out_shape=jax.ShapeDtypeStruct((m, n), F32),
        compiler_params=_cparams("parallel", "parallel"),
        name=name,
    )(a, w, res)


def _ffn_up_kernel(x_ref, nw_ref, wg_ref, wu_ref, o_ref, u_scr):
    @pl.when(pl.program_id(1) == 0)
    def _():
        u_scr[...] = _rms_rows(x_ref[...], nw_ref[...]).astype(BF16)

    u = u_scr[...]
    g = _dot(u, wg_ref[...])
    o_ref[...] = (_silu(g) * _dot(u, wu_ref[...])).astype(o_ref.dtype)


def _ffn_up(x, nw, wg, wu, tm, tn):
    m, k = x.shape
    n = wg.shape[1]
    assert m % tm == 0 and n % tn == 0
    return pl.pallas_call(
        _ffn_up_kernel,
        grid=(m // tm, n // tn),
        in_specs=[pl.BlockSpec((tm, k), lambda i, j: (i, 0)),
                  pl.BlockSpec((1, k), lambda i, j: (0, 0)),
                  pl.BlockSpec((k, tn), lambda i, j: (0, j)),
                  pl.BlockSpec((k, tn), lambda i, j: (0, j))],
        out_specs=pl.BlockSpec((tm, tn), lambda i, j: (i, j)),
        out_shape=jax.ShapeDtypeStruct((m, n), BF16),
        scratch_shapes=[pltpu.VMEM((tm, k), BF16)],
        compiler_params=_cparams("parallel", "arbitrary"),
        name="ffn_up",
    )(x, nw, wg, wu)


def _mix_kernel(ya_ref, yb_ref, ga_ref, gb_ref, wa_ref, wb_ref, o_ref):
    a = _dot(ya_ref[...], wa_ref[...])
    b = _dot(yb_ref[...], wb_ref[...])
    o_ref[...] = (_sigmoid(ga_ref[...]) * a + _sigmoid(gb_ref[...]) * b).astype(o_ref.dtype)


def _mix(ya, yb, proj, wa, wb, tm, tn):
    m, k = ya.shape
    n = wa.shape[1]
    nb = n // tn
    assert m % tm == 0 and n % tn == 0
    return pl.pallas_call(
        _mix_kernel,
        grid=(m // tm, nb),
        in_specs=[pl.BlockSpec((tm, k), lambda i, j: (i, 0)),
                  pl.BlockSpec((tm, k), lambda i, j: (i, 0)),
                  pl.BlockSpec((tm, tn), lambda i, j: (i, j)),
                  pl.BlockSpec((tm, tn), lambda i, j: (i, nb + j)),
                  pl.BlockSpec((k, tn), lambda i, j: (0, j)),
                  pl.BlockSpec((k, tn), lambda i, j: (0, j))],
        out_specs=pl.BlockSpec((tm, tn), lambda i, j: (i, j)),
        out_shape=jax.ShapeDtypeStruct((m, n), BF16),
        compiler_params=_cparams("parallel", "parallel"),
        name="mix",
    )(ya, yb, proj, proj, wa, wb)


def _split3(x):
    a = x.astype(BF16)
    r = x - a.astype(F32)
    b = r.astype(BF16)
    c = (r - b.astype(F32)).astype(BF16)
    return a, b, c


def _gdn_kernel(q_ref, k_ref, v_ref, z_ref, bd_ref, cwq_ref, cwk_ref, cwv_ref,
                alog_ref, dtb_ref, onw_ref, o_ref, s_scr, tail_scr, *, tb):
    hp = GDN_HEADS_PER_STEP
    C = GDN_CHUNK
    D = HEAD_DIM
    head0 = pl.program_id(1) * hp

    @pl.when(pl.program_id(2) == 0)
    def _():
        s_scr[...] = jnp.zeros_like(s_scr)
        tail_scr[...] = jnp.zeros_like(tail_scr)

    row8 = lax.broadcasted_iota(jnp.int32, (8, hp * D), 0)
    tails = []

    def conv_silu(u_ref, cw_ref, idx):
        u = u_ref[0]
        w = cw_ref[...]
        tail = tail_scr[idx]
        y = None
        ytop = None
        for s in (3, 2, 1):
            sh = pltpu.roll(u, s, axis=0)
            top = jnp.where(row8 < s, pltpu.roll(tail, s, axis=0), sh[0:8])
            wj = w[3 - s:4 - s]
            y = sh * wj if y is None else y + sh * wj
            ytop = top * wj if ytop is None else ytop + top * wj
        y = y + u * w[3:4]
        ytop = ytop + u[0:8] * w[3:4]
        tails.append(u[tb - 8:tb])
        y = jnp.concatenate([ytop, y[8:]], axis=0)
        return _silu(y)

    q_all = conv_silu(q_ref, cwq_ref, 0)
    k_all = conv_silu(k_ref, cwk_ref, 1)
    v_all = conv_silu(v_ref, cwv_ref, 2)

    bd = bd_ref[0]
    lane = lax.broadcasted_iota(jnp.int32, (tb, LANES), 1)
    beta_all = _sigmoid(bd)
    xg = bd + dtb_ref[...]
    softplus = jnp.maximum(xg, 0.0) + jnp.log1p(jnp.exp(-jnp.abs(xg)))
    g_all = -jnp.exp(alog_ref[...]) * softplus
    betas = [jnp.sum(jnp.where(lane == head0 + e, beta_all, 0.0), axis=-1, keepdims=True)
             for e in range(hp)]
    gs = [jnp.sum(jnp.where(lane == head0 + e + GDN_HEADS, g_all, 0.0), axis=-1, keepdims=True)
          for e in range(hp)]

    nc = tb // C
    ri = lax.broadcasted_iota(jnp.int32, (tb, tb), 0)
    ci = lax.broadcasted_iota(jnp.int32, (tb, tb), 1)
    same = (ri // C) == (ci // C)
    tril16 = jnp.where(same & (ri >= ci), 1.0, 0.0).astype(BF16)
    g_b = jnp.concatenate([jnp.broadcast_to(g, (tb, LANES)) for g in gs], axis=1)
    gc_all = sum(_dot(tril16, piece) for piece in _split3(g_b))

    iw = lax.broadcasted_iota(jnp.int32, (C, tb), 0)
    jw = lax.broadcasted_iota(jnp.int32, (C, tb), 1) % C
    incl_w = iw >= jw
    strict_w = iw > jw
    eye_w = jnp.where(iw == jw, 1.0, 0.0).astype(F32)
    low_half = (lax.broadcasted_iota(jnp.int32, (C, LANES), 1) < C)

    def block_diag(wide16):
        return jnp.where(same, jnp.concatenate([wide16] * nc, axis=0), jnp.zeros((), BF16))

    def pair_blocks(full):
        return jnp.where(low_half, full[:C], full[C:])

    E = range(hp)
    pairs = range(tb // LANES)

    def l2n(x):
        return x * lax.rsqrt(jnp.sum(x * x, axis=-1, keepdims=True) + RMS_EPS)

    qs = [l2n(q_all[:, e * D:(e + 1) * D]) * (D ** -0.5) for e in E]
    ks = [l2n(k_all[:, e * D:(e + 1) * D]) for e in E]
    vs = [v_all[:, e * D:(e + 1) * D] for e in E]
    gcs = [gc_all[:, e * D:(e + 1) * D] for e in E]
    g_col = [jnp.concatenate([pair_blocks(gc[p * LANES:(p + 1) * LANES]) for p in pairs], axis=1)
             for gc in gcs]
    g_row = [jnp.concatenate([gc[p * LANES:(p + 1) * LANES, :].T[0:1, :] for p in pairs], axis=1)
             for gc in gcs]
    decay = [jnp.where(incl_w, jnp.exp(jnp.where(incl_w, g_col[e] - g_row[e], 0.0)), 0.0) for e in E]
    k16 = [k.astype(BF16) for k in ks]
    q16 = [q.astype(BF16) for q in qs]
    kb = [ks[e] * betas[e] for e in E]
    kb16 = [x.astype(BF16) for x in kb]

    def pair_products(a16, b16):
        return jnp.concatenate([pair_blocks(_dot_nt(a16[p * LANES:(p + 1) * LANES],
                                                    b16[p * LANES:(p + 1) * LANES])) for p in pairs], axis=1)

    lmat = [jnp.where(strict_w, pair_products(kb16[e], k16[e]) * decay[e], 0.0) for e in E]
    amat = [jnp.where(incl_w, pair_products(q16[e], k16[e]) * decay[e], 0.0) for e in E]
    base = GDN_INV_BASE
    blk_i, blk_j = iw // base, jw // base
    neg_d16 = [jnp.where(blk_i == blk_j, -lmat[e], 0.0).astype(BF16) for e in E]
    tinv = [eye_w + neg_d16[e].astype(F32) for e in E]
    pw = [_dot(neg_d16[e], block_diag(neg_d16[e])) for e in E]
    span = 2
    while 2 * span < base:
        p16 = [pw[e].astype(BF16) for e in E]
        prod = [_dot(jnp.concatenate([p16[e], tinv[e].astype(BF16)], axis=0), block_diag(p16[e])) for e in E]
        pw = [prod[e][:C] for e in E]
        tinv = [tinv[e] + prod[e][C:] for e in E]
        span *= 2
    tinv = [tinv[e] + _dot(tinv[e].astype(BF16), block_diag(pw[e].astype(BF16))) for e in E]
    s = base
    while s < C:
        below_left = (iw // (2 * s) == jw // (2 * s)) & ((iw // s) % 2 == 1) & ((jw // s) % 2 == 0)
        c16 = [jnp.where(below_left, lmat[e], 0.0).astype(BF16) for e in E]
        cx = [_dot(c16[e], block_diag(tinv[e].astype(BF16))) for e in E]
        tinv = [tinv[e] - _dot(tinv[e].astype(BF16), block_diag(cx[e].astype(BF16))) for e in E]
        s *= 2
    eg = [jnp.exp(gc) for gc in gcs]
    rhs = [jnp.concatenate([kb[e] * eg[e], vs[e] * betas[e]], axis=1).astype(BF16) for e in E]
    wu = [_dot(block_diag(tinv[e].astype(BF16)), rhs[e]).astype(BF16) for e in E]
    au = [_dot(block_diag(amat[e].astype(BF16)), wu[e]) for e in E]
    q_eff = [(qs[e] * eg[e] - au[e][:, :D]).astype(BF16) for e in E]
    gl = [jnp.concatenate([jnp.broadcast_to(gc[c * C + C - 1:c * C + C, :], (C, LANES))
                           for c in range(nc)], axis=0) for gc in gcs]
    k_dec = [(ks[e] * jnp.exp(gl[e] - gcs[e])).astype(BF16) for e in E]
    pn = [[_dot_tn(k_dec[e][c * C:(c + 1) * C], wu[e][c * C:(c + 1) * C]) for c in range(nc)]
          for e in E]

    states = [s_scr[e] for e in E]
    onw = onw_ref[...]
    for c in range(nc):
        r0 = c * C
        lhs = [jnp.concatenate([pn[e][c][:, :D].astype(BF16), q_eff[e][r0:r0 + C]], axis=0) for e in E]
        res = [_dot(lhs[e], states[e].astype(BF16)) for e in E]
        outs = [_rms_rows(res[e][D:] + au[e][r0:r0 + C, D:], onw) for e in E]
        states = [states[e] * jnp.exp(gl[e][r0:r0 + 1, :]) - res[e][:D] + pn[e][c][:, D:] for e in E]
        zc = z_ref[0, r0:r0 + C, :]
        o_ref[0, r0:r0 + C, :] = (jnp.concatenate(outs, axis=1) * _silu(zc)).astype(o_ref.dtype)
    for e in range(hp):
        s_scr[e] = states[e]
    for idx in range(3):
        tail_scr[idx] = tails[idx]


def _gdn(proj3, conv_w, alog_row, dtb_row, onw, col0, bd_col, tb):
    b, t, _ = proj3.shape
    hp = GDN_HEADS_PER_STEP
    ng = GDN_HEADS // hp
    w = hp * HEAD_DIM

    def col(base):
        return pl.BlockSpec((1, tb, w), lambda bi, hi, ti: (bi, ti, base // hp + hi))

    def cw(base):
        return pl.BlockSpec((GDN_CONV, w), lambda bi, hi, ti: (0, base // hp + hi))

    assert col0 % hp == 0
    row = pl.BlockSpec((1, LANES), lambda bi, hi, ti: (0, 0))
    return pl.pallas_call(
        functools.partial(_gdn_kernel, tb=tb),
        grid=(b, ng, t // tb),
        in_specs=[col(col0), col(col0 + GDN_HEADS), col(col0 + 2 * GDN_HEADS), col(col0 + 3 * GDN_HEADS),
                  pl.BlockSpec((1, tb, LANES), lambda bi, hi, ti: (bi, ti, bd_col)),
                  cw(0), cw(GDN_HEADS), cw(2 * GDN_HEADS), row, row, row],
        out_specs=pl.BlockSpec((1, tb, w), lambda bi, hi, ti: (bi, ti, hi)),
        out_shape=jax.ShapeDtypeStruct((b, t, GDN_HEADS * HEAD_DIM), BF16),
        scratch_shapes=[pltpu.VMEM((hp, HEAD_DIM, HEAD_DIM), F32),
                        pltpu.VMEM((3, 8, w), F32)],
        compiler_params=_cparams("parallel", "parallel", "arbitrary"),
        name="gdn",
    )(proj3, proj3, proj3, proj3, proj3, conv_w, conv_w, conv_w, alog_row, dtb_row, onw)


def _bias_kernel(rel_ref, o_ref):
    h = pl.program_id(0)
    bs = MOBA_BLOCK
    max_exact = REL_BUCKETS // 2
    x = lax.broadcasted_iota(jnp.int32, (8, 2 * bs), 1)
    for d in range(BIAS_TILES):
        dist = jnp.maximum(d * bs + x - bs, 0)
        df = dist.astype(F32)
        log_ratio = jnp.log(jnp.maximum(df, float(max_exact)) / max_exact) / math.log(REL_MAX_DIST / max_exact)
        large = max_exact + (log_ratio * (REL_BUCKETS - max_exact)).astype(jnp.int32)
        large = jnp.minimum(large, REL_BUCKETS - 1)
        bucket = jnp.where(dist < max_exact, dist, large)
        row = jnp.zeros((8, 2 * bs), F32)
        for b in range(REL_BUCKETS):
            row = jnp.where(bucket == b, rel_ref[b, h], row)
        base = jnp.broadcast_to(row[0:1, :] * LOG2E, (bs, 2 * bs))
        tile = pltpu.roll(base, 0, 1, stride=1, stride_axis=0)[:, bs:]
        if d == 0:
            kk = lax.broadcasted_iota(jnp.int32, (bs, bs), 0)
            qq = lax.broadcasted_iota(jnp.int32, (bs, bs), 1)
            tile = jnp.where(kk <= qq, tile, NEG_INF)
        o_ref[0, d] = tile


def _bias_tiles(rel_bias):
    bs = MOBA_BLOCK
    max_exact = REL_BUCKETS // 2
    nearest = (BIAS_TILES - 1) * bs - (bs - 1)
    assert max_exact + int(math.log(nearest / max_exact) / math.log(REL_MAX_DIST / max_exact)
                           * (REL_BUCKETS - max_exact)) >= REL_BUCKETS - 1
    return pl.pallas_call(
        _bias_kernel,
        grid=(MOBA_HEADS,),
        in_specs=[pl.BlockSpec(memory_space=pltpu.SMEM)],
        out_specs=pl.BlockSpec((1, BIAS_TILES, bs, bs), lambda h: (h, 0, 0, 0)),
        out_shape=jax.ShapeDtypeStruct((MOBA_HEADS, BIAS_TILES, bs, bs), F32),
        compiler_params=_cparams("parallel"),
        name="bias",
    )(rel_bias)


def _moba_prep_kernel(q_ref, k_ref, v_ref, qw_ref, kw_ref, rel_ref, q16_ref, kn_ref, vt_ref, add_ref, km_scr,
                      *, rows, nb):
    bs = MOBA_BLOCK
    nsub = rows // bs
    ti = pl.program_id(2)

    @pl.when(ti == 0)
    def _():
        km_scr[...] = jnp.zeros_like(km_scr)

    kn = _rms_rows(k_ref[0], kw_ref[...])
    kn_ref[0, 0] = kn.astype(BF16)
    for s in range(nsub):
        km_scr[pl.ds(ti * nsub + s, 1), :] = jnp.mean(kn[s * bs:(s + 1) * bs], axis=0, keepdims=True)
        vt_ref[0, 0, 0:HEAD_DIM, s * bs:(s + 1) * bs] = v_ref[0, s * bs:(s + 1) * bs, :].T.astype(BF16)
    pad_row = lax.broadcasted_iota(jnp.int32, (MOBA_V_ROWS - HEAD_DIM, rows), 0)
    vt_ref[0, 0, HEAD_DIM:, :] = jnp.where(pad_row == 0, 1.0, 0.0).astype(BF16)

    q = _rms_rows(q_ref[0], qw_ref[...])
    q16_ref[0, 0] = (q * ((HEAD_DIM ** -0.5) * LOG2E)).astype(BF16)

    blk = lax.broadcasted_iota(jnp.int32, (nb, rows), 0)
    own_blk = ti * nsub + lax.broadcasted_iota(jnp.int32, (nb, rows), 1) // bs
    blkf = blk.astype(F32)
    valid = blk < own_blk
    work = jnp.where(valid, _dot_nt(km_scr[...], q, HIGHEST), NEG_INF)
    picked = jnp.zeros((nb, rows), F32)
    for _ in range(MOBA_TOPK):
        best = jnp.max(work, axis=0, keepdims=True)
        first = jnp.min(jnp.where(work == best, blkf, float(nb)), axis=0, keepdims=True)
        hit = blkf == first
        picked = jnp.where(hit, 1.0, picked)
        work = jnp.where(hit, -jnp.inf, work)
    far_bias = rel_ref[REL_BUCKETS - 1, pl.program_id(1)] * LOG2E
    attended = (valid & (picked > 0.0)) | (blk == own_blk)
    add_ref[0, 0] = jnp.where(attended, far_bias, NEG_INF)


def _moba_prep(proj3, qw, kw, rel_bias, col0, rows):
    b, t, _ = proj3.shape
    hh, bs, dh = MOBA_HEADS, MOBA_BLOCK, HEAD_DIM
    nb = t // bs

    def col(base):
        return pl.BlockSpec((1, rows, LANES), lambda bi, hi, ti: (bi, ti, base + hi))

    row = pl.BlockSpec((1, LANES), lambda bi, hi, ti: (0, 0))
    return pl.pallas_call(
        functools.partial(_moba_prep_kernel, rows=rows, nb=nb),
        grid=(b, hh, t // rows),
        in_specs=[col(col0), col(col0 + hh), col(col0 + 2 * hh), row, row,
                  pl.BlockSpec(memory_space=pltpu.SMEM)],
        out_specs=[pl.BlockSpec((1, 1, rows, dh), lambda bi, hi, ti: (bi, hi, ti, 0)),
                   pl.BlockSpec((1, 1, rows, dh), lambda bi, hi, ti: (bi, hi, ti, 0)),
                   pl.BlockSpec((1, 1, MOBA_V_ROWS, rows), lambda bi, hi, ti: (bi, hi, 0, ti)),
                   pl.BlockSpec((1, 1, nb, rows), lambda bi, hi, ti: (bi, hi, 0, ti))],
        out_shape=[jax.ShapeDtypeStruct((b, hh, t, dh), BF16),
                   jax.ShapeDtypeStruct((b, hh, t, dh), BF16),
                   jax.ShapeDtypeStruct((b, hh, MOBA_V_ROWS, t), BF16),
                   jax.ShapeDtypeStruct((b, hh, nb, t), F32)],
        scratch_shapes=[pltpu.VMEM((nb, dh), F32)],
        compiler_params=_cparams("parallel", "parallel", "arbitrary"),
        name="moba_prep",
    )(proj3, proj3, proj3, qw, kw, rel_bias)


def _moba_kernel(q_ref, kn_ref, vt_ref, add_ref, bias_ref, o_ref, buf_a, buf_b, *, nb):
    i = pl.program_id(2)
    bs = MOBA_BLOCK
    ch = MOBA_TILES_PER_STEP
    E = range(MOBA_STREAMS)
    q16 = [q_ref[e, 0] for e in E]

    crows = ch * bs
    n_pairs = (i + 2 * ch) // (2 * ch)
    last_chunk = nb // ch - 1

    def chunk_scores(buf, e, c, far):
        r = pl.multiple_of(c * crows, crows)
        raw = _dot_nt(kn_ref[e, 0, pl.ds(r, crows), :], q16[e])
        mx = None
        for t in range(ch):
            j = c * ch + t
            add = add_ref[e, 0, pl.ds(j, 1), :]
            if far:
                st = raw[t * bs:(t + 1) * bs, :] + add
            else:
                d = jnp.clip(i - j, 0, BIAS_TILES - 1)
                st = jnp.where(add > 0.5 * NEG_INF, raw[t * bs:(t + 1) * bs, :] + bias_ref[0, d], NEG_INF)
            buf[e, t * bs:(t + 1) * bs, :] = st
            tmx = jnp.max(st, axis=0, keepdims=True)
            mx = tmx if mx is None else jnp.maximum(mx, tmx)
        return mx

    def absorb(buf, e, c, mx, m, acc):
        r = pl.multiple_of(c * crows, crows)
        m_new = jnp.maximum(m, mx)
        p = jnp.exp2(buf[e] - m_new)
        acc = jnp.exp2(m - m_new) * acc + _dot(vt_ref[e, 0, :, pl.ds(r, crows)], p.astype(BF16))
        return m_new, acc

    def pair(far, c, carry):
        m, acc, mx_a = (list(x) for x in carry)
        mx_b = [chunk_scores(buf_b, e, 2 * c + 1, far) for e in E]
        for e in E:
            m[e], acc[e] = absorb(buf_a, e, 2 * c, mx_a[e], m[e], acc[e])
        mx_a = [chunk_scores(buf_a, e, jnp.minimum(2 * c + 2, last_chunk), far) for e in E]
        for e in E:
            m[e], acc[e] = absorb(buf_b, e, 2 * c + 1, mx_b[e], m[e], acc[e])
        return tuple(m), tuple(acc), tuple(mx_a)

    def step(c, carry):
        all_far = (2 * c + 3) * ch - 1 + (BIAS_TILES - 1) <= i
        return lax.cond(all_far, functools.partial(pair, True, c), functools.partial(pair, False, c), carry)

    m0 = tuple(jnp.full((1, bs), MOBA_M_INIT, F32) for e in E)
    acc0 = tuple(jnp.zeros((MOBA_V_ROWS, bs), F32) for e in E)
    mx_a = tuple(chunk_scores(buf_a, e, 0, False) for e in E)
    m, acc, _ = lax.fori_loop(0, n_pairs, step, (m0, acc0, mx_a))
    for e in E:
        o_ref[e] = (acc[e][:HEAD_DIM] / acc[e][HEAD_DIM:HEAD_DIM + 1]).T.astype(o_ref.dtype)


def _moba(q16, kn, vt, add, bias):
    b, hh, t, dh = kn.shape
    bs = MOBA_BLOCK
    nb = t // bs
    ns = MOBA_STREAMS
    crows = MOBA_TILES_PER_STEP * bs
    assert b % ns == 0 and nb % (2 * MOBA_TILES_PER_STEP) == 0
    return pl.pallas_call(
        functools.partial(_moba_kernel, nb=nb),
        grid=(b // ns, hh, nb),
        in_specs=[pl.BlockSpec((ns, 1, bs, dh), lambda bi, hi, ti: (bi, hi, ti, 0)),
                  pl.BlockSpec((ns, 1, t, dh), lambda bi, hi, ti: (bi, hi, 0, 0)),
                  pl.BlockSpec((ns, 1, MOBA_V_ROWS, t), lambda bi, hi, ti: (bi, hi, 0, 0)),
                  pl.BlockSpec((ns, 1, nb, bs), lambda bi, hi, ti: (bi, hi, 0, ti)),
                  pl.BlockSpec((1, BIAS_TILES, bs, bs), lambda bi, hi, ti: (hi, 0, 0, 0))],
        out_specs=pl.BlockSpec((ns, bs, dh), lambda bi, hi, ti: (bi, ti, hi)),
        out_shape=jax.ShapeDtypeStruct((b, t, hh * dh), BF16),
        scratch_shapes=[pltpu.VMEM((ns, crows, bs), F32),
                        pltpu.VMEM((ns, crows, bs), F32)],
        compiler_params=_cparams("parallel", "parallel", "parallel"),
        name="moba",
    )(q16, kn, vt, add, bias)


def _tile(n, want):
    t = min(n, want)
    assert n % t == 0, (n, want)
    return t


def kernel(x, norm_mix_w, w_in, conv_w, a_log, dt_bias, gdn_o_norm_w, q_norm_w, k_norm_w, rel_bias,
           w_branch_gdn, w_branch_moba, w_out, norm_ffn_w, w_ffn_gate, w_ffn_up, w_ffn_down):
    b, t, d = x.shape
    m = b * t
    gw = GDN_HEADS * HEAD_DIM
    mw = MOBA_HEADS * HEAD_DIM
    assert t % MOBA_PREP_ROWS == 0 and w_in.shape[0] == 1
    assert w_in.shape[2] == 4 * gw + 2 * GDN_HEADS + 3 * mw + 2 * d

    o_bd = 4 * gw
    o_moba = o_bd + 2 * GDN_HEADS
    o_gate = o_moba + 3 * mw
    n_main = 2 * d + 4 * gw + 3 * mw
    tn_proj = PROJ_TILE[1]
    n_proj = -(-(n_main + LANES) // tn_proj) * tn_proj
    w_proj = _regroup_weights(w_in, (o_bd, o_moba, o_gate), n_proj)
    col_gdn = (2 * d) // LANES
    col_moba = col_gdn + 4 * GDN_HEADS
    col_bd = col_moba + 3 * MOBA_HEADS

    x2 = x.reshape(m, d)
    proj = _norm_matmul(x2, norm_mix_w, w_proj, _tile(m, PROJ_TILE[0]), tn_proj, F32, "proj")
    proj3 = proj.reshape(b, t, n_proj)

    lane_pad = jnp.zeros((1, LANES - 2 * GDN_HEADS), F32)
    head_pad = jnp.zeros((1, GDN_HEADS), F32)
    alog_row = jnp.concatenate([head_pad, a_log, lane_pad], axis=1)
    dtb_row = jnp.concatenate([head_pad, dt_bias, lane_pad], axis=1)
    y_a = _gdn(proj3, conv_w[0], alog_row, dtb_row, gdn_o_norm_w, col_gdn, col_bd, _tile(t, GDN_BLOCK))

    bias = _bias_tiles(rel_bias)
    q16, kn, vt, add = _moba_prep(proj3, q_norm_w, k_norm_w, rel_bias, col_moba, MOBA_PREP_ROWS)
    y_b = _moba(q16, kn, vt, add, bias)

    mix = _mix(y_a.reshape(m, gw), y_b.reshape(m, mw), proj,
               w_branch_gdn[0].astype(BF16), w_branch_moba[0].astype(BF16), _tile(m, MIX_ROWS), d)
    h1 = _matmul_res(mix, w_out[0].astype(BF16), x2, _tile(m, OUT_ROWS), d, "out")

    hid = _ffn_up(h1, norm_ffn_w, w_ffn_gate[0].astype(BF16), w_ffn_up[0].astype(BF16),
                  _tile(m, FFN_UP_TILE[0]), FFN_UP_TILE[1])
    h2 = _matmul_res(hid, w_ffn_down[0].astype(BF16), h1, _tile(m, FFN_DOWN_TILE[0]), FFN_DOWN_TILE[1],
                     "ffn_down")
    return h2.reshape(b, t, d)
```

```python
import functools
import math

import jax
import jax.numpy as jnp
from jax import lax
from jax.experimental import pallas as pl
from jax.experimental.pallas import tpu as pltpu

F32 = jnp.float32
BF16 = jnp.bfloat16
HIGHEST = lax.Precision.HIGHEST

LANES = 128
HEAD_DIM = 128
GDN_HEADS = 8
GDN_CONV = 4
GDN_CHUNK = 64
GDN_INV_BASE = 8
GDN_HEADS_PER_STEP = 8
NEG_INF = -1e30
MOBA_HEADS = 8
MOBA_BLOCK = 256
MOBA_TOPK = 3
REL_BUCKETS = 32
REL_MAX_DIST = 2048
BIAS_TILES = 8
MOBA_TILES_PER_STEP = 4
MOBA_PREP_ROWS = 4096
MOBA_STREAMS = 4
MOBA_V_ROWS = HEAD_DIM + 16
MOBA_M_INIT = 0.1 * NEG_INF
LOG2E = math.log2(math.e)
RMS_EPS = 1e-6
V7X_VMEM_BYTES = 64 * 1024 * 1024
VMEM_LIMIT = V7X_VMEM_BYTES * 7 // 8

PROJ_TILE = (1024, 1280)
MIX_ROWS = 256
OUT_ROWS = 512
FFN_UP_TILE = (1024, 512)
FFN_DOWN_TILE = (1024, 512)
GDN_BLOCK = 256
REGROUP_ROWS = 128


def _cparams(*sem):
    return pltpu.CompilerParams(dimension_semantics=sem, vmem_limit_bytes=VMEM_LIMIT)


def _sigmoid(x):
    return 0.5 + 0.5 * jnp.tanh(0.5 * x)


def _silu(x):
    h = 0.5 * x
    return h + h * jnp.tanh(h)


def _dot(a, b, precision=None):
    return jnp.dot(a, b, preferred_element_type=F32, precision=precision)


def _dot_nt(a, b, precision=None):
    return lax.dot_general(a, b, (((1,), (1,)), ((), ())),
                           preferred_element_type=F32, precision=precision)


def _dot_tn(a, b):
    return lax.dot_general(a, b, (((0,), (0,)), ((), ())), preferred_element_type=F32)


def _rms_rows(x, w):
    return x * lax.rsqrt(jnp.mean(x * x, axis=-1, keepdims=True) + RMS_EPS) * w


def _regroup_kernel(w_ref, o_ref, *, cuts):
    o_bd, o_moba, o_gate = cuts
    x = w_ref[...]
    parts = [x[:, o_gate:], x[:, :o_bd], x[:, o_moba:o_gate], x[:, o_bd:o_moba]]
    used = sum(p.shape[1] for p in parts)
    parts.append(jnp.zeros((x.shape[0], o_ref.shape[1] - used), x.dtype))
    o_ref[...] = jnp.concatenate(parts, axis=1).astype(o_ref.dtype)


def _regroup_weights(w, cuts, n_out):
    _, k, n_in = w.shape
    rows = _tile(k, REGROUP_ROWS)
    return pl.pallas_call(
        functools.partial(_regroup_kernel, cuts=cuts),
        grid=(k // rows,),
        in_specs=[pl.BlockSpec((None, rows, n_in), lambda i: (0, i, 0))],
        out_specs=pl.BlockSpec((rows, n_out), lambda i: (i, 0)),
        out_shape=jax.ShapeDtypeStruct((k, n_out), BF16),
        compiler_params=_cparams("parallel"),
        name="regroup",
    )(w)


def _norm_matmul_kernel(x_ref, nw_ref, w_ref, o_ref, u_scr):
    @pl.when(pl.program_id(1) == 0)
    def _():
        u_scr[...] = _rms_rows(x_ref[...], nw_ref[...]).astype(BF16)

    o_ref[...] = _dot(u_scr[...], w_ref[...]).astype(o_ref.dtype)


def _norm_matmul(x, nw, w, tm, tn, out_dtype, name):
    m, k = x.shape
    n = w.shape[1]
    assert m % tm == 0 and n % tn == 0
    return pl.pallas_call(
        _norm_matmul_kernel,
        grid=(m // tm, n // tn),
        in_specs=[pl.BlockSpec((tm, k), lambda i, j: (i, 0)),
                  pl.BlockSpec((1, k), lambda i, j: (0, 0)),
                  pl.BlockSpec((k, tn), lambda i, j: (0, j))],
        out_specs=pl.BlockSpec((tm, tn), lambda i, j: (i, j)),
        out_shape=jax.ShapeDtypeStruct((m, n), out_dtype),
        scratch_shapes=[pltpu.VMEM((tm, k), BF16)],
        compiler_params=_cparams("parallel", "arbitrary"),
        name=name,
    )(x, nw, w)


def _matmul_res_kernel(a_ref, w_ref, r_ref, o_ref):
    o_ref[...] = r_ref[...] + _dot(a_ref[...], w_ref[...])


def _matmul_res(a, w, res, tm, tn, name):
    m, k = a.shape
    n = w.shape[1]
    assert m % tm == 0 and n % tn == 0
    return pl.pallas_call(
        _matmul_res_kernel,
        grid=(m // tm, n // tn),
        in_specs=[pl.BlockSpec((tm, k), lambda i, j: (i, 0)),
                  pl.BlockSpec((k, tn), lambda i, j: (0, j)),
                  pl.BlockSpec((tm, tn), lambda i, j: (i, j))],
        out_specs=pl.BlockSpec((tm, tn), lambda i, j: (i, j)),
        out_shape=jax.ShapeDtypeStruct((m, n), F32),
        compiler_params=_cparams("parallel", "parallel"),
        name=name,
    )(a, w, res)


def _ffn_up_kernel(x_ref, nw_ref, wg_ref, wu_ref, o_ref, u_scr):
    @pl.when(pl.program_id(1) == 0)
    def _():
        u_scr[...] = _rms_rows(x_ref[...], nw_ref[...]).astype(BF16)

    u = u_scr[...]
    g = _dot(u, wg_ref[...])
    o_ref[...] = (_silu(g) * _dot(u, wu_ref[...])).astype(o_ref.dtype)


def _ffn_up(x, nw, wg, wu, tm, tn):
    m, k = x.shape
    n = wg.shape[1]
    assert m % tm == 0 and n % tn == 0
    return pl.pallas_call(
        _ffn_up_kernel,
        grid=(m // tm, n // tn),
        in_specs=[pl.BlockSpec((tm, k), lambda i, j: (i, 0)),
                  pl.BlockSpec((1, k), lambda i, j: (0, 0)),
                  pl.BlockSpec((k, tn), lambda i, j: (0, j)),
                  pl.BlockSpec((k, tn), lambda i, j: (0, j))],
        out_specs=pl.BlockSpec((tm, tn), lambda i, j: (i, j)),
        out_shape=jax.ShapeDtypeStruct((m, n), BF16),
        scratch_shapes=[pltpu.VMEM((tm, k), BF16)],
        compiler_params=_cparams("parallel", "arbitrary"),
        name="ffn_up",
    )(x, nw, wg, wu)


def _mix_kernel(ya_ref, yb_ref, ga_ref, gb_ref, wa_ref, wb_ref, o_ref):
    a = _dot(ya_ref[...], wa_ref[...])
    b = _dot(yb_ref[...], wb_ref[...])
    o_ref[...] = (_sigmoid(ga_ref[...]) * a + _sigmoid(gb_ref[...]) * b).astype(o_ref.dtype)


def _mix(ya, yb, proj, wa, wb, tm, tn):
    m, k = ya.shape
    n = wa.shape[1]
    nb = n // tn
    assert m % tm == 0 and n % tn == 0
    return pl.pallas_call(
        _mix_kernel,
        grid=(m // tm, nb),
        in_specs=[pl.BlockSpec((tm, k), lambda i, j: (i, 0)),
                  pl.BlockSpec((tm, k), lambda i, j: (i, 0)),
                  pl.BlockSpec((tm, tn), lambda i, j: (i, j)),
                  pl.BlockSpec((tm, tn), lambda i, j: (i, nb + j)),
                  pl.BlockSpec((k, tn), lambda i, j: (0, j)),
                  pl.BlockSpec((k, tn), lambda i, j: (0, j))],
        out_specs=pl.BlockSpec((tm, tn), lambda i, j: (i, j)),
        out_shape=jax.ShapeDtypeStruct((m, n), BF16),
        compiler_params=_cparams("parallel", "parallel"),
        name="mix",
    )(ya, yb, proj, proj, wa, wb)


def _mix_out_kernel(ya_ref, yb_ref, ga_ref, gb_ref, wa_ref, wb_ref, wo_ref, x_ref, o_ref):
    a = _dot(ya_ref[...], wa_ref[...])
    b = _dot(yb_ref[...], wb_ref[...])
    mix = (_sigmoid(ga_ref[...]) * a + _sigmoid(gb_ref[...]) * b).astype(BF16)
    o_ref[...] = x_ref[...] + _dot(mix, wo_ref[...])


def _mix_out(ya, yb, proj, wa, wb, wo, x, tm):
    m, k = ya.shape
    n = wa.shape[1]
    assert m % tm == 0

    def resident(shape):
        return pl.BlockSpec(shape, lambda i: (0, 0), pipeline_mode=pl.Buffered(1))

    return pl.pallas_call(
        _mix_out_kernel,
        grid=(m // tm,),
        in_specs=[pl.BlockSpec((tm, k), lambda i: (i, 0)),
                  pl.BlockSpec((tm, k), lambda i: (i, 0)),
                  pl.BlockSpec((tm, n), lambda i: (i, 0)),
                  pl.BlockSpec((tm, n), lambda i: (i, 1)),
                  resident((k, n)), resident((k, n)), resident((n, n)),
                  pl.BlockSpec((tm, n), lambda i: (i, 0))],
        out_specs=pl.BlockSpec((tm, n), lambda i: (i, 0)),
        out_shape=jax.ShapeDtypeStruct((m, n), F32),
        compiler_params=_cparams("parallel"),
        name="mix_out",
    )(ya, yb, proj, proj, wa, wb, wo, x)


def _split3(x):
    a = x.astype(BF16)
    r = x - a.astype(F32)
    b = r.astype(BF16)
    c = (r - b.astype(F32)).astype(BF16)
    return a, b, c


def _gdn_kernel(q_ref, k_ref, v_ref, z_ref, bd_ref, cwq_ref, cwk_ref, cwv_ref,
                alog_ref, dtb_ref, onw_ref, o_ref, s_scr, tail_scr, *, tb):
    hp = GDN_HEADS_PER_STEP
    C = GDN_CHUNK
    D = HEAD_DIM
    head0 = pl.program_id(1) * hp

    @pl.when(pl.program_id(2) == 0)
    def _():
        s_scr[...] = jnp.zeros_like(s_scr)
        tail_scr[...] = jnp.zeros_like(tail_scr)

    row8 = lax.broadcasted_iota(jnp.int32, (8, hp * D), 0)
    tails = []

    def conv_silu(u_ref, cw_ref, idx):
        u = u_ref[0]
        w = cw_ref[...]
        tail = tail_scr[idx]
        y = None
        ytop = None
        for s in (3, 2, 1):
            sh = pltpu.roll(u, s, axis=0)
            top = jnp.where(row8 < s, pltpu.roll(tail, s, axis=0), sh[0:8])
            wj = w[3 - s:4 - s]
            y = sh * wj if y is None else y + sh * wj
            ytop = top * wj if ytop is None else ytop + top * wj
        y = y + u * w[3:4]
        ytop = ytop + u[0:8] * w[3:4]
        tails.append(u[tb - 8:tb])
        y = jnp.concatenate([ytop, y[8:]], axis=0)
        return _silu(y)

    q_all = conv_silu(q_ref, cwq_ref, 0)
    k_all = conv_silu(k_ref, cwk_ref, 1)
    v_all = conv_silu(v_ref, cwv_ref, 2)

    bd = bd_ref[0]
    lane = lax.broadcasted_iota(jnp.int32, (tb, LANES), 1)
    beta_all = _sigmoid(bd)
    xg = bd + dtb_ref[...]
    softplus = jnp.maximum(xg, 0.0) + jnp.log1p(jnp.exp(-jnp.abs(xg)))
    g_all = -jnp.exp(alog_ref[...]) * softplus
    betas = [jnp.sum(jnp.where(lane == head0 + e, beta_all, 0.0), axis=-1, keepdims=True)
             for e in range(hp)]
    gs = [jnp.sum(jnp.where(lane == head0 + e + GDN_HEADS, g_all, 0.0), axis=-1, keepdims=True)
          for e in range(hp)]

    nc = tb // C
    ri = lax.broadcasted_iota(jnp.int32, (tb, tb), 0)
    ci = lax.broadcasted_iota(jnp.int32, (tb, tb), 1)
    same = (ri // C) == (ci // C)
    tril16 = jnp.where(same & (ri >= ci), 1.0, 0.0).astype(BF16)
    g_b = jnp.concatenate([jnp.broadcast_to(g, (tb, LANES)) for g in gs], axis=1)
    gc_all = sum(_dot(tril16, piece) for piece in _split3(g_b))

    iw = lax.broadcasted_iota(jnp.int32, (C, tb), 0)
    jw = lax.broadcasted_iota(jnp.int32, (C, tb), 1) % C
    incl_w = iw >= jw
    strict_w = iw > jw
    eye_w = jnp.where(iw == jw, 1.0, 0.0).astype(F32)
    low_half = (lax.broadcasted_iota(jnp.int32, (C, LANES), 1) < C)

    def block_diag(wide16):
        return jnp.where(same, jnp.concatenate([wide16] * nc, axis=0), jnp.zeros((), BF16))

    def pair_blocks(full):
        return jnp.where(low_half, full[:C], full[C:])

    E = range(hp)
    pairs = range(tb // LANES)

    def l2n(x):
        return x * lax.rsqrt(jnp.sum(x * x, axis=-1, keepdims=True) + RMS_EPS)

    qs = [l2n(q_all[:, e * D:(e + 1) * D]) * (D ** -0.5) for e in E]
    ks = [l2n(k_all[:, e * D:(e + 1) * D]) for e in E]
    vs = [v_all[:, e * D:(e + 1) * D] for e in E]
    gcs = [gc_all[:, e * D:(e + 1) * D] for e in E]
    g_col = [jnp.concatenate([pair_blocks(gc[p * LANES:(p + 1) * LANES]) for p in pairs], axis=1)
             for gc in gcs]
    g_row = [jnp.concatenate([gc[p * LANES:(p + 1) * LANES, :].T[0:1, :] for p in pairs], axis=1)
             for gc in gcs]
    decay = [jnp.where(incl_w, jnp.exp(jnp.where(incl_w, g_col[e] - g_row[e], 0.0)), 0.0) for e in E]
    k16 = [k.astype(BF16) for k in ks]
    q16 = [q.astype(BF16) for q in qs]
    kb = [ks[e] * betas[e] for e in E]
    kb16 = [x.astype(BF16) for x in kb]

    def pair_products(a16, b16):
        return jnp.concatenate([pair_blocks(_dot_nt(a16[p * LANES:(p + 1) * LANES],
                                                    b16[p * LANES:(p + 1) * LANES])) for p in pairs], axis=1)

    lmat = [jnp.where(strict_w, pair_products(kb16[e], k16[e]) * decay[e], 0.0) for e in E]
    amat = [jnp.where(incl_w, pair_products(q16[e], k16[e]) * decay[e], 0.0) for e in E]
    base = GDN_INV_BASE
    blk_i, blk_j = iw // base, jw // base
    neg_d16 = [jnp.where(blk_i == blk_j, -lmat[e], 0.0).astype(BF16) for e in E]
    tinv = [eye_w + neg_d16[e].astype(F32) for e in E]
    pw = [_dot(neg_d16[e], block_diag(neg_d16[e])) for e in E]
    span = 2
    while 2 * span < base:
        p16 = [pw[e].astype(BF16) for e in E]
        prod = [_dot(jnp.concatenate([p16[e], tinv[e].astype(BF16)], axis=0), block_diag(p16[e])) for e in E]
        pw = [prod[e][:C] for e in E]
        tinv = [tinv[e] + prod[e][C:] for e in E]
        span *= 2
    tinv = [tinv[e] + _dot(tinv[e].astype(BF16), block_diag(pw[e].astype(BF16))) for e in E]
    s = base
    while s < C:
        below_left = (iw // (2 * s) == jw // (2 * s)) & ((iw // s) % 2 == 1) & ((jw // s) % 2 == 0)
        c16 = [jnp.where(below_left, lmat[e], 0.0).astype(BF16) for e in E]
        cx = [_dot(c16[e], block_diag(tinv[e].astype(BF16))) for e in E]
        tinv = [tinv[e] - _dot(tinv[e].astype(BF16), block_diag(cx[e].astype(BF16))) for e in E]
        s *= 2
    eg = [jnp.exp(gc) for gc in gcs]
    rhs = [jnp.concatenate([kb[e] * eg[e], vs[e] * betas[e]], axis=1).astype(BF16) for e in E]
    wu = [_dot(block_diag(tinv[e].astype(BF16)), rhs[e]).astype(BF16) for e in E]
    au = [_dot(block_diag(amat[e].astype(BF16)), wu[e]) for e in E]
    q_eff = [(qs[e] * eg[e] - au[e][:, :D]).astype(BF16) for e in E]
    gl = [jnp.concatenate([jnp.broadcast_to(gc[c * C + C - 1:c * C + C, :], (C, LANES))
                           for c in range(nc)], axis=0) for gc in gcs]
    k_dec = [(ks[e] * jnp.exp(gl[e] - gcs[e])).astype(BF16) for e in E]
    pn = [[_dot_tn(k_dec[e][c * C:(c + 1) * C], wu[e][c * C:(c + 1) * C]) for c in range(nc)]
          for e in E]

    states = [s_scr[e] for e in E]
    onw = onw_ref[...]
    for c in range(nc):
        r0 = c * C
        lhs = [jnp.concatenate([pn[e][c][:, :D].astype(BF16), q_eff[e][r0:r0 + C]], axis=0) for e in E]
        res = [_dot(lhs[e], states[e].astype(BF16)) for e in E]
        outs = [_rms_rows(res[e][D:] + au[e][r0:r0 + C, D:], onw) for e in E]
        states = [states[e] * jnp.exp(gl[e][r0:r0 + 1, :]) - res[e][:D] + pn[e][c][:, D:] for e in E]
        zc = z_ref[0, r0:r0 + C, :]
        o_ref[0, r0:r0 + C, :] = (jnp.concatenate(outs, axis=1) * _silu(zc)).astype(o_ref.dtype)
    for e in range(hp):
        s_scr[e] = states[e]
    for idx in range(3):
        tail_scr[idx] = tails[idx]


def _gdn(proj3, conv_w, alog_row, dtb_row, onw, col0, bd_col, tb):
    b, t, _ = proj3.shape
    hp = GDN_HEADS_PER_STEP
    ng = GDN_HEADS // hp
    w = hp * HEAD_DIM

    def col(base):
        return pl.BlockSpec((1, tb, w), lambda bi, hi, ti: (bi, ti, base // hp + hi))

    def cw(base):
        return pl.BlockSpec((GDN_CONV, w), lambda bi, hi, ti: (0, base // hp + hi))

    assert col0 % hp == 0
    row = pl.BlockSpec((1, LANES), lambda bi, hi, ti: (0, 0))
    return pl.pallas_call(
        functools.partial(_gdn_kernel, tb=tb),
        grid=(b, ng, t // tb),
        in_specs=[col(col0), col(col0 + GDN_HEADS), col(col0 + 2 * GDN_HEADS), col(col0 + 3 * GDN_HEADS),
                  pl.BlockSpec((1, tb, LANES), lambda bi, hi, ti: (bi, ti, bd_col)),
                  cw(0), cw(GDN_HEADS), cw(2 * GDN_HEADS), row, row, row],
        out_specs=pl.BlockSpec((1, tb, w), lambda bi, hi, ti: (bi, ti, hi)),
        out_shape=jax.ShapeDtypeStruct((b, t, GDN_HEADS * HEAD_DIM), BF16),
        scratch_shapes=[pltpu.VMEM((hp, HEAD_DIM, HEAD_DIM), F32),
                        pltpu.VMEM((3, 8, w), F32)],
        compiler_params=_cparams("parallel", "parallel", "arbitrary"),
        name="gdn",
    )(proj3, proj3, proj3, proj3, proj3, conv_w, conv_w, conv_w, alog_row, dtb_row, onw)


def _bias_kernel(rel_ref, o_ref):
    h = pl.program_id(0)
    bs = MOBA_BLOCK
    max_exact = REL_BUCKETS // 2
    x = lax.broadcasted_iota(jnp.int32, (8, 2 * bs), 1)
    for d in range(BIAS_TILES):
        dist = jnp.maximum(d * bs + x - bs, 0)
        df = dist.astype(F32)
        log_ratio = jnp.log(jnp.maximum(df, float(max_exact)) / max_exact) / math.log(REL_MAX_DIST / max_exact)
        large = max_exact + (log_ratio * (REL_BUCKETS - max_exact)).astype(jnp.int32)
        large = jnp.minimum(large, REL_BUCKETS - 1)
        bucket = jnp.where(dist < max_exact, dist, large)
        row = jnp.zeros((8, 2 * bs), F32)
        for b in range(REL_BUCKETS):
            row = jnp.where(bucket == b, rel_ref[b, h], row)
        base = jnp.broadcast_to(row[0:1, :] * LOG2E, (bs, 2 * bs))
        tile = pltpu.roll(base, 0, 1, stride=1, stride_axis=0)[:, bs:]
        if d == 0:
            kk = lax.broadcasted_iota(jnp.int32, (bs, bs), 0)
            qq = lax.broadcasted_iota(jnp.int32, (bs, bs), 1)
            tile = jnp.where(kk <= qq, tile, NEG_INF)
        o_ref[0, d] = tile


def _bias_tiles(rel_bias):
    bs = MOBA_BLOCK
    max_exact = REL_BUCKETS // 2
    nearest = (BIAS_TILES - 1) * bs - (bs - 1)
    assert max_exact + int(math.log(nearest / max_exact) / math.log(REL_MAX_DIST / max_exact)
                           * (REL_BUCKETS - max_exact)) >= REL_BUCKETS - 1
    return pl.pallas_call(
        _bias_kernel,
        grid=(MOBA_HEADS,),
        in_specs=[pl.BlockSpec(memory_space=pltpu.SMEM)],
        out_specs=pl.BlockSpec((1, BIAS_TILES, bs, bs), lambda h: (h, 0, 0, 0)),
        out_shape=jax.ShapeDtypeStruct((MOBA_HEADS, BIAS_TILES, bs, bs), F32),
        compiler_params=_cparams("parallel"),
        name="bias",
    )(rel_bias)


def _moba_prep_kernel(q_ref, k_ref, v_ref, qw_ref, kw_ref, rel_ref, q16_ref, kn_ref, vt_ref, add_ref, km_scr,
                      *, rows, nb):
    bs = MOBA_BLOCK
    nsub = rows // bs
    ti = pl.program_id(2)

    @pl.when(ti == 0)
    def _():
        km_scr[...] = jnp.zeros_like(km_scr)

    kn = _rms_rows(k_ref[0], kw_ref[...])
    kn_ref[0, 0] = kn.astype(BF16)
    for s in range(nsub):
        km_scr[pl.ds(ti * nsub + s, 1), :] = jnp.mean(kn[s * bs:(s + 1) * bs], axis=0, keepdims=True)
        vt_ref[0, 0, 0:HEAD_DIM, s * bs:(s + 1) * bs] = v_ref[0, s * bs:(s + 1) * bs, :].T.astype(BF16)
    pad_row = lax.broadcasted_iota(jnp.int32, (MOBA_V_ROWS - HEAD_DIM, rows), 0)
    vt_ref[0, 0, HEAD_DIM:, :] = jnp.where(pad_row == 0, 1.0, 0.0).astype(BF16)

    q = _rms_rows(q_ref[0], qw_ref[...])
    q16_ref[0, 0] = (q * ((HEAD_DIM ** -0.5) * LOG2E)).astype(BF16)

    blk = lax.broadcasted_iota(jnp.int32, (nb, rows), 0)
    own_blk = ti * nsub + lax.broadcasted_iota(jnp.int32, (nb, rows), 1) // bs
    blkf = blk.astype(F32)
    valid = blk < own_blk
    work = jnp.where(valid, _dot_nt(km_scr[...], q, HIGHEST), NEG_INF)
    picked = jnp.zeros((nb, rows), F32)
    for _ in range(MOBA_TOPK):
        best = jnp.max(work, axis=0, keepdims=True)
        first = jnp.min(jnp.where(work == best, blkf, float(nb)), axis=0, keepdims=True)
        hit = blkf == first
        picked = jnp.where(hit, 1.0, picked)
        work = jnp.where(hit, -jnp.inf, work)
    far_bias = rel_ref[REL_BUCKETS - 1, pl.program_id(1)] * LOG2E
    attended = (valid & (picked > 0.0)) | (blk == own_blk)
    add_ref[0, 0] = jnp.where(attended, far_bias, NEG_INF)


def _moba_prep(proj3, qw, kw, rel_bias, col0, rows):
    b, t, _ = proj3.shape
    hh, bs, dh = MOBA_HEADS, MOBA_BLOCK, HEAD_DIM
    nb = t // bs

    def col(base):
        return pl.BlockSpec((1, rows, LANES), lambda bi, hi, ti: (bi, ti, base + hi))

    row = pl.BlockSpec((1, LANES), lambda bi, hi, ti: (0, 0))
    return pl.pallas_call(
        functools.partial(_moba_prep_kernel, rows=rows, nb=nb),
        grid=(b, hh, t // rows),
        in_specs=[col(col0), col(col0 + hh), col(col0 + 2 * hh), row, row,
                  pl.BlockSpec(memory_space=pltpu.SMEM)],
        out_specs=[pl.BlockSpec((1, 1, rows, dh), lambda bi, hi, ti: (bi, hi, ti, 0)),
                   pl.BlockSpec((1, 1, rows, dh), lambda bi, hi, ti: (bi, hi, ti, 0)),
                   pl.BlockSpec((1, 1, MOBA_V_ROWS, rows), lambda bi, hi, ti: (bi, hi, 0, ti)),
                   pl.BlockSpec((1, 1, nb, rows), lambda bi, hi, ti: (bi, hi, 0, ti))],
        out_shape=[jax.ShapeDtypeStruct((b, hh, t, dh), BF16),
                   jax.ShapeDtypeStruct((b, hh, t, dh), BF16),
                   jax.ShapeDtypeStruct((b, hh, MOBA_V_ROWS, t), BF16),
                   jax.ShapeDtypeStruct((b, hh, nb, t), F32)],
        scratch_shapes=[pltpu.VMEM((nb, dh), F32)],
        compiler_params=_cparams("parallel", "parallel", "arbitrary"),
        name="moba_prep",
    )(proj3, proj3, proj3, qw, kw, rel_bias)


def _moba_kernel(q_ref, kn_ref, vt_ref, add_ref, bias_ref, o_ref, buf_a, buf_b, *, nb):
    i = pl.program_id(2)
    bs = MOBA_BLOCK
    ch = MOBA_TILES_PER_STEP
    E = range(MOBA_STREAMS)
    q16 = [q_ref[e, 0] for e in E]

    crows = ch * bs
    n_pairs = (i + 2 * ch) // (2 * ch)
    last_chunk = nb // ch - 1

    def chunk_scores(buf, e, c, far):
        r = pl.multiple_of(c * crows, crows)
        raw = _dot_nt(kn_ref[e, 0, pl.ds(r, crows), :], q16[e])
        mx = None
        for t in range(ch):
            j = c * ch + t
            add = add_ref[e, 0, pl.ds(j, 1), :]
            if far:
                st = raw[t * bs:(t + 1) * bs, :] + add
            else:
                d = jnp.clip(i - j, 0, BIAS_TILES - 1)
                st = jnp.where(add > 0.5 * NEG_INF, raw[t * bs:(t + 1) * bs, :] + bias_ref[0, d], NEG_INF)
            buf[e, t * bs:(t + 1) * bs, :] = st
            tmx = jnp.max(st, axis=0, keepdims=True)
            mx = tmx if mx is None else jnp.maximum(mx, tmx)
        return mx

    def absorb(buf, e, c, mx, m, acc):
        r = pl.multiple_of(c * crows, crows)
        m_new = jnp.maximum(m, mx)
        p = jnp.exp2(buf[e] - m_new)
        acc = jnp.exp2(m - m_new) * acc + _dot(vt_ref[e, 0, :, pl.ds(r, crows)], p.astype(BF16))
        return m_new, acc

    def pair(far, c, carry):
        m, acc, mx_a = (list(x) for x in carry)
        mx_b = [chunk_scores(buf_b, e, 2 * c + 1, far) for e in E]
        for e in E:
            m[e], acc[e] = absorb(buf_a, e, 2 * c, mx_a[e], m[e], acc[e])
        mx_a = [chunk_scores(buf_a, e, jnp.minimum(2 * c + 2, last_chunk), far) for e in E]
        for e in E:
            m[e], acc[e] = absorb(buf_b, e, 2 * c + 1, mx_b[e], m[e], acc[e])
        return tuple(m), tuple(acc), tuple(mx_a)

    def step(c, carry):
        all_far = (2 * c + 3) * ch - 1 + (BIAS_TILES - 1) <= i
        return lax.cond(all_far, functools.partial(pair, True, c), functools.partial(pair, False, c), carry)

    m0 = tuple(jnp.full((1, bs), MOBA_M_INIT, F32) for e in E)
    acc0 = tuple(jnp.zeros((MOBA_V_ROWS, bs), F32) for e in E)
    mx_a = tuple(chunk_scores(buf_a, e, 0, False) for e in E)
    m, acc, _ = lax.fori_loop(0, n_pairs, step, (m0, acc0, mx_a))
    for e in E:
        o_ref[e] = (acc[e][:HEAD_DIM] / acc[e][HEAD_DIM:HEAD_DIM + 1]).T.astype(o_ref.dtype)


def _moba(q16, kn, vt, add, bias):
    b, hh, t, dh = kn.shape
    bs = MOBA_BLOCK
    nb = t // bs
    ns = MOBA_STREAMS
    crows = MOBA_TILES_PER_STEP * bs
    assert b % ns == 0 and nb % (2 * MOBA_TILES_PER_STEP) == 0
    return pl.pallas_call(
        functools.partial(_moba_kernel, nb=nb),
        grid=(b // ns, hh, nb),
        in_specs=[pl.BlockSpec((ns, 1, bs, dh), lambda bi, hi, ti: (bi, hi, ti, 0)),
                  pl.BlockSpec((ns, 1, t, dh), lambda bi, hi, ti: (bi, hi, 0, 0)),
                  pl.BlockSpec((ns, 1, MOBA_V_ROWS, t), lambda bi, hi, ti: (bi, hi, 0, 0)),
                  pl.BlockSpec((ns, 1, nb, bs), lambda bi, hi, ti: (bi, hi, 0, ti)),
                  pl.BlockSpec((1, BIAS_TILES, bs, bs), lambda bi, hi, ti: (hi, 0, 0, 0))],
        out_specs=pl.BlockSpec((ns, bs, dh), lambda bi, hi, ti: (bi, ti, hi)),
        out_shape=jax.ShapeDtypeStruct((b, t, hh * dh), BF16),
        scratch_shapes=[pltpu.VMEM((ns, crows, bs), F32),
                        pltpu.VMEM((ns, crows, bs), F32)],
        compiler_params=_cparams("parallel", "parallel", "parallel"),
        name="moba",
    )(q16, kn, vt, add, bias)


def _tile(n, want):
    t = min(n, want)
    assert n % t == 0, (n, want)
    return t


def kernel(x, norm_mix_w, w_in, conv_w, a_log, dt_bias, gdn_o_norm_w, q_norm_w, k_norm_w, rel_bias,
           w_branch_gdn, w_branch_moba, w_out, norm_ffn_w, w_ffn_gate, w_ffn_up, w_ffn_down):
    b, t, d = x.shape
    m = b * t
    gw = GDN_HEADS * HEAD_DIM
    mw = MOBA_HEADS * HEAD_DIM
    assert t % MOBA_PREP_ROWS == 0 and w_in.shape[0] == 1
    assert w_in.shape[2] == 4 * gw + 2 * GDN_HEADS + 3 * mw + 2 * d

    o_bd = 4 * gw
    o_moba = o_bd + 2 * GDN_HEADS
    o_gate = o_moba + 3 * mw
    n_main = 2 * d + 4 * gw + 3 * mw
    tn_proj = PROJ_TILE[1]
    n_proj = -(-(n_main + LANES) // tn_proj) * tn_proj
    w_proj = _regroup_weights(w_in, (o_bd, o_moba, o_gate), n_proj)
    col_gdn = (2 * d) // LANES
    col_moba = col_gdn + 4 * GDN_HEADS
    col_bd = col_moba + 3 * MOBA_HEADS

    x2 = x.reshape(m, d)
    proj = _norm_matmul(x2, norm_mix_w, w_proj, _tile(m, PROJ_TILE[0]), tn_proj, F32, "proj")
    proj3 = proj.reshape(b, t, n_proj)

    lane_pad = jnp.zeros((1, LANES - 2 * GDN_HEADS), F32)
    head_pad = jnp.zeros((1, GDN_HEADS), F32)
    alog_row = jnp.concatenate([head_pad, a_log, lane_pad], axis=1)
    dtb_row = jnp.concatenate([head_pad, dt_bias, lane_pad], axis=1)
    y_a = _gdn(proj3, conv_w[0], alog_row, dtb_row, gdn_o_norm_w, col_gdn, col_bd, _tile(t, GDN_BLOCK))

    bias = _bias_tiles(rel_bias)
    q16, kn, vt, add = _moba_prep(proj3, q_norm_w, k_norm_w, rel_bias, col_moba, MOBA_PREP_ROWS)
    y_b = _moba(q16, kn, vt, add, bias)

    h1 = _mix_out(y_a.reshape(m, gw), y_b.reshape(m, mw), proj,
                  w_branch_gdn[0].astype(BF16), w_branch_moba[0].astype(BF16), w_out[0].astype(BF16),
                  x2, _tile(m, MIX_ROWS))

    hid = _ffn_up(h1, norm_ffn_w, w_ffn_gate[0].astype(BF16), w_ffn_up[0].astype(BF16),
                  _tile(m, FFN_UP_TILE[0]), FFN_UP_TILE[1])
    h2 = _matmul_res(hid, w_ffn_down[0].astype(BF16), h1, _tile(m, FFN_DOWN_TILE[0]), FFN_DOWN_TILE[1],
                     "ffn_down")
    return h2.reshape(b, t, d)
```
